```python
import math
import jax, jax.numpy as jnp
from jax import lax
import numpy as np

D_MODEL = 1024
BATCH = 4
SEQ = 4096
DEPTH = 1
DEC_BATCH = 128
DEC_SEQ = 8
PAST_LEN = 2048
PAGE_SIZE = 128

MIX_WIDTH = D_MODEL
POOL_WIDTH = MIX_WIDTH // 4
POOL_WINDOWS = (2, 4, 8, 16)
POOL_GROUPS = len(POOL_WINDOWS)
POOL_GW = POOL_WIDTH // POOL_GROUPS
POOL_STATE = max(POOL_WINDOWS) - 1
ATTN_WIDTH = MIX_WIDTH - POOL_WIDTH
HEAD_DIM = 64
N_HEADS = ATTN_WIDTH // HEAD_DIM
BRANCHES = ((128, 1), (512, 4), (2048, 16))
STEPS = BRANCHES[0][0] // BRANCHES[0][1]
ATTN_REACH = max(w for w, _ in BRANCHES)
MAX_DIL = max(d for _, d in BRANCHES)
BLOCK = STEPS
PROJ_WIDTH = POOL_WIDTH + 3 * ATTN_WIDTH
N_GROUPS = 4
EXPERTS_PER_GROUP = 8
EXPERT_FF = 256
TOP_K_INNER = 2
RMS_EPS = 1e-6

kernel_name = 'hymba_pool_dilated_alibi_hmoe_step'


def alibi_slopes():
    def geometric(n):
        start = 2.0 ** (-8.0 / n)
        return [start ** (i + 1) for i in range(n)]
    c = 2 ** int(math.floor(math.log2(N_HEADS)))
    s = geometric(c)
    if c < N_HEADS:
        s = s + geometric(2 * c)[0::2][: N_HEADS - c]
    return jnp.asarray(s, dtype=jnp.float32)


def rmsnorm(x, g):
    xf = x.astype(jnp.float32)
    y = xf * lax.rsqrt(jnp.mean(xf * xf, axis=-1, keepdims=True) + RMS_EPS)
    return (y * g.astype(jnp.float32)).astype(x.dtype)


def mix_in(x, g_mix, w_in):
    n, t, _ = x.shape
    proj = jnp.einsum('btd,dc->btc', rmsnorm(x, g_mix), w_in)
    u = proj[..., :POOL_WIDTH]
    qkv = proj[..., POOL_WIDTH:].reshape(n, t, 3, N_HEADS, HEAD_DIM)
    return u, qkv[:, :, 0], qkv[:, :, 1], qkv[:, :, 2]


def pool_mix(u_ext, w_pool, pool_scale, n_out):
    L = u_ext.shape[1]
    uf = u_ext.astype(jnp.float32)
    cs = jnp.pad(jnp.cumsum(uf, axis=1), ((0, 0), (1, 0), (0, 0)))
    rows = jnp.arange(L - n_out, L)
    hi = cs[:, rows + 1]
    x_tok = uf[:, rows]
    outs = []
    for g, w in enumerate(POOL_WINDOWS):
        sl = slice(g * POOL_GW, (g + 1) * POOL_GW)
        lo = jnp.maximum(rows + 1 - w, 0)
        cnt = (rows + 1 - lo).astype(jnp.float32)
        mean = (hi[..., sl] - cs[:, lo, sl]) / cnt[None, :, None]
        outs.append(jnp.einsum('ntc,cd->ntd', mean - x_tok[..., sl], w_pool[g].astype(jnp.float32)))
    return (jnp.concatenate(outs, axis=-1) * pool_scale.astype(jnp.float32)).astype(u_ext.dtype)


def merge_branches(outs, lses):
    w = jax.nn.softmax(jnp.stack(lses), axis=0)
    return jnp.einsum('i...h,i...he->...he', w, jnp.stack(outs))


def branch_banded(q, k, v, d, slopes):
    b, s_pad, h, e = q.shape
    n = s_pad // d
    nb = n // BLOCK

    def split(t):
        t = t.reshape(b, n, d, h, e).transpose(0, 2, 1, 3, 4)
        return t.reshape(b, d, nb, BLOCK, h, e)

    def with_prev(t):
        prev = jnp.concatenate([jnp.zeros_like(t[:, :, :1]), t[:, :, :-1]], axis=2)
        return jnp.concatenate([prev, t], axis=3)

    qb = split(q)
    kc = with_prev(split(k))
    vc = with_prev(split(v))
    scores = jnp.einsum('brnqhe,brnkhe->brnhqk', qb, kc).astype(jnp.float32) * (HEAD_DIM ** -0.5)
    qi = jnp.arange(BLOCK)[:, None]
    ki = jnp.arange(2 * BLOCK)[None, :]
    step = qi + BLOCK - ki
    band = (step >= 0) & (step <= STEPS)
    valid = band[None] & ((jnp.arange(nb)[:, None, None] > 0) | (ki[None] >= BLOCK))
    bias = -slopes[:, None, None] * (d * step).astype(jnp.float32)[None]
    scores = jnp.where(valid[None, None, :, None], scores + bias, -jnp.inf)
    m = jnp.max(scores, axis=-1, keepdims=True)
    p = jnp.exp(scores - m)
    l = jnp.sum(p, axis=-1)
    o = jnp.einsum('brnhqk,brnkhe->brnqhe', p, vc.astype(jnp.float32)) / jnp.swapaxes(l, -1, -2)[..., None]
    lse = jnp.swapaxes(m[..., 0] + jnp.log(l), -1, -2)
    o = o.reshape(b, d, n, h, e).transpose(0, 2, 1, 3, 4).reshape(b, s_pad, h, e)
    lse = lse.reshape(b, d, n, h).transpose(0, 2, 1, 3).reshape(b, s_pad, h)
    return o, lse


def dilated_attention_prompt(q, k, v, slopes):
    s = q.shape[1]
    span = BLOCK * MAX_DIL
    s_pad = -(-s // span) * span
    pad = ((0, 0), (0, s_pad - s), (0, 0), (0, 0))
    qp, kp, vp = jnp.pad(q, pad), jnp.pad(k, pad), jnp.pad(v, pad)
    outs, lses = [], []
    for _, d in BRANCHES:
        o, lse = branch_banded(qp, kp, vp, d, slopes)
        outs.append(o[:, :s])
        lses.append(lse[:, :s])
    return merge_branches(outs, lses).astype(q.dtype)


def dilated_attention_sample(q, k_ext, v_ext, slopes):
    t = q.shape[1]
    lk = k_ext.shape[1]
    q_loc = lk - t + jnp.arange(t)
    j = jnp.arange(STEPS + 1)
    outs, lses = [], []
    for _, d in BRANCHES:
        idx = q_loc[:, None] - d * j[None, :]
        valid = idx >= 0
        idx = jnp.maximum(idx, 0)
        kg = k_ext[:, idx]
        vg = v_ext[:, idx]
        scores = jnp.einsum('nthe,ntjhe->nthj', q, kg).astype(jnp.float32) * (HEAD_DIM ** -0.5)
        scores = scores - slopes[:, None] * (d * j).astype(jnp.float32)[None, :]
        scores = jnp.where(valid[None, :, None, :], scores, -jnp.inf)
        m = jnp.max(scores, axis=-1, keepdims=True)
        p = jnp.exp(scores - m)
        l = jnp.sum(p, axis=-1)
        o = jnp.einsum('nthj,ntjhe->nthe', p, vg.astype(jnp.float32)) / l[..., None]
        outs.append(o)
        lses.append(m[..., 0] + jnp.log(l))
    return merge_branches(outs, lses).astype(q.dtype)


def mix_out(pool_out, attn_out, w_out):
    n, t = pool_out.shape[:2]
    cat = jnp.concatenate([pool_out, attn_out.reshape(n, t, ATTN_WIDTH)], axis=-1)
    return jnp.einsum('btc,cd->btd', cat, w_out)


def hier_moe(x, w_rg, b_rg, w_re, b_re, w_gate, w_up, w_down):
    lg = jnp.einsum('btd,dg->btg', x, w_rg).astype(jnp.float32) + b_rg.astype(jnp.float32)
    pg = jax.nn.softmax(lg, axis=-1)
    g_top = jnp.argmax(lg, axis=-1)
    p_sel = jnp.take_along_axis(pg, g_top[..., None], axis=-1)
    le = jnp.einsum('btd,gde->btge', x, w_re).astype(jnp.float32) + b_re.astype(jnp.float32)
    le_sel = jnp.take_along_axis(le, g_top[..., None, None], axis=2)[..., 0, :]
    pe = jax.nn.softmax(le_sel, axis=-1)
    top_v, top_i = lax.top_k(pe, TOP_K_INNER)
    top_v = top_v / jnp.sum(top_v, axis=-1, keepdims=True)
    gate_e = jnp.sum(jax.nn.one_hot(top_i, EXPERTS_PER_GROUP, dtype=jnp.float32) * top_v[..., None], axis=-2)
    group_gate = jax.nn.one_hot(g_top, N_GROUPS, dtype=jnp.float32) * p_sel
    out = jnp.zeros(x.shape, jnp.float32)
    for g in range(N_GROUPS):
        hg = jax.nn.silu(jnp.einsum('btd,edf->btef', x, w_gate[g])) * jnp.einsum('btd,edf->btef', x, w_up[g])
        wg = (group_gate[..., g:g + 1] * gate_e).astype(hg.dtype)
        out = out + jnp.einsum('btef,efd->btd', hg * wg[..., None], w_down[g]).astype(jnp.float32)
    return out.astype(x.dtype)


def setup_inputs(seed: int = 0) -> dict:
    key = jax.random.key(seed)
    ks = jax.random.split(key, 20)
    f32 = jnp.float32

    def nrm(k, shape, scale):
        return jax.random.normal(k, shape, f32) * scale

    kv_keep = min(ATTN_REACH, PAST_LEN)
    G, E, F, D = N_GROUPS, EXPERTS_PER_GROUP, EXPERT_FF, D_MODEL
    return {
        'x_prompt': nrm(ks[0], (BATCH, SEQ, D), 1.0),
        'x_sample': nrm(ks[1], (DEC_BATCH, DEC_SEQ, D), 1.0),
        'cache_k': nrm(ks[2], (DEPTH, DEC_BATCH, kv_keep, N_HEADS, HEAD_DIM), 1.0),
        'cache_v': nrm(ks[3], (DEPTH, DEC_BATCH, kv_keep, N_HEADS, HEAD_DIM), 1.0),
        'state_pool': nrm(ks[4], (DEPTH, DEC_BATCH, POOL_STATE, POOL_WIDTH), 1.0),
        'g_mix': 1.0 + nrm(ks[5], (DEPTH, D), 0.02),
        'w_in': nrm(ks[6], (DEPTH, D, PROJ_WIDTH), D ** -0.5),
        'w_pool': nrm(ks[7], (DEPTH, POOL_GROUPS, POOL_GW, POOL_GW), POOL_GW ** -0.5),
        'pool_scale': 1.0 + nrm(ks[8], (DEPTH, POOL_WIDTH), 0.1),
        'w_out': nrm(ks[9], (DEPTH, MIX_WIDTH, D), MIX_WIDTH ** -0.5),
        'g_ffn': 1.0 + nrm(ks[10], (DEPTH, D), 0.02),
        'w_router_group': nrm(ks[11], (DEPTH, D, G), D ** -0.5),
        'b_router_group': nrm(ks[12], (DEPTH, G), 0.01),
        'w_router_expert': nrm(ks[13], (DEPTH, G, D, E), D ** -0.5),
        'b_router_expert': nrm(ks[14], (DEPTH, G, E), 0.01),
        'w_gate': nrm(ks[15], (DEPTH, G, E, D, F), D ** -0.5),
        'w_up': nrm(ks[16], (DEPTH, G, E, D, F), D ** -0.5),
        'w_down': nrm(ks[17], (DEPTH, G, E, F, D), F ** -0.5),
        'g_final': 1.0 + nrm(ks[18], (D,), 0.02),
    }


def reference(x_prompt, x_sample, cache_k, cache_v, state_pool, g_mix, w_in, w_pool, pool_scale,
              w_out, g_ffn, w_router_group, b_router_group, w_router_expert, b_router_expert,
              w_gate, w_up, w_down, g_final):
    slopes = alibi_slopes()
    hp, hs = x_prompt, x_sample
    seq = x_prompt.shape[1]
    keep = min(ATTN_REACH, seq)
    kp_l, vp_l, pp_l, ks_l, vs_l, ps_l = [], [], [], [], [], []
    for layer in range(DEPTH):
        moe_w = (w_router_group[layer], b_router_group[layer], w_router_expert[layer],
                 b_router_expert[layer], w_gate[layer], w_up[layer], w_down[layer])
        u, q, k, v = mix_in(hp, g_mix[layer], w_in[layer])
        attn = dilated_attention_prompt(q, k, v, slopes)
        pool = pool_mix(u, w_pool[layer], pool_scale[layer], seq)
        hp = hp + mix_out(pool, attn, w_out[layer])
        hp = hp + hier_moe(rmsnorm(hp, g_ffn[layer]), *moe_w)
        kp_l.append(k[:, seq - keep:])
        vp_l.append(v[:, seq - keep:])
        pp_l.append(u[:, seq - POOL_STATE:])
        u, q, k, v = mix_in(hs, g_mix[layer], w_in[layer])
        k_ext = jnp.concatenate([cache_k[layer].astype(k.dtype), k], axis=1)
        v_ext = jnp.concatenate([cache_v[layer].astype(v.dtype), v], axis=1)
        attn = dilated_attention_sample(q, k_ext, v_ext, slopes)
        u_ext = jnp.concatenate([state_pool[layer].astype(u.dtype), u], axis=1)
        pool = pool_mix(u_ext, w_pool[layer], pool_scale[layer], hs.shape[1])
        hs = hs + mix_out(pool, attn, w_out[layer])
        hs = hs + hier_moe(rmsnorm(hs, g_ffn[layer]), *moe_w)
        ks_l.append(k)
        vs_l.append(v)
        ps_l.append(u_ext[:, -POOL_STATE:])
    y_prompt = rmsnorm(hp, g_final)
    y_sample = rmsnorm(hs, g_final)
    return (y_prompt, y_sample, jnp.stack(kp_l), jnp.stack(vp_l), jnp.stack(pp_l),
            jnp.stack(ks_l), jnp.stack(vs_l), jnp.stack(ps_l))
```

```python
import functools
import math

import numpy as np
import jax
import jax.numpy as jnp
from jax import lax
from jax.experimental import pallas as pl
from jax.experimental.pallas import tpu as pltpu

F32 = jnp.float32
BF16 = jnp.bfloat16
I32 = jnp.int32

HEAD_DIM = 64
POOL_WINDOWS = (2, 4, 8, 16)
POOL_STATE = max(POOL_WINDOWS) - 1
BRANCHES = ((128, 1), (512, 4), (2048, 16))
STEPS = BRANCHES[0][0] // BRANCHES[0][1]
ATTN_REACH = max(w for w, _ in BRANCHES)
TOP_K_INNER = 2
RMS_EPS = 1e-6
LANES = 128
NEG_INF = float("-inf")

VMEM_LIMIT = 56 * 1024 * 1024

TM_PROJ = 512
TM_MIX = 512
TM_ROW = 256
TM_GEMM = 256


def _alibi_slopes(n_heads):
    def geometric(n):
        start = 2.0 ** (-8.0 / n)
        return [start ** (i + 1) for i in range(n)]
    c = 2 ** int(math.floor(math.log2(n_heads)))
    s = geometric(c)
    if c < n_heads:
        s = s + geometric(2 * c)[0::2][: n_heads - c]
    return jnp.asarray(s, dtype=F32)


def _rms(x, g):
    return x * lax.rsqrt(jnp.mean(x * x, axis=-1, keepdims=True) + RMS_EPS) * g


def _params(*sem):
    return pltpu.CompilerParams(dimension_semantics=sem, vmem_limit_bytes=VMEM_LIMIT)


def _proj_in_body(x_ref, g_ref, w_ref, u_ref, q_ref, k_ref, v_ref, *t_refs, pool_w, attn_w, tiles_per_seq,
                  keep_tiles):
    xn = _rms(x_ref[...], g_ref[...]).astype(BF16)

    def proj(lo, n):
        return jnp.dot(xn, w_ref[:, lo:lo + n], preferred_element_type=F32)

    u_ref[...] = proj(0, pool_w)
    q_ref[...] = proj(pool_w, attn_w) * (HEAD_DIM ** -0.5)
    k = proj(pool_w + attn_w, attn_w)
    v = proj(pool_w + 2 * attn_w, attn_w)
    k_ref[...] = k
    v_ref[...] = v
    if t_refs:
        kt_ref, vt_ref = t_refs
        j = pl.program_id(0) % tiles_per_seq

        @pl.when(j >= tiles_per_seq - keep_tiles)
        def _():
            kt_ref[0] = k.T
            vt_ref[0] = v.T


def _proj_in(x, g, w_bf16, *, pool_w, attn_w, seq=None, keep=None):
    n, d = x.shape
    tm = TM_PROJ
    grid = (n // tm,)
    row = lambda i: (i, 0)
    out_shape = [jax.ShapeDtypeStruct((n, pool_w), F32)] + [jax.ShapeDtypeStruct((n, attn_w), F32)] * 3
    out_specs = [pl.BlockSpec((tm, pool_w), row)] + [pl.BlockSpec((tm, attn_w), row)] * 3
    tiles_per_seq = keep_tiles = 0
    if seq is not None:
        tiles_per_seq, keep_tiles = seq // tm, keep // tm
        first = tiles_per_seq - keep_tiles
        tmap = lambda i: (i // tiles_per_seq, 0, jnp.maximum(i % tiles_per_seq - first, 0))
        out_shape += [jax.ShapeDtypeStruct((n // seq, attn_w, keep), F32)] * 2
        out_specs += [pl.BlockSpec((1, attn_w, tm), tmap)] * 2
    body = functools.partial(_proj_in_body, pool_w=pool_w, attn_w=attn_w, tiles_per_seq=tiles_per_seq,
                             keep_tiles=keep_tiles)
    return pl.pallas_call(
        body, grid=grid,
        in_specs=[pl.BlockSpec((tm, d), row), pl.BlockSpec((1, d), lambda i: (0, 0)),
                  pl.BlockSpec(w_bf16.shape, lambda i: (0, 0))],
        out_specs=out_specs, out_shape=out_shape,
        compiler_params=_params("arbitrary"), name="proj_in",
    )(x, g, w_bf16)


def _attn_prompt_body(slopes_ref, q_ref, k_ref, v_ref, o_ref, bias_scr, o_scr, m_scr, l_scr, *, seq):
    hp = pl.program_id(1)
    blk = STEPS
    n_it = seq // blk
    lane = lax.broadcasted_iota(I32, (blk, LANES), 1)
    head_a = lane < HEAD_DIM
    qi = lax.broadcasted_iota(I32, (blk, 2 * blk), 0)
    ki = lax.broadcasted_iota(I32, (blk, 2 * blk), 1)
    step = qi + blk - ki
    band = (step >= 0) & (step <= STEPS)

    for bi, (_, d) in enumerate(BRANCHES):
        nb = n_it // d
        for hh in range(2):
            slope = slopes_ref[2 * hp + hh]
            bias_scr[hh] = jnp.where(band, -slope * (d * step).astype(F32), NEG_INF)

        def rows(start, d=d):
            return pl.ds(start, blk) if d == 1 else pl.ds(start, blk, stride=d)

        def block(it, carry, bi=bi, d=d, nb=nb, rows=rows):
            r = it // nb
            n = it % nb
            cur = r + n * (blk * d)
            prev = r + jnp.maximum(n - 1, 0) * (blk * d)
            qb = q_ref[0, rows(cur), :]
            kc = jnp.concatenate([k_ref[0, rows(prev), :], k_ref[0, rows(cur), :]], axis=0).astype(BF16)
            vc = jnp.concatenate([v_ref[0, rows(prev), :], v_ref[0, rows(cur), :]], axis=0).astype(BF16)
            no_prev = ki < jnp.where(n == 0, blk, 0)
            outs = []
            for hh in range(2):
                qm = jnp.where(head_a if hh == 0 else ~head_a, qb, 0.0).astype(BF16)
                s = lax.dot_general(qm, kc, (((1,), (1,)), ((), ())), preferred_element_type=F32)
                s = jnp.where(no_prev, NEG_INF, s + bias_scr[hh])
                m = jnp.max(s, axis=1, keepdims=True)
                p = jnp.exp(s - m)
                l = jnp.sum(p, axis=1, keepdims=True)
                o = jnp.dot(p.astype(BF16), vc, preferred_element_type=F32)
                outs.append((o, m, l))
            (oa, ma, la), (ob, mb, lb) = outs
            o_scr[bi, rows(cur), :] = jnp.where(head_a, oa, ob)
            m_scr[bi, rows(cur), :] = jnp.where(head_a, ma, mb)
            l_scr[bi, rows(cur), :] = jnp.where(head_a, la, lb)
            return carry

        lax.fori_loop(0, n_it, block, 0)

    ch = 2 * blk

    def merge(c, carry):
        rs = pl.ds(pl.multiple_of(c * ch, ch), ch)
        ms = [m_scr[b, rs, :] for b in range(len(BRANCHES))]
        mx = functools.reduce(jnp.maximum, ms)
        num = jnp.zeros((ch, LANES), F32)
        den = jnp.zeros((ch, LANES), F32)
        for b in range(len(BRANCHES)):
            a = jnp.exp(ms[b] - mx)
            num = num + a * o_scr[b, rs, :]
            den = den + a * l_scr[b, rs, :]
        o_ref[0, rs, :] = num / den
        return carry

    lax.fori_loop(0, seq // ch, merge, 0)


def _attn_prompt(slopes, q, k, v):
    b, s, hw = q.shape
    assert s % (STEPS * max(d for _, d in BRANCHES)) == 0, "sequence must be a multiple of the widest span"
    assert hw % LANES == 0 and LANES == 2 * HEAD_DIM
    spec = pl.BlockSpec((1, s, LANES), lambda i, j: (i, 0, j))
    nbr = len(BRANCHES)
    return pl.pallas_call(
        functools.partial(_attn_prompt_body, seq=s),
        grid=(b, hw // LANES),
        in_specs=[pl.BlockSpec(memory_space=pltpu.SMEM), spec, spec, spec],
        out_specs=spec,
        out_shape=jax.ShapeDtypeStruct((b, s, hw), F32),
        scratch_shapes=[pltpu.VMEM((2, STEPS, 2 * STEPS), F32)] + [pltpu.VMEM((nbr, s, LANES), F32)] * 3,
        compiler_params=_params("arbitrary", "arbitrary"), name="attn_prompt",
    )(slopes, q, k, v)


def _multiplicity(dist):
    mult = jnp.zeros(dist.shape, F32)
    for w, d in BRANCHES:
        assert d & (d - 1) == 0
        mult = mult + ((dist >= 0) & ((dist & (d - 1)) == 0) & (dist <= w)).astype(F32)
    return mult


def _attn_sample_body(slopes_ref, q_ref, kn_ref, vn_ref, kt_ref, vt_ref, o_ref, *, n_heads):
    t_new = q_ref.shape[1]
    past = kt_ref.shape[3]
    dist = past + lax.broadcasted_iota(I32, (t_new, past), 0) - lax.broadcasted_iota(I32, (t_new, past), 1)
    dist_n = lax.broadcasted_iota(I32, (t_new, t_new), 0) - lax.broadcasted_iota(I32, (t_new, t_new), 1)
    mult, mult_n = _multiplicity(dist), _multiplicity(dist_n)
    dist_f, dist_nf = dist.astype(F32), dist_n.astype(F32)
    nt = (((1,), (1,)), ((), ()))
    for h in range(n_heads):
        slope = slopes_ref[h]
        cols = slice(h * HEAD_DIM, (h + 1) * HEAD_DIM)
        qh = q_ref[0, :, cols].astype(BF16)
        kn = kn_ref[0, :, cols].astype(BF16)
        vn = vn_ref[0, :, cols].astype(BF16)
        s = jnp.dot(qh, kt_ref[0, h].astype(BF16), preferred_element_type=F32)
        s = jnp.where(mult > 0, s - slope * dist_f, NEG_INF)
        sn = lax.dot_general(qh, kn, nt, preferred_element_type=F32)
        sn = jnp.where(mult_n > 0, sn - slope * dist_nf, NEG_INF)
        m = jnp.maximum(jnp.max(s, axis=1, keepdims=True), jnp.max(sn, axis=1, keepdims=True))
        p = mult * jnp.exp(s - m)
        pn = mult_n * jnp.exp(sn - m)
        l = jnp.sum(p, axis=1, keepdims=True) + jnp.sum(pn, axis=1, keepdims=True)
        o = lax.dot_general(p.astype(BF16), vt_ref[0, h].astype(BF16), nt, preferred_element_type=F32)
        o = o + jnp.dot(pn.astype(BF16), vn, preferred_element_type=F32)
        o_ref[0, :, cols] = o / l


def _attn_sample(slopes, q, k_new, v_new, cache_kt, cache_vt):
    n, t, hw = q.shape
    _, h, e, past = cache_kt.shape
    assert past >= ATTN_REACH, "every strided key of every branch must exist in the window buffer"
    new = pl.BlockSpec((1, t, hw), lambda i: (i, 0, 0))
    old = pl.BlockSpec((1, h, e, past), lambda i: (i, 0, 0, 0))
    return pl.pallas_call(
        functools.partial(_attn_sample_body, n_heads=h),
        grid=(n,),
        in_specs=[pl.BlockSpec(memory_space=pltpu.SMEM), new, new, new, old, old],
        out_specs=new,
        out_shape=jax.ShapeDtypeStruct((n, t, hw), F32),
        compiler_params=_params("arbitrary"), name="attn_sample",
    )(slopes, q, k_new, v_new, cache_kt, cache_vt)


def _pool_windows(width):
    gw = width // len(POOL_WINDOWS)
    lane = lax.broadcasted_iota(I32, (1, width), 1)
    win = jnp.zeros((1, width), I32)
    for g, w in enumerate(POOL_WINDOWS):
        win = jnp.where((lane >= g * gw) & (lane < (g + 1) * gw), w, win)
    return win


def _pool_prompt_body(u_ref, w_ref, sc_ref, o_ref, ext_scr, *, tm):
    j = pl.program_id(1)
    width = u_ref.shape[2]
    pad = 2 * (POOL_STATE + 1)
    start = pl.multiple_of(j * tm, tm)
    lead = POOL_STATE + 1
    prev = u_ref[0, pl.ds(pl.multiple_of(jnp.maximum(start - lead, 0), lead), lead), :]
    ext_scr[0:pad - lead, :] = jnp.zeros((pad - lead, width), F32)
    ext_scr[pad - lead:pad, :] = jnp.where(j > 0, prev, 0.0)
    ext_scr[pad:, :] = u_ref[0, pl.ds(start, tm), :]
    win = _pool_windows(width)
    tok = ext_scr[pad:, :]
    acc = tok
    for i in range(1, max(POOL_WINDOWS)):
        acc = acc + jnp.where(i < win, ext_scr[pl.ds(pad - i, tm), :], 0.0)
    pos = start + lax.broadcasted_iota(I32, (tm, width), 0)
    cnt = jnp.minimum(win, pos + 1).astype(F32)
    diff = (acc / cnt - tok).astype(BF16)
    o_ref[0] = jnp.dot(diff, w_ref[...], preferred_element_type=F32) * sc_ref[...]


def _pool_prompt(u, w_bd, scale):
    b, s, w = u.shape
    tm = TM_PROJ
    return pl.pallas_call(
        functools.partial(_pool_prompt_body, tm=tm),
        grid=(b, s // tm),
        in_specs=[pl.BlockSpec((1, s, w), lambda i, j: (i, 0, 0)), pl.BlockSpec((w, w), lambda i, j: (0, 0)),
                  pl.BlockSpec((1, w), lambda i, j: (0, 0))],
        out_specs=pl.BlockSpec((1, tm, w), lambda i, j: (i, j, 0)),
        out_shape=jax.ShapeDtypeStruct((b, s, w), F32),
        scratch_shapes=[pltpu.VMEM((tm + 2 * (POOL_STATE + 1), w), F32)],
        compiler_params=_params("arbitrary", "arbitrary"), name="pool_prompt",
    )(u, w_bd, scale)


def _pool_sample_body(st_ref, u_ref, w_ref, sc_ref, o_ref):
    t_new, _, width = u_ref.shape
    n_state = st_ref.shape[0]
    win = _pool_windows(width)

    def row(k):
        return st_ref[k] if k < n_state else u_ref[k - n_state]

    for t in range(t_new):
        tok = u_ref[t]
        acc = tok
        for i in range(1, max(POOL_WINDOWS)):
            acc = acc + jnp.where(i < win, row(n_state + t - i), 0.0)
        diff = (acc / win.astype(F32) - tok).astype(BF16)
        o_ref[t] = jnp.dot(diff, w_ref[...], preferred_element_type=F32) * sc_ref[...]


def _pool_sample(state_t, u_t, w_bd, scale):
    assert state_t.shape[0] >= POOL_STATE
    return pl.pallas_call(
        _pool_sample_body,
        out_shape=jax.ShapeDtypeStruct(u_t.shape, F32),
        compiler_params=pltpu.CompilerParams(vmem_limit_bytes=VMEM_LIMIT), name="pool_sample",
    )(state_t, u_t, w_bd, scale)


ROUTE_F1, ROUTE_F2, ROUTE_G1, ROUTE_G2, ROUTE_R1, ROUTE_R2 = range(6)


def _split_bf16(x):
    hi = x.astype(BF16)
    return hi, (x - hi.astype(F32)).astype(BF16)


def _mix_route_body(x_ref, pool_ref, attn_ref, wo_ref, g_ref, wr_ref, br_ref, cnt_in_ref, h_ref, route_ref,
                    cnt_out_ref, carry_scr, *, n_groups, n_experts):
    i = pl.program_id(0)
    tm = x_ref.shape[0]
    pool_w = pool_ref.shape[1]

    @pl.when(i == 0)
    def _():
        carry_scr[...] = cnt_in_ref[...]

    h = x_ref[...]
    h = h + jnp.dot(pool_ref[...].astype(BF16), wo_ref[0:pool_w, :], preferred_element_type=F32)
    h = h + jnp.dot(attn_ref[...].astype(BF16), wo_ref[pool_w:, :], preferred_element_type=F32)
    h_ref[...] = h

    hn_hi, hn_lo = _split_bf16(_rms(h, g_ref[...]))
    w_hi, w_lo = _split_bf16(wr_ref[...])
    logits = (jnp.dot(hn_hi, w_hi, preferred_element_type=F32) + jnp.dot(hn_hi, w_lo, preferred_element_type=F32)
              + jnp.dot(hn_lo, w_hi, preferred_element_type=F32)) + br_ref[...]

    lane = lax.broadcasted_iota(I32, (tm, LANES), 1).astype(F32)

    def first_lane(mask):
        return jnp.min(jnp.where(mask, lane, float(LANES)), axis=1, keepdims=True)

    is_g = lane < n_groups
    lg = jnp.where(is_g, logits, NEG_INF)
    mg = jnp.max(lg, axis=1, keepdims=True)
    p_sel = 1.0 / jnp.sum(jnp.exp(lg - mg), axis=1, keepdims=True)
    g_top = first_lane(lg == mg)
    lo = n_groups + g_top * n_experts
    in_grp = (lane >= lo) & (lane < lo + n_experts)
    le = jnp.where(in_grp, logits, NEG_INF)
    ee = jnp.exp(le - jnp.max(le, axis=1, keepdims=True))
    pe = ee / jnp.sum(ee, axis=1, keepdims=True)
    v1 = jnp.max(jnp.where(in_grp, pe, -1.0), axis=1, keepdims=True)
    i1 = first_lane(in_grp & (pe == v1))
    rest = in_grp & (lane != i1)
    v2 = jnp.max(jnp.where(rest, pe, -1.0), axis=1, keepdims=True)
    i2 = first_lane(rest & (pe == v2))
    gate1 = p_sel * (v1 / (v1 + v2))
    gate2 = p_sel * (v2 / (v1 + v2))

    sel1, sel2 = lane == i1, lane == i2
    onehot = (sel1 | sel2).astype(BF16)
    tri = (lax.broadcasted_iota(I32, (tm, tm), 1) < lax.broadcasted_iota(I32, (tm, tm), 0)).astype(BF16)
    running = jnp.dot(tri, onehot, preferred_element_type=F32) + carry_scr[...]
    rank1 = jnp.sum(jnp.where(sel1, running, 0.0), axis=1, keepdims=True)
    rank2 = jnp.sum(jnp.where(sel2, running, 0.0), axis=1, keepdims=True)
    carry_scr[...] = carry_scr[...] + jnp.sum(onehot.astype(F32), axis=0, keepdims=True)
    cnt_out_ref[...] = carry_scr[...]

    rec = jnp.zeros((tm, LANES), F32)
    for idx, val in ((ROUTE_F1, i1 - n_groups), (ROUTE_F2, i2 - n_groups),
                     (ROUTE_G1, gate1), (ROUTE_G2, gate2), (ROUTE_R1, rank1), (ROUTE_R2, rank2)):
        rec = jnp.where(lane == idx, val, rec)
    route_ref[...] = rec


def _mix_route(x, pool, attn, w_out_bf16, g_ffn, w_route, b_route, cnt_in, *, n_groups, n_experts):
    n, d = x.shape
    tm = TM_MIX
    row = lambda i: (i, 0)
    fix = lambda i: (0, 0)
    return pl.pallas_call(
        functools.partial(_mix_route_body, n_groups=n_groups, n_experts=n_experts),
        grid=(n // tm,),
        in_specs=[pl.BlockSpec((tm, d), row), pl.BlockSpec((tm, pool.shape[1]), row),
                  pl.BlockSpec((tm, attn.shape[1]), row), pl.BlockSpec(w_out_bf16.shape, fix),
                  pl.BlockSpec((1, d), fix), pl.BlockSpec(w_route.shape, fix), pl.BlockSpec((1, LANES), fix),
                  pl.BlockSpec((1, LANES), fix)],
        out_specs=[pl.BlockSpec((tm, d), row), pl.BlockSpec((tm, LANES), row), pl.BlockSpec((1, LANES), fix)],
        out_shape=[jax.ShapeDtypeStruct((n, d), F32), jax.ShapeDtypeStruct((n, LANES), F32),
                   jax.ShapeDtypeStruct((1, LANES), F32)],
        scratch_shapes=[pltpu.VMEM((1, LANES), F32)],
        compiler_params=_params("arbitrary"), name="mix_route",
    )(x, pool, attn, w_out_bf16, g_ffn, w_route, b_route, cnt_in)


def _dispatch_body(zero_ref, dest_ref, hp_ref, hs_ref, g_ref, xs_ref, xn_scr, zero_scr, sems, *, tiles_p, n_tiles):
    tm = hp_ref.shape[0]
    i = pl.program_id(0)

    @pl.when(i == 0)
    def _():
        zero_scr[...] = jnp.zeros(zero_scr.shape, F32)
        for start in (True, False):
            def fill(t, c, start=start):
                @pl.when(zero_ref[t] > 0)
                def _():
                    cp = pltpu.make_async_copy(zero_scr, xs_ref.at[pl.ds(t * TM_GEMM, TM_GEMM)], sems.at[TOP_K_INNER])
                    cp.start() if start else cp.wait()
                return c
            lax.fori_loop(0, n_tiles, fill, 0)

    @pl.when(i < tiles_p)
    def _():
        xn_scr[...] = _rms(hp_ref[...], g_ref[...])

    @pl.when(i >= tiles_p)
    def _():
        xn_scr[...] = _rms(hs_ref[...], g_ref[...])

    def send(r, c):
        for slot in range(TOP_K_INNER):
            d = dest_ref[0, 0, slot * tm + r]
            pltpu.make_async_copy(xn_scr.at[pl.ds(r, 1)], xs_ref.at[pl.ds(d, 1)], sems.at[slot]).start()
        return c

    lax.fori_loop(0, tm, send, 0)
    for slot in range(TOP_K_INNER):
        pltpu.make_async_copy(xn_scr, xs_ref.at[pl.ds(0, tm)], sems.at[slot]).wait()


def _dispatch(tile_zero, dest, h_p, h_s, g_ffn):
    tm = TM_ROW
    d = h_p.shape[1]
    tiles_p, tiles_s = h_p.shape[0] // tm, h_s.shape[0] // tm
    n_tiles = tile_zero.shape[0]
    return pl.pallas_call(
        functools.partial(_dispatch_body, tiles_p=tiles_p, n_tiles=n_tiles),
        grid_spec=pltpu.PrefetchScalarGridSpec(
            num_scalar_prefetch=1, grid=(tiles_p + tiles_s,),
            in_specs=[pl.BlockSpec((1, 1, TOP_K_INNER * tm), lambda i, z: (i, 0, 0), memory_space=pltpu.SMEM),
                      pl.BlockSpec((tm, d), lambda i, z: (jnp.minimum(i, tiles_p - 1), 0)),
                      pl.BlockSpec((tm, d), lambda i, z: (jnp.maximum(i - tiles_p, 0), 0)),
                      pl.BlockSpec((1, d), lambda i, z: (0, 0))],
            out_specs=pl.BlockSpec(memory_space=pl.ANY),
            scratch_shapes=[pltpu.VMEM((tm, d), F32), pltpu.VMEM((TM_GEMM, d), F32),
                            pltpu.SemaphoreType.DMA((TOP_K_INNER + 1,))]),
        out_shape=jax.ShapeDtypeStruct((n_tiles * TM_GEMM, d), F32),
        compiler_params=_params("arbitrary"), name="dispatch",
    )(tile_zero, dest, h_p, h_s, g_ffn)


def _moe_gemm_body(expert_ref, block_ref, valid_ref, x_ref, wg_ref, wu_ref, wd_ref, y_ref):
    t = pl.program_id(0)

    @pl.when(valid_ref[t] > 0)
    def _():
        x = x_ref[...].astype(BF16)
        gate = jnp.dot(x, wg_ref[0].astype(BF16), preferred_element_type=F32)
        up = jnp.dot(x, wu_ref[0].astype(BF16), preferred_element_type=F32)
        mid = (gate * jax.nn.sigmoid(gate) * up).astype(BF16)
        y_ref[...] = jnp.dot(mid, wd_ref[0].astype(BF16), preferred_element_type=F32)

    @pl.when(valid_ref[t] == 0)
    def _():
        y_ref[...] = jnp.zeros(y_ref.shape, F32)


def _moe_gemm(tile_expert, tile_block, tile_valid, xs, w_gate, w_up, w_down):
    rows, d = xs.shape
    _, _, f = w_gate.shape
    tm = TM_GEMM
    xmap = lambda t, e, b, v: (b[t], 0)
    return pl.pallas_call(
        _moe_gemm_body,
        grid_spec=pltpu.PrefetchScalarGridSpec(
            num_scalar_prefetch=3, grid=(rows // tm,),
            in_specs=[pl.BlockSpec((tm, d), xmap),
                      pl.BlockSpec((1, d, f), lambda t, e, b, v: (e[t], 0, 0)),
                      pl.BlockSpec((1, d, f), lambda t, e, b, v: (e[t], 0, 0)),
                      pl.BlockSpec((1, f, d), lambda t, e, b, v: (e[t], 0, 0))],
            out_specs=pl.BlockSpec((tm, d), lambda t, e, b, v: (t, 0))),
        out_shape=jax.ShapeDtypeStruct((rows, d), F32),
        compiler_params=_params("arbitrary"), name="moe_gemm",
    )(tile_expert, tile_block, tile_valid, xs, w_gate, w_up, w_down)


def _combine_body(dest_ref, h_ref, route_ref, g_ref, ys_ref, y_ref, rows_scr, sems):
    tm = h_ref.shape[0]

    def fetch(r, c):
        for slot in range(TOP_K_INNER):
            d = dest_ref[0, 0, slot * tm + r]
            pltpu.make_async_copy(ys_ref.at[pl.ds(d, 1)], rows_scr.at[slot, pl.ds(r, 1)], sems.at[slot]).start()
        return c

    lax.fori_loop(0, tm, fetch, 0)
    for slot in range(TOP_K_INNER):
        pltpu.make_async_copy(ys_ref.at[pl.ds(0, tm)], rows_scr.at[slot], sems.at[slot]).wait()
    route = route_ref[...]
    out = h_ref[...] + (route[:, ROUTE_G1:ROUTE_G1 + 1] * rows_scr[0] + route[:, ROUTE_G2:ROUTE_G2 + 1] * rows_scr[1])
    y_ref[...] = _rms(out, g_ref[...])


def _combine(dest, h, route, g_final, ys):
    n, d = h.shape
    tm = TM_ROW
    row = lambda i: (i, 0)
    return pl.pallas_call(
        _combine_body,
        grid=(n // tm,),
        in_specs=[pl.BlockSpec((1, 1, TOP_K_INNER * tm), lambda i: (i, 0, 0), memory_space=pltpu.SMEM),
                  pl.BlockSpec((tm, d), row), pl.BlockSpec((tm, LANES), row), pl.BlockSpec((1, d), lambda i: (0, 0)),
                  pl.BlockSpec(memory_space=pl.ANY)],
        out_specs=pl.BlockSpec((tm, d), row),
        out_shape=jax.ShapeDtypeStruct((n, d), F32),
        scratch_shapes=[pltpu.VMEM((TOP_K_INNER, tm, d), F32), pltpu.SemaphoreType.DMA((TOP_K_INNER,))],
        compiler_params=_params("arbitrary"), name="combine",
    )(dest, h, route, g_final, ys)


def _sort_tables(counts, n_tiles):
    padded = ((counts + TM_GEMM - 1) // TM_GEMM) * TM_GEMM
    ends = jnp.cumsum(padded)
    offsets = ends - padded
    total = ends[-1]
    tile_start = jnp.arange(n_tiles, dtype=I32) * TM_GEMM
    tile_valid = (tile_start < total).astype(I32)
    last_block = jnp.maximum(total // TM_GEMM - 1, 0)
    tile_block = jnp.minimum(jnp.arange(n_tiles, dtype=I32), last_block)
    n_flat = counts.shape[0]
    tile_expert = jnp.minimum(jnp.sum((tile_block[:, None] * TM_GEMM >= ends[None, :]).astype(I32), axis=1), n_flat - 1)
    seg_last = jnp.any((tile_start[:, None] + TM_GEMM == ends[None, :]) & (padded[None, :] > 0), axis=1)
    tile_zero = (seg_last | (tile_valid == 0)).astype(I32)
    return offsets, tile_expert.astype(I32), tile_block.astype(I32), tile_valid, tile_zero


def _dest_blocks(route, offsets):
    n = route.shape[0]
    f = route[:, ROUTE_F1:ROUTE_F2 + 1].astype(I32)
    rank = route[:, ROUTE_R1:ROUTE_R2 + 1].astype(I32)
    dest = jnp.sum(jnp.where(f[..., None] == jnp.arange(offsets.shape[0], dtype=I32), offsets, 0), axis=-1) + rank
    return dest.reshape(n // TM_ROW, TM_ROW, TOP_K_INNER).transpose(0, 2, 1).reshape(n // TM_ROW, 1, TOP_K_INNER * TM_ROW)


def kernel(x_prompt, x_sample, cache_k, cache_v, state_pool, g_mix, w_in, w_pool, pool_scale, w_out, g_ffn,
           w_router_group, b_router_group, w_router_expert, b_router_expert, w_gate, w_up, w_down, g_final):
    depth = g_mix.shape[0]
    assert depth == 1, "single-layer step"
    b, s, d = x_prompt.shape
    nd, t_new, _ = x_sample.shape
    n_heads = cache_k.shape[3]
    attn_w = n_heads * HEAD_DIM
    pool_w = state_pool.shape[3]
    past = cache_k.shape[2]
    keep = min(ATTN_REACH, s)
    n_groups, n_experts = w_router_expert.shape[1], w_router_expert.shape[3]
    n_flat = n_groups * n_experts
    assert n_groups + n_flat <= LANES
    slopes = _alibi_slopes(n_heads)

    w_in_b = w_in[0].astype(BF16)
    w_out_b = w_out[0].astype(BF16)
    gw = pool_w // len(POOL_WINDOWS)
    w_bd = jnp.zeros((pool_w, pool_w), F32)
    for g in range(len(POOL_WINDOWS)):
        w_bd = w_bd.at[g * gw:(g + 1) * gw, g * gw:(g + 1) * gw].set(w_pool[0, g])
    w_bd = w_bd.astype(BF16)
    w_route = jnp.concatenate([w_router_group[0], jnp.transpose(w_router_expert[0], (1, 0, 2)).reshape(d, n_flat)], axis=1)
    w_route = jnp.pad(w_route, ((0, 0), (0, LANES - n_groups - n_flat)))
    b_route = jnp.pad(jnp.concatenate([b_router_group[0], b_router_expert[0].reshape(n_flat)]),
                      (0, LANES - n_groups - n_flat))[None]
    w_gate_f, w_up_f = w_gate[0].reshape(n_flat, d, -1), w_up[0].reshape(n_flat, d, -1)
    w_down_f = w_down[0].reshape(n_flat, -1, d)

    n_p = b * s
    u_p, q_p, k_p, v_p, kt_p, vt_p = _proj_in(x_prompt.reshape(n_p, d), g_mix, w_in_b, pool_w=pool_w, attn_w=attn_w,
                                               seq=s, keep=keep)
    attn_p = _attn_prompt(slopes, q_p.reshape(b, s, attn_w), k_p.reshape(b, s, attn_w), v_p.reshape(b, s, attn_w))
    u_p3 = u_p.reshape(b, s, pool_w)
    pool_p = _pool_prompt(u_p3, w_bd, pool_scale)

    n_s = nd * t_new
    u_s, q_s, k_s, v_s = _proj_in(x_sample.reshape(n_s, d), g_mix, w_in_b, pool_w=pool_w, attn_w=attn_w)
    cache_kt = jnp.transpose(cache_k[0], (0, 2, 3, 1))
    cache_vt = jnp.transpose(cache_v[0], (0, 2, 3, 1))
    as3 = lambda a: a.reshape(nd, t_new, attn_w)
    attn_s = _attn_sample(slopes, as3(q_s), as3(k_s), as3(v_s), cache_kt, cache_vt)
    state_t = jnp.transpose(state_pool[0], (1, 0, 2))
    u_st = jnp.transpose(u_s.reshape(nd, t_new, pool_w), (1, 0, 2))
    pool_st = _pool_sample(state_t, u_st, w_bd, pool_scale)

    tmajor = lambda a: jnp.transpose(a.reshape(nd, t_new, -1), (1, 0, 2)).reshape(n_s, -1)
    route_kw = dict(n_groups=n_groups, n_experts=n_experts)
    h_p, route_p, cnt_p = _mix_route(x_prompt.reshape(n_p, d), pool_p.reshape(n_p, pool_w), attn_p.reshape(n_p, attn_w),
                                     w_out_b, g_ffn, w_route, b_route, jnp.zeros((1, LANES), F32), **route_kw)
    h_s, route_s, cnt_all = _mix_route(tmajor(x_sample), pool_st.reshape(n_s, pool_w), tmajor(attn_s),
                                       w_out_b, g_ffn, w_route, b_route, cnt_p, **route_kw)

    counts = cnt_all[0, n_groups:n_groups + n_flat].astype(I32)
    n_tiles = (TOP_K_INNER * (n_p + n_s) + n_flat * (TM_GEMM - 1)) // TM_GEMM
    offsets, tile_expert, tile_block, tile_valid, tile_zero = _sort_tables(counts, n_tiles)
    dest_p, dest_s = _dest_blocks(route_p, offsets), _dest_blocks(route_s, offsets)

    xs = _dispatch(tile_zero, jnp.concatenate([dest_p, dest_s], axis=0), h_p, h_s, g_ffn)
    ys = _moe_gemm(tile_expert, tile_block, tile_valid, xs, w_gate_f, w_up_f, w_down_f)
    y_p = _combine(dest_p, h_p, route_p, g_final[None], ys)
    y_s = _combine(dest_s, h_s, route_s, g_final[None], ys)

    y_prompt = y_p.reshape(b, s, d)
    y_sample = jnp.transpose(y_s.reshape(t_new, nd, d), (1, 0, 2))
    k_prompt = jnp.transpose(kt_p.reshape(b, n_heads, HEAD_DIM, keep), (0, 3, 1, 2))[None]
    v_prompt = jnp.transpose(vt_p.reshape(b, n_heads, HEAD_DIM, keep), (0, 3, 1, 2))[None]
    pool_prompt = u_p3[:, s - POOL_STATE:][None]
    k_sample = k_s.reshape(1, nd, t_new, n_heads, HEAD_DIM)
    v_sample = v_s.reshape(1, nd, t_new, n_heads, HEAD_DIM)
    pool_sample = jnp.transpose(jnp.concatenate([state_t, u_st], axis=0)[-POOL_STATE:], (1, 0, 2))[None]
    return (y_prompt, y_sample, k_prompt, v_prompt, pool_prompt, k_sample, v_sample, pool_sample)
```

```python
import functools
import math

import numpy as np
import jax
import jax.numpy as jnp
from jax import lax
from jax.experimental import pallas as pl
from jax.experimental.pallas import tpu as pltpu

F32 = jnp.float32
BF16 = jnp.bfloat16
I32 = jnp.int32

HEAD_DIM = 64
POOL_WINDOWS = (2, 4, 8, 16)
POOL_STATE = max(POOL_WINDOWS) - 1
BRANCHES = ((128, 1), (512, 4), (2048, 16))
STEPS = BRANCHES[0][0] // BRANCHES[0][1]
ATTN_REACH = max(w for w, _ in BRANCHES)
MAX_DIL = max(d for _, d in BRANCHES)
TOP_K_INNER = 2
RMS_EPS = 1e-6
LANES = 128
NEG_INF = float("-inf")

VMEM_LIMIT = 56 * 1024 * 1024

TM_PROJ = 512
TM_MIX = 512
TM_ROW = 256
TM_GEMM = 256
ATTN_GROUP = 8


def _alibi_slopes(n_heads):
    def geometric(n):
        start = 2.0 ** (-8.0 / n)
        return [start ** (i + 1) for i in range(n)]
    c = 2 ** int(math.floor(math.log2(n_heads)))
    s = geometric(c)
    if c < n_heads:
        s = s + geometric(2 * c)[0::2][: n_heads - c]
    return jnp.asarray(s, dtype=F32)


def _rms(x, g):
    return x * lax.rsqrt(jnp.mean(x * x, axis=-1, keepdims=True) + RMS_EPS) * g


def _params(*sem):
    return pltpu.CompilerParams(dimension_semantics=sem, vmem_limit_bytes=VMEM_LIMIT)


def _proj_in_body(x_ref, g_ref, w_ref, u_ref, q_ref, k_ref, v_ref, *t_refs, pool_w, attn_w, tiles_per_seq,
                  keep_tiles):
    xn = _rms(x_ref[...], g_ref[...]).astype(BF16)

    def proj(lo, n):
        return jnp.dot(xn, w_ref[:, lo:lo + n], preferred_element_type=F32)

    u_ref[...] = proj(0, pool_w)
    q_ref[...] = proj(pool_w, attn_w) * (HEAD_DIM ** -0.5)
    k = proj(pool_w + attn_w, attn_w)
    v = proj(pool_w + 2 * attn_w, attn_w)
    k_ref[...] = k
    v_ref[...] = v
    if t_refs:
        kt_ref, vt_ref = t_refs
        j = pl.program_id(0) % tiles_per_seq

        @pl.when(j >= tiles_per_seq - keep_tiles)
        def _():
            kt_ref[0] = k.T
            vt_ref[0] = v.T


def _proj_in(x, g, w_bf16, *, pool_w, attn_w, seq=None, keep=None):
    n, d = x.shape
    tm = TM_PROJ
    grid = (n // tm,)
    row = lambda i: (i, 0)
    out_shape = [jax.ShapeDtypeStruct((n, pool_w), F32)] + [jax.ShapeDtypeStruct((n, attn_w), F32)] * 3
    out_specs = [pl.BlockSpec((tm, pool_w), row)] + [pl.BlockSpec((tm, attn_w), row)] * 3
    tiles_per_seq = keep_tiles = 0
    if seq is not None:
        tiles_per_seq, keep_tiles = seq // tm, keep // tm
        first = tiles_per_seq - keep_tiles
        tmap = lambda i: (i // tiles_per_seq, 0, jnp.maximum(i % tiles_per_seq - first, 0))
        out_shape += [jax.ShapeDtypeStruct((n // seq, attn_w, keep), F32)] * 2
        out_specs += [pl.BlockSpec((1, attn_w, tm), tmap)] * 2
    body = functools.partial(_proj_in_body, pool_w=pool_w, attn_w=attn_w, tiles_per_seq=tiles_per_seq,
                             keep_tiles=keep_tiles)
    return pl.pallas_call(
        body, grid=grid,
        in_specs=[pl.BlockSpec((tm, d), row), pl.BlockSpec((1, d), lambda i: (0, 0)),
                  pl.BlockSpec(w_bf16.shape, lambda i: (0, 0))],
        out_specs=out_specs, out_shape=out_shape,
        compiler_params=_params("arbitrary"), name="proj_in",
    )(x, g, w_bf16)


def _attn_prompt_body(slopes_ref, q_ref, k_ref, v_ref, o_ref, qp_scr, kp_scr, vp_scr, bias_scr, s_scr, p_scr, o_scr,
                      m_scr, l_scr, *, seq):
    hp = pl.program_id(1)
    blk = STEPS
    n_it = seq // blk
    cls = seq // MAX_DIL

    for c in range(MAX_DIL):
        for src, dst in ((q_ref, qp_scr), (k_ref, kp_scr), (v_ref, vp_scr)):
            dst[c * cls:(c + 1) * cls, :] = src[0, pl.ds(c, cls, stride=MAX_DIL), :]

    lane = lax.broadcasted_iota(I32, (blk, LANES), 1)
    head_a = lane < HEAD_DIM
    row = lax.broadcasted_iota(I32, (2 * blk, 2 * blk), 0)
    col = lax.broadcasted_iota(I32, (2 * blk, 2 * blk), 1)
    slope = jnp.where(row < blk, slopes_ref[2 * hp], slopes_ref[2 * hp + 1])

    for bi, (_, d) in enumerate(BRANCHES):
        nb = n_it // d
        n_chunk = MAX_DIL // d
        a_rows = blk // n_chunk
        sh = a_rows.bit_length() - 1
        assert a_rows == 1 << sh and a_rows % 8 == 0

        def seq_index(i, n_chunk=n_chunk, a_rows=a_rows, sh=sh):
            i = i & (blk - 1)
            return (i & (a_rows - 1)) * n_chunk + (i >> sh)

        step = seq_index(row) + blk - (seq_index(col) + (col & blk))
        bias = jnp.where((step >= 0) & (step <= STEPS), -slope * (d * step).astype(F32), NEG_INF)
        bias_scr[0] = bias
        bias_scr[1] = jnp.where(col < blk, NEG_INF, bias)

        def group(j, carry, bi=bi, d=d, nb=nb, n_chunk=n_chunk, a_rows=a_rows):
            def chunks(g):
                it = j * ATTN_GROUP + g
                r = it // nb
                n = it % nb
                cur = [pl.ds(pl.multiple_of((r + d * c) * cls + n * a_rows, 8), a_rows) for c in range(n_chunk)]
                prev = [pl.ds(pl.multiple_of((r + d * c) * cls + jnp.maximum(n - 1, 0) * a_rows, 8), a_rows)
                        for c in range(n_chunk)]
                return n, cur, prev

            def gather(ref, sls):
                return jnp.concatenate([ref[sl, :] for sl in sls], axis=0)

            for g in range(ATTN_GROUP):
                n, cur, prev = chunks(g)
                qb = gather(qp_scr, cur)
                q2 = jnp.concatenate([jnp.where(head_a, qb, 0.0), jnp.where(head_a, 0.0, qb)], axis=0).astype(BF16)
                kc = gather(kp_scr, prev + cur).astype(BF16)
                s = lax.dot_general(q2, kc, (((1,), (1,)), ((), ())), preferred_element_type=F32)
                s_scr[g] = s + bias_scr[jnp.where(n == 0, 1, 0)]
            for g in range(ATTN_GROUP):
                _, cur, _ = chunks(g)
                s = s_scr[g]
                m = jnp.max(s, axis=1, keepdims=True)
                p = jnp.exp(s - m)
                l = jnp.sum(p, axis=1, keepdims=True)
                p_scr[g] = p.astype(BF16)
                m2 = jnp.where(head_a, m[:blk], m[blk:])
                l2 = jnp.where(head_a, l[:blk], l[blk:])
                for c, sl in enumerate(cur):
                    m_scr[bi, sl, :] = m2[c * a_rows:(c + 1) * a_rows]
                    l_scr[bi, sl, :] = l2[c * a_rows:(c + 1) * a_rows]
            for g in range(ATTN_GROUP):
                _, cur, prev = chunks(g)
                vc = gather(vp_scr, prev + cur).astype(BF16)
                o = jnp.dot(p_scr[g], vc, preferred_element_type=F32)
                o2 = jnp.where(head_a, o[:blk], o[blk:])
                for c, sl in enumerate(cur):
                    o_scr[bi, sl, :] = o2[c * a_rows:(c + 1) * a_rows]
            return carry

        lax.fori_loop(0, n_it // ATTN_GROUP, group, 0)

    def merge(c, carry):
        rs = pl.ds(pl.multiple_of(c * cls, cls), cls)
        ms = [m_scr[b, rs, :] for b in range(len(BRANCHES))]
        mx = functools.reduce(jnp.maximum, ms)
        num = jnp.zeros((cls, LANES), F32)
        den = jnp.zeros((cls, LANES), F32)
        for b in range(len(BRANCHES)):
            a = jnp.exp(ms[b] - mx)
            num = num + a * o_scr[b, rs, :]
            den = den + a * l_scr[b, rs, :]
        o_ref[0, pl.ds(c, cls, stride=MAX_DIL), :] = num / den
        return carry

    lax.fori_loop(0, MAX_DIL, merge, 0)


def _attn_prompt(slopes, q, k, v):
    b, s, hw = q.shape
    assert s % (STEPS * max(d for _, d in BRANCHES)) == 0, "sequence must be a multiple of the widest span"
    assert hw % LANES == 0 and LANES == 2 * HEAD_DIM
    spec = pl.BlockSpec((1, s, LANES), lambda i, j: (i, 0, j))
    nbr = len(BRANCHES)
    return pl.pallas_call(
        functools.partial(_attn_prompt_body, seq=s),
        grid=(b, hw // LANES),
        in_specs=[pl.BlockSpec(memory_space=pltpu.SMEM), spec, spec, spec],
        out_specs=spec,
        out_shape=jax.ShapeDtypeStruct((b, s, hw), F32),
        scratch_shapes=[pltpu.VMEM((s, LANES), F32)] * 3
        + [pltpu.VMEM((2, 2 * STEPS, 2 * STEPS), F32),
                        pltpu.VMEM((ATTN_GROUP, 2 * STEPS, 2 * STEPS), F32),
                        pltpu.VMEM((ATTN_GROUP, 2 * STEPS, 2 * STEPS), BF16)]
        + [pltpu.VMEM((nbr, s, LANES), F32)] * 3,
        compiler_params=_params("arbitrary", "arbitrary"), name="attn_prompt",
    )(slopes, q, k, v)


def _multiplicity(dist):
    mult = jnp.zeros(dist.shape, F32)
    for w, d in BRANCHES:
        assert d & (d - 1) == 0
        mult = mult + ((dist >= 0) & ((dist & (d - 1)) == 0) & (dist <= w)).astype(F32)
    return mult


def _attn_sample_body(slopes_ref, q_ref, kn_ref, vn_ref, kt_ref, vt_ref, o_ref, *, n_heads):
    t_new = q_ref.shape[1]
    past = kt_ref.shape[3]
    dist = past + lax.broadcasted_iota(I32, (t_new, past), 0) - lax.broadcasted_iota(I32, (t_new, past), 1)
    dist_n = lax.broadcasted_iota(I32, (t_new, t_new), 0) - lax.broadcasted_iota(I32, (t_new, t_new), 1)
    mult, mult_n = _multiplicity(dist), _multiplicity(dist_n)
    dist_f, dist_nf = dist.astype(F32), dist_n.astype(F32)
    nt = (((1,), (1,)), ((), ()))
    for h in range(n_heads):
        slope = slopes_ref[h]
        cols = slice(h * HEAD_DIM, (h + 1) * HEAD_DIM)
        qh = q_ref[0, :, cols].astype(BF16)
        kn = kn_ref[0, :, cols].astype(BF16)
        vn = vn_ref[0, :, cols].astype(BF16)
        s = jnp.dot(qh, kt_ref[0, h].astype(BF16), preferred_element_type=F32)
        s = jnp.where(mult > 0, s - slope * dist_f, NEG_INF)
        sn = lax.dot_general(qh, kn, nt, preferred_element_type=F32)
        sn = jnp.where(mult_n > 0, sn - slope * dist_nf, NEG_INF)
        m = jnp.maximum(jnp.max(s, axis=1, keepdims=True), jnp.max(sn, axis=1, keepdims=True))
        p = mult * jnp.exp(s - m)
        pn = mult_n * jnp.exp(sn - m)
        l = jnp.sum(p, axis=1, keepdims=True) + jnp.sum(pn, axis=1, keepdims=True)
        o = lax.dot_general(p.astype(BF16), vt_ref[0, h].astype(BF16), nt, preferred_element_type=F32)
        o = o + jnp.dot(pn.astype(BF16), vn, preferred_element_type=F32)
        o_ref[0, :, cols] = o / l


def _attn_sample(slopes, q, k_new, v_new, cache_kt, cache_vt):
    n, t, hw = q.shape
    _, h, e, past = cache_kt.shape
    assert past >= ATTN_REACH, "every strided key of every branch must exist in the window buffer"
    new = pl.BlockSpec((1, t, hw), lambda i: (i, 0, 0))
    old = pl.BlockSpec((1, h, e, past), lambda i: (i, 0, 0, 0))
    return pl.pallas_call(
        functools.partial(_attn_sample_body, n_heads=h),
        grid=(n,),
        in_specs=[pl.BlockSpec(memory_space=pltpu.SMEM), new, new, new, old, old],
        out_specs=new,
        out_shape=jax.ShapeDtypeStruct((n, t, hw), F32),
        compiler_params=_params("arbitrary"), name="attn_sample",
    )(slopes, q, k_new, v_new, cache_kt, cache_vt)


def _pool_windows(width):
    gw = width // len(POOL_WINDOWS)
    lane = lax.broadcasted_iota(I32, (1, width), 1)
    win = jnp.zeros((1, width), I32)
    for g, w in enumerate(POOL_WINDOWS):
        win = jnp.where((lane >= g * gw) & (lane < (g + 1) * gw), w, win)
    return win


def _pool_prompt_body(u_ref, w_ref, sc_ref, o_ref, ext_scr, *, tm):
    j = pl.program_id(1)
    width = u_ref.shape[2]
    pad = 2 * (POOL_STATE + 1)
    start = pl.multiple_of(j * tm, tm)
    lead = POOL_STATE + 1
    prev = u_ref[0, pl.ds(pl.multiple_of(jnp.maximum(start - lead, 0), lead), lead), :]
    ext_scr[0:pad - lead, :] = jnp.zeros((pad - lead, width), F32)
    ext_scr[pad - lead:pad, :] = jnp.where(j > 0, prev, 0.0)
    ext_scr[pad:, :] = u_ref[0, pl.ds(start, tm), :]
    win = _pool_windows(width)
    tok = ext_scr[pad:, :]
    acc = tok
    for i in range(1, max(POOL_WINDOWS)):
        acc = acc + jnp.where(i < win, ext_scr[pl.ds(pad - i, tm), :], 0.0)
    pos = start + lax.broadcasted_iota(I32, (tm, width), 0)
    cnt = jnp.minimum(win, pos + 1).astype(F32)
    diff = (acc / cnt - tok).astype(BF16)
    o_ref[0] = jnp.dot(diff, w_ref[...], preferred_element_type=F32) * sc_ref[...]


def _pool_prompt(u, w_bd, scale):
    b, s, w = u.shape
    tm = TM_PROJ
    return pl.pallas_call(
        functools.partial(_pool_prompt_body, tm=tm),
        grid=(b, s // tm),
        in_specs=[pl.BlockSpec((1, s, w), lambda i, j: (i, 0, 0)), pl.BlockSpec((w, w), lambda i, j: (0, 0)),
                  pl.BlockSpec((1, w), lambda i, j: (0, 0))],
        out_specs=pl.BlockSpec((1, tm, w), lambda i, j: (i, j, 0)),
        out_shape=jax.ShapeDtypeStruct((b, s, w), F32),
        scratch_shapes=[pltpu.VMEM((tm + 2 * (POOL_STATE + 1), w), F32)],
        compiler_params=_params("arbitrary", "arbitrary"), name="pool_prompt",
    )(u, w_bd, scale)


def _pool_sample_body(st_ref, u_ref, w_ref, sc_ref, o_ref):
    t_new, _, width = u_ref.shape
    n_state = st_ref.shape[0]
    win = _pool_windows(width)

    def row(k):
        return st_ref[k] if k < n_state else u_ref[k - n_state]

    for t in range(t_new):
        tok = u_ref[t]
        acc = tok
        for i in range(1, max(POOL_WINDOWS)):
            acc = acc + jnp.where(i < win, row(n_state + t - i), 0.0)
        diff = (acc / win.astype(F32) - tok).astype(BF16)
        o_ref[t] = jnp.dot(diff, w_ref[...], preferred_element_type=F32) * sc_ref[...]


def _pool_sample(state_t, u_t, w_bd, scale):
    assert state_t.shape[0] >= POOL_STATE
    return pl.pallas_call(
        _pool_sample_body,
        out_shape=jax.ShapeDtypeStruct(u_t.shape, F32),
        compiler_params=pltpu.CompilerParams(vmem_limit_bytes=VMEM_LIMIT), name="pool_sample",
    )(state_t, u_t, w_bd, scale)


ROUTE_F1, ROUTE_F2, ROUTE_G1, ROUTE_G2, ROUTE_R1, ROUTE_R2 = range(6)


def _split_bf16(x):
    hi = x.astype(BF16)
    return hi, (x - hi.astype(F32)).astype(BF16)


def _mix_route_body(x_ref, pool_ref, attn_ref, wo_ref, g_ref, wr_ref, br_ref, cnt_in_ref, h_ref, route_ref,
                    cnt_out_ref, carry_scr, *, n_groups, n_experts):
    i = pl.program_id(0)
    tm = x_ref.shape[0]
    pool_w = pool_ref.shape[1]

    @pl.when(i == 0)
    def _():
        carry_scr[...] = cnt_in_ref[...]

    h = x_ref[...]
    h = h + jnp.dot(pool_ref[...].astype(BF16), wo_ref[0:pool_w, :], preferred_element_type=F32)
    h = h + jnp.dot(attn_ref[...].astype(BF16), wo_ref[pool_w:, :], preferred_element_type=F32)
    h_ref[...] = h

    hn_hi, hn_lo = _split_bf16(_rms(h, g_ref[...]))
    w_hi, w_lo = _split_bf16(wr_ref[...])
    logits = (jnp.dot(hn_hi, w_hi, preferred_element_type=F32) + jnp.dot(hn_hi, w_lo, preferred_element_type=F32)
              + jnp.dot(hn_lo, w_hi, preferred_element_type=F32)) + br_ref[...]

    lane = lax.broadcasted_iota(I32, (tm, LANES), 1).astype(F32)

    def first_lane(mask):
        return jnp.min(jnp.where(mask, lane, float(LANES)), axis=1, keepdims=True)

    is_g = lane < n_groups
    lg = jnp.where(is_g, logits, NEG_INF)
    mg = jnp.max(lg, axis=1, keepdims=True)
    p_sel = 1.0 / jnp.sum(jnp.exp(lg - mg), axis=1, keepdims=True)
    g_top = first_lane(lg == mg)
    lo = n_groups + g_top * n_experts
    in_grp = (lane >= lo) & (lane < lo + n_experts)
    le = jnp.where(in_grp, logits, NEG_INF)
    ee = jnp.exp(le - jnp.max(le, axis=1, keepdims=True))
    pe = ee / jnp.sum(ee, axis=1, keepdims=True)
    v1 = jnp.max(jnp.where(in_grp, pe, -1.0), axis=1, keepdims=True)
    i1 = first_lane(in_grp & (pe == v1))
    rest = in_grp & (lane != i1)
    v2 = jnp.max(jnp.where(rest, pe, -1.0), axis=1, keepdims=True)
    i2 = first_lane(rest & (pe == v2))
    gate1 = p_sel * (v1 / (v1 + v2))
    gate2 = p_sel * (v2 / (v1 + v2))

    sel1, sel2 = lane == i1, lane == i2
    onehot = (sel1 | sel2).astype(BF16)
    tri = (lax.broadcasted_iota(I32, (tm, tm), 1) < lax.broadcasted_iota(I32, (tm, tm), 0)).astype(BF16)
    running = jnp.dot(tri, onehot, preferred_element_type=F32) + carry_scr[...]
    rank1 = jnp.sum(jnp.where(sel1, running, 0.0), axis=1, keepdims=True)
    rank2 = jnp.sum(jnp.where(sel2, running, 0.0), axis=1, keepdims=True)
    carry_scr[...] = carry_scr[...] + jnp.sum(onehot.astype(F32), axis=0, keepdims=True)
    cnt_out_ref[...] = carry_scr[...]

    rec = jnp.zeros((tm, LANES), F32)
    for idx, val in ((ROUTE_F1, i1 - n_groups), (ROUTE_F2, i2 - n_groups),
                     (ROUTE_G1, gate1), (ROUTE_G2, gate2), (ROUTE_R1, rank1), (ROUTE_R2, rank2)):
        rec = jnp.where(lane == idx, val, rec)
    route_ref[...] = rec


def _mix_route(x, pool, attn, w_out_bf16, g_ffn, w_route, b_route, cnt_in, *, n_groups, n_experts):
    n, d = x.shape
    tm = TM_MIX
    row = lambda i: (i, 0)
    fix = lambda i: (0, 0)
    return pl.pallas_call(
        functools.partial(_mix_route_body, n_groups=n_groups, n_experts=n_experts),
        grid=(n // tm,),
        in_specs=[pl.BlockSpec((tm, d), row), pl.BlockSpec((tm, pool.shape[1]), row),
                  pl.BlockSpec((tm, attn.shape[1]), row), pl.BlockSpec(w_out_bf16.shape, fix),
                  pl.BlockSpec((1, d), fix), pl.BlockSpec(w_route.shape, fix), pl.BlockSpec((1, LANES), fix),
                  pl.BlockSpec((1, LANES), fix)],
        out_specs=[pl.BlockSpec((tm, d), row), pl.BlockSpec((tm, LANES), row), pl.BlockSpec((1, LANES), fix)],
        out_shape=[jax.ShapeDtypeStruct((n, d), F32), jax.ShapeDtypeStruct((n, LANES), F32),
                   jax.ShapeDtypeStruct((1, LANES), F32)],
        scratch_shapes=[pltpu.VMEM((1, LANES), F32)],
        compiler_params=_params("arbitrary"), name="mix_route",
    )(x, pool, attn, w_out_bf16, g_ffn, w_route, b_route, cnt_in)


def _dispatch_body(zero_ref, dest_ref, hp_ref, hs_ref, g_ref, xs_ref, xn_scr, zero_scr, sems, *, tiles_p, n_tiles):
    tm = hp_ref.shape[0]
    i = pl.program_id(0)

    @pl.when(i == 0)
    def _():
        zero_scr[...] = jnp.zeros(zero_scr.shape, F32)
        for start in (True, False):
            def fill(t, c, start=start):
                @pl.when(zero_ref[t] > 0)
                def _():
                    cp = pltpu.make_async_copy(zero_scr, xs_ref.at[pl.ds(t * TM_GEMM, TM_GEMM)], sems.at[TOP_K_INNER])
                    cp.start() if start else cp.wait()
                return c
            lax.fori_loop(0, n_tiles, fill, 0)

    @pl.when(i < tiles_p)
    def _():
        xn_scr[...] = _rms(hp_ref[...], g_ref[...])

    @pl.when(i >= tiles_p)
    def _():
        xn_scr[...] = _rms(hs_ref[...], g_ref[...])

    def send(r, c):
        for slot in range(TOP_K_INNER):
            d = dest_ref[0, 0, slot * tm + r]
            pltpu.make_async_copy(xn_scr.at[pl.ds(r, 1)], xs_ref.at[pl.ds(d, 1)], sems.at[slot]).start()
        return c

    lax.fori_loop(0, tm, send, 0)
    for slot in range(TOP_K_INNER):
        pltpu.make_async_copy(xn_scr, xs_ref.at[pl.ds(0, tm)], sems.at[slot]).wait()


def _dispatch(tile_zero, dest, h_p, h_s, g_ffn):
    tm = TM_ROW
    d = h_p.shape[1]
    tiles_p, tiles_s = h_p.shape[0] // tm, h_s.shape[0] // tm
    n_tiles = tile_zero.shape[0]
    return pl.pallas_call(
        functools.partial(_dispatch_body, tiles_p=tiles_p, n_tiles=n_tiles),
        grid_spec=pltpu.PrefetchScalarGridSpec(
            num_scalar_prefetch=1, grid=(tiles_p + tiles_s,),
            in_specs=[pl.BlockSpec((1, 1, TOP_K_INNER * tm), lambda i, z: (i, 0, 0), memory_space=pltpu.SMEM),
                      pl.BlockSpec((tm, d), lambda i, z: (jnp.minimum(i, tiles_p - 1), 0)),
                      pl.BlockSpec((tm, d), lambda i, z: (jnp.maximum(i - tiles_p, 0), 0)),
                      pl.BlockSpec((1, d), lambda i, z: (0, 0))],
            out_specs=pl.BlockSpec(memory_space=pl.ANY),
            scratch_shapes=[pltpu.VMEM((tm, d), F32), pltpu.VMEM((TM_GEMM, d), F32),
                            pltpu.SemaphoreType.DMA((TOP_K_INNER + 1,))]),
        out_shape=jax.ShapeDtypeStruct((n_tiles * TM_GEMM, d), F32),
        compiler_params=_params("arbitrary"), name="dispatch",
    )(tile_zero, dest, h_p, h_s, g_ffn)


def _moe_gemm_body(expert_ref, block_ref, valid_ref, x_ref, wg_ref, wu_ref, wd_ref, y_ref):
    t = pl.program_id(0)

    @pl.when(valid_ref[t] > 0)
    def _():
        x = x_ref[...].astype(BF16)
        gate = jnp.dot(x, wg_ref[0].astype(BF16), preferred_element_type=F32)
        up = jnp.dot(x, wu_ref[0].astype(BF16), preferred_element_type=F32)
        mid = (gate * jax.nn.sigmoid(gate) * up).astype(BF16)
        y_ref[...] = jnp.dot(mid, wd_ref[0].astype(BF16), preferred_element_type=F32)

    @pl.when(valid_ref[t] == 0)
    def _():
        y_ref[...] = jnp.zeros(y_ref.shape, F32)


def _moe_gemm(tile_expert, tile_block, tile_valid, xs, w_gate, w_up, w_down):
    rows, d = xs.shape
    _, _, f = w_gate.shape
    tm = TM_GEMM
    xmap = lambda t, e, b, v: (b[t], 0)
    return pl.pallas_call(
        _moe_gemm_body,
        grid_spec=pltpu.PrefetchScalarGridSpec(
            num_scalar_prefetch=3, grid=(rows // tm,),
            in_specs=[pl.BlockSpec((tm, d), xmap),
                      pl.BlockSpec((1, d, f), lambda t, e, b, v: (e[t], 0, 0)),
                      pl.BlockSpec((1, d, f), lambda t, e, b, v: (e[t], 0, 0)),
                      pl.BlockSpec((1, f, d), lambda t, e, b, v: (e[t], 0, 0))],
            out_specs=pl.BlockSpec((tm, d), lambda t, e, b, v: (t, 0))),
        out_shape=jax.ShapeDtypeStruct((rows, d), F32),
        compiler_params=_params("arbitrary"), name="moe_gemm",
    )(tile_expert, tile_block, tile_valid, xs, w_gate, w_up, w_down)


def _combine_body(dest_ref, h_ref, route_ref, g_ref, ys_ref, y_ref, rows_scr, sems):
    tm = h_ref.shape[0]

    def fetch(r, c):
        for slot in range(TOP_K_INNER):
            d = dest_ref[0, 0, slot * tm + r]
            pltpu.make_async_copy(ys_ref.at[pl.ds(d, 1)], rows_scr.at[slot, pl.ds(r, 1)], sems.at[slot]).start()
        return c

    lax.fori_loop(0, tm, fetch, 0)
    for slot in range(TOP_K_INNER):
        pltpu.make_async_copy(ys_ref.at[pl.ds(0, tm)], rows_scr.at[slot], sems.at[slot]).wait()
    route = route_ref[...]
    out = h_ref[...] + (route[:, ROUTE_G1:ROUTE_G1 + 1] * rows_scr[0] + route[:, ROUTE_G2:ROUTE_G2 + 1] * rows_scr[1])
    y_ref[...] = _rms(out, g_ref[...])


def _combine(dest, h, route, g_final, ys):
    n, d = h.shape
    tm = TM_ROW
    row = lambda i: (i, 0)
    return pl.pallas_call(
        _combine_body,
        grid=(n // tm,),
        in_specs=[pl.BlockSpec((1, 1, TOP_K_INNER * tm), lambda i: (i, 0, 0), memory_space=pltpu.SMEM),
                  pl.BlockSpec((tm, d), row), pl.BlockSpec((tm, LANES), row), pl.BlockSpec((1, d), lambda i: (0, 0)),
                  pl.BlockSpec(memory_space=pl.ANY)],
        out_specs=pl.BlockSpec((tm, d), row),
        out_shape=jax.ShapeDtypeStruct((n, d), F32),
        scratch_shapes=[pltpu.VMEM((TOP_K_INNER, tm, d), F32), pltpu.SemaphoreType.DMA((TOP_K_INNER,))],
        compiler_params=_params("arbitrary"), name="combine",
    )(dest, h, route, g_final, ys)


def _sort_tables(counts, n_tiles):
    padded = ((counts + TM_GEMM - 1) // TM_GEMM) * TM_GEMM
    ends = jnp.cumsum(padded)
    offsets = ends - padded
    total = ends[-1]
    tile_start = jnp.arange(n_tiles, dtype=I32) * TM_GEMM
    tile_valid = (tile_start < total).astype(I32)
    last_block = jnp.maximum(total // TM_GEMM - 1, 0)
    tile_block = jnp.minimum(jnp.arange(n_tiles, dtype=I32), last_block)
    n_flat = counts.shape[0]
    tile_expert = jnp.minimum(jnp.sum((tile_block[:, None] * TM_GEMM >= ends[None, :]).astype(I32), axis=1), n_flat - 1)
    seg_last = jnp.any((tile_start[:, None] + TM_GEMM == ends[None, :]) & (padded[None, :] > 0), axis=1)
    tile_zero = (seg_last | (tile_valid == 0)).astype(I32)
    return offsets, tile_expert.astype(I32), tile_block.astype(I32), tile_valid, tile_zero


def _dest_blocks(route, offsets):
    n = route.shape[0]
    f = route[:, ROUTE_F1:ROUTE_F2 + 1].astype(I32)
    rank = route[:, ROUTE_R1:ROUTE_R2 + 1].astype(I32)
    dest = jnp.sum(jnp.where(f[..., None] == jnp.arange(offsets.shape[0], dtype=I32), offsets, 0), axis=-1) + rank
    return dest.reshape(n // TM_ROW, TM_ROW, TOP_K_INNER).transpose(0, 2, 1).reshape(n // TM_ROW, 1, TOP_K_INNER * TM_ROW)


def kernel(x_prompt, x_sample, cache_k, cache_v, state_pool, g_mix, w_in, w_pool, pool_scale, w_out, g_ffn,
           w_router_group, b_router_group, w_router_expert, b_router_expert, w_gate, w_up, w_down, g_final):
    depth = g_mix.shape[0]
    assert depth == 1, "single-layer step"
    b, s, d = x_prompt.shape
    nd, t_new, _ = x_sample.shape
    n_heads = cache_k.shape[3]
    attn_w = n_heads * HEAD_DIM
    pool_w = state_pool.shape[3]
    past = cache_k.shape[2]
    keep = min(ATTN_REACH, s)
    n_groups, n_experts = w_router_expert.shape[1], w_router_expert.shape[3]
    n_flat = n_groups * n_experts
    assert n_groups + n_flat <= LANES
    slopes = _alibi_slopes(n_heads)

    w_in_b = w_in[0].astype(BF16)
    w_out_b = w_out[0].astype(BF16)
    gw = pool_w // len(POOL_WINDOWS)
    w_bd = jnp.zeros((pool_w, pool_w), F32)
    for g in range(len(POOL_WINDOWS)):
        w_bd = w_bd.at[g * gw:(g + 1) * gw, g * gw:(g + 1) * gw].set(w_pool[0, g])
    w_bd = w_bd.astype(BF16)
    w_route = jnp.concatenate([w_router_group[0], jnp.transpose(w_router_expert[0], (1, 0, 2)).reshape(d, n_flat)], axis=1)
    w_route = jnp.pad(w_route, ((0, 0), (0, LANES - n_groups - n_flat)))
    b_route = jnp.pad(jnp.concatenate([b_router_group[0], b_router_expert[0].reshape(n_flat)]),
                      (0, LANES - n_groups - n_flat))[None]
    w_gate_f, w_up_f = w_gate[0].reshape(n_flat, d, -1), w_up[0].reshape(n_flat, d, -1)
    w_down_f = w_down[0].reshape(n_flat, -1, d)

    n_p = b * s
    u_p, q_p, k_p, v_p, kt_p, vt_p = _proj_in(x_prompt.reshape(n_p, d), g_mix, w_in_b, pool_w=pool_w, attn_w=attn_w,
                                               seq=s, keep=keep)
    attn_p = _attn_prompt(slopes, q_p.reshape(b, s, attn_w), k_p.reshape(b, s, attn_w), v_p.reshape(b, s, attn_w))
    u_p3 = u_p.reshape(b, s, pool_w)
    pool_p = _pool_prompt(u_p3, w_bd, pool_scale)

    n_s = nd * t_new
    u_s, q_s, k_s, v_s = _proj_in(x_sample.reshape(n_s, d), g_mix, w_in_b, pool_w=pool_w, attn_w=attn_w)
    cache_kt = jnp.transpose(cache_k[0], (0, 2, 3, 1))
    cache_vt = jnp.transpose(cache_v[0], (0, 2, 3, 1))
    as3 = lambda a: a.reshape(nd, t_new, attn_w)
    attn_s = _attn_sample(slopes, as3(q_s), as3(k_s), as3(v_s), cache_kt, cache_vt)
    state_t = jnp.transpose(state_pool[0], (1, 0, 2))
    u_st = jnp.transpose(u_s.reshape(nd, t_new, pool_w), (1, 0, 2))
    pool_st = _pool_sample(state_t, u_st, w_bd, pool_scale)

    tmajor = lambda a: jnp.transpose(a.reshape(nd, t_new, -1), (1, 0, 2)).reshape(n_s, -1)
    route_kw = dict(n_groups=n_groups, n_experts=n_experts)
    h_p, route_p, cnt_p = _mix_route(x_prompt.reshape(n_p, d), pool_p.reshape(n_p, pool_w), attn_p.reshape(n_p, attn_w),
                                     w_out_b, g_ffn, w_route, b_route, jnp.zeros((1, LANES), F32), **route_kw)
    h_s, route_s, cnt_all = _mix_route(tmajor(x_sample), pool_st.reshape(n_s, pool_w), tmajor(attn_s),
                                       w_out_b, g_ffn, w_route, b_route, cnt_p, **route_kw)

    counts = cnt_all[0, n_groups:n_groups + n_flat].astype(I32)
    n_tiles = (TOP_K_INNER * (n_p + n_s) + n_flat * (TM_GEMM - 1)) // TM_GEMM
    offsets, tile_expert, tile_block, tile_valid, tile_zero = _sort_tables(counts, n_tiles)
    dest_p, dest_s = _dest_blocks(route_p, offsets), _dest_blocks(route_s, offsets)

    xs = _dispatch(tile_zero, jnp.concatenate([dest_p, dest_s], axis=0), h_p, h_s, g_ffn)
    ys = _moe_gemm(tile_expert, tile_block, tile_valid, xs, w_gate_f, w_up_f, w_down_f)
    y_p = _combine(dest_p, h_p, route_p, g_final[None], ys)
    y_s = _combine(dest_s, h_s, route_s, g_final[None], ys)

    y_prompt = y_p.reshape(b, s, d)
    y_sample = jnp.transpose(y_s.reshape(t_new, nd, d), (1, 0, 2))
    k_prompt = jnp.transpose(kt_p.reshape(b, n_heads, HEAD_DIM, keep), (0, 3, 1, 2))[None]
    v_prompt = jnp.transpose(vt_p.reshape(b, n_heads, HEAD_DIM, keep), (0, 3, 1, 2))[None]
    pool_prompt = u_p3[:, s - POOL_STATE:][None]
    k_sample = k_s.reshape(1, nd, t_new, n_heads, HEAD_DIM)
    v_sample = v_s.reshape(1, nd, t_new, n_heads, HEAD_DIM)
    pool_sample = jnp.transpose(jnp.concatenate([state_t, u_st], axis=0)[-POOL_STATE:], (1, 0, 2))[None]
    return (y_prompt, y_sample, k_prompt, v_prompt, pool_prompt, k_sample, v_sample, pool_sample)
```

```python
import functools
import math

import numpy as np
import jax
import jax.numpy as jnp
from jax import lax
from jax.experimental import pallas as pl
from jax.experimental.pallas import tpu as pltpu

F32 = jnp.float32
BF16 = jnp.bfloat16
I32 = jnp.int32

HEAD_DIM = 64
POOL_WINDOWS = (2, 4, 8, 16)
POOL_STATE = max(POOL_WINDOWS) - 1
BRANCHES = ((128, 1), (512, 4), (2048, 16))
STEPS = BRANCHES[0][0] // BRANCHES[0][1]
ATTN_REACH = max(w for w, _ in BRANCHES)
MAX_DIL = max(d for _, d in BRANCHES)
TOP_K_INNER = 2
RMS_EPS = 1e-6
LANES = 128
NEG_INF = float("-inf")

VMEM_LIMIT = 56 * 1024 * 1024

TM_PROJ = 512
TM_MIX = 512
TM_ROW = 256
TM_GEMM = 256
ATTN_GROUP = 8


def _alibi_slopes(n_heads):
    def geometric(n):
        start = 2.0 ** (-8.0 / n)
        return [start ** (i + 1) for i in range(n)]
    c = 2 ** int(math.floor(math.log2(n_heads)))
    s = geometric(c)
    if c < n_heads:
        s = s + geometric(2 * c)[0::2][: n_heads - c]
    return jnp.asarray(s, dtype=F32)


def _rms(x, g):
    return x * lax.rsqrt(jnp.mean(x * x, axis=-1, keepdims=True) + RMS_EPS) * g


def _params(*sem):
    return pltpu.CompilerParams(dimension_semantics=sem, vmem_limit_bytes=VMEM_LIMIT)


def _proj_in_body(x_ref, g_ref, w_ref, u_ref, q_ref, k_ref, v_ref, *t_refs, pool_w, attn_w, tiles_per_seq,
                  keep_tiles):
    xn = _rms(x_ref[...], g_ref[...]).astype(BF16)

    def proj(lo, n):
        return jnp.dot(xn, w_ref[:, lo:lo + n], preferred_element_type=F32)

    u_ref[...] = proj(0, pool_w)
    q_ref[...] = proj(pool_w, attn_w) * (HEAD_DIM ** -0.5)
    k = proj(pool_w + attn_w, attn_w)
    v = proj(pool_w + 2 * attn_w, attn_w)
    k_ref[...] = k
    v_ref[...] = v
    if t_refs:
        kt_ref, vt_ref = t_refs
        j = pl.program_id(0) % tiles_per_seq

        @pl.when(j >= tiles_per_seq - keep_tiles)
        def _():
            kt_ref[0] = k.T
            vt_ref[0] = v.T


def _proj_in(x, g, w_bf16, *, pool_w, attn_w, seq=None, keep=None):
    n, d = x.shape
    tm = TM_PROJ
    grid = (n // tm,)
    row = lambda i: (i, 0)
    out_shape = [jax.ShapeDtypeStruct((n, pool_w), F32)] + [jax.ShapeDtypeStruct((n, attn_w), F32)] * 3
    out_specs = [pl.BlockSpec((tm, pool_w), row)] + [pl.BlockSpec((tm, attn_w), row)] * 3
    tiles_per_seq = keep_tiles = 0
    if seq is not None:
        tiles_per_seq, keep_tiles = seq // tm, keep // tm
        first = tiles_per_seq - keep_tiles
        tmap = lambda i: (i // tiles_per_seq, 0, jnp.maximum(i % tiles_per_seq - first, 0))
        out_shape += [jax.ShapeDtypeStruct((n // seq, attn_w, keep), F32)] * 2
        out_specs += [pl.BlockSpec((1, attn_w, tm), tmap)] * 2
    body = functools.partial(_proj_in_body, pool_w=pool_w, attn_w=attn_w, tiles_per_seq=tiles_per_seq,
                             keep_tiles=keep_tiles)
    return pl.pallas_call(
        body, grid=grid,
        in_specs=[pl.BlockSpec((tm, d), row), pl.BlockSpec((1, d), lambda i: (0, 0)),
                  pl.BlockSpec(w_bf16.shape, lambda i: (0, 0))],
        out_specs=out_specs, out_shape=out_shape,
        compiler_params=_params("arbitrary"), name="proj_in",
    )(x, g, w_bf16)


def _attn_prompt_body(slopes_ref, q_ref, k_ref, v_ref, o_ref, qp_scr, kp_scr, vp_scr, bias_scr, s_scr, p_scr, o_scr,
                      m_scr, l_scr, *, seq):
    hp = pl.program_id(1)
    blk = STEPS
    n_it = seq // blk
    cls = seq // MAX_DIL

    for c in range(MAX_DIL):
        for src, dst in ((q_ref, qp_scr), (k_ref, kp_scr), (v_ref, vp_scr)):
            dst[c * cls:(c + 1) * cls, :] = src[0, pl.ds(c, cls, stride=MAX_DIL), :]

    lane = lax.broadcasted_iota(I32, (blk, LANES), 1)
    head_a = lane < HEAD_DIM
    row = lax.broadcasted_iota(I32, (2 * blk, 2 * blk), 0)
    col = lax.broadcasted_iota(I32, (2 * blk, 2 * blk), 1)
    slope = jnp.where(row < blk, slopes_ref[2 * hp], slopes_ref[2 * hp + 1])

    for bi, (_, d) in enumerate(BRANCHES):
        nb = n_it // d
        n_chunk = MAX_DIL // d
        a_rows = blk // n_chunk
        sh = a_rows.bit_length() - 1
        assert a_rows == 1 << sh and a_rows % 8 == 0

        def seq_index(i, n_chunk=n_chunk, a_rows=a_rows, sh=sh):
            i = i & (blk - 1)
            return (i & (a_rows - 1)) * n_chunk + (i >> sh)

        step = seq_index(row) + blk - (seq_index(col) + (col & blk))
        bias = jnp.where((step >= 0) & (step <= STEPS), -slope * (d * step).astype(F32), NEG_INF)
        bias_scr[0] = bias
        bias_scr[1] = jnp.where(col < blk, NEG_INF, bias)

        def group(j, carry, bi=bi, d=d, nb=nb, n_chunk=n_chunk, a_rows=a_rows):
            def chunks(g):
                it = j * ATTN_GROUP + g
                r = it // nb
                n = it % nb
                cur = [pl.ds(pl.multiple_of((r + d * c) * cls + n * a_rows, 8), a_rows) for c in range(n_chunk)]
                prev = [pl.ds(pl.multiple_of((r + d * c) * cls + jnp.maximum(n - 1, 0) * a_rows, 8), a_rows)
                        for c in range(n_chunk)]
                return n, cur, prev

            def gather(ref, sls):
                return jnp.concatenate([ref[sl, :] for sl in sls], axis=0)

            for g in range(ATTN_GROUP):
                n, cur, prev = chunks(g)
                qb = gather(qp_scr, cur)
                q2 = jnp.concatenate([jnp.where(head_a, qb, 0.0), jnp.where(head_a, 0.0, qb)], axis=0).astype(BF16)
                kc = gather(kp_scr, prev + cur).astype(BF16)
                s = lax.dot_general(q2, kc, (((1,), (1,)), ((), ())), preferred_element_type=F32)
                s_scr[g] = s + bias_scr[jnp.where(n == 0, 1, 0)]
            for g in range(ATTN_GROUP):
                _, cur, _ = chunks(g)
                s = s_scr[g]
                m = jnp.max(s, axis=1, keepdims=True)
                p = jnp.exp(s - m)
                l = jnp.sum(p, axis=1, keepdims=True)
                p_scr[g] = p.astype(BF16)
                m2 = jnp.where(head_a, m[:blk], m[blk:])
                l2 = jnp.where(head_a, l[:blk], l[blk:])
                for c, sl in enumerate(cur):
                    m_scr[bi, sl, :] = m2[c * a_rows:(c + 1) * a_rows]
                    l_scr[bi, sl, :] = l2[c * a_rows:(c + 1) * a_rows]
            for g in range(ATTN_GROUP):
                _, cur, prev = chunks(g)
                vc = gather(vp_scr, prev + cur).astype(BF16)
                o = jnp.dot(p_scr[g], vc, preferred_element_type=F32)
                o2 = jnp.where(head_a, o[:blk], o[blk:])
                for c, sl in enumerate(cur):
                    o_scr[bi, sl, :] = o2[c * a_rows:(c + 1) * a_rows]
            return carry

        lax.fori_loop(0, n_it // ATTN_GROUP, group, 0)

    def merge(c, carry):
        rs = pl.ds(pl.multiple_of(c * cls, cls), cls)
        ms = [m_scr[b, rs, :] for b in range(len(BRANCHES))]
        mx = functools.reduce(jnp.maximum, ms)
        num = jnp.zeros((cls, LANES), F32)
        den = jnp.zeros((cls, LANES), F32)
        for b in range(len(BRANCHES)):
            a = jnp.exp(ms[b] - mx)
            num = num + a * o_scr[b, rs, :]
            den = den + a * l_scr[b, rs, :]
        o_ref[0, pl.ds(c, cls, stride=MAX_DIL), :] = num / den
        return carry

    lax.fori_loop(0, MAX_DIL, merge, 0)


def _attn_prompt(slopes, q, k, v):
    b, s, hw = q.shape
    assert s % (STEPS * max(d for _, d in BRANCHES)) == 0, "sequence must be a multiple of the widest span"
    assert hw % LANES == 0 and LANES == 2 * HEAD_DIM
    spec = pl.BlockSpec((1, s, LANES), lambda i, j: (i, 0, j))
    nbr = len(BRANCHES)
    return pl.pallas_call(
        functools.partial(_attn_prompt_body, seq=s),
        grid=(b, hw // LANES),
        in_specs=[pl.BlockSpec(memory_space=pltpu.SMEM), spec, spec, spec],
        out_specs=spec,
        out_shape=jax.ShapeDtypeStruct((b, s, hw), F32),
        scratch_shapes=[pltpu.VMEM((s, LANES), F32)] * 3
        + [pltpu.VMEM((2, 2 * STEPS, 2 * STEPS), F32),
                        pltpu.VMEM((ATTN_GROUP, 2 * STEPS, 2 * STEPS), F32),
                        pltpu.VMEM((ATTN_GROUP, 2 * STEPS, 2 * STEPS), BF16)]
        + [pltpu.VMEM((nbr, s, LANES), F32)] * 3,
        compiler_params=_params("arbitrary", "arbitrary"), name="attn_prompt",
    )(slopes, q, k, v)


def _multiplicity(dist):
    mult = jnp.zeros(dist.shape, F32)
    for w, d in BRANCHES:
        assert d & (d - 1) == 0
        mult = mult + ((dist >= 0) & ((dist & (d - 1)) == 0) & (dist <= w)).astype(F32)
    return mult


def _attn_sample_body(slopes_ref, q_ref, kn_ref, vn_ref, kt_ref, vt_ref, o_ref, *, n_heads):
    t_new = q_ref.shape[1]
    past = kt_ref.shape[3]
    dist = past + lax.broadcasted_iota(I32, (t_new, past), 0) - lax.broadcasted_iota(I32, (t_new, past), 1)
    dist_n = lax.broadcasted_iota(I32, (t_new, t_new), 0) - lax.broadcasted_iota(I32, (t_new, t_new), 1)
    mult, mult_n = _multiplicity(dist), _multiplicity(dist_n)
    dist_f, dist_nf = dist.astype(F32), dist_n.astype(F32)
    nt = (((1,), (1,)), ((), ()))
    for h in range(n_heads):
        slope = slopes_ref[h]
        cols = slice(h * HEAD_DIM, (h + 1) * HEAD_DIM)
        qh = q_ref[0, :, cols].astype(BF16)
        kn = kn_ref[0, :, cols].astype(BF16)
        vn = vn_ref[0, :, cols].astype(BF16)
        s = jnp.dot(qh, kt_ref[0, h].astype(BF16), preferred_element_type=F32)
        s = jnp.where(mult > 0, s - slope * dist_f, NEG_INF)
        sn = lax.dot_general(qh, kn, nt, preferred_element_type=F32)
        sn = jnp.where(mult_n > 0, sn - slope * dist_nf, NEG_INF)
        m = jnp.maximum(jnp.max(s, axis=1, keepdims=True), jnp.max(sn, axis=1, keepdims=True))
        p = mult * jnp.exp(s - m)
        pn = mult_n * jnp.exp(sn - m)
        l = jnp.sum(p, axis=1, keepdims=True) + jnp.sum(pn, axis=1, keepdims=True)
        o = lax.dot_general(p.astype(BF16), vt_ref[0, h].astype(BF16), nt, preferred_element_type=F32)
        o = o + jnp.dot(pn.astype(BF16), vn, preferred_element_type=F32)
        o_ref[0, :, cols] = o / l


def _attn_sample(slopes, q, k_new, v_new, cache_kt, cache_vt):
    n, t, hw = q.shape
    _, h, e, past = cache_kt.shape
    assert past >= ATTN_REACH, "every strided key of every branch must exist in the window buffer"
    new = pl.BlockSpec((1, t, hw), lambda i: (i, 0, 0))
    old = pl.BlockSpec((1, h, e, past), lambda i: (i, 0, 0, 0))
    return pl.pallas_call(
        functools.partial(_attn_sample_body, n_heads=h),
        grid=(n,),
        in_specs=[pl.BlockSpec(memory_space=pltpu.SMEM), new, new, new, old, old],
        out_specs=new,
        out_shape=jax.ShapeDtypeStruct((n, t, hw), F32),
        compiler_params=_params("arbitrary"), name="attn_sample",
    )(slopes, q, k_new, v_new, cache_kt, cache_vt)


def _pool_windows(width):
    gw = width // len(POOL_WINDOWS)
    lane = lax.broadcasted_iota(I32, (1, width), 1)
    win = jnp.zeros((1, width), I32)
    for g, w in enumerate(POOL_WINDOWS):
        win = jnp.where((lane >= g * gw) & (lane < (g + 1) * gw), w, win)
    return win


def _pool_prompt_body(u_ref, w_ref, sc_ref, o_ref, ext_scr, *, tm):
    j = pl.program_id(1)
    width = u_ref.shape[2]
    pad = 2 * (POOL_STATE + 1)
    start = pl.multiple_of(j * tm, tm)
    lead = POOL_STATE + 1
    prev = u_ref[0, pl.ds(pl.multiple_of(jnp.maximum(start - lead, 0), lead), lead), :]
    ext_scr[0:pad - lead, :] = jnp.zeros((pad - lead, width), F32)
    ext_scr[pad - lead:pad, :] = jnp.where(j > 0, prev, 0.0)
    ext_scr[pad:, :] = u_ref[0, pl.ds(start, tm), :]
    win = _pool_windows(width)
    tok = ext_scr[pad:, :]
    acc = tok
    for i in range(1, max(POOL_WINDOWS)):
        acc = acc + jnp.where(i < win, ext_scr[pl.ds(pad - i, tm), :], 0.0)
    pos = start + lax.broadcasted_iota(I32, (tm, width), 0)
    cnt = jnp.minimum(win, pos + 1).astype(F32)
    diff = (acc / cnt - tok).astype(BF16)
    o_ref[0] = jnp.dot(diff, w_ref[...], preferred_element_type=F32) * sc_ref[...]


def _pool_prompt(u, w_bd, scale):
    b, s, w = u.shape
    tm = TM_PROJ
    return pl.pallas_call(
        functools.partial(_pool_prompt_body, tm=tm),
        grid=(b, s // tm),
        in_specs=[pl.BlockSpec((1, s, w), lambda i, j: (i, 0, 0)), pl.BlockSpec((w, w), lambda i, j: (0, 0)),
                  pl.BlockSpec((1, w), lambda i, j: (0, 0))],
        out_specs=pl.BlockSpec((1, tm, w), lambda i, j: (i, j, 0)),
        out_shape=jax.ShapeDtypeStruct((b, s, w), F32),
        scratch_shapes=[pltpu.VMEM((tm + 2 * (POOL_STATE + 1), w), F32)],
        compiler_params=_params("arbitrary", "arbitrary"), name="pool_prompt",
    )(u, w_bd, scale)


def _pool_sample_body(st_ref, u_ref, w_ref, sc_ref, o_ref):
    t_new, _, width = u_ref.shape
    n_state = st_ref.shape[0]
    win = _pool_windows(width)

    def row(k):
        return st_ref[k] if k < n_state else u_ref[k - n_state]

    for t in range(t_new):
        tok = u_ref[t]
        acc = tok
        for i in range(1, max(POOL_WINDOWS)):
            acc = acc + jnp.where(i < win, row(n_state + t - i), 0.0)
        diff = (acc / win.astype(F32) - tok).astype(BF16)
        o_ref[t] = jnp.dot(diff, w_ref[...], preferred_element_type=F32) * sc_ref[...]


def _pool_sample(state_t, u_t, w_bd, scale):
    assert state_t.shape[0] >= POOL_STATE
    return pl.pallas_call(
        _pool_sample_body,
        out_shape=jax.ShapeDtypeStruct(u_t.shape, F32),
        compiler_params=pltpu.CompilerParams(vmem_limit_bytes=VMEM_LIMIT), name="pool_sample",
    )(state_t, u_t, w_bd, scale)


ROUTE_F1, ROUTE_F2, ROUTE_G1, ROUTE_G2, ROUTE_R1, ROUTE_R2 = range(6)


def _split_bf16(x):
    hi = x.astype(BF16)
    return hi, (x - hi.astype(F32)).astype(BF16)


def _mix_route_body(x_ref, pool_ref, attn_ref, wo_ref, g_ref, wr_ref, br_ref, cnt_in_ref, h_ref, route_ref,
                    cnt_out_ref, carry_scr, *, n_groups, n_experts):
    i = pl.program_id(0)
    tm = x_ref.shape[0]
    pool_w = pool_ref.shape[1]

    @pl.when(i == 0)
    def _():
        carry_scr[...] = cnt_in_ref[...]

    h = x_ref[...]
    h = h + jnp.dot(pool_ref[...].astype(BF16), wo_ref[0:pool_w, :], preferred_element_type=F32)
    h = h + jnp.dot(attn_ref[...].astype(BF16), wo_ref[pool_w:, :], preferred_element_type=F32)
    h_ref[...] = h

    hn_hi, hn_lo = _split_bf16(_rms(h, g_ref[...]))
    w_hi, w_lo = _split_bf16(wr_ref[...])
    logits = (jnp.dot(hn_hi, w_hi, preferred_element_type=F32) + jnp.dot(hn_hi, w_lo, preferred_element_type=F32)
              + jnp.dot(hn_lo, w_hi, preferred_element_type=F32)) + br_ref[...]

    lane = lax.broadcasted_iota(I32, (tm, LANES), 1).astype(F32)

    def first_lane(mask):
        return jnp.min(jnp.where(mask, lane, float(LANES)), axis=1, keepdims=True)

    is_g = lane < n_groups
    lg = jnp.where(is_g, logits, NEG_INF)
    mg = jnp.max(lg, axis=1, keepdims=True)
    p_sel = 1.0 / jnp.sum(jnp.exp(lg - mg), axis=1, keepdims=True)
    g_top = first_lane(lg == mg)
    lo = n_groups + g_top * n_experts
    in_grp = (lane >= lo) & (lane < lo + n_experts)
    le = jnp.where(in_grp, logits, NEG_INF)
    ee = jnp.exp(le - jnp.max(le, axis=1, keepdims=True))
    pe = ee / jnp.sum(ee, axis=1, keepdims=True)
    v1 = jnp.max(jnp.where(in_grp, pe, -1.0), axis=1, keepdims=True)
    i1 = first_lane(in_grp & (pe == v1))
    rest = in_grp & (lane != i1)
    v2 = jnp.max(jnp.where(rest, pe, -1.0), axis=1, keepdims=True)
    i2 = first_lane(rest & (pe == v2))
    gate1 = p_sel * (v1 / (v1 + v2))
    gate2 = p_sel * (v2 / (v1 + v2))

    sel1, sel2 = lane == i1, lane == i2
    onehot = (sel1 | sel2).astype(BF16)
    tri = (lax.broadcasted_iota(I32, (tm, tm), 1) < lax.broadcasted_iota(I32, (tm, tm), 0)).astype(BF16)
    running = jnp.dot(tri, onehot, preferred_element_type=F32) + carry_scr[...]
    rank1 = jnp.sum(jnp.where(sel1, running, 0.0), axis=1, keepdims=True)
    rank2 = jnp.sum(jnp.where(sel2, running, 0.0), axis=1, keepdims=True)
    carry_scr[...] = carry_scr[...] + jnp.sum(onehot.astype(F32), axis=0, keepdims=True)
    cnt_out_ref[...] = carry_scr[...]

    rec = jnp.zeros((tm, LANES), F32)
    for idx, val in ((ROUTE_F1, i1 - n_groups), (ROUTE_F2, i2 - n_groups),
                     (ROUTE_G1, gate1), (ROUTE_G2, gate2), (ROUTE_R1, rank1), (ROUTE_R2, rank2)):
        rec = jnp.where(lane == idx, val, rec)
    route_ref[...] = rec


def _mix_route(x, pool, attn, w_out_bf16, g_ffn, w_route, b_route, cnt_in, *, n_groups, n_experts):
    n, d = x.shape
    tm = TM_MIX
    row = lambda i: (i, 0)
    fix = lambda i: (0, 0)
    return pl.pallas_call(
        functools.partial(_mix_route_body, n_groups=n_groups, n_experts=n_experts),
        grid=(n // tm,),
        in_specs=[pl.BlockSpec((tm, d), row), pl.BlockSpec((tm, pool.shape[1]), row),
                  pl.BlockSpec((tm, attn.shape[1]), row), pl.BlockSpec(w_out_bf16.shape, fix),
                  pl.BlockSpec((1, d), fix), pl.BlockSpec(w_route.shape, fix), pl.BlockSpec((1, LANES), fix),
                  pl.BlockSpec((1, LANES), fix)],
        out_specs=[pl.BlockSpec((tm, d), row), pl.BlockSpec((tm, LANES), row), pl.BlockSpec((1, LANES), fix)],
        out_shape=[jax.ShapeDtypeStruct((n, d), F32), jax.ShapeDtypeStruct((n, LANES), F32),
                   jax.ShapeDtypeStruct((1, LANES), F32)],
        scratch_shapes=[pltpu.VMEM((1, LANES), F32)],
        compiler_params=_params("arbitrary"), name="mix_route",
    )(x, pool, attn, w_out_bf16, g_ffn, w_route, b_route, cnt_in)


def _dispatch_body(zero_ref, dest_ref, hp_ref, hs_ref, g_ref, xs_ref, xn_scr, zero_scr, sems, *, tiles_p, n_steps,
                   n_tiles):
    tm = hp_ref.shape[0]
    i = pl.program_id(0)
    zero_sem = 2 * TOP_K_INNER

    @pl.when(i == 0)
    def _():
        zero_scr[...] = jnp.zeros(zero_scr.shape, F32)
        for start in (True, False):
            def fill(t, c, start=start):
                @pl.when(zero_ref[t] > 0)
                def _():
                    cp = pltpu.make_async_copy(zero_scr, xs_ref.at[pl.ds(t * TM_GEMM, TM_GEMM)], sems.at[zero_sem])
                    cp.start() if start else cp.wait()
                return c
            lax.fori_loop(0, n_tiles, fill, 0)

    buf = i % 2

    def wait_rows(b):
        for slot in range(TOP_K_INNER):
            pltpu.make_async_copy(xn_scr.at[b], xs_ref.at[pl.ds(0, tm)], sems.at[b * TOP_K_INNER + slot]).wait()

    @pl.when(i >= 2)
    def _():
        wait_rows(buf)

    @pl.when(i < tiles_p)
    def _():
        xn_scr[buf] = _rms(hp_ref[...], g_ref[...])

    @pl.when(i >= tiles_p)
    def _():
        xn_scr[buf] = _rms(hs_ref[...], g_ref[...])

    def send(r, c):
        for slot in range(TOP_K_INNER):
            d = dest_ref[0, 0, slot * tm + r]
            pltpu.make_async_copy(xn_scr.at[buf, pl.ds(r, 1)], xs_ref.at[pl.ds(d, 1)],
                                  sems.at[buf * TOP_K_INNER + slot]).start(priority=slot)
        return c

    lax.fori_loop(0, tm, send, 0, unroll=4)

    @pl.when(i == n_steps - 1)
    def _():
        if n_steps > 1:
            wait_rows(1 - buf)
        wait_rows(buf)


def _dispatch(tile_zero, dest, h_p, h_s, g_ffn):
    tm = TM_ROW
    d = h_p.shape[1]
    tiles_p, tiles_s = h_p.shape[0] // tm, h_s.shape[0] // tm
    n_tiles = tile_zero.shape[0]
    return pl.pallas_call(
        functools.partial(_dispatch_body, tiles_p=tiles_p, n_steps=tiles_p + tiles_s, n_tiles=n_tiles),
        grid_spec=pltpu.PrefetchScalarGridSpec(
            num_scalar_prefetch=1, grid=(tiles_p + tiles_s,),
            in_specs=[pl.BlockSpec((1, 1, TOP_K_INNER * tm), lambda i, z: (i, 0, 0), memory_space=pltpu.SMEM),
                      pl.BlockSpec((tm, d), lambda i, z: (jnp.minimum(i, tiles_p - 1), 0)),
                      pl.BlockSpec((tm, d), lambda i, z: (jnp.maximum(i - tiles_p, 0), 0)),
                      pl.BlockSpec((1, d), lambda i, z: (0, 0))],
            out_specs=pl.BlockSpec(memory_space=pl.ANY),
            scratch_shapes=[pltpu.VMEM((2, tm, d), F32), pltpu.VMEM((TM_GEMM, d), F32),
                            pltpu.SemaphoreType.DMA((2 * TOP_K_INNER + 1,))]),
        out_shape=jax.ShapeDtypeStruct((n_tiles * TM_GEMM, d), F32),
        compiler_params=_params("arbitrary"), name="dispatch",
    )(tile_zero, dest, h_p, h_s, g_ffn)


def _moe_gemm_body(expert_ref, block_ref, valid_ref, x_ref, wg_ref, wu_ref, wd_ref, y_ref):
    t = pl.program_id(0)

    @pl.when(valid_ref[t] > 0)
    def _():
        x = x_ref[...].astype(BF16)
        gate = jnp.dot(x, wg_ref[0].astype(BF16), preferred_element_type=F32)
        up = jnp.dot(x, wu_ref[0].astype(BF16), preferred_element_type=F32)
        mid = (gate * jax.nn.sigmoid(gate) * up).astype(BF16)
        y_ref[...] = jnp.dot(mid, wd_ref[0].astype(BF16), preferred_element_type=F32)

    @pl.when(valid_ref[t] == 0)
    def _():
        y_ref[...] = jnp.zeros(y_ref.shape, F32)


def _moe_gemm(tile_expert, tile_block, tile_valid, xs, w_gate, w_up, w_down):
    rows, d = xs.shape
    _, _, f = w_gate.shape
    tm = TM_GEMM
    xmap = lambda t, e, b, v: (b[t], 0)
    return pl.pallas_call(
        _moe_gemm_body,
        grid_spec=pltpu.PrefetchScalarGridSpec(
            num_scalar_prefetch=3, grid=(rows // tm,),
            in_specs=[pl.BlockSpec((tm, d), xmap),
                      pl.BlockSpec((1, d, f), lambda t, e, b, v: (e[t], 0, 0)),
                      pl.BlockSpec((1, d, f), lambda t, e, b, v: (e[t], 0, 0)),
                      pl.BlockSpec((1, f, d), lambda t, e, b, v: (e[t], 0, 0))],
            out_specs=pl.BlockSpec((tm, d), lambda t, e, b, v: (t, 0))),
        out_shape=jax.ShapeDtypeStruct((rows, d), F32),
        compiler_params=_params("arbitrary"), name="moe_gemm",
    )(tile_expert, tile_block, tile_valid, xs, w_gate, w_up, w_down)


def _combine_body(dest_ref, next_ref, h_ref, route_ref, g_ref, ys_ref, y_ref, rows_scr, sems, *, n_steps):
    tm = h_ref.shape[0]
    i = pl.program_id(0)
    buf = i % 2

    def fetch(idx_ref, b):
        def body(r, c):
            for slot in range(TOP_K_INNER):
                d = idx_ref[0, 0, slot * tm + r]
                pltpu.make_async_copy(ys_ref.at[pl.ds(d, 1)], rows_scr.at[b, slot, pl.ds(r, 1)],
                                      sems.at[b * TOP_K_INNER + slot]).start(priority=slot)
            return c
        lax.fori_loop(0, tm, body, 0, unroll=4)

    @pl.when(i == 0)
    def _():
        fetch(dest_ref, 0)

    @pl.when(i + 1 < n_steps)
    def _():
        fetch(next_ref, 1 - buf)

    for slot in range(TOP_K_INNER):
        pltpu.make_async_copy(ys_ref.at[pl.ds(0, tm)], rows_scr.at[buf, slot], sems.at[buf * TOP_K_INNER + slot]).wait()
    route = route_ref[...]
    out = h_ref[...] + (route[:, ROUTE_G1:ROUTE_G1 + 1] * rows_scr[buf, 0]
                        + route[:, ROUTE_G2:ROUTE_G2 + 1] * rows_scr[buf, 1])
    y_ref[...] = _rms(out, g_ref[...])


def _combine(dest, h, route, g_final, ys):
    n, d = h.shape
    tm = TM_ROW
    n_steps = n // tm
    row = lambda i: (i, 0)
    idx_block = (1, 1, TOP_K_INNER * tm)
    return pl.pallas_call(
        functools.partial(_combine_body, n_steps=n_steps),
        grid=(n_steps,),
        in_specs=[pl.BlockSpec(idx_block, lambda i: (i, 0, 0), memory_space=pltpu.SMEM),
                  pl.BlockSpec(idx_block, lambda i: (jnp.minimum(i + 1, n_steps - 1), 0, 0), memory_space=pltpu.SMEM),
                  pl.BlockSpec((tm, d), row), pl.BlockSpec((tm, LANES), row), pl.BlockSpec((1, d), lambda i: (0, 0)),
                  pl.BlockSpec(memory_space=pl.ANY)],
        out_specs=pl.BlockSpec((tm, d), row),
        out_shape=jax.ShapeDtypeStruct((n, d), F32),
        scratch_shapes=[pltpu.VMEM((2, TOP_K_INNER, tm, d), F32), pltpu.SemaphoreType.DMA((2 * TOP_K_INNER,))],
        compiler_params=_params("arbitrary"), name="combine",
    )(dest, dest, h, route, g_final, ys)


def _sort_tables(counts, n_tiles):
    padded = ((counts + TM_GEMM - 1) // TM_GEMM) * TM_GEMM
    ends = jnp.cumsum(padded)
    offsets = ends - padded
    total = ends[-1]
    tile_start = jnp.arange(n_tiles, dtype=I32) * TM_GEMM
    tile_valid = (tile_start < total).astype(I32)
    last_block = jnp.maximum(total // TM_GEMM - 1, 0)
    tile_block = jnp.minimum(jnp.arange(n_tiles, dtype=I32), last_block)
    n_flat = counts.shape[0]
    tile_expert = jnp.minimum(jnp.sum((tile_block[:, None] * TM_GEMM >= ends[None, :]).astype(I32), axis=1), n_flat - 1)
    seg_last = jnp.any((tile_start[:, None] + TM_GEMM == ends[None, :]) & (padded[None, :] > 0), axis=1)
    tile_zero = (seg_last | (tile_valid == 0)).astype(I32)
    return offsets, tile_expert.astype(I32), tile_block.astype(I32), tile_valid, tile_zero


def _dest_blocks(route, offsets):
    n = route.shape[0]
    f = route[:, ROUTE_F1:ROUTE_F2 + 1].astype(I32)
    rank = route[:, ROUTE_R1:ROUTE_R2 + 1].astype(I32)
    dest = jnp.sum(jnp.where(f[..., None] == jnp.arange(offsets.shape[0], dtype=I32), offsets, 0), axis=-1) + rank
    return dest.reshape(n // TM_ROW, TM_ROW, TOP_K_INNER).transpose(0, 2, 1).reshape(n // TM_ROW, 1, TOP_K_INNER * TM_ROW)


def kernel(x_prompt, x_sample, cache_k, cache_v, state_pool, g_mix, w_in, w_pool, pool_scale, w_out, g_ffn,
           w_router_group, b_router_group, w_router_expert, b_router_expert, w_gate, w_up, w_down, g_final):
    depth = g_mix.shape[0]
    assert depth == 1, "single-layer step"
    b, s, d = x_prompt.shape
    nd, t_new, _ = x_sample.shape
    n_heads = cache_k.shape[3]
    attn_w = n_heads * HEAD_DIM
    pool_w = state_pool.shape[3]
    past = cache_k.shape[2]
    keep = min(ATTN_REACH, s)
    n_groups, n_experts = w_router_expert.shape[1], w_router_expert.shape[3]
    n_flat = n_groups * n_experts
    assert n_groups + n_flat <= LANES
    slopes = _alibi_slopes(n_heads)

    w_in_b = w_in[0].astype(BF16)
    w_out_b = w_out[0].astype(BF16)
    gw = pool_w // len(POOL_WINDOWS)
    w_bd = jnp.zeros((pool_w, pool_w), F32)
    for g in range(len(POOL_WINDOWS)):
        w_bd = w_bd.at[g * gw:(g + 1) * gw, g * gw:(g + 1) * gw].set(w_pool[0, g])
    w_bd = w_bd.astype(BF16)
    w_route = jnp.concatenate([w_router_group[0], jnp.transpose(w_router_expert[0], (1, 0, 2)).reshape(d, n_flat)], axis=1)
    w_route = jnp.pad(w_route, ((0, 0), (0, LANES - n_groups - n_flat)))
    b_route = jnp.pad(jnp.concatenate([b_router_group[0], b_router_expert[0].reshape(n_flat)]),
                      (0, LANES - n_groups - n_flat))[None]
    w_gate_f, w_up_f = w_gate[0].reshape(n_flat, d, -1), w_up[0].reshape(n_flat, d, -1)
    w_down_f = w_down[0].reshape(n_flat, -1, d)

    n_p = b * s
    u_p, q_p, k_p, v_p, kt_p, vt_p = _proj_in(x_prompt.reshape(n_p, d), g_mix, w_in_b, pool_w=pool_w, attn_w=attn_w,
                                               seq=s, keep=keep)
    attn_p = _attn_prompt(slopes, q_p.reshape(b, s, attn_w), k_p.reshape(b, s, attn_w), v_p.reshape(b, s, attn_w))
    u_p3 = u_p.reshape(b, s, pool_w)
    pool_p = _pool_prompt(u_p3, w_bd, pool_scale)

    n_s = nd * t_new
    u_s, q_s, k_s, v_s = _proj_in(x_sample.reshape(n_s, d), g_mix, w_in_b, pool_w=pool_w, attn_w=attn_w)
    cache_kt = jnp.transpose(cache_k[0], (0, 2, 3, 1))
    cache_vt = jnp.transpose(cache_v[0], (0, 2, 3, 1))
    as3 = lambda a: a.reshape(nd, t_new, attn_w)
    attn_s = _attn_sample(slopes, as3(q_s), as3(k_s), as3(v_s), cache_kt, cache_vt)
    state_t = jnp.transpose(state_pool[0], (1, 0, 2))
    u_st = jnp.transpose(u_s.reshape(nd, t_new, pool_w), (1, 0, 2))
    pool_st = _pool_sample(state_t, u_st, w_bd, pool_scale)

    tmajor = lambda a: jnp.transpose(a.reshape(nd, t_new, -1), (1, 0, 2)).reshape(n_s, -1)
    route_kw = dict(n_groups=n_groups, n_experts=n_experts)
    h_p, route_p, cnt_p = _mix_route(x_prompt.reshape(n_p, d), pool_p.reshape(n_p, pool_w), attn_p.reshape(n_p, attn_w),
                                     w_out_b, g_ffn, w_route, b_route, jnp.zeros((1, LANES), F32), **route_kw)
    h_s, route_s, cnt_all = _mix_route(tmajor(x_sample), pool_st.reshape(n_s, pool_w), tmajor(attn_s),
                                       w_out_b, g_ffn, w_route, b_route, cnt_p, **route_kw)

    counts = cnt_all[0, n_groups:n_groups + n_flat].astype(I32)
    n_tiles = (TOP_K_INNER * (n_p + n_s) + n_flat * (TM_GEMM - 1)) // TM_GEMM
    offsets, tile_expert, tile_block, tile_valid, tile_zero = _sort_tables(counts, n_tiles)
    dest_p, dest_s = _dest_blocks(route_p, offsets), _dest_blocks(route_s, offsets)

    xs = _dispatch(tile_zero, jnp.concatenate([dest_p, dest_s], axis=0), h_p, h_s, g_ffn)
    ys = _moe_gemm(tile_expert, tile_block, tile_valid, xs, w_gate_f, w_up_f, w_down_f)
    y_p = _combine(dest_p, h_p, route_p, g_final[None], ys)
    y_s = _combine(dest_s, h_s, route_s, g_final[None], ys)

    y_prompt = y_p.reshape(b, s, d)
    y_sample = jnp.transpose(y_s.reshape(t_new, nd, d), (1, 0, 2))
    k_prompt = jnp.transpose(kt_p.reshape(b, n_heads, HEAD_DIM, keep), (0, 3, 1, 2))[None]
    v_prompt = jnp.transpose(vt_p.reshape(b, n_heads, HEAD_DIM, keep), (0, 3, 1, 2))[None]
    pool_prompt = u_p3[:, s - POOL_STATE:][None]
    k_sample = k_s.reshape(1, nd, t_new, n_heads, HEAD_DIM)
    v_sample = v_s.reshape(1, nd, t_new, n_heads, HEAD_DIM)
    pool_sample = jnp.transpose(jnp.concatenate([state_t, u_st], axis=0)[-POOL_STATE:], (1, 0, 2))[None]
    return (y_prompt, y_sample, k_prompt, v_prompt, pool_prompt, k_sample, v_sample, pool_sample)
```

```python
import functools
import math

import numpy as np
import jax
import jax.numpy as jnp
from jax import lax
from jax.experimental import pallas as pl
from jax.experimental.pallas import tpu as pltpu

F32 = jnp.float32
BF16 = jnp.bfloat16
I32 = jnp.int32

HEAD_DIM = 64
POOL_WINDOWS = (2, 4, 8, 16)
POOL_STATE = max(POOL_WINDOWS) - 1
BRANCHES = ((128, 1), (512, 4), (2048, 16))
STEPS = BRANCHES[0][0] // BRANCHES[0][1]
ATTN_REACH = max(w for w, _ in BRANCHES)
MAX_DIL = max(d for _, d in BRANCHES)
TOP_K_INNER = 2
RMS_EPS = 1e-6
LANES = 128
NEG_INF = float("-inf")

VMEM_LIMIT = 56 * 1024 * 1024

TM_PROJ = 512
TM_MIX = 512
TM_ROW = 256
TM_GEMM = 256
ATTN_GROUP = 8


def _alibi_slopes(n_heads):
    def geometric(n):
        start = 2.0 ** (-8.0 / n)
        return [start ** (i + 1) for i in range(n)]
    c = 2 ** int(math.floor(math.log2(n_heads)))
    s = geometric(c)
    if c < n_heads:
        s = s + geometric(2 * c)[0::2][: n_heads - c]
    return jnp.asarray(s, dtype=F32)


def _rms(x, g):
    return x * lax.rsqrt(jnp.mean(x * x, axis=-1, keepdims=True) + RMS_EPS) * g


def _params(*sem):
    return pltpu.CompilerParams(dimension_semantics=sem, vmem_limit_bytes=VMEM_LIMIT)


def _proj_in_body(x_ref, g_ref, w_ref, u_ref, q_ref, k_ref, v_ref, *t_refs, pool_w, attn_w, tiles_per_seq,
                  keep_tiles):
    xn = _rms(x_ref[...], g_ref[...]).astype(BF16)

    def proj(lo, n):
        return jnp.dot(xn, w_ref[:, lo:lo + n], preferred_element_type=F32)

    u_ref[...] = proj(0, pool_w)
    q_ref[...] = proj(pool_w, attn_w) * (HEAD_DIM ** -0.5)
    k = proj(pool_w + attn_w, attn_w)
    v = proj(pool_w + 2 * attn_w, attn_w)
    k_ref[...] = k
    v_ref[...] = v
    if t_refs:
        kt_ref, vt_ref = t_refs
        j = pl.program_id(0) % tiles_per_seq

        @pl.when(j >= tiles_per_seq - keep_tiles)
        def _():
            kt_ref[0] = k.T
            vt_ref[0] = v.T


def _proj_in(x, g, w_bf16, *, pool_w, attn_w, seq=None, keep=None):
    n, d = x.shape
    tm = TM_PROJ
    grid = (n // tm,)
    row = lambda i: (i, 0)
    out_shape = [jax.ShapeDtypeStruct((n, pool_w), F32)] + [jax.ShapeDtypeStruct((n, attn_w), F32)] * 3
    out_specs = [pl.BlockSpec((tm, pool_w), row)] + [pl.BlockSpec((tm, attn_w), row)] * 3
    tiles_per_seq = keep_tiles = 0
    if seq is not None:
        tiles_per_seq, keep_tiles = seq // tm, keep // tm
        first = tiles_per_seq - keep_tiles
        tmap = lambda i: (i // tiles_per_seq, 0, jnp.maximum(i % tiles_per_seq - first, 0))
        out_shape += [jax.ShapeDtypeStruct((n // seq, attn_w, keep), F32)] * 2
        out_specs += [pl.BlockSpec((1, attn_w, tm), tmap)] * 2
    body = functools.partial(_proj_in_body, pool_w=pool_w, attn_w=attn_w, tiles_per_seq=tiles_per_seq,
                             keep_tiles=keep_tiles)
    return pl.pallas_call(
        body, grid=grid,
        in_specs=[pl.BlockSpec((tm, d), row), pl.BlockSpec((1, d), lambda i: (0, 0)),
                  pl.BlockSpec(w_bf16.shape, lambda i: (0, 0))],
        out_specs=out_specs, out_shape=out_shape,
        compiler_params=_params("arbitrary"), name="proj_in",
    )(x, g, w_bf16)


def _attn_prompt_body(slopes_ref, q_ref, k_ref, v_ref, o_ref, qp_scr, kp_scr, vp_scr, bias_scr, s_scr, p_scr, o_scr,
                      m_scr, l_scr, *, seq):
    hp = pl.program_id(1)
    blk = STEPS
    n_it = seq // blk
    cls = seq // MAX_DIL

    for c in range(MAX_DIL):
        for src, dst in ((q_ref, qp_scr), (k_ref, kp_scr), (v_ref, vp_scr)):
            dst[c * cls:(c + 1) * cls, :] = src[0, pl.ds(c, cls, stride=MAX_DIL), :]

    lane = lax.broadcasted_iota(I32, (blk, LANES), 1)
    head_a = lane < HEAD_DIM
    row = lax.broadcasted_iota(I32, (2 * blk, 2 * blk), 0)
    col = lax.broadcasted_iota(I32, (2 * blk, 2 * blk), 1)
    slope = jnp.where(row < blk, slopes_ref[2 * hp], slopes_ref[2 * hp + 1])

    for bi, (_, d) in enumerate(BRANCHES):
        nb = n_it // d
        n_chunk = MAX_DIL // d
        a_rows = blk // n_chunk
        sh = a_rows.bit_length() - 1
        assert a_rows == 1 << sh and a_rows % 8 == 0

        def seq_index(i, n_chunk=n_chunk, a_rows=a_rows, sh=sh):
            i = i & (blk - 1)
            return (i & (a_rows - 1)) * n_chunk + (i >> sh)

        step = seq_index(row) + blk - (seq_index(col) + (col & blk))
        bias = jnp.where((step >= 0) & (step <= STEPS), -slope * (d * step).astype(F32), NEG_INF)
        bias_scr[0] = bias
        bias_scr[1] = jnp.where(col < blk, NEG_INF, bias)

        def group(j, carry, bi=bi, d=d, nb=nb, n_chunk=n_chunk, a_rows=a_rows):
            def chunks(g):
                it = j * ATTN_GROUP + g
                r = it // nb
                n = it % nb
                cur = [pl.ds(pl.multiple_of((r + d * c) * cls + n * a_rows, 8), a_rows) for c in range(n_chunk)]
                prev = [pl.ds(pl.multiple_of((r + d * c) * cls + jnp.maximum(n - 1, 0) * a_rows, 8), a_rows)
                        for c in range(n_chunk)]
                return n, cur, prev

            def gather(ref, sls):
                return jnp.concatenate([ref[sl, :] for sl in sls], axis=0)

            for g in range(ATTN_GROUP):
                n, cur, prev = chunks(g)
                qb = gather(qp_scr, cur)
                q2 = jnp.concatenate([jnp.where(head_a, qb, 0.0), jnp.where(head_a, 0.0, qb)], axis=0).astype(BF16)
                kc = gather(kp_scr, prev + cur).astype(BF16)
                s = lax.dot_general(q2, kc, (((1,), (1,)), ((), ())), preferred_element_type=F32)
                s_scr[g] = s + bias_scr[jnp.where(n == 0, 1, 0)]
            for g in range(ATTN_GROUP):
                _, cur, _ = chunks(g)
                s = s_scr[g]
                m = jnp.max(s, axis=1, keepdims=True)
                p = jnp.exp(s - m)
                l = jnp.sum(p, axis=1, keepdims=True)
                p_scr[g] = p.astype(BF16)
                m2 = jnp.where(head_a, m[:blk], m[blk:])
                l2 = jnp.where(head_a, l[:blk], l[blk:])
                for c, sl in enumerate(cur):
                    m_scr[bi, sl, :] = m2[c * a_rows:(c + 1) * a_rows]
                    l_scr[bi, sl, :] = l2[c * a_rows:(c + 1) * a_rows]
            for g in range(ATTN_GROUP):
                _, cur, prev = chunks(g)
                vc = gather(vp_scr, prev + cur).astype(BF16)
                o = jnp.dot(p_scr[g], vc, preferred_element_type=F32)
                o2 = jnp.where(head_a, o[:blk], o[blk:])
                for c, sl in enumerate(cur):
                    o_scr[bi, sl, :] = o2[c * a_rows:(c + 1) * a_rows]
            return carry

        lax.fori_loop(0, n_it // ATTN_GROUP, group, 0)

    def merge(c, carry):
        rs = pl.ds(pl.multiple_of(c * cls, cls), cls)
        ms = [m_scr[b, rs, :] for b in range(len(BRANCHES))]
        mx = functools.reduce(jnp.maximum, ms)
        num = jnp.zeros((cls, LANES), F32)
        den = jnp.zeros((cls, LANES), F32)
        for b in range(len(BRANCHES)):
            a = jnp.exp(ms[b] - mx)
            num = num + a * o_scr[b, rs, :]
            den = den + a * l_scr[b, rs, :]
        o_ref[0, pl.ds(c, cls, stride=MAX_DIL), :] = num / den
        return carry

    lax.fori_loop(0, MAX_DIL, merge, 0)


def _attn_prompt(slopes, q, k, v):
    b, s, hw = q.shape
    assert s % (STEPS * max(d for _, d in BRANCHES)) == 0, "sequence must be a multiple of the widest span"
    assert hw % LANES == 0 and LANES == 2 * HEAD_DIM
    spec = pl.BlockSpec((1, s, LANES), lambda i, j: (i, 0, j))
    nbr = len(BRANCHES)
    return pl.pallas_call(
        functools.partial(_attn_prompt_body, seq=s),
        grid=(b, hw // LANES),
        in_specs=[pl.BlockSpec(memory_space=pltpu.SMEM), spec, spec, spec],
        out_specs=spec,
        out_shape=jax.ShapeDtypeStruct((b, s, hw), F32),
        scratch_shapes=[pltpu.VMEM((s, LANES), F32)] * 3
        + [pltpu.VMEM((2, 2 * STEPS, 2 * STEPS), F32),
                        pltpu.VMEM((ATTN_GROUP, 2 * STEPS, 2 * STEPS), F32),
                        pltpu.VMEM((ATTN_GROUP, 2 * STEPS, 2 * STEPS), BF16)]
        + [pltpu.VMEM((nbr, s, LANES), F32)] * 3,
        compiler_params=_params("arbitrary", "arbitrary"), name="attn_prompt",
    )(slopes, q, k, v)


def _multiplicity(dist):
    mult = jnp.zeros(dist.shape, F32)
    for w, d in BRANCHES:
        assert d & (d - 1) == 0
        mult = mult + ((dist >= 0) & ((dist & (d - 1)) == 0) & (dist <= w)).astype(F32)
    return mult


def _attn_sample_body(slopes_ref, q_ref, kn_ref, vn_ref, kt_ref, vt_ref, o_ref, *, n_heads):
    t_new = q_ref.shape[1]
    past = kt_ref.shape[3]
    dist = past + lax.broadcasted_iota(I32, (t_new, past), 0) - lax.broadcasted_iota(I32, (t_new, past), 1)
    dist_n = lax.broadcasted_iota(I32, (t_new, t_new), 0) - lax.broadcasted_iota(I32, (t_new, t_new), 1)
    mult, mult_n = _multiplicity(dist), _multiplicity(dist_n)
    dist_f, dist_nf = dist.astype(F32), dist_n.astype(F32)
    nt = (((1,), (1,)), ((), ()))
    for h in range(n_heads):
        slope = slopes_ref[h]
        cols = slice(h * HEAD_DIM, (h + 1) * HEAD_DIM)
        qh = q_ref[0, :, cols].astype(BF16)
        kn = kn_ref[0, :, cols].astype(BF16)
        vn = vn_ref[0, :, cols].astype(BF16)
        s = jnp.dot(qh, kt_ref[0, h].astype(BF16), preferred_element_type=F32)
        s = jnp.where(mult > 0, s - slope * dist_f, NEG_INF)
        sn = lax.dot_general(qh, kn, nt, preferred_element_type=F32)
        sn = jnp.where(mult_n > 0, sn - slope * dist_nf, NEG_INF)
        m = jnp.maximum(jnp.max(s, axis=1, keepdims=True), jnp.max(sn, axis=1, keepdims=True))
        p = mult * jnp.exp(s - m)
        pn = mult_n * jnp.exp(sn - m)
        l = jnp.sum(p, axis=1, keepdims=True) + jnp.sum(pn, axis=1, keepdims=True)
        o = lax.dot_general(p.astype(BF16), vt_ref[0, h].astype(BF16), nt, preferred_element_type=F32)
        o = o + jnp.dot(pn.astype(BF16), vn, preferred_element_type=F32)
        o_ref[0, :, cols] = o / l


def _attn_sample(slopes, q, k_new, v_new, cache_kt, cache_vt):
    n, t, hw = q.shape
    _, h, e, past = cache_kt.shape
    assert past >= ATTN_REACH, "every strided key of every branch must exist in the window buffer"
    new = pl.BlockSpec((1, t, hw), lambda i: (i, 0, 0))
    old = pl.BlockSpec((1, h, e, past), lambda i: (i, 0, 0, 0))
    return pl.pallas_call(
        functools.partial(_attn_sample_body, n_heads=h),
        grid=(n,),
        in_specs=[pl.BlockSpec(memory_space=pltpu.SMEM), new, new, new, old, old],
        out_specs=new,
        out_shape=jax.ShapeDtypeStruct((n, t, hw), F32),
        compiler_params=_params("arbitrary"), name="attn_sample",
    )(slopes, q, k_new, v_new, cache_kt, cache_vt)


def _pool_windows(width):
    gw = width // len(POOL_WINDOWS)
    lane = lax.broadcasted_iota(I32, (1, width), 1)
    win = jnp.zeros((1, width), I32)
    for g, w in enumerate(POOL_WINDOWS):
        win = jnp.where((lane >= g * gw) & (lane < (g + 1) * gw), w, win)
    return win


def _pool_prompt_body(u_ref, w_ref, sc_ref, o_ref, ext_scr, *, tm):
    j = pl.program_id(1)
    width = u_ref.shape[2]
    pad = 2 * (POOL_STATE + 1)
    start = pl.multiple_of(j * tm, tm)
    lead = POOL_STATE + 1
    prev = u_ref[0, pl.ds(pl.multiple_of(jnp.maximum(start - lead, 0), lead), lead), :]
    ext_scr[0:pad - lead, :] = jnp.zeros((pad - lead, width), F32)
    ext_scr[pad - lead:pad, :] = jnp.where(j > 0, prev, 0.0)
    ext_scr[pad:, :] = u_ref[0, pl.ds(start, tm), :]
    win = _pool_windows(width)
    tok = ext_scr[pad:, :]
    acc = tok
    for i in range(1, max(POOL_WINDOWS)):
        acc = acc + jnp.where(i < win, ext_scr[pl.ds(pad - i, tm), :], 0.0)
    pos = start + lax.broadcasted_iota(I32, (tm, width), 0)
    cnt = jnp.minimum(win, pos + 1).astype(F32)
    diff = (acc / cnt - tok).astype(BF16)
    o_ref[0] = jnp.dot(diff, w_ref[...], preferred_element_type=F32) * sc_ref[...]


def _pool_prompt(u, w_bd, scale):
    b, s, w = u.shape
    tm = TM_PROJ
    return pl.pallas_call(
        functools.partial(_pool_prompt_body, tm=tm),
        grid=(b, s // tm),
        in_specs=[pl.BlockSpec((1, s, w), lambda i, j: (i, 0, 0)), pl.BlockSpec((w, w), lambda i, j: (0, 0)),
                  pl.BlockSpec((1, w), lambda i, j: (0, 0))],
        out_specs=pl.BlockSpec((1, tm, w), lambda i, j: (i, j, 0)),
        out_shape=jax.ShapeDtypeStruct((b, s, w), F32),
        scratch_shapes=[pltpu.VMEM((tm + 2 * (POOL_STATE + 1), w), F32)],
        compiler_params=_params("arbitrary", "arbitrary"), name="pool_prompt",
    )(u, w_bd, scale)


def _pool_sample_body(st_ref, u_ref, w_ref, sc_ref, o_ref, ns_ref):
    t_new, _, width = u_ref.shape
    n_state = st_ref.shape[0]
    win = _pool_windows(width)

    def row(k):
        return st_ref[k] if k < n_state else u_ref[k - n_state]

    for t in range(t_new):
        tok = u_ref[t]
        acc = tok
        for i in range(1, max(POOL_WINDOWS)):
            acc = acc + jnp.where(i < win, row(n_state + t - i), 0.0)
        diff = (acc / win.astype(F32) - tok).astype(BF16)
        o_ref[t] = jnp.dot(diff, w_ref[...], preferred_element_type=F32) * sc_ref[...]
    for k in range(n_state):
        ns_ref[k] = row(k + t_new)


def _pool_sample(state_t, u, w_bd, scale):
    n_state = state_t.shape[0]
    assert n_state >= POOL_STATE
    return pl.pallas_call(
        _pool_sample_body,
        out_shape=[jax.ShapeDtypeStruct(u.shape, F32), jax.ShapeDtypeStruct(state_t.shape, F32)],
        compiler_params=pltpu.CompilerParams(vmem_limit_bytes=VMEM_LIMIT), name="pool_sample",
    )(state_t, u, w_bd, scale)


ROUTE_F1, ROUTE_F2, ROUTE_G1, ROUTE_G2, ROUTE_R1, ROUTE_R2 = range(6)


def _split_bf16(x):
    hi = x.astype(BF16)
    return hi, (x - hi.astype(F32)).astype(BF16)


def _mix_route_body(x_ref, pool_ref, attn_ref, wo_ref, g_ref, wr_ref, br_ref, cnt_in_ref, h_ref, route_ref,
                    cnt_out_ref, carry_scr, *, n_groups, n_experts):
    i = pl.program_id(0)
    tm = x_ref.shape[0]
    pool_w = pool_ref.shape[1]

    @pl.when(i == 0)
    def _():
        carry_scr[...] = cnt_in_ref[...]

    h = x_ref[...]
    h = h + jnp.dot(pool_ref[...].astype(BF16), wo_ref[0:pool_w, :], preferred_element_type=F32)
    h = h + jnp.dot(attn_ref[...].astype(BF16), wo_ref[pool_w:, :], preferred_element_type=F32)
    h_ref[...] = h

    hn_hi, hn_lo = _split_bf16(_rms(h, g_ref[...]))
    w_hi, w_lo = _split_bf16(wr_ref[...])
    hi_both = jnp.dot(hn_hi, jnp.concatenate([w_hi, w_lo], axis=1), preferred_element_type=F32)
    logits = (hi_both[:, :LANES] + hi_both[:, LANES:]
              + jnp.dot(hn_lo, w_hi, preferred_element_type=F32)) + br_ref[...]

    lane = lax.broadcasted_iota(I32, (tm, LANES), 1).astype(F32)

    def first_lane(mask):
        return jnp.min(jnp.where(mask, lane, float(LANES)), axis=1, keepdims=True)

    is_g = lane < n_groups
    lg = jnp.where(is_g, logits, NEG_INF)
    mg = jnp.max(lg, axis=1, keepdims=True)
    p_sel = 1.0 / jnp.sum(jnp.exp(lg - mg), axis=1, keepdims=True)
    g_top = first_lane(lg == mg)
    lo = n_groups + g_top * n_experts
    in_grp = (lane >= lo) & (lane < lo + n_experts)
    le = jnp.where(in_grp, logits, NEG_INF)
    ee = jnp.exp(le - jnp.max(le, axis=1, keepdims=True))
    pe = ee / jnp.sum(ee, axis=1, keepdims=True)
    v1 = jnp.max(jnp.where(in_grp, pe, -1.0), axis=1, keepdims=True)
    i1 = first_lane(in_grp & (pe == v1))
    rest = in_grp & (lane != i1)
    v2 = jnp.max(jnp.where(rest, pe, -1.0), axis=1, keepdims=True)
    i2 = first_lane(rest & (pe == v2))
    gate1 = p_sel * (v1 / (v1 + v2))
    gate2 = p_sel * (v2 / (v1 + v2))

    sel1, sel2 = lane == i1, lane == i2
    onehot = (sel1 | sel2).astype(BF16)
    tri = (lax.broadcasted_iota(I32, (tm, tm), 1) < lax.broadcasted_iota(I32, (tm, tm), 0)).astype(BF16)
    running = jnp.dot(tri, onehot, preferred_element_type=F32) + carry_scr[...]
    rank1 = jnp.sum(jnp.where(sel1, running, 0.0), axis=1, keepdims=True)
    rank2 = jnp.sum(jnp.where(sel2, running, 0.0), axis=1, keepdims=True)
    carry_scr[...] = carry_scr[...] + jnp.sum(onehot.astype(F32), axis=0, keepdims=True)
    cnt_out_ref[...] = carry_scr[...]

    rec = jnp.zeros((tm, LANES), F32)
    for idx, val in ((ROUTE_F1, i1 - n_groups), (ROUTE_F2, i2 - n_groups),
                     (ROUTE_G1, gate1), (ROUTE_G2, gate2), (ROUTE_R1, rank1), (ROUTE_R2, rank2)):
        rec = jnp.where(lane == idx, val, rec)
    route_ref[...] = rec


def _mix_route(x, pool, attn, w_out_bf16, g_ffn, w_route, b_route, cnt_in, *, n_groups, n_experts):
    n, d = x.shape
    tm = TM_MIX
    row = lambda i: (i, 0)
    fix = lambda i: (0, 0)
    return pl.pallas_call(
        functools.partial(_mix_route_body, n_groups=n_groups, n_experts=n_experts),
        grid=(n // tm,),
        in_specs=[pl.BlockSpec((tm, d), row), pl.BlockSpec((tm, pool.shape[1]), row),
                  pl.BlockSpec((tm, attn.shape[1]), row), pl.BlockSpec(w_out_bf16.shape, fix),
                  pl.BlockSpec((1, d), fix), pl.BlockSpec(w_route.shape, fix), pl.BlockSpec((1, LANES), fix),
                  pl.BlockSpec((1, LANES), fix)],
        out_specs=[pl.BlockSpec((tm, d), row), pl.BlockSpec((tm, LANES), row), pl.BlockSpec((1, LANES), fix)],
        out_shape=[jax.ShapeDtypeStruct((n, d), F32), jax.ShapeDtypeStruct((n, LANES), F32),
                   jax.ShapeDtypeStruct((1, LANES), F32)],
        scratch_shapes=[pltpu.VMEM((1, LANES), F32)],
        compiler_params=_params("arbitrary"), name="mix_route",
    )(x, pool, attn, w_out_bf16, g_ffn, w_route, b_route, cnt_in)


def _dispatch_body(zero_ref, dest_ref, hp_ref, hs_ref, g_ref, xs_ref, xn_scr, zero_scr, sems, *, tiles_p, n_steps,
                   n_tiles):
    tm = hp_ref.shape[0]
    i = pl.program_id(0)
    zero_sem = 2 * TOP_K_INNER

    @pl.when(i == 0)
    def _():
        zero_scr[...] = jnp.zeros(zero_scr.shape, F32)
        for start in (True, False):
            def fill(t, c, start=start):
                @pl.when(zero_ref[t] > 0)
                def _():
                    cp = pltpu.make_async_copy(zero_scr, xs_ref.at[pl.ds(t * TM_GEMM, TM_GEMM)], sems.at[zero_sem])
                    cp.start() if start else cp.wait()
                return c
            lax.fori_loop(0, n_tiles, fill, 0)

    buf = i % 2

    def wait_rows(b):
        for slot in range(TOP_K_INNER):
            pltpu.make_async_copy(xn_scr.at[b], xs_ref.at[pl.ds(0, tm)], sems.at[b * TOP_K_INNER + slot]).wait()

    @pl.when(i >= 2)
    def _():
        wait_rows(buf)

    @pl.when(i < tiles_p)
    def _():
        xn_scr[buf] = _rms(hp_ref[...], g_ref[...])

    @pl.when(i >= tiles_p)
    def _():
        xn_scr[buf] = _rms(hs_ref[...], g_ref[...])

    def send(r, c):
        for slot in range(TOP_K_INNER):
            d = dest_ref[0, 0, slot * tm + r]
            pltpu.make_async_copy(xn_scr.at[buf, pl.ds(r, 1)], xs_ref.at[pl.ds(d, 1)],
                                  sems.at[buf * TOP_K_INNER + slot]).start(priority=slot)
        return c

    lax.fori_loop(0, tm, send, 0, unroll=4)

    @pl.when(i == n_steps - 1)
    def _():
        if n_steps > 1:
            wait_rows(1 - buf)
        wait_rows(buf)


def _dispatch(tile_zero, dest, h_p, h_s, g_ffn):
    tm = TM_ROW
    d = h_p.shape[1]
    tiles_p, tiles_s = h_p.shape[0] // tm, h_s.shape[0] // tm
    n_tiles = tile_zero.shape[0]
    return pl.pallas_call(
        functools.partial(_dispatch_body, tiles_p=tiles_p, n_steps=tiles_p + tiles_s, n_tiles=n_tiles),
        grid_spec=pltpu.PrefetchScalarGridSpec(
            num_scalar_prefetch=1, grid=(tiles_p + tiles_s,),
            in_specs=[pl.BlockSpec((1, 1, TOP_K_INNER * tm), lambda i, z: (i, 0, 0), memory_space=pltpu.SMEM),
                      pl.BlockSpec((tm, d), lambda i, z: (jnp.minimum(i, tiles_p - 1), 0)),
                      pl.BlockSpec((tm, d), lambda i, z: (jnp.maximum(i - tiles_p, 0), 0)),
                      pl.BlockSpec((1, d), lambda i, z: (0, 0))],
            out_specs=pl.BlockSpec(memory_space=pl.ANY),
            scratch_shapes=[pltpu.VMEM((2, tm, d), F32), pltpu.VMEM((TM_GEMM, d), F32),
                            pltpu.SemaphoreType.DMA((2 * TOP_K_INNER + 1,))]),
        out_shape=jax.ShapeDtypeStruct((n_tiles * TM_GEMM, d), F32),
        compiler_params=_params("arbitrary"), name="dispatch",
    )(tile_zero, dest, h_p, h_s, g_ffn)


def _moe_gemm_body(expert_ref, block_ref, valid_ref, x_ref, wg_ref, wu_ref, wd_ref, y_ref, wg_scr, wu_scr, wd_scr):
    t = pl.program_id(0)

    @pl.when((t == 0) | (expert_ref[t] != expert_ref[jnp.maximum(t - 1, 0)]))
    def _():
        wg_scr[...] = wg_ref[0].astype(BF16)
        wu_scr[...] = wu_ref[0].astype(BF16)
        wd_scr[...] = wd_ref[0].astype(BF16)

    @pl.when(valid_ref[t] > 0)
    def _():
        x = x_ref[...].astype(BF16)
        gate = jnp.dot(x, wg_scr[...], preferred_element_type=F32)
        up = jnp.dot(x, wu_scr[...], preferred_element_type=F32)
        mid = (gate * jax.nn.sigmoid(gate) * up).astype(BF16)
        y_ref[...] = jnp.dot(mid, wd_scr[...], preferred_element_type=F32)

    @pl.when(valid_ref[t] == 0)
    def _():
        y_ref[...] = jnp.zeros(y_ref.shape, F32)


def _moe_gemm(tile_expert, tile_block, tile_valid, xs, w_gate, w_up, w_down):
    rows, d = xs.shape
    _, _, f = w_gate.shape
    tm = TM_GEMM
    xmap = lambda t, e, b, v: (b[t], 0)
    return pl.pallas_call(
        _moe_gemm_body,
        grid_spec=pltpu.PrefetchScalarGridSpec(
            num_scalar_prefetch=3, grid=(rows // tm,),
            in_specs=[pl.BlockSpec((tm, d), xmap),
                      pl.BlockSpec((1, d, f), lambda t, e, b, v: (e[t], 0, 0)),
                      pl.BlockSpec((1, d, f), lambda t, e, b, v: (e[t], 0, 0)),
                      pl.BlockSpec((1, f, d), lambda t, e, b, v: (e[t], 0, 0))],
            out_specs=pl.BlockSpec((tm, d), lambda t, e, b, v: (t, 0)),
            scratch_shapes=[pltpu.VMEM((d, f), BF16), pltpu.VMEM((d, f), BF16), pltpu.VMEM((f, d), BF16)]),
        out_shape=jax.ShapeDtypeStruct((rows, d), F32),
        compiler_params=_params("arbitrary"), name="moe_gemm",
    )(tile_expert, tile_block, tile_valid, xs, w_gate, w_up, w_down)


def _combine_body(dest_ref, next_ref, h_ref, route_ref, g_ref, ys_ref, y_ref, rows_scr, sems, *, n_steps):
    tm = h_ref.shape[0]
    i = pl.program_id(0)
    buf = i % 2

    def fetch(idx_ref, b):
        def body(r, c):
            for slot in range(TOP_K_INNER):
                d = idx_ref[0, 0, slot * tm + r]
                pltpu.make_async_copy(ys_ref.at[pl.ds(d, 1)], rows_scr.at[b, slot, pl.ds(r, 1)],
                                      sems.at[b * TOP_K_INNER + slot]).start(priority=slot)
            return c
        lax.fori_loop(0, tm, body, 0, unroll=4)

    @pl.when(i == 0)
    def _():
        fetch(dest_ref, 0)

    @pl.when(i + 1 < n_steps)
    def _():
        fetch(next_ref, 1 - buf)

    for slot in range(TOP_K_INNER):
        pltpu.make_async_copy(ys_ref.at[pl.ds(0, tm)], rows_scr.at[buf, slot], sems.at[buf * TOP_K_INNER + slot]).wait()
    route = route_ref[...]
    out = h_ref[...] + (route[:, ROUTE_G1:ROUTE_G1 + 1] * rows_scr[buf, 0]
                        + route[:, ROUTE_G2:ROUTE_G2 + 1] * rows_scr[buf, 1])
    y_ref[...] = _rms(out, g_ref[...])


def _combine(dest, h, route, g_final, ys):
    n, d = h.shape
    tm = TM_ROW
    n_steps = n // tm
    row = lambda i: (i, 0)
    idx_block = (1, 1, TOP_K_INNER * tm)
    return pl.pallas_call(
        functools.partial(_combine_body, n_steps=n_steps),
        grid=(n_steps,),
        in_specs=[pl.BlockSpec(idx_block, lambda i: (i, 0, 0), memory_space=pltpu.SMEM),
                  pl.BlockSpec(idx_block, lambda i: (jnp.minimum(i + 1, n_steps - 1), 0, 0), memory_space=pltpu.SMEM),
                  pl.BlockSpec((tm, d), row), pl.BlockSpec((tm, LANES), row), pl.BlockSpec((1, d), lambda i: (0, 0)),
                  pl.BlockSpec(memory_space=pl.ANY)],
        out_specs=pl.BlockSpec((tm, d), row),
        out_shape=jax.ShapeDtypeStruct((n, d), F32),
        scratch_shapes=[pltpu.VMEM((2, TOP_K_INNER, tm, d), F32), pltpu.SemaphoreType.DMA((2 * TOP_K_INNER,))],
        compiler_params=_params("arbitrary"), name="combine",
    )(dest, dest, h, route, g_final, ys)


def _sort_tables(counts, n_tiles):
    padded = ((counts + TM_GEMM - 1) // TM_GEMM) * TM_GEMM
    ends = jnp.cumsum(padded)
    offsets = ends - padded
    total = ends[-1]
    tile_start = jnp.arange(n_tiles, dtype=I32) * TM_GEMM
    tile_valid = (tile_start < total).astype(I32)
    last_block = jnp.maximum(total // TM_GEMM - 1, 0)
    tile_block = jnp.minimum(jnp.arange(n_tiles, dtype=I32), last_block)
    n_flat = counts.shape[0]
    tile_expert = jnp.minimum(jnp.sum((tile_block[:, None] * TM_GEMM >= ends[None, :]).astype(I32), axis=1), n_flat - 1)
    seg_last = jnp.any((tile_start[:, None] + TM_GEMM == ends[None, :]) & (padded[None, :] > 0), axis=1)
    tile_zero = (seg_last | (tile_valid == 0)).astype(I32)
    return offsets, tile_expert.astype(I32), tile_block.astype(I32), tile_valid, tile_zero


def _dest_blocks(route, offsets):
    n = route.shape[0]
    f = route[:, ROUTE_F1:ROUTE_F2 + 1].astype(I32)
    rank = route[:, ROUTE_R1:ROUTE_R2 + 1].astype(I32)
    dest = jnp.sum(jnp.where(f[..., None] == jnp.arange(offsets.shape[0], dtype=I32), offsets, 0), axis=-1) + rank
    return dest.reshape(n // TM_ROW, TM_ROW, TOP_K_INNER).transpose(0, 2, 1).reshape(n // TM_ROW, 1, TOP_K_INNER * TM_ROW)


def kernel(x_prompt, x_sample, cache_k, cache_v, state_pool, g_mix, w_in, w_pool, pool_scale, w_out, g_ffn,
           w_router_group, b_router_group, w_router_expert, b_router_expert, w_gate, w_up, w_down, g_final):
    depth = g_mix.shape[0]
    assert depth == 1, "single-layer step"
    b, s, d = x_prompt.shape
    nd, t_new, _ = x_sample.shape
    n_heads = cache_k.shape[3]
    attn_w = n_heads * HEAD_DIM
    pool_w = state_pool.shape[3]
    past = cache_k.shape[2]
    keep = min(ATTN_REACH, s)
    n_groups, n_experts = w_router_expert.shape[1], w_router_expert.shape[3]
    n_flat = n_groups * n_experts
    assert n_groups + n_flat <= LANES
    slopes = _alibi_slopes(n_heads)

    w_in_b = w_in[0].astype(BF16)
    w_out_b = w_out[0].astype(BF16)
    gw = pool_w // len(POOL_WINDOWS)
    w_bd = jnp.zeros((pool_w, pool_w), F32)
    for g in range(len(POOL_WINDOWS)):
        w_bd = w_bd.at[g * gw:(g + 1) * gw, g * gw:(g + 1) * gw].set(w_pool[0, g])
    w_bd = w_bd.astype(BF16)
    w_route = jnp.concatenate([w_router_group[0], jnp.transpose(w_router_expert[0], (1, 0, 2)).reshape(d, n_flat)], axis=1)
    w_route = jnp.pad(w_route, ((0, 0), (0, LANES - n_groups - n_flat)))
    b_route = jnp.pad(jnp.concatenate([b_router_group[0], b_router_expert[0].reshape(n_flat)]),
                      (0, LANES - n_groups - n_flat))[None]
    w_gate_f, w_up_f = w_gate[0].reshape(n_flat, d, -1), w_up[0].reshape(n_flat, d, -1)
    w_down_f = w_down[0].reshape(n_flat, -1, d)

    n_p = b * s
    u_p, q_p, k_p, v_p, kt_p, vt_p = _proj_in(x_prompt.reshape(n_p, d), g_mix, w_in_b, pool_w=pool_w, attn_w=attn_w,
                                               seq=s, keep=keep)
    attn_p = _attn_prompt(slopes, q_p.reshape(b, s, attn_w), k_p.reshape(b, s, attn_w), v_p.reshape(b, s, attn_w))
    u_p3 = u_p.reshape(b, s, pool_w)
    pool_p = _pool_prompt(u_p3, w_bd, pool_scale)

    n_s = nd * t_new
    u_s, q_s, k_s, v_s = _proj_in(x_sample.reshape(n_s, d), g_mix, w_in_b, pool_w=pool_w, attn_w=attn_w)
    cache_kt = jnp.transpose(cache_k[0], (0, 2, 3, 1))
    cache_vt = jnp.transpose(cache_v[0], (0, 2, 3, 1))
    as3 = lambda a: a.reshape(nd, t_new, attn_w)
    attn_s = _attn_sample(slopes, as3(q_s), as3(k_s), as3(v_s), cache_kt, cache_vt)
    state_t = jnp.transpose(state_pool[0], (1, 0, 2))
    u_st = jnp.transpose(u_s.reshape(nd, t_new, pool_w), (1, 0, 2))
    pool_st, new_state_t = _pool_sample(state_t, u_st, w_bd, pool_scale)
    pool_s = jnp.transpose(pool_st, (1, 0, 2)).reshape(n_s, pool_w)

    route_kw = dict(n_groups=n_groups, n_experts=n_experts)
    h_p, route_p, cnt_p = _mix_route(x_prompt.reshape(n_p, d), pool_p.reshape(n_p, pool_w), attn_p.reshape(n_p, attn_w),
                                     w_out_b, g_ffn, w_route, b_route, jnp.zeros((1, LANES), F32), **route_kw)
    h_s, route_s, cnt_all = _mix_route(x_sample.reshape(n_s, d), pool_s, attn_s.reshape(n_s, attn_w),
                                       w_out_b, g_ffn, w_route, b_route, cnt_p, **route_kw)

    counts = cnt_all[0, n_groups:n_groups + n_flat].astype(I32)
    n_tiles = (TOP_K_INNER * (n_p + n_s) + n_flat * (TM_GEMM - 1)) // TM_GEMM
    offsets, tile_expert, tile_block, tile_valid, tile_zero = _sort_tables(counts, n_tiles)
    dest_p, dest_s = _dest_blocks(route_p, offsets), _dest_blocks(route_s, offsets)

    xs = _dispatch(tile_zero, jnp.concatenate([dest_p, dest_s], axis=0), h_p, h_s, g_ffn)
    ys = _moe_gemm(tile_expert, tile_block, tile_valid, xs, w_gate_f, w_up_f, w_down_f)
    y_p = _combine(dest_p, h_p, route_p, g_final[None], ys)
    y_s = _combine(dest_s, h_s, route_s, g_final[None], ys)

    y_prompt = y_p.reshape(b, s, d)
    y_sample = y_s.reshape(nd, t_new, d)
    k_prompt = jnp.transpose(kt_p.reshape(b, n_heads, HEAD_DIM, keep), (0, 3, 1, 2))[None]
    v_prompt = jnp.transpose(vt_p.reshape(b, n_heads, HEAD_DIM, keep), (0, 3, 1, 2))[None]
    pool_prompt = u_p3[:, s - POOL_STATE:][None]
    k_sample = k_s.reshape(1, nd, t_new, n_heads, HEAD_DIM)
    v_sample = v_s.reshape(1, nd, t_new, n_heads, HEAD_DIM)
    pool_sample = jnp.transpose(new_state_t[-POOL_STATE:], (1, 0, 2))[None]
    return (y_prompt, y_sample, k_prompt, v_prompt, pool_prompt, k_sample, v_sample, pool_sample)
```

```python
import functools
import math

import numpy as np
import jax
import jax.numpy as jnp
from jax import lax
from jax.experimental import pallas as pl
from jax.experimental.pallas import tpu as pltpu

F32 = jnp.float32
BF16 = jnp.bfloat16
I32 = jnp.int32

HEAD_DIM = 64
POOL_WINDOWS = (2, 4, 8, 16)
POOL_STATE = max(POOL_WINDOWS) - 1
BRANCHES = ((128, 1), (512, 4), (2048, 16))
STEPS = BRANCHES[0][0] // BRANCHES[0][1]
ATTN_REACH = max(w for w, _ in BRANCHES)
MAX_DIL = max(d for _, d in BRANCHES)
TOP_K_INNER = 2
RMS_EPS = 1e-6
LANES = 128
NEG_INF = float("-inf")

VMEM_LIMIT = 56 * 1024 * 1024

TM_PROJ = 512
TM_MIX = 512
TM_ROW = 512
TM_GEMM = 512
ZERO_ROWS = 64
ATTN_GROUP = 8


def _alibi_slopes(n_heads):
    def geometric(n):
        start = 2.0 ** (-8.0 / n)
        return [start ** (i + 1) for i in range(n)]
    c = 2 ** int(math.floor(math.log2(n_heads)))
    s = geometric(c)
    if c < n_heads:
        s = s + geometric(2 * c)[0::2][: n_heads - c]
    return jnp.asarray(s, dtype=F32)


def _rms(x, g):
    return x * lax.rsqrt(jnp.mean(x * x, axis=-1, keepdims=True) + RMS_EPS) * g


def _params(*sem):
    return pltpu.CompilerParams(dimension_semantics=sem, vmem_limit_bytes=VMEM_LIMIT)


def _proj_in_body(x_ref, g_ref, w_ref, u_ref, q_ref, k_ref, v_ref, *t_refs, pool_w, attn_w, tiles_per_seq,
                  keep_tiles):
    xn = _rms(x_ref[...], g_ref[...]).astype(BF16)

    def proj(lo, n):
        return jnp.dot(xn, w_ref[:, lo:lo + n], preferred_element_type=F32)

    u_ref[...] = proj(0, pool_w)
    q_ref[...] = proj(pool_w, attn_w) * (HEAD_DIM ** -0.5)
    k = proj(pool_w + attn_w, attn_w)
    v = proj(pool_w + 2 * attn_w, attn_w)
    k_ref[...] = k
    v_ref[...] = v
    if t_refs:
        kt_ref, vt_ref = t_refs
        j = pl.program_id(0) % tiles_per_seq

        @pl.when(j >= tiles_per_seq - keep_tiles)
        def _():
            kt_ref[0] = k.T
            vt_ref[0] = v.T


def _proj_in(x, g, w_bf16, *, pool_w, attn_w, seq=None, keep=None):
    n, d = x.shape
    tm = TM_PROJ
    grid = (n // tm,)
    row = lambda i: (i, 0)
    out_shape = [jax.ShapeDtypeStruct((n, pool_w), F32)] + [jax.ShapeDtypeStruct((n, attn_w), F32)] * 3
    out_specs = [pl.BlockSpec((tm, pool_w), row)] + [pl.BlockSpec((tm, attn_w), row)] * 3
    tiles_per_seq = keep_tiles = 0
    if seq is not None:
        tiles_per_seq, keep_tiles = seq // tm, keep // tm
        first = tiles_per_seq - keep_tiles
        tmap = lambda i: (i // tiles_per_seq, 0, jnp.maximum(i % tiles_per_seq - first, 0))
        out_shape += [jax.ShapeDtypeStruct((n // seq, attn_w, keep), F32)] * 2
        out_specs += [pl.BlockSpec((1, attn_w, tm), tmap)] * 2
    body = functools.partial(_proj_in_body, pool_w=pool_w, attn_w=attn_w, tiles_per_seq=tiles_per_seq,
                             keep_tiles=keep_tiles)
    return pl.pallas_call(
        body, grid=grid,
        in_specs=[pl.BlockSpec((tm, d), row), pl.BlockSpec((1, d), lambda i: (0, 0)),
                  pl.BlockSpec(w_bf16.shape, lambda i: (0, 0))],
        out_specs=out_specs, out_shape=out_shape,
        compiler_params=_params("arbitrary"), name="proj_in",
    )(x, g, w_bf16)


def _attn_prompt_body(slopes_ref, q_ref, k_ref, v_ref, o_ref, qp_scr, kp_scr, vp_scr, bias_scr, s_scr, p_scr, o_scr,
                      m_scr, l_scr, *, seq):
    hp = pl.program_id(1)
    blk = STEPS
    n_it = seq // blk
    cls = seq // MAX_DIL

    for c in range(MAX_DIL):
        for src, dst in ((q_ref, qp_scr), (k_ref, kp_scr), (v_ref, vp_scr)):
            dst[c * cls:(c + 1) * cls, :] = src[0, pl.ds(c, cls, stride=MAX_DIL), :]

    lane = lax.broadcasted_iota(I32, (blk, LANES), 1)
    head_a = lane < HEAD_DIM
    row = lax.broadcasted_iota(I32, (2 * blk, 2 * blk), 0)
    col = lax.broadcasted_iota(I32, (2 * blk, 2 * blk), 1)
    slope = jnp.where(row < blk, slopes_ref[2 * hp], slopes_ref[2 * hp + 1])

    for bi, (_, d) in enumerate(BRANCHES):
        nb = n_it // d
        n_chunk = MAX_DIL // d
        a_rows = blk // n_chunk
        sh = a_rows.bit_length() - 1
        assert a_rows == 1 << sh and a_rows % 8 == 0

        def seq_index(i, n_chunk=n_chunk, a_rows=a_rows, sh=sh):
            i = i & (blk - 1)
            return (i & (a_rows - 1)) * n_chunk + (i >> sh)

        step = seq_index(row) + blk - (seq_index(col) + (col & blk))
        bias = jnp.where((step >= 0) & (step <= STEPS), -slope * (d * step).astype(F32), NEG_INF)
        bias_scr[0] = bias
        bias_scr[1] = jnp.where(col < blk, NEG_INF, bias)

        def group(j, carry, bi=bi, d=d, nb=nb, n_chunk=n_chunk, a_rows=a_rows):
            def chunks(g):
                it = j * ATTN_GROUP + g
                r = it // nb
                n = it % nb
                cur = [pl.ds(pl.multiple_of((r + d * c) * cls + n * a_rows, 8), a_rows) for c in range(n_chunk)]
                prev = [pl.ds(pl.multiple_of((r + d * c) * cls + jnp.maximum(n - 1, 0) * a_rows, 8), a_rows)
                        for c in range(n_chunk)]
                return n, cur, prev

            def gather(ref, sls):
                return jnp.concatenate([ref[sl, :] for sl in sls], axis=0)

            for g in range(ATTN_GROUP):
                n, cur, prev = chunks(g)
                qb = gather(qp_scr, cur)
                q2 = jnp.concatenate([jnp.where(head_a, qb, 0.0), jnp.where(head_a, 0.0, qb)], axis=0).astype(BF16)
                kc = gather(kp_scr, prev + cur).astype(BF16)
                s = lax.dot_general(q2, kc, (((1,), (1,)), ((), ())), preferred_element_type=F32)
                s_scr[g] = s + bias_scr[jnp.where(n == 0, 1, 0)]
            for g in range(ATTN_GROUP):
                _, cur, _ = chunks(g)
                s = s_scr[g]
                m = jnp.max(s, axis=1, keepdims=True)
                p = jnp.exp(s - m)
                l = jnp.sum(p, axis=1, keepdims=True)
                p_scr[g] = p.astype(BF16)
                m2 = jnp.where(head_a, m[:blk], m[blk:])
                l2 = jnp.where(head_a, l[:blk], l[blk:])
                for c, sl in enumerate(cur):
                    m_scr[bi, sl, :] = m2[c * a_rows:(c + 1) * a_rows]
                    l_scr[bi, sl, :] = l2[c * a_rows:(c + 1) * a_rows]
            for g in range(ATTN_GROUP):
                _, cur, prev = chunks(g)
                vc = gather(vp_scr, prev + cur).astype(BF16)
                o = jnp.dot(p_scr[g], vc, preferred_element_type=F32)
                o2 = jnp.where(head_a, o[:blk], o[blk:])
                for c, sl in enumerate(cur):
                    o_scr[bi, sl, :] = o2[c * a_rows:(c + 1) * a_rows]
            return carry

        lax.fori_loop(0, n_it // ATTN_GROUP, group, 0)

    def merge(c, carry):
        rs = pl.ds(pl.multiple_of(c * cls, cls), cls)
        ms = [m_scr[b, rs, :] for b in range(len(BRANCHES))]
        mx = functools.reduce(jnp.maximum, ms)
        num = jnp.zeros((cls, LANES), F32)
        den = jnp.zeros((cls, LANES), F32)
        for b in range(len(BRANCHES)):
            a = jnp.exp(ms[b] - mx)
            num = num + a * o_scr[b, rs, :]
            den = den + a * l_scr[b, rs, :]
        o_ref[0, pl.ds(c, cls, stride=MAX_DIL), :] = num / den
        return carry

    lax.fori_loop(0, MAX_DIL, merge, 0)


def _attn_prompt(slopes, q, k, v):
    b, s, hw = q.shape
    assert s % (STEPS * max(d for _, d in BRANCHES)) == 0, "sequence must be a multiple of the widest span"
    assert hw % LANES == 0 and LANES == 2 * HEAD_DIM
    spec = pl.BlockSpec((1, s, LANES), lambda i, j: (i, 0, j))
    nbr = len(BRANCHES)
    return pl.pallas_call(
        functools.partial(_attn_prompt_body, seq=s),
        grid=(b, hw // LANES),
        in_specs=[pl.BlockSpec(memory_space=pltpu.SMEM), spec, spec, spec],
        out_specs=spec,
        out_shape=jax.ShapeDtypeStruct((b, s, hw), F32),
        scratch_shapes=[pltpu.VMEM((s, LANES), F32)] * 3
        + [pltpu.VMEM((2, 2 * STEPS, 2 * STEPS), F32),
                        pltpu.VMEM((ATTN_GROUP, 2 * STEPS, 2 * STEPS), F32),
                        pltpu.VMEM((ATTN_GROUP, 2 * STEPS, 2 * STEPS), BF16)]
        + [pltpu.VMEM((nbr, s, LANES), F32)] * 3,
        compiler_params=_params("arbitrary", "arbitrary"), name="attn_prompt",
    )(slopes, q, k, v)


def _multiplicity(dist):
    mult = jnp.zeros(dist.shape, F32)
    for w, d in BRANCHES:
        assert d & (d - 1) == 0
        mult = mult + ((dist >= 0) & ((dist & (d - 1)) == 0) & (dist <= w)).astype(F32)
    return mult


def _attn_sample_body(slopes_ref, q_ref, kn_ref, vn_ref, kt_ref, vt_ref, o_ref, *, n_heads):
    t_new = q_ref.shape[1]
    past = kt_ref.shape[3]
    dist = past + lax.broadcasted_iota(I32, (t_new, past), 0) - lax.broadcasted_iota(I32, (t_new, past), 1)
    dist_n = lax.broadcasted_iota(I32, (t_new, t_new), 0) - lax.broadcasted_iota(I32, (t_new, t_new), 1)
    mult, mult_n = _multiplicity(dist), _multiplicity(dist_n)
    dist_f, dist_nf = dist.astype(F32), dist_n.astype(F32)
    nt = (((1,), (1,)), ((), ()))
    for h in range(n_heads):
        slope = slopes_ref[h]
        cols = slice(h * HEAD_DIM, (h + 1) * HEAD_DIM)
        qh = q_ref[0, :, cols].astype(BF16)
        kn = kn_ref[0, :, cols].astype(BF16)
        vn = vn_ref[0, :, cols].astype(BF16)
        s = jnp.dot(qh, kt_ref[0, h].astype(BF16), preferred_element_type=F32)
        s = jnp.where(mult > 0, s - slope * dist_f, NEG_INF)
        sn = lax.dot_general(qh, kn, nt, preferred_element_type=F32)
        sn = jnp.where(mult_n > 0, sn - slope * dist_nf, NEG_INF)
        m = jnp.maximum(jnp.max(s, axis=1, keepdims=True), jnp.max(sn, axis=1, keepdims=True))
        p = mult * jnp.exp(s - m)
        pn = mult_n * jnp.exp(sn - m)
        l = jnp.sum(p, axis=1, keepdims=True) + jnp.sum(pn, axis=1, keepdims=True)
        o = lax.dot_general(p.astype(BF16), vt_ref[0, h].astype(BF16), nt, preferred_element_type=F32)
        o = o + jnp.dot(pn.astype(BF16), vn, preferred_element_type=F32)
        o_ref[0, :, cols] = o / l


def _attn_sample(slopes, q, k_new, v_new, cache_kt, cache_vt):
    n, t, hw = q.shape
    _, h, e, past = cache_kt.shape
    assert past >= ATTN_REACH, "every strided key of every branch must exist in the window buffer"
    new = pl.BlockSpec((1, t, hw), lambda i: (i, 0, 0))
    old = pl.BlockSpec((1, h, e, past), lambda i: (i, 0, 0, 0))
    return pl.pallas_call(
        functools.partial(_attn_sample_body, n_heads=h),
        grid=(n,),
        in_specs=[pl.BlockSpec(memory_space=pltpu.SMEM), new, new, new, old, old],
        out_specs=new,
        out_shape=jax.ShapeDtypeStruct((n, t, hw), F32),
        compiler_params=_params("arbitrary"), name="attn_sample",
    )(slopes, q, k_new, v_new, cache_kt, cache_vt)


def _pool_windows(width):
    gw = width // len(POOL_WINDOWS)
    lane = lax.broadcasted_iota(I32, (1, width), 1)
    win = jnp.zeros((1, width), I32)
    for g, w in enumerate(POOL_WINDOWS):
        win = jnp.where((lane >= g * gw) & (lane < (g + 1) * gw), w, win)
    return win


def _pool_prompt_body(u_ref, w_ref, sc_ref, o_ref, ext_scr, *, tm):
    j = pl.program_id(1)
    width = u_ref.shape[2]
    pad = 2 * (POOL_STATE + 1)
    start = pl.multiple_of(j * tm, tm)
    lead = POOL_STATE + 1
    prev = u_ref[0, pl.ds(pl.multiple_of(jnp.maximum(start - lead, 0), lead), lead), :]
    ext_scr[0:pad - lead, :] = jnp.zeros((pad - lead, width), F32)
    ext_scr[pad - lead:pad, :] = jnp.where(j > 0, prev, 0.0)
    ext_scr[pad:, :] = u_ref[0, pl.ds(start, tm), :]
    win = _pool_windows(width)
    tok = ext_scr[pad:, :]
    acc = tok
    for i in range(1, max(POOL_WINDOWS)):
        acc = acc + jnp.where(i < win, ext_scr[pl.ds(pad - i, tm), :], 0.0)
    pos = start + lax.broadcasted_iota(I32, (tm, width), 0)
    cnt = jnp.minimum(win, pos + 1).astype(F32)
    diff = (acc / cnt - tok).astype(BF16)
    o_ref[0] = jnp.dot(diff, w_ref[...], preferred_element_type=F32) * sc_ref[...]


def _pool_prompt(u, w_bd, scale):
    b, s, w = u.shape
    tm = TM_PROJ
    return pl.pallas_call(
        functools.partial(_pool_prompt_body, tm=tm),
        grid=(b, s // tm),
        in_specs=[pl.BlockSpec((1, s, w), lambda i, j: (i, 0, 0)), pl.BlockSpec((w, w), lambda i, j: (0, 0)),
                  pl.BlockSpec((1, w), lambda i, j: (0, 0))],
        out_specs=pl.BlockSpec((1, tm, w), lambda i, j: (i, j, 0)),
        out_shape=jax.ShapeDtypeStruct((b, s, w), F32),
        scratch_shapes=[pltpu.VMEM((tm + 2 * (POOL_STATE + 1), w), F32)],
        compiler_params=_params("arbitrary", "arbitrary"), name="pool_prompt",
    )(u, w_bd, scale)


def _pool_sample_body(st_ref, u_ref, w_ref, sc_ref, o_ref, ns_ref):
    t_new, _, width = u_ref.shape
    n_state = st_ref.shape[0]
    win = _pool_windows(width)

    def row(k):
        return st_ref[k] if k < n_state else u_ref[k - n_state]

    for t in range(t_new):
        tok = u_ref[t]
        acc = tok
        for i in range(1, max(POOL_WINDOWS)):
            acc = acc + jnp.where(i < win, row(n_state + t - i), 0.0)
        diff = (acc / win.astype(F32) - tok).astype(BF16)
        o_ref[t] = jnp.dot(diff, w_ref[...], preferred_element_type=F32) * sc_ref[...]
    for k in range(n_state):
        ns_ref[k] = row(k + t_new)


def _pool_sample(state_t, u, w_bd, scale):
    n_state = state_t.shape[0]
    assert n_state >= POOL_STATE
    return pl.pallas_call(
        _pool_sample_body,
        out_shape=[jax.ShapeDtypeStruct(u.shape, F32), jax.ShapeDtypeStruct(state_t.shape, F32)],
        compiler_params=pltpu.CompilerParams(vmem_limit_bytes=VMEM_LIMIT), name="pool_sample",
    )(state_t, u, w_bd, scale)


ROUTE_F1, ROUTE_F2, ROUTE_G1, ROUTE_G2, ROUTE_R1, ROUTE_R2 = range(6)


def _split_bf16(x):
    hi = x.astype(BF16)
    return hi, (x - hi.astype(F32)).astype(BF16)


def _mix_route_body(x_ref, pool_ref, attn_ref, wo_ref, g_ref, wr_ref, br_ref, cnt_in_ref, h_ref, route_ref,
                    cnt_out_ref, carry_scr, *, n_groups, n_experts):
    i = pl.program_id(0)
    tm = x_ref.shape[0]
    pool_w = pool_ref.shape[1]

    @pl.when(i == 0)
    def _():
        carry_scr[...] = cnt_in_ref[...]

    h = x_ref[...]
    h = h + jnp.dot(pool_ref[...].astype(BF16), wo_ref[0:pool_w, :], preferred_element_type=F32)
    h = h + jnp.dot(attn_ref[...].astype(BF16), wo_ref[pool_w:, :], preferred_element_type=F32)
    h_ref[...] = h

    hn_hi, hn_lo = _split_bf16(_rms(h, g_ref[...]))
    w_hi, w_lo = _split_bf16(wr_ref[...])
    hi_both = jnp.dot(hn_hi, jnp.concatenate([w_hi, w_lo], axis=1), preferred_element_type=F32)
    logits = (hi_both[:, :LANES] + hi_both[:, LANES:]
              + jnp.dot(hn_lo, w_hi, preferred_element_type=F32)) + br_ref[...]

    lane = lax.broadcasted_iota(I32, (tm, LANES), 1).astype(F32)

    def first_lane(mask):
        return jnp.min(jnp.where(mask, lane, float(LANES)), axis=1, keepdims=True)

    is_g = lane < n_groups
    lg = jnp.where(is_g, logits, NEG_INF)
    mg = jnp.max(lg, axis=1, keepdims=True)
    p_sel = 1.0 / jnp.sum(jnp.exp(lg - mg), axis=1, keepdims=True)
    g_top = first_lane(lg == mg)
    lo = n_groups + g_top * n_experts
    in_grp = (lane >= lo) & (lane < lo + n_experts)
    le = jnp.where(in_grp, logits, NEG_INF)
    ee = jnp.exp(le - jnp.max(le, axis=1, keepdims=True))
    pe = ee / jnp.sum(ee, axis=1, keepdims=True)
    v1 = jnp.max(jnp.where(in_grp, pe, -1.0), axis=1, keepdims=True)
    i1 = first_lane(in_grp & (pe == v1))
    rest = in_grp & (lane != i1)
    v2 = jnp.max(jnp.where(rest, pe, -1.0), axis=1, keepdims=True)
    i2 = first_lane(rest & (pe == v2))
    gate1 = p_sel * (v1 / (v1 + v2))
    gate2 = p_sel * (v2 / (v1 + v2))

    sel1, sel2 = lane == i1, lane == i2
    onehot = (sel1 | sel2).astype(BF16)
    tri = (lax.broadcasted_iota(I32, (tm, tm), 1) < lax.broadcasted_iota(I32, (tm, tm), 0)).astype(BF16)
    running = jnp.dot(tri, onehot, preferred_element_type=F32) + carry_scr[...]
    rank1 = jnp.sum(jnp.where(sel1, running, 0.0), axis=1, keepdims=True)
    rank2 = jnp.sum(jnp.where(sel2, running, 0.0), axis=1, keepdims=True)
    carry_scr[...] = carry_scr[...] + jnp.sum(onehot.astype(F32), axis=0, keepdims=True)
    cnt_out_ref[...] = carry_scr[...]

    rec = jnp.zeros((tm, LANES), F32)
    for idx, val in ((ROUTE_F1, i1 - n_groups), (ROUTE_F2, i2 - n_groups),
                     (ROUTE_G1, gate1), (ROUTE_G2, gate2), (ROUTE_R1, rank1), (ROUTE_R2, rank2)):
        rec = jnp.where(lane == idx, val, rec)
    route_ref[...] = rec


def _mix_route(x, pool, attn, w_out_bf16, g_ffn, w_route, b_route, cnt_in, *, n_groups, n_experts):
    n, d = x.shape
    tm = TM_MIX
    row = lambda i: (i, 0)
    fix = lambda i: (0, 0)
    return pl.pallas_call(
        functools.partial(_mix_route_body, n_groups=n_groups, n_experts=n_experts),
        grid=(n // tm,),
        in_specs=[pl.BlockSpec((tm, d), row), pl.BlockSpec((tm, pool.shape[1]), row),
                  pl.BlockSpec((tm, attn.shape[1]), row), pl.BlockSpec(w_out_bf16.shape, fix),
                  pl.BlockSpec((1, d), fix), pl.BlockSpec(w_route.shape, fix), pl.BlockSpec((1, LANES), fix),
                  pl.BlockSpec((1, LANES), fix)],
        out_specs=[pl.BlockSpec((tm, d), row), pl.BlockSpec((tm, LANES), row), pl.BlockSpec((1, LANES), fix)],
        out_shape=[jax.ShapeDtypeStruct((n, d), F32), jax.ShapeDtypeStruct((n, LANES), F32),
                   jax.ShapeDtypeStruct((1, LANES), F32)],
        scratch_shapes=[pltpu.VMEM((1, LANES), F32)],
        compiler_params=_params("arbitrary"), name="mix_route",
    )(x, pool, attn, w_out_bf16, g_ffn, w_route, b_route, cnt_in)


def _dispatch_body(zlo_ref, zhi_ref, dest_ref, hp_ref, hs_ref, g_ref, xs_ref, xn_scr, zero_scr, sems, *, tiles_p,
                   n_steps):
    tm = hp_ref.shape[0]
    i = pl.program_id(0)
    zero_sem = 2 * TOP_K_INNER

    @pl.when(i == 0)
    def _():
        zero_scr[...] = jnp.zeros(zero_scr.shape, F32)
        for start in (True, False):
            def chunk(c, carry, start=start):
                cp = pltpu.make_async_copy(zero_scr, xs_ref.at[pl.ds(pl.multiple_of(c * ZERO_ROWS, ZERO_ROWS), ZERO_ROWS)],
                                           sems.at[zero_sem])
                cp.start() if start else cp.wait()
                return carry

            def segment(e, carry, chunk=chunk):
                return lax.fori_loop(zlo_ref[e], zhi_ref[e], chunk, carry)
            lax.fori_loop(0, zlo_ref.shape[0], segment, 0)

    buf = i % 2

    def wait_rows(b):
        for slot in range(TOP_K_INNER):
            pltpu.make_async_copy(xn_scr.at[b], xs_ref.at[pl.ds(0, tm)], sems.at[b * TOP_K_INNER + slot]).wait()

    @pl.when(i >= 2)
    def _():
        wait_rows(buf)

    @pl.when(i < tiles_p)
    def _():
        xn_scr[buf] = _rms(hp_ref[...], g_ref[...])

    @pl.when(i >= tiles_p)
    def _():
        xn_scr[buf] = _rms(hs_ref[...], g_ref[...])

    def send(r, c):
        for slot in range(TOP_K_INNER):
            d = dest_ref[0, 0, slot * tm + r]
            pltpu.make_async_copy(xn_scr.at[buf, pl.ds(r, 1)], xs_ref.at[pl.ds(d, 1)],
                                  sems.at[buf * TOP_K_INNER + slot]).start(priority=slot)
        return c

    lax.fori_loop(0, tm, send, 0, unroll=4)

    @pl.when(i == n_steps - 1)
    def _():
        if n_steps > 1:
            wait_rows(1 - buf)
        wait_rows(buf)


def _dispatch(zero_lo, zero_hi, dest, h_p, h_s, g_ffn, *, rows):
    tm = TM_ROW
    d = h_p.shape[1]
    tiles_p, tiles_s = h_p.shape[0] // tm, h_s.shape[0] // tm
    return pl.pallas_call(
        functools.partial(_dispatch_body, tiles_p=tiles_p, n_steps=tiles_p + tiles_s),
        grid_spec=pltpu.PrefetchScalarGridSpec(
            num_scalar_prefetch=2, grid=(tiles_p + tiles_s,),
            in_specs=[pl.BlockSpec((1, 1, TOP_K_INNER * tm), lambda i, lo, hi: (i, 0, 0), memory_space=pltpu.SMEM),
                      pl.BlockSpec((tm, d), lambda i, lo, hi: (jnp.minimum(i, tiles_p - 1), 0)),
                      pl.BlockSpec((tm, d), lambda i, lo, hi: (jnp.maximum(i - tiles_p, 0), 0)),
                      pl.BlockSpec((1, d), lambda i, lo, hi: (0, 0))],
            out_specs=pl.BlockSpec(memory_space=pl.ANY),
            scratch_shapes=[pltpu.VMEM((2, tm, d), F32), pltpu.VMEM((ZERO_ROWS, d), F32),
                            pltpu.SemaphoreType.DMA((2 * TOP_K_INNER + 1,))]),
        out_shape=jax.ShapeDtypeStruct((rows, d), F32),
        compiler_params=_params("arbitrary"), name="dispatch",
    )(zero_lo, zero_hi, dest, h_p, h_s, g_ffn)


def _moe_gemm_body(expert_ref, block_ref, valid_ref, x_ref, wg_ref, wu_ref, wd_ref, y_ref, wg_scr, wu_scr, wd_scr):
    t = pl.program_id(0)

    @pl.when((t == 0) | (expert_ref[t] != expert_ref[jnp.maximum(t - 1, 0)]))
    def _():
        wg_scr[...] = wg_ref[0].astype(BF16)
        wu_scr[...] = wu_ref[0].astype(BF16)
        wd_scr[...] = wd_ref[0].astype(BF16)

    @pl.when(valid_ref[t] > 0)
    def _():
        x = x_ref[...].astype(BF16)
        gate = jnp.dot(x, wg_scr[...], preferred_element_type=F32)
        up = jnp.dot(x, wu_scr[...], preferred_element_type=F32)
        mid = (gate * jax.nn.sigmoid(gate) * up).astype(BF16)
        y_ref[...] = jnp.dot(mid, wd_scr[...], preferred_element_type=F32)

    @pl.when(valid_ref[t] == 0)
    def _():
        y_ref[...] = jnp.zeros(y_ref.shape, F32)


def _moe_gemm(tile_expert, tile_block, tile_valid, xs, w_gate, w_up, w_down):
    rows, d = xs.shape
    _, _, f = w_gate.shape
    tm = TM_GEMM
    xmap = lambda t, e, b, v: (b[t], 0)
    return pl.pallas_call(
        _moe_gemm_body,
        grid_spec=pltpu.PrefetchScalarGridSpec(
            num_scalar_prefetch=3, grid=(rows // tm,),
            in_specs=[pl.BlockSpec((tm, d), xmap),
                      pl.BlockSpec((1, d, f), lambda t, e, b, v: (e[t], 0, 0)),
                      pl.BlockSpec((1, d, f), lambda t, e, b, v: (e[t], 0, 0)),
                      pl.BlockSpec((1, f, d), lambda t, e, b, v: (e[t], 0, 0))],
            out_specs=pl.BlockSpec((tm, d), lambda t, e, b, v: (t, 0)),
            scratch_shapes=[pltpu.VMEM((d, f), BF16), pltpu.VMEM((d, f), BF16), pltpu.VMEM((f, d), BF16)]),
        out_shape=jax.ShapeDtypeStruct((rows, d), F32),
        compiler_params=_params("arbitrary"), name="moe_gemm",
    )(tile_expert, tile_block, tile_valid, xs, w_gate, w_up, w_down)


def _combine_body(dest_ref, next_ref, h_ref, route_ref, g_ref, ys_ref, y_ref, rows_scr, sems, *, n_steps):
    tm = h_ref.shape[0]
    i = pl.program_id(0)
    buf = i % 2

    def fetch(idx_ref, b):
        def body(r, c):
            for slot in range(TOP_K_INNER):
                d = idx_ref[0, 0, slot * tm + r]
                pltpu.make_async_copy(ys_ref.at[pl.ds(d, 1)], rows_scr.at[b, slot, pl.ds(r, 1)],
                                      sems.at[b * TOP_K_INNER + slot]).start(priority=slot)
            return c
        lax.fori_loop(0, tm, body, 0, unroll=4)

    @pl.when(i == 0)
    def _():
        fetch(dest_ref, 0)

    @pl.when(i + 1 < n_steps)
    def _():
        fetch(next_ref, 1 - buf)

    for slot in range(TOP_K_INNER):
        pltpu.make_async_copy(ys_ref.at[pl.ds(0, tm)], rows_scr.at[buf, slot], sems.at[buf * TOP_K_INNER + slot]).wait()
    route = route_ref[...]
    out = h_ref[...] + (route[:, ROUTE_G1:ROUTE_G1 + 1] * rows_scr[buf, 0]
                        + route[:, ROUTE_G2:ROUTE_G2 + 1] * rows_scr[buf, 1])
    y_ref[...] = _rms(out, g_ref[...])


def _combine(dest, h, route, g_final, ys):
    n, d = h.shape
    tm = TM_ROW
    n_steps = n // tm
    row = lambda i: (i, 0)
    idx_block = (1, 1, TOP_K_INNER * tm)
    return pl.pallas_call(
        functools.partial(_combine_body, n_steps=n_steps),
        grid=(n_steps,),
        in_specs=[pl.BlockSpec(idx_block, lambda i: (i, 0, 0), memory_space=pltpu.SMEM),
                  pl.BlockSpec(idx_block, lambda i: (jnp.minimum(i + 1, n_steps - 1), 0, 0), memory_space=pltpu.SMEM),
                  pl.BlockSpec((tm, d), row), pl.BlockSpec((tm, LANES), row), pl.BlockSpec((1, d), lambda i: (0, 0)),
                  pl.BlockSpec(memory_space=pl.ANY)],
        out_specs=pl.BlockSpec((tm, d), row),
        out_shape=jax.ShapeDtypeStruct((n, d), F32),
        scratch_shapes=[pltpu.VMEM((2, TOP_K_INNER, tm, d), F32), pltpu.SemaphoreType.DMA((2 * TOP_K_INNER,))],
        compiler_params=_params("arbitrary"), name="combine",
    )(dest, dest, h, route, g_final, ys)


def _sort_tables(counts, n_tiles):
    padded = ((counts + TM_GEMM - 1) // TM_GEMM) * TM_GEMM
    ends = jnp.cumsum(padded)
    offsets = ends - padded
    total = ends[-1]
    tile_start = jnp.arange(n_tiles, dtype=I32) * TM_GEMM
    tile_valid = (tile_start < total).astype(I32)
    last_block = jnp.maximum(total // TM_GEMM - 1, 0)
    tile_block = jnp.minimum(jnp.arange(n_tiles, dtype=I32), last_block)
    n_flat = counts.shape[0]
    tile_expert = jnp.minimum(jnp.sum((tile_block[:, None] * TM_GEMM >= ends[None, :]).astype(I32), axis=1), n_flat - 1)
    zero_lo = jnp.concatenate([(offsets + counts) // ZERO_ROWS, total[None] // ZERO_ROWS]).astype(I32)
    zero_hi = jnp.concatenate([ends // ZERO_ROWS, jnp.full((1,), n_tiles * TM_GEMM // ZERO_ROWS, I32)]).astype(I32)
    return offsets, tile_expert.astype(I32), tile_block.astype(I32), tile_valid, zero_lo, zero_hi


def _dest_blocks(route, offsets):
    n = route.shape[0]
    f = route[:, ROUTE_F1:ROUTE_F2 + 1].astype(I32)
    rank = route[:, ROUTE_R1:ROUTE_R2 + 1].astype(I32)
    dest = jnp.sum(jnp.where(f[..., None] == jnp.arange(offsets.shape[0], dtype=I32), offsets, 0), axis=-1) + rank
    return dest.reshape(n // TM_ROW, TM_ROW, TOP_K_INNER).transpose(0, 2, 1).reshape(n // TM_ROW, 1, TOP_K_INNER * TM_ROW)


def kernel(x_prompt, x_sample, cache_k, cache_v, state_pool, g_mix, w_in, w_pool, pool_scale, w_out, g_ffn,
           w_router_group, b_router_group, w_router_expert, b_router_expert, w_gate, w_up, w_down, g_final):
    depth = g_mix.shape[0]
    assert depth == 1, "single-layer step"
    b, s, d = x_prompt.shape
    nd, t_new, _ = x_sample.shape
    n_heads = cache_k.shape[3]
    attn_w = n_heads * HEAD_DIM
    pool_w = state_pool.shape[3]
    past = cache_k.shape[2]
    keep = min(ATTN_REACH, s)
    n_groups, n_experts = w_router_expert.shape[1], w_router_expert.shape[3]
    n_flat = n_groups * n_experts
    assert n_groups + n_flat <= LANES
    slopes = _alibi_slopes(n_heads)

    w_in_b = w_in[0].astype(BF16)
    w_out_b = w_out[0].astype(BF16)
    gw = pool_w // len(POOL_WINDOWS)
    w_bd = jnp.zeros((pool_w, pool_w), F32)
    for g in range(len(POOL_WINDOWS)):
        w_bd = w_bd.at[g * gw:(g + 1) * gw, g * gw:(g + 1) * gw].set(w_pool[0, g])
    w_bd = w_bd.astype(BF16)
    w_route = jnp.concatenate([w_router_group[0], jnp.transpose(w_router_expert[0], (1, 0, 2)).reshape(d, n_flat)], axis=1)
    w_route = jnp.pad(w_route, ((0, 0), (0, LANES - n_groups - n_flat)))
    b_route = jnp.pad(jnp.concatenate([b_router_group[0], b_router_expert[0].reshape(n_flat)]),
                      (0, LANES - n_groups - n_flat))[None]
    w_gate_f, w_up_f = w_gate[0].reshape(n_flat, d, -1), w_up[0].reshape(n_flat, d, -1)
    w_down_f = w_down[0].reshape(n_flat, -1, d)

    n_p = b * s
    u_p, q_p, k_p, v_p, kt_p, vt_p = _proj_in(x_prompt.reshape(n_p, d), g_mix, w_in_b, pool_w=pool_w, attn_w=attn_w,
                                               seq=s, keep=keep)
    attn_p = _attn_prompt(slopes, q_p.reshape(b, s, attn_w), k_p.reshape(b, s, attn_w), v_p.reshape(b, s, attn_w))
    u_p3 = u_p.reshape(b, s, pool_w)
    pool_p = _pool_prompt(u_p3, w_bd, pool_scale)

    n_s = nd * t_new
    u_s, q_s, k_s, v_s = _proj_in(x_sample.reshape(n_s, d), g_mix, w_in_b, pool_w=pool_w, attn_w=attn_w)
    cache_kt = jnp.transpose(cache_k[0], (0, 2, 3, 1))
    cache_vt = jnp.transpose(cache_v[0], (0, 2, 3, 1))
    as3 = lambda a: a.reshape(nd, t_new, attn_w)
    attn_s = _attn_sample(slopes, as3(q_s), as3(k_s), as3(v_s), cache_kt, cache_vt)
    state_t = jnp.transpose(state_pool[0], (1, 0, 2))
    u_st = jnp.transpose(u_s.reshape(nd, t_new, pool_w), (1, 0, 2))
    pool_st, new_state_t = _pool_sample(state_t, u_st, w_bd, pool_scale)
    pool_s = jnp.transpose(pool_st, (1, 0, 2)).reshape(n_s, pool_w)

    route_kw = dict(n_groups=n_groups, n_experts=n_experts)
    h_p, route_p, cnt_p = _mix_route(x_prompt.reshape(n_p, d), pool_p.reshape(n_p, pool_w), attn_p.reshape(n_p, attn_w),
                                     w_out_b, g_ffn, w_route, b_route, jnp.zeros((1, LANES), F32), **route_kw)
    h_s, route_s, cnt_all = _mix_route(x_sample.reshape(n_s, d), pool_s, attn_s.reshape(n_s, attn_w),
                                       w_out_b, g_ffn, w_route, b_route, cnt_p, **route_kw)

    counts = cnt_all[0, n_groups:n_groups + n_flat].astype(I32)
    n_tiles = (TOP_K_INNER * (n_p + n_s) + n_flat * (TM_GEMM - 1)) // TM_GEMM
    offsets, tile_expert, tile_block, tile_valid, zero_lo, zero_hi = _sort_tables(counts, n_tiles)
    dest_p, dest_s = _dest_blocks(route_p, offsets), _dest_blocks(route_s, offsets)

    xs = _dispatch(zero_lo, zero_hi, jnp.concatenate([dest_p, dest_s], axis=0), h_p, h_s, g_ffn,
                   rows=n_tiles * TM_GEMM)
    ys = _moe_gemm(tile_expert, tile_block, tile_valid, xs, w_gate_f, w_up_f, w_down_f)
    y_p = _combine(dest_p, h_p, route_p, g_final[None], ys)
    y_s = _combine(dest_s, h_s, route_s, g_final[None], ys)

    y_prompt = y_p.reshape(b, s, d)
    y_sample = y_s.reshape(nd, t_new, d)
    k_prompt = jnp.transpose(kt_p.reshape(b, n_heads, HEAD_DIM, keep), (0, 3, 1, 2))[None]
    v_prompt = jnp.transpose(vt_p.reshape(b, n_heads, HEAD_DIM, keep), (0, 3, 1, 2))[None]
    pool_prompt = u_p3[:, s - POOL_STATE:][None]
    k_sample = k_s.reshape(1, nd, t_new, n_heads, HEAD_DIM)
    v_sample = v_s.reshape(1, nd, t_new, n_heads, HEAD_DIM)
    pool_sample = jnp.transpose(new_state_t[-POOL_STATE:], (1, 0, 2))[None]
    return (y_prompt, y_sample, k_prompt, v_prompt, pool_prompt, k_sample, v_sample, pool_sample)
```

```python
import functools
import math

import numpy as np
import jax
import jax.numpy as jnp
from jax import lax
from jax.experimental import pallas as pl
from jax.experimental.pallas import tpu as pltpu

F32 = jnp.float32
BF16 = jnp.bfloat16
I32 = jnp.int32

HEAD_DIM = 64
POOL_WINDOWS = (2, 4, 8, 16)
POOL_STATE = max(POOL_WINDOWS) - 1
BRANCHES = ((128, 1), (512, 4), (2048, 16))
STEPS = BRANCHES[0][0] // BRANCHES[0][1]
ATTN_REACH = max(w for w, _ in BRANCHES)
MAX_DIL = max(d for _, d in BRANCHES)
TOP_K_INNER = 2
RMS_EPS = 1e-6
LANES = 128
NEG_INF = float("-inf")

VMEM_LIMIT = 56 * 1024 * 1024

TM_PROJ = 512
TM_MIX = 512
TM_ROW = 512
TM_GEMM = 512
ZERO_ROWS = 64
ATTN_GROUP = 8


def _alibi_slopes(n_heads):
    def geometric(n):
        start = 2.0 ** (-8.0 / n)
        return [start ** (i + 1) for i in range(n)]
    c = 2 ** int(math.floor(math.log2(n_heads)))
    s = geometric(c)
    if c < n_heads:
        s = s + geometric(2 * c)[0::2][: n_heads - c]
    return jnp.asarray(s, dtype=F32)


def _rms(x, g):
    return x * lax.rsqrt(jnp.mean(x * x, axis=-1, keepdims=True) + RMS_EPS) * g


def _params(*sem):
    return pltpu.CompilerParams(dimension_semantics=sem, vmem_limit_bytes=VMEM_LIMIT)


def _proj_in_body(x_ref, g_ref, w_ref, u_ref, q_ref, k_ref, v_ref, *t_refs, pool_w, attn_w, tiles_per_seq,
                  keep_tiles):
    xn = _rms(x_ref[...], g_ref[...]).astype(BF16)

    def proj(lo, n):
        return jnp.dot(xn, w_ref[:, lo:lo + n], preferred_element_type=F32)

    u_ref[...] = proj(0, pool_w)
    q_ref[...] = proj(pool_w, attn_w) * (HEAD_DIM ** -0.5)
    k = proj(pool_w + attn_w, attn_w)
    v = proj(pool_w + 2 * attn_w, attn_w)
    k_ref[...] = k
    v_ref[...] = v
    if t_refs:
        kt_ref, vt_ref = t_refs
        j = pl.program_id(0) % tiles_per_seq

        @pl.when(j >= tiles_per_seq - keep_tiles)
        def _():
            kt_ref[0] = k.T
            vt_ref[0] = v.T


def _proj_in(x, g, w_bf16, *, pool_w, attn_w, seq=None, keep=None):
    n, d = x.shape
    tm = TM_PROJ
    grid = (n // tm,)
    row = lambda i: (i, 0)
    out_shape = [jax.ShapeDtypeStruct((n, pool_w), F32)] + [jax.ShapeDtypeStruct((n, attn_w), F32)] * 3
    out_specs = [pl.BlockSpec((tm, pool_w), row)] + [pl.BlockSpec((tm, attn_w), row)] * 3
    tiles_per_seq = keep_tiles = 0
    if seq is not None:
        tiles_per_seq, keep_tiles = seq // tm, keep // tm
        first = tiles_per_seq - keep_tiles
        tmap = lambda i: (i // tiles_per_seq, 0, jnp.maximum(i % tiles_per_seq - first, 0))
        out_shape += [jax.ShapeDtypeStruct((n // seq, attn_w, keep), F32)] * 2
        out_specs += [pl.BlockSpec((1, attn_w, tm), tmap)] * 2
    body = functools.partial(_proj_in_body, pool_w=pool_w, attn_w=attn_w, tiles_per_seq=tiles_per_seq,
                             keep_tiles=keep_tiles)
    return pl.pallas_call(
        body, grid=grid,
        in_specs=[pl.BlockSpec((tm, d), row), pl.BlockSpec((1, d), lambda i: (0, 0)),
                  pl.BlockSpec(w_bf16.shape, lambda i: (0, 0))],
        out_specs=out_specs, out_shape=out_shape,
        compiler_params=_params("arbitrary"), name="proj_in",
    )(x, g, w_bf16)


def _attn_prompt_body(slopes_ref, q_ref, k_ref, v_ref, o_ref, qp_scr, kp_scr, vp_scr, bias_scr, s_scr, p_scr, o_scr,
                      m_scr, l_scr, *, seq):
    hp = pl.program_id(1)
    blk = STEPS
    n_it = seq // blk
    cls = seq // MAX_DIL

    for c in range(MAX_DIL):
        for src, dst in ((q_ref, qp_scr), (k_ref, kp_scr), (v_ref, vp_scr)):
            dst[c * cls:(c + 1) * cls, :] = src[0, pl.ds(c, cls, stride=MAX_DIL), :]

    lane = lax.broadcasted_iota(I32, (blk, LANES), 1)
    head_a = lane < HEAD_DIM
    row = lax.broadcasted_iota(I32, (2 * blk, 2 * blk), 0)
    col = lax.broadcasted_iota(I32, (2 * blk, 2 * blk), 1)
    slope = jnp.where(row < blk, slopes_ref[2 * hp], slopes_ref[2 * hp + 1])

    for bi, (_, d) in enumerate(BRANCHES):
        nb = n_it // d
        n_chunk = MAX_DIL // d
        a_rows = blk // n_chunk
        sh = a_rows.bit_length() - 1
        assert a_rows == 1 << sh and a_rows % 8 == 0

        def seq_index(i, n_chunk=n_chunk, a_rows=a_rows, sh=sh):
            i = i & (blk - 1)
            return (i & (a_rows - 1)) * n_chunk + (i >> sh)

        step = seq_index(row) + blk - (seq_index(col) + (col & blk))
        bias = jnp.where((step >= 0) & (step <= STEPS), -slope * (d * step).astype(F32), NEG_INF)
        bias_scr[0] = bias
        bias_scr[1] = jnp.where(col < blk, NEG_INF, bias)

        def group(j, carry, bi=bi, d=d, nb=nb, n_chunk=n_chunk, a_rows=a_rows):
            def chunks(g):
                it = j * ATTN_GROUP + g
                r = it // nb
                n = it % nb
                cur = [pl.ds(pl.multiple_of((r + d * c) * cls + n * a_rows, 8), a_rows) for c in range(n_chunk)]
                prev = [pl.ds(pl.multiple_of((r + d * c) * cls + jnp.maximum(n - 1, 0) * a_rows, 8), a_rows)
                        for c in range(n_chunk)]
                return n, cur, prev

            def gather(ref, sls):
                return jnp.concatenate([ref[sl, :] for sl in sls], axis=0)

            for g in range(ATTN_GROUP):
                n, cur, prev = chunks(g)
                qb = gather(qp_scr, cur)
                q2 = jnp.concatenate([jnp.where(head_a, qb, 0.0), jnp.where(head_a, 0.0, qb)], axis=0).astype(BF16)
                kc = gather(kp_scr, prev + cur).astype(BF16)
                s = lax.dot_general(q2, kc, (((1,), (1,)), ((), ())), preferred_element_type=F32)
                s_scr[g] = s + bias_scr[jnp.where(n == 0, 1, 0)]
            for g in range(ATTN_GROUP):
                _, cur, _ = chunks(g)
                s = s_scr[g]
                m = jnp.max(s, axis=1, keepdims=True)
                p = jnp.exp(s - m)
                l = jnp.sum(p, axis=1, keepdims=True)
                p_scr[g] = p.astype(BF16)
                m2 = jnp.where(head_a, m[:blk], m[blk:])
                l2 = jnp.where(head_a, l[:blk], l[blk:])
                for c, sl in enumerate(cur):
                    m_scr[bi, sl, :] = m2[c * a_rows:(c + 1) * a_rows]
                    l_scr[bi, sl, :] = l2[c * a_rows:(c + 1) * a_rows]
            for g in range(ATTN_GROUP):
                _, cur, prev = chunks(g)
                vc = gather(vp_scr, prev + cur).astype(BF16)
                o = jnp.dot(p_scr[g], vc, preferred_element_type=F32)
                o2 = jnp.where(head_a, o[:blk], o[blk:])
                for c, sl in enumerate(cur):
                    o_scr[bi, sl, :] = o2[c * a_rows:(c + 1) * a_rows]
            return carry

        lax.fori_loop(0, n_it // ATTN_GROUP, group, 0)

    def merge(c, carry):
        rs = pl.ds(pl.multiple_of(c * cls, cls), cls)
        ms = [m_scr[b, rs, :] for b in range(len(BRANCHES))]
        mx = functools.reduce(jnp.maximum, ms)
        num = jnp.zeros((cls, LANES), F32)
        den = jnp.zeros((cls, LANES), F32)
        for b in range(len(BRANCHES)):
            a = jnp.exp(ms[b] - mx)
            num = num + a * o_scr[b, rs, :]
            den = den + a * l_scr[b, rs, :]
        o_ref[0, pl.ds(c, cls, stride=MAX_DIL), :] = num / den
        return carry

    lax.fori_loop(0, MAX_DIL, merge, 0)


def _attn_prompt(slopes, q, k, v):
    b, s, hw = q.shape
    assert s % (STEPS * max(d for _, d in BRANCHES)) == 0, "sequence must be a multiple of the widest span"
    assert hw % LANES == 0 and LANES == 2 * HEAD_DIM
    spec = pl.BlockSpec((1, s, LANES), lambda i, j: (i, 0, j))
    nbr = len(BRANCHES)
    return pl.pallas_call(
        functools.partial(_attn_prompt_body, seq=s),
        grid=(b, hw // LANES),
        in_specs=[pl.BlockSpec(memory_space=pltpu.SMEM), spec, spec, spec],
        out_specs=spec,
        out_shape=jax.ShapeDtypeStruct((b, s, hw), F32),
        scratch_shapes=[pltpu.VMEM((s, LANES), F32)] * 3
        + [pltpu.VMEM((2, 2 * STEPS, 2 * STEPS), F32),
                        pltpu.VMEM((ATTN_GROUP, 2 * STEPS, 2 * STEPS), F32),
                        pltpu.VMEM((ATTN_GROUP, 2 * STEPS, 2 * STEPS), BF16)]
        + [pltpu.VMEM((nbr, s, LANES), F32)] * 3,
        compiler_params=_params("arbitrary", "arbitrary"), name="attn_prompt",
    )(slopes, q, k, v)


def _multiplicity(dist):
    mult = jnp.zeros(dist.shape, F32)
    for w, d in BRANCHES:
        assert d & (d - 1) == 0
        mult = mult + ((dist >= 0) & ((dist & (d - 1)) == 0) & (dist <= w)).astype(F32)
    return mult


def _attn_sample_body(slopes_ref, q_ref, kn_ref, vn_ref, kt_ref, vt_ref, o_ref, *, n_heads):
    t_new = q_ref.shape[1]
    past = kt_ref.shape[3]
    dist = past + lax.broadcasted_iota(I32, (t_new, past), 0) - lax.broadcasted_iota(I32, (t_new, past), 1)
    dist_n = lax.broadcasted_iota(I32, (t_new, t_new), 0) - lax.broadcasted_iota(I32, (t_new, t_new), 1)
    mult, mult_n = _multiplicity(dist), _multiplicity(dist_n)
    dist_f, dist_nf = dist.astype(F32), dist_n.astype(F32)
    nt = (((1,), (1,)), ((), ()))
    for h in range(n_heads):
        slope = slopes_ref[h]
        cols = slice(h * HEAD_DIM, (h + 1) * HEAD_DIM)
        qh = q_ref[0, :, cols].astype(BF16)
        kn = kn_ref[0, :, cols].astype(BF16)
        vn = vn_ref[0, :, cols].astype(BF16)
        s = jnp.dot(qh, kt_ref[0, h].astype(BF16), preferred_element_type=F32)
        s = jnp.where(mult > 0, s - slope * dist_f, NEG_INF)
        sn = lax.dot_general(qh, kn, nt, preferred_element_type=F32)
        sn = jnp.where(mult_n > 0, sn - slope * dist_nf, NEG_INF)
        m = jnp.maximum(jnp.max(s, axis=1, keepdims=True), jnp.max(sn, axis=1, keepdims=True))
        p = mult * jnp.exp(s - m)
        pn = mult_n * jnp.exp(sn - m)
        l = jnp.sum(p, axis=1, keepdims=True) + jnp.sum(pn, axis=1, keepdims=True)
        o = lax.dot_general(p.astype(BF16), vt_ref[0, h].astype(BF16), nt, preferred_element_type=F32)
        o = o + jnp.dot(pn.astype(BF16), vn, preferred_element_type=F32)
        o_ref[0, :, cols] = o / l


def _attn_sample(slopes, q, k_new, v_new, cache_kt, cache_vt):
    n, t, hw = q.shape
    _, h, e, past = cache_kt.shape
    assert past >= ATTN_REACH, "every strided key of every branch must exist in the window buffer"
    new = pl.BlockSpec((1, t, hw), lambda i: (i, 0, 0))
    old = pl.BlockSpec((1, h, e, past), lambda i: (i, 0, 0, 0))
    return pl.pallas_call(
        functools.partial(_attn_sample_body, n_heads=h),
        grid=(n,),
        in_specs=[pl.BlockSpec(memory_space=pltpu.SMEM), new, new, new, old, old],
        out_specs=new,
        out_shape=jax.ShapeDtypeStruct((n, t, hw), F32),
        compiler_params=_params("arbitrary"), name="attn_sample",
    )(slopes, q, k_new, v_new, cache_kt, cache_vt)


def _pool_windows(width):
    gw = width // len(POOL_WINDOWS)
    lane = lax.broadcasted_iota(I32, (1, width), 1)
    win = jnp.zeros((1, width), I32)
    for g, w in enumerate(POOL_WINDOWS):
        win = jnp.where((lane >= g * gw) & (lane < (g + 1) * gw), w, win)
    return win


def _pool_prompt_body(u_ref, w_ref, sc_ref, o_ref, ext_scr, *, tm):
    j = pl.program_id(1)
    width = u_ref.shape[2]
    pad = 2 * (POOL_STATE + 1)
    start = pl.multiple_of(j * tm, tm)
    lead = POOL_STATE + 1
    prev = u_ref[0, pl.ds(pl.multiple_of(jnp.maximum(start - lead, 0), lead), lead), :]
    ext_scr[0:pad - lead, :] = jnp.zeros((pad - lead, width), F32)
    ext_scr[pad - lead:pad, :] = jnp.where(j > 0, prev, 0.0)
    ext_scr[pad:, :] = u_ref[0, pl.ds(start, tm), :]
    win = _pool_windows(width)
    tok = ext_scr[pad:, :]
    acc = tok
    for i in range(1, max(POOL_WINDOWS)):
        acc = acc + jnp.where(i < win, ext_scr[pl.ds(pad - i, tm), :], 0.0)
    pos = start + lax.broadcasted_iota(I32, (tm, width), 0)
    cnt = jnp.minimum(win, pos + 1).astype(F32)
    diff = (acc / cnt - tok).astype(BF16)
    o_ref[0] = jnp.dot(diff, w_ref[...], preferred_element_type=F32) * sc_ref[...]


def _pool_prompt(u, w_bd, scale):
    b, s, w = u.shape
    tm = TM_PROJ
    return pl.pallas_call(
        functools.partial(_pool_prompt_body, tm=tm),
        grid=(b, s // tm),
        in_specs=[pl.BlockSpec((1, s, w), lambda i, j: (i, 0, 0)), pl.BlockSpec((w, w), lambda i, j: (0, 0)),
                  pl.BlockSpec((1, w), lambda i, j: (0, 0))],
        out_specs=pl.BlockSpec((1, tm, w), lambda i, j: (i, j, 0)),
        out_shape=jax.ShapeDtypeStruct((b, s, w), F32),
        scratch_shapes=[pltpu.VMEM((tm + 2 * (POOL_STATE + 1), w), F32)],
        compiler_params=_params("arbitrary", "arbitrary"), name="pool_prompt",
    )(u, w_bd, scale)


def _pool_sample_body(st_ref, u_ref, w_ref, sc_ref, o_ref, ns_ref):
    t_new, _, width = u_ref.shape
    n_state = st_ref.shape[0]
    win = _pool_windows(width)

    def row(k):
        return st_ref[k] if k < n_state else u_ref[k - n_state]

    for t in range(t_new):
        tok = u_ref[t]
        acc = tok
        for i in range(1, max(POOL_WINDOWS)):
            acc = acc + jnp.where(i < win, row(n_state + t - i), 0.0)
        diff = (acc / win.astype(F32) - tok).astype(BF16)
        o_ref[t] = jnp.dot(diff, w_ref[...], preferred_element_type=F32) * sc_ref[...]
    for k in range(n_state):
        ns_ref[k] = row(k + t_new)


def _pool_sample(state_t, u, w_bd, scale):
    n_state = state_t.shape[0]
    assert n_state >= POOL_STATE
    return pl.pallas_call(
        _pool_sample_body,
        out_shape=[jax.ShapeDtypeStruct(u.shape, F32), jax.ShapeDtypeStruct(state_t.shape, F32)],
        compiler_params=pltpu.CompilerParams(vmem_limit_bytes=VMEM_LIMIT), name="pool_sample",
    )(state_t, u, w_bd, scale)


ROUTE_F1, ROUTE_F2, ROUTE_G1, ROUTE_G2, ROUTE_R1, ROUTE_R2 = range(6)


def _split_bf16(x):
    hi = x.astype(BF16)
    return hi, (x - hi.astype(F32)).astype(BF16)


def _mix_route_body(x_ref, pool_ref, attn_ref, wo_ref, g_ref, wr_ref, br_ref, cnt_in_ref, h_ref, route_ref,
                    cnt_out_ref, carry_scr, *, n_groups, n_experts):
    i = pl.program_id(0)
    tm = x_ref.shape[0]
    pool_w = pool_ref.shape[1]

    @pl.when(i == 0)
    def _():
        carry_scr[...] = cnt_in_ref[...]

    h = x_ref[...]
    h = h + jnp.dot(pool_ref[...].astype(BF16), wo_ref[0:pool_w, :], preferred_element_type=F32)
    h = h + jnp.dot(attn_ref[...].astype(BF16), wo_ref[pool_w:, :], preferred_element_type=F32)
    h_ref[...] = h

    hn_hi, hn_lo = _split_bf16(_rms(h, g_ref[...]))
    w_hi, w_lo = _split_bf16(wr_ref[...])
    hi_both = jnp.dot(hn_hi, jnp.concatenate([w_hi, w_lo], axis=1), preferred_element_type=F32)
    logits = (hi_both[:, :LANES] + hi_both[:, LANES:]
              + jnp.dot(hn_lo, w_hi, preferred_element_type=F32)) + br_ref[...]

    lane = lax.broadcasted_iota(I32, (tm, LANES), 1).astype(F32)

    def first_lane(mask):
        return jnp.min(jnp.where(mask, lane, float(LANES)), axis=1, keepdims=True)

    is_g = lane < n_groups
    lg = jnp.where(is_g, logits, NEG_INF)
    mg = jnp.max(lg, axis=1, keepdims=True)
    p_sel = 1.0 / jnp.sum(jnp.exp(lg - mg), axis=1, keepdims=True)
    g_top = first_lane(lg == mg)
    lo = n_groups + g_top * n_experts
    in_grp = (lane >= lo) & (lane < lo + n_experts)
    le = jnp.where(in_grp, logits, NEG_INF)
    ee = jnp.exp(le - jnp.max(le, axis=1, keepdims=True))
    pe = ee / jnp.sum(ee, axis=1, keepdims=True)
    v1 = jnp.max(jnp.where(in_grp, pe, -1.0), axis=1, keepdims=True)
    i1 = first_lane(in_grp & (pe == v1))
    rest = in_grp & (lane != i1)
    v2 = jnp.max(jnp.where(rest, pe, -1.0), axis=1, keepdims=True)
    i2 = first_lane(rest & (pe == v2))
    gate1 = p_sel * (v1 / (v1 + v2))
    gate2 = p_sel * (v2 / (v1 + v2))

    sel1, sel2 = lane == i1, lane == i2
    onehot = (sel1 | sel2).astype(BF16)
    tri = (lax.broadcasted_iota(I32, (tm, tm), 1) < lax.broadcasted_iota(I32, (tm, tm), 0)).astype(BF16)
    running = jnp.dot(tri, onehot, preferred_element_type=F32) + carry_scr[...]
    rank1 = jnp.sum(jnp.where(sel1, running, 0.0), axis=1, keepdims=True)
    rank2 = jnp.sum(jnp.where(sel2, running, 0.0), axis=1, keepdims=True)
    carry_scr[...] = carry_scr[...] + jnp.sum(onehot.astype(F32), axis=0, keepdims=True)
    cnt_out_ref[...] = carry_scr[...]

    rec = jnp.zeros((tm, LANES), F32)
    for idx, val in ((ROUTE_F1, i1 - n_groups), (ROUTE_F2, i2 - n_groups),
                     (ROUTE_G1, gate1), (ROUTE_G2, gate2), (ROUTE_R1, rank1), (ROUTE_R2, rank2)):
        rec = jnp.where(lane == idx, val, rec)
    route_ref[...] = rec


def _mix_route(x, pool, attn, w_out_bf16, g_ffn, w_route, b_route, cnt_in, *, n_groups, n_experts):
    n, d = x.shape
    tm = TM_MIX
    row = lambda i: (i, 0)
    fix = lambda i: (0, 0)
    return pl.pallas_call(
        functools.partial(_mix_route_body, n_groups=n_groups, n_experts=n_experts),
        grid=(n // tm,),
        in_specs=[pl.BlockSpec((tm, d), row), pl.BlockSpec((tm, pool.shape[1]), row),
                  pl.BlockSpec((tm, attn.shape[1]), row), pl.BlockSpec(w_out_bf16.shape, fix),
                  pl.BlockSpec((1, d), fix), pl.BlockSpec(w_route.shape, fix), pl.BlockSpec((1, LANES), fix),
                  pl.BlockSpec((1, LANES), fix)],
        out_specs=[pl.BlockSpec((tm, d), row), pl.BlockSpec((tm, LANES), row), pl.BlockSpec((1, LANES), fix)],
        out_shape=[jax.ShapeDtypeStruct((n, d), F32), jax.ShapeDtypeStruct((n, LANES), F32),
                   jax.ShapeDtypeStruct((1, LANES), F32)],
        scratch_shapes=[pltpu.VMEM((1, LANES), F32)],
        compiler_params=_params("arbitrary"), name="mix_route",
    )(x, pool, attn, w_out_bf16, g_ffn, w_route, b_route, cnt_in)


def _dispatch_body(zlo_ref, zhi_ref, dest_ref, hp_ref, hs_ref, g_ref, xs_ref, xn_scr, zero_scr, sems, *, tiles_p,
                   n_steps):
    tm = hp_ref.shape[0]
    i = pl.program_id(0)
    zero_sem = 2 * TOP_K_INNER

    @pl.when(i == 0)
    def _():
        zero_scr[...] = jnp.zeros(zero_scr.shape, F32)
        for start in (True, False):
            def chunk(c, carry, start=start):
                cp = pltpu.make_async_copy(zero_scr, xs_ref.at[pl.ds(pl.multiple_of(c * ZERO_ROWS, ZERO_ROWS), ZERO_ROWS)],
                                           sems.at[zero_sem])
                cp.start() if start else cp.wait()
                return carry

            def segment(e, carry, chunk=chunk):
                return lax.fori_loop(zlo_ref[e], zhi_ref[e], chunk, carry)
            lax.fori_loop(0, zlo_ref.shape[0], segment, 0)

    buf = i % 2

    def wait_rows(b):
        for slot in range(TOP_K_INNER):
            pltpu.make_async_copy(xn_scr.at[b], xs_ref.at[pl.ds(0, tm)], sems.at[b * TOP_K_INNER + slot]).wait()

    @pl.when(i >= 2)
    def _():
        wait_rows(buf)

    @pl.when(i < tiles_p)
    def _():
        xn_scr[buf] = _rms(hp_ref[...], g_ref[...])

    @pl.when(i >= tiles_p)
    def _():
        xn_scr[buf] = _rms(hs_ref[...], g_ref[...])

    for r in range(tm):
        for slot in range(TOP_K_INNER):
            d = dest_ref[0, 0, slot * tm + r]
            pltpu.make_async_copy(xn_scr.at[buf, pl.ds(r, 1)], xs_ref.at[pl.ds(d, 1)],
                                  sems.at[buf * TOP_K_INNER + slot]).start(priority=slot)

    @pl.when(i == n_steps - 1)
    def _():
        if n_steps > 1:
            wait_rows(1 - buf)
        wait_rows(buf)


def _dispatch(zero_lo, zero_hi, dest, h_p, h_s, g_ffn, *, rows):
    tm = TM_ROW
    d = h_p.shape[1]
    tiles_p, tiles_s = h_p.shape[0] // tm, h_s.shape[0] // tm
    return pl.pallas_call(
        functools.partial(_dispatch_body, tiles_p=tiles_p, n_steps=tiles_p + tiles_s),
        grid_spec=pltpu.PrefetchScalarGridSpec(
            num_scalar_prefetch=2, grid=(tiles_p + tiles_s,),
            in_specs=[pl.BlockSpec((1, 1, TOP_K_INNER * tm), lambda i, lo, hi: (i, 0, 0), memory_space=pltpu.SMEM),
                      pl.BlockSpec((tm, d), lambda i, lo, hi: (jnp.minimum(i, tiles_p - 1), 0)),
                      pl.BlockSpec((tm, d), lambda i, lo, hi: (jnp.maximum(i - tiles_p, 0), 0)),
                      pl.BlockSpec((1, d), lambda i, lo, hi: (0, 0))],
            out_specs=pl.BlockSpec(memory_space=pl.ANY),
            scratch_shapes=[pltpu.VMEM((2, tm, d), F32), pltpu.VMEM((ZERO_ROWS, d), F32),
                            pltpu.SemaphoreType.DMA((2 * TOP_K_INNER + 1,))]),
        out_shape=jax.ShapeDtypeStruct((rows, d), F32),
        compiler_params=_params("arbitrary"), name="dispatch",
    )(zero_lo, zero_hi, dest, h_p, h_s, g_ffn)


def _moe_gemm_body(expert_ref, block_ref, valid_ref, x_ref, wg_ref, wu_ref, wd_ref, y_ref, wg_scr, wu_scr, wd_scr):
    t = pl.program_id(0)

    @pl.when((t == 0) | (expert_ref[t] != expert_ref[jnp.maximum(t - 1, 0)]))
    def _():
        wg_scr[...] = wg_ref[0].astype(BF16)
        wu_scr[...] = wu_ref[0].astype(BF16)
        wd_scr[...] = wd_ref[0].astype(BF16)

    @pl.when(valid_ref[t] > 0)
    def _():
        x = x_ref[...].astype(BF16)
        gate = jnp.dot(x, wg_scr[...], preferred_element_type=F32)
        up = jnp.dot(x, wu_scr[...], preferred_element_type=F32)
        mid = (gate * jax.nn.sigmoid(gate) * up).astype(BF16)
        y_ref[...] = jnp.dot(mid, wd_scr[...], preferred_element_type=F32)

    @pl.when(valid_ref[t] == 0)
    def _():
        y_ref[...] = jnp.zeros(y_ref.shape, F32)


def _moe_gemm(tile_expert, tile_block, tile_valid, xs, w_gate, w_up, w_down):
    rows, d = xs.shape
    _, _, f = w_gate.shape
    tm = TM_GEMM
    xmap = lambda t, e, b, v: (b[t], 0)
    return pl.pallas_call(
        _moe_gemm_body,
        grid_spec=pltpu.PrefetchScalarGridSpec(
            num_scalar_prefetch=3, grid=(rows // tm,),
            in_specs=[pl.BlockSpec((tm, d), xmap),
                      pl.BlockSpec((1, d, f), lambda t, e, b, v: (e[t], 0, 0)),
                      pl.BlockSpec((1, d, f), lambda t, e, b, v: (e[t], 0, 0)),
                      pl.BlockSpec((1, f, d), lambda t, e, b, v: (e[t], 0, 0))],
            out_specs=pl.BlockSpec((tm, d), lambda t, e, b, v: (t, 0)),
            scratch_shapes=[pltpu.VMEM((d, f), BF16), pltpu.VMEM((d, f), BF16), pltpu.VMEM((f, d), BF16)]),
        out_shape=jax.ShapeDtypeStruct((rows, d), F32),
        compiler_params=_params("arbitrary"), name="moe_gemm",
    )(tile_expert, tile_block, tile_valid, xs, w_gate, w_up, w_down)


def _combine_body(dest_ref, next_ref, h_ref, route_ref, g_ref, ys_ref, y_ref, rows_scr, sems, *, n_steps):
    tm = h_ref.shape[0]
    i = pl.program_id(0)
    buf = i % 2

    def fetch(idx_ref, b):
        for r in range(tm):
            for slot in range(TOP_K_INNER):
                d = idx_ref[0, 0, slot * tm + r]
                pltpu.make_async_copy(ys_ref.at[pl.ds(d, 1)], rows_scr.at[b, slot, pl.ds(r, 1)],
                                      sems.at[b * TOP_K_INNER + slot]).start(priority=slot)

    @pl.when(i == 0)
    def _():
        fetch(dest_ref, 0)

    @pl.when(i + 1 < n_steps)
    def _():
        fetch(next_ref, 1 - buf)

    for slot in range(TOP_K_INNER):
        pltpu.make_async_copy(ys_ref.at[pl.ds(0, tm)], rows_scr.at[buf, slot], sems.at[buf * TOP_K_INNER + slot]).wait()
    route = route_ref[...]
    out = h_ref[...] + (route[:, ROUTE_G1:ROUTE_G1 + 1] * rows_scr[buf, 0]
                        + route[:, ROUTE_G2:ROUTE_G2 + 1] * rows_scr[buf, 1])
    y_ref[...] = _rms(out, g_ref[...])


def _combine(dest, h, route, g_final, ys):
    n, d = h.shape
    tm = TM_ROW
    n_steps = n // tm
    row = lambda i: (i, 0)
    idx_block = (1, 1, TOP_K_INNER * tm)
    return pl.pallas_call(
        functools.partial(_combine_body, n_steps=n_steps),
        grid=(n_steps,),
        in_specs=[pl.BlockSpec(idx_block, lambda i: (i, 0, 0), memory_space=pltpu.SMEM),
                  pl.BlockSpec(idx_block, lambda i: (jnp.minimum(i + 1, n_steps - 1), 0, 0), memory_space=pltpu.SMEM),
                  pl.BlockSpec((tm, d), row), pl.BlockSpec((tm, LANES), row), pl.BlockSpec((1, d), lambda i: (0, 0)),
                  pl.BlockSpec(memory_space=pl.ANY)],
        out_specs=pl.BlockSpec((tm, d), row),
        out_shape=jax.ShapeDtypeStruct((n, d), F32),
        scratch_shapes=[pltpu.VMEM((2, TOP_K_INNER, tm, d), F32), pltpu.SemaphoreType.DMA((2 * TOP_K_INNER,))],
        compiler_params=_params("arbitrary"), name="combine",
    )(dest, dest, h, route, g_final, ys)


def _sort_tables(counts, n_tiles):
    padded = ((counts + TM_GEMM - 1) // TM_GEMM) * TM_GEMM
    ends = jnp.cumsum(padded)
    offsets = ends - padded
    total = ends[-1]
    tile_start = jnp.arange(n_tiles, dtype=I32) * TM_GEMM
    tile_valid = (tile_start < total).astype(I32)
    last_block = jnp.maximum(total // TM_GEMM - 1, 0)
    tile_block = jnp.minimum(jnp.arange(n_tiles, dtype=I32), last_block)
    n_flat = counts.shape[0]
    tile_expert = jnp.minimum(jnp.sum((tile_block[:, None] * TM_GEMM >= ends[None, :]).astype(I32), axis=1), n_flat - 1)
    zero_lo = jnp.concatenate([(offsets + counts) // ZERO_ROWS, total[None] // ZERO_ROWS]).astype(I32)
    zero_hi = jnp.concatenate([ends // ZERO_ROWS, jnp.full((1,), n_tiles * TM_GEMM // ZERO_ROWS, I32)]).astype(I32)
    return offsets, tile_expert.astype(I32), tile_block.astype(I32), tile_valid, zero_lo, zero_hi


def _dest_blocks(route, offsets):
    n = route.shape[0]
    f = route[:, ROUTE_F1:ROUTE_F2 + 1].astype(I32)
    rank = route[:, ROUTE_R1:ROUTE_R2 + 1].astype(I32)
    dest = jnp.sum(jnp.where(f[..., None] == jnp.arange(offsets.shape[0], dtype=I32), offsets, 0), axis=-1) + rank
    return dest.reshape(n // TM_ROW, TM_ROW, TOP_K_INNER).transpose(0, 2, 1).reshape(n // TM_ROW, 1, TOP_K_INNER * TM_ROW)


def kernel(x_prompt, x_sample, cache_k, cache_v, state_pool, g_mix, w_in, w_pool, pool_scale, w_out, g_ffn,
           w_router_group, b_router_group, w_router_expert, b_router_expert, w_gate, w_up, w_down, g_final):
    depth = g_mix.shape[0]
    assert depth == 1, "single-layer step"
    b, s, d = x_prompt.shape
    nd, t_new, _ = x_sample.shape
    n_heads = cache_k.shape[3]
    attn_w = n_heads * HEAD_DIM
    pool_w = state_pool.shape[3]
    past = cache_k.shape[2]
    keep = min(ATTN_REACH, s)
    n_groups, n_experts = w_router_expert.shape[1], w_router_expert.shape[3]
    n_flat = n_groups * n_experts
    assert n_groups + n_flat <= LANES
    slopes = _alibi_slopes(n_heads)

    w_in_b = w_in[0].astype(BF16)
    w_out_b = w_out[0].astype(BF16)
    gw = pool_w // len(POOL_WINDOWS)
    w_bd = jnp.zeros((pool_w, pool_w), F32)
    for g in range(len(POOL_WINDOWS)):
        w_bd = w_bd.at[g * gw:(g + 1) * gw, g * gw:(g + 1) * gw].set(w_pool[0, g])
    w_bd = w_bd.astype(BF16)
    w_route = jnp.concatenate([w_router_group[0], jnp.transpose(w_router_expert[0], (1, 0, 2)).reshape(d, n_flat)], axis=1)
    w_route = jnp.pad(w_route, ((0, 0), (0, LANES - n_groups - n_flat)))
    b_route = jnp.pad(jnp.concatenate([b_router_group[0], b_router_expert[0].reshape(n_flat)]),
                      (0, LANES - n_groups - n_flat))[None]
    w_gate_f, w_up_f = w_gate[0].reshape(n_flat, d, -1), w_up[0].reshape(n_flat, d, -1)
    w_down_f = w_down[0].reshape(n_flat, -1, d)

    n_p = b * s
    u_p, q_p, k_p, v_p, kt_p, vt_p = _proj_in(x_prompt.reshape(n_p, d), g_mix, w_in_b, pool_w=pool_w, attn_w=attn_w,
                                               seq=s, keep=keep)
    attn_p = _attn_prompt(slopes, q_p.reshape(b, s, attn_w), k_p.reshape(b, s, attn_w), v_p.reshape(b, s, attn_w))
    u_p3 = u_p.reshape(b, s, pool_w)
    pool_p = _pool_prompt(u_p3, w_bd, pool_scale)

    n_s = nd * t_new
    u_s, q_s, k_s, v_s = _proj_in(x_sample.reshape(n_s, d), g_mix, w_in_b, pool_w=pool_w, attn_w=attn_w)
    cache_kt = jnp.transpose(cache_k[0], (0, 2, 3, 1))
    cache_vt = jnp.transpose(cache_v[0], (0, 2, 3, 1))
    as3 = lambda a: a.reshape(nd, t_new, attn_w)
    attn_s = _attn_sample(slopes, as3(q_s), as3(k_s), as3(v_s), cache_kt, cache_vt)
    state_t = jnp.transpose(state_pool[0], (1, 0, 2))
    u_st = jnp.transpose(u_s.reshape(nd, t_new, pool_w), (1, 0, 2))
    pool_st, new_state_t = _pool_sample(state_t, u_st, w_bd, pool_scale)
    pool_s = jnp.transpose(pool_st, (1, 0, 2)).reshape(n_s, pool_w)

    route_kw = dict(n_groups=n_groups, n_experts=n_experts)
    h_p, route_p, cnt_p = _mix_route(x_prompt.reshape(n_p, d), pool_p.reshape(n_p, pool_w), attn_p.reshape(n_p, attn_w),
                                     w_out_b, g_ffn, w_route, b_route, jnp.zeros((1, LANES), F32), **route_kw)
    h_s, route_s, cnt_all = _mix_route(x_sample.reshape(n_s, d), pool_s, attn_s.reshape(n_s, attn_w),
                                       w_out_b, g_ffn, w_route, b_route, cnt_p, **route_kw)

    counts = cnt_all[0, n_groups:n_groups + n_flat].astype(I32)
    n_tiles = (TOP_K_INNER * (n_p + n_s) + n_flat * (TM_GEMM - 1)) // TM_GEMM
    offsets, tile_expert, tile_block, tile_valid, zero_lo, zero_hi = _sort_tables(counts, n_tiles)
    dest_p, dest_s = _dest_blocks(route_p, offsets), _dest_blocks(route_s, offsets)

    xs = _dispatch(zero_lo, zero_hi, jnp.concatenate([dest_p, dest_s], axis=0), h_p, h_s, g_ffn,
                   rows=n_tiles * TM_GEMM)
    ys = _moe_gemm(tile_expert, tile_block, tile_valid, xs, w_gate_f, w_up_f, w_down_f)
    y_p = _combine(dest_p, h_p, route_p, g_final[None], ys)
    y_s = _combine(dest_s, h_s, route_s, g_final[None], ys)

    y_prompt = y_p.reshape(b, s, d)
    y_sample = y_s.reshape(nd, t_new, d)
    k_prompt = jnp.transpose(kt_p.reshape(b, n_heads, HEAD_DIM, keep), (0, 3, 1, 2))[None]
    v_prompt = jnp.transpose(vt_p.reshape(b, n_heads, HEAD_DIM, keep), (0, 3, 1, 2))[None]
    pool_prompt = u_p3[:, s - POOL_STATE:][None]
    k_sample = k_s.reshape(1, nd, t_new, n_heads, HEAD_DIM)
    v_sample = v_s.reshape(1, nd, t_new, n_heads, HEAD_DIM)
    pool_sample = jnp.transpose(new_state_t[-POOL_STATE:], (1, 0, 2))[None]
    return (y_prompt, y_sample, k_prompt, v_prompt, pool_prompt, k_sample, v_sample, pool_sample)
```

```python
import functools
import math

import numpy as np
import jax
import jax.numpy as jnp
from jax import lax
from jax.experimental import pallas as pl
from jax.experimental.pallas import tpu as pltpu

F32 = jnp.float32
BF16 = jnp.bfloat16
I32 = jnp.int32

HEAD_DIM = 64
POOL_WINDOWS = (2, 4, 8, 16)
POOL_STATE = max(POOL_WINDOWS) - 1
BRANCHES = ((128, 1), (512, 4), (2048, 16))
STEPS = BRANCHES[0][0] // BRANCHES[0][1]
ATTN_REACH = max(w for w, _ in BRANCHES)
MAX_DIL = max(d for _, d in BRANCHES)
TOP_K_INNER = 2
RMS_EPS = 1e-6
LANES = 128
NEG_INF = float("-inf")

VMEM_LIMIT = 56 * 1024 * 1024

TM_PROJ = 512
TM_MIX = 512
TM_ROW = 512
TM_GEMM = 512
ZERO_ROWS = 64
ATTN_GROUP = 8


def _alibi_slopes(n_heads):
    def geometric(n):
        start = 2.0 ** (-8.0 / n)
        return [start ** (i + 1) for i in range(n)]
    c = 2 ** int(math.floor(math.log2(n_heads)))
    s = geometric(c)
    if c < n_heads:
        s = s + geometric(2 * c)[0::2][: n_heads - c]
    return jnp.asarray(s, dtype=F32)


def _rms(x, g):
    return x * lax.rsqrt(jnp.mean(x * x, axis=-1, keepdims=True) + RMS_EPS) * g


def _params(*sem):
    return pltpu.CompilerParams(dimension_semantics=sem, vmem_limit_bytes=VMEM_LIMIT)


def _proj_in_body(x_ref, g_ref, w_ref, u_ref, q_ref, k_ref, v_ref, *t_refs, pool_w, attn_w, tiles_per_seq,
                  keep_tiles):
    xn = _rms(x_ref[...], g_ref[...]).astype(BF16)

    def proj(lo, n):
        return jnp.dot(xn, w_ref[:, lo:lo + n], preferred_element_type=F32)

    u_ref[...] = proj(0, pool_w)
    q_ref[...] = proj(pool_w, attn_w) * (HEAD_DIM ** -0.5)
    k = proj(pool_w + attn_w, attn_w)
    v = proj(pool_w + 2 * attn_w, attn_w)
    k_ref[...] = k
    v_ref[...] = v
    if t_refs:
        kt_ref, vt_ref = t_refs
        j = pl.program_id(0) % tiles_per_seq

        @pl.when(j >= tiles_per_seq - keep_tiles)
        def _():
            kt_ref[0] = k.T
            vt_ref[0] = v.T


def _proj_in(x, g, w_bf16, *, pool_w, attn_w, seq=None, keep=None):
    n, d = x.shape
    tm = TM_PROJ
    grid = (n // tm,)
    row = lambda i: (i, 0)
    out_shape = [jax.ShapeDtypeStruct((n, pool_w), F32)] + [jax.ShapeDtypeStruct((n, attn_w), F32)] * 3
    out_specs = [pl.BlockSpec((tm, pool_w), row)] + [pl.BlockSpec((tm, attn_w), row)] * 3
    tiles_per_seq = keep_tiles = 0
    if seq is not None:
        tiles_per_seq, keep_tiles = seq // tm, keep // tm
        first = tiles_per_seq - keep_tiles
        tmap = lambda i: (i // tiles_per_seq, 0, jnp.maximum(i % tiles_per_seq - first, 0))
        out_shape += [jax.ShapeDtypeStruct((n // seq, attn_w, keep), F32)] * 2
        out_specs += [pl.BlockSpec((1, attn_w, tm), tmap)] * 2
    body = functools.partial(_proj_in_body, pool_w=pool_w, attn_w=attn_w, tiles_per_seq=tiles_per_seq,
                             keep_tiles=keep_tiles)
    return pl.pallas_call(
        body, grid=grid,
        in_specs=[pl.BlockSpec((tm, d), row), pl.BlockSpec((1, d), lambda i: (0, 0)),
                  pl.BlockSpec(w_bf16.shape, lambda i: (0, 0))],
        out_specs=out_specs, out_shape=out_shape,
        compiler_params=_params("arbitrary"), name="proj_in",
    )(x, g, w_bf16)


def _attn_prompt_body(slopes_ref, q_hbm, k_hbm, v_hbm, o_ref, qkv_scr, bias_scr, s_scr, p_scr, o_scr, m_scr, l_scr, sems,
                      *, seq, n_pairs, n_steps):
    hp = pl.program_id(1)
    gstep = pl.program_id(0) * n_pairs + hp
    buf = gstep % 2
    blk = STEPS
    n_it = seq // blk
    cls = seq // MAX_DIL

    def class_copies(st, b):
        lanes = pl.ds(pl.multiple_of((st % n_pairs) * LANES, LANES), LANES)
        return [pltpu.make_async_copy(src.at[st // n_pairs, :, c, lanes], qkv_scr.at[b, a, pl.ds(c * cls, cls), :],
                                      sems.at[b, a])
                for a, src in enumerate((q_hbm, k_hbm, v_hbm)) for c in range(MAX_DIL)]

    @pl.when(gstep == 0)
    def _():
        for cp in class_copies(gstep, 0):
            cp.start()

    @pl.when(gstep + 1 < n_steps)
    def _():
        for cp in class_copies(gstep + 1, 1 - buf):
            cp.start()

    for cp in class_copies(gstep, buf):
        cp.wait()
    qp_scr, kp_scr, vp_scr = (qkv_scr.at[buf, a] for a in range(3))

    lane = lax.broadcasted_iota(I32, (blk, LANES), 1)
    head_a = lane < HEAD_DIM
    row = lax.broadcasted_iota(I32, (2 * blk, 2 * blk), 0)
    col = lax.broadcasted_iota(I32, (2 * blk, 2 * blk), 1)
    slope = jnp.where(row < blk, slopes_ref[2 * hp], slopes_ref[2 * hp + 1])

    for bi, (_, d) in enumerate(BRANCHES):
        nb = n_it // d
        n_chunk = MAX_DIL // d
        a_rows = blk // n_chunk
        sh = a_rows.bit_length() - 1
        assert a_rows == 1 << sh and a_rows % 8 == 0

        def seq_index(i, n_chunk=n_chunk, a_rows=a_rows, sh=sh):
            i = i & (blk - 1)
            return (i & (a_rows - 1)) * n_chunk + (i >> sh)

        step = seq_index(row) + blk - (seq_index(col) + (col & blk))
        bias = jnp.where((step >= 0) & (step <= STEPS), -slope * (d * step).astype(F32), NEG_INF)
        bias_scr[0] = bias
        bias_scr[1] = jnp.where(col < blk, NEG_INF, bias)

        def group(j, carry, bi=bi, d=d, nb=nb, n_chunk=n_chunk, a_rows=a_rows):
            def chunks(g):
                it = j * ATTN_GROUP + g
                r = it // nb
                n = it % nb
                cur = [pl.ds(pl.multiple_of((r + d * c) * cls + n * a_rows, 8), a_rows) for c in range(n_chunk)]
                prev = [pl.ds(pl.multiple_of((r + d * c) * cls + jnp.maximum(n - 1, 0) * a_rows, 8), a_rows)
                        for c in range(n_chunk)]
                return n, cur, prev

            def gather(ref, sls):
                return jnp.concatenate([ref[sl, :] for sl in sls], axis=0)

            for g in range(ATTN_GROUP):
                n, cur, prev = chunks(g)
                qb = gather(qp_scr, cur)
                q2 = jnp.concatenate([jnp.where(head_a, qb, 0.0), jnp.where(head_a, 0.0, qb)], axis=0).astype(BF16)
                kc = gather(kp_scr, prev + cur).astype(BF16)
                s = lax.dot_general(q2, kc, (((1,), (1,)), ((), ())), preferred_element_type=F32)
                s_scr[g] = s + bias_scr[jnp.where(n == 0, 1, 0)]
            for g in range(ATTN_GROUP):
                _, cur, _ = chunks(g)
                s = s_scr[g]
                m = jnp.max(s, axis=1, keepdims=True)
                p_scr[g] = jnp.exp(s - m).astype(BF16)
                m2 = jnp.where(head_a, m[:blk], m[blk:])
                for c, sl in enumerate(cur):
                    m_scr[bi, sl, :] = m2[c * a_rows:(c + 1) * a_rows]
            ones = jnp.ones((2 * blk, LANES), BF16)
            for g in range(ATTN_GROUP):
                _, cur, prev = chunks(g)
                vc = jnp.concatenate([gather(vp_scr, prev + cur).astype(BF16), ones], axis=1)
                ol = jnp.dot(p_scr[g], vc, preferred_element_type=F32)
                o2 = jnp.where(head_a, ol[:blk, :LANES], ol[blk:, :LANES])
                l2 = jnp.where(head_a, ol[:blk, LANES:], ol[blk:, LANES:])
                for c, sl in enumerate(cur):
                    o_scr[bi, sl, :] = o2[c * a_rows:(c + 1) * a_rows]
                    l_scr[bi, sl, :] = l2[c * a_rows:(c + 1) * a_rows]
            return carry

        lax.fori_loop(0, n_it // ATTN_GROUP, group, 0)

    def merge(c, carry):
        rs = pl.ds(pl.multiple_of(c * cls, cls), cls)
        ms = [m_scr[b, rs, :] for b in range(len(BRANCHES))]
        mx = functools.reduce(jnp.maximum, ms)
        num = jnp.zeros((cls, LANES), F32)
        den = jnp.zeros((cls, LANES), F32)
        for b in range(len(BRANCHES)):
            a = jnp.exp(ms[b] - mx)
            num = num + a * o_scr[b, rs, :]
            den = den + a * l_scr[b, rs, :]
        o_ref[0, pl.ds(c, cls, stride=MAX_DIL), :] = num / den
        return carry

    lax.fori_loop(0, MAX_DIL, merge, 0)


def _attn_prompt(slopes, q, k, v):
    b, s, hw = q.shape
    assert s % (STEPS * max(d for _, d in BRANCHES)) == 0, "sequence must be a multiple of the widest span"
    assert hw % LANES == 0 and LANES == 2 * HEAD_DIM
    n_pairs = hw // LANES
    nbr = len(BRANCHES)
    by_class = lambda a: a.reshape(b, s // MAX_DIL, MAX_DIL, hw)
    hbm = pl.BlockSpec(memory_space=pl.ANY)
    return pl.pallas_call(
        functools.partial(_attn_prompt_body, seq=s, n_pairs=n_pairs, n_steps=b * n_pairs),
        grid=(b, n_pairs),
        in_specs=[pl.BlockSpec(memory_space=pltpu.SMEM), hbm, hbm, hbm],
        out_specs=pl.BlockSpec((1, s, LANES), lambda i, j: (i, 0, j)),
        out_shape=jax.ShapeDtypeStruct((b, s, hw), F32),
        scratch_shapes=[pltpu.VMEM((2, 3, s, LANES), F32),
                        pltpu.VMEM((2, 2 * STEPS, 2 * STEPS), F32),
                        pltpu.VMEM((ATTN_GROUP, 2 * STEPS, 2 * STEPS), F32),
                        pltpu.VMEM((ATTN_GROUP, 2 * STEPS, 2 * STEPS), BF16)]
        + [pltpu.VMEM((nbr, s, LANES), F32)] * 3 + [pltpu.SemaphoreType.DMA((2, 3))],
        compiler_params=_params("arbitrary", "arbitrary"), name="attn_prompt",
    )(slopes, by_class(q), by_class(k), by_class(v))


def _multiplicity(dist):
    mult = jnp.zeros(dist.shape, F32)
    for w, d in BRANCHES:
        assert d & (d - 1) == 0
        mult = mult + ((dist >= 0) & ((dist & (d - 1)) == 0) & (dist <= w)).astype(F32)
    return mult


def _attn_sample_body(slopes_ref, q_ref, kn_ref, vn_ref, kt_ref, vt_ref, o_ref, *, n_heads):
    t_new = q_ref.shape[1]
    past = kt_ref.shape[3]
    dist = past + lax.broadcasted_iota(I32, (t_new, past), 0) - lax.broadcasted_iota(I32, (t_new, past), 1)
    dist_n = lax.broadcasted_iota(I32, (t_new, t_new), 0) - lax.broadcasted_iota(I32, (t_new, t_new), 1)
    mult, mult_n = _multiplicity(dist), _multiplicity(dist_n)
    dist_f, dist_nf = dist.astype(F32), dist_n.astype(F32)
    nt = (((1,), (1,)), ((), ()))
    for h in range(n_heads):
        slope = slopes_ref[h]
        cols = slice(h * HEAD_DIM, (h + 1) * HEAD_DIM)
        qh = q_ref[0, :, cols].astype(BF16)
        kn = kn_ref[0, :, cols].astype(BF16)
        vn = vn_ref[0, :, cols].astype(BF16)
        s = jnp.dot(qh, kt_ref[0, h].astype(BF16), preferred_element_type=F32)
        s = jnp.where(mult > 0, s - slope * dist_f, NEG_INF)
        sn = lax.dot_general(qh, kn, nt, preferred_element_type=F32)
        sn = jnp.where(mult_n > 0, sn - slope * dist_nf, NEG_INF)
        m = jnp.maximum(jnp.max(s, axis=1, keepdims=True), jnp.max(sn, axis=1, keepdims=True))
        p = mult * jnp.exp(s - m)
        pn = mult_n * jnp.exp(sn - m)
        l = jnp.sum(p, axis=1, keepdims=True) + jnp.sum(pn, axis=1, keepdims=True)
        o = lax.dot_general(p.astype(BF16), vt_ref[0, h].astype(BF16), nt, preferred_element_type=F32)
        o = o + jnp.dot(pn.astype(BF16), vn, preferred_element_type=F32)
        o_ref[0, :, cols] = o / l


def _attn_sample(slopes, q, k_new, v_new, cache_kt, cache_vt):
    n, t, hw = q.shape
    _, h, e, past = cache_kt.shape
    assert past >= ATTN_REACH, "every strided key of every branch must exist in the window buffer"
    new = pl.BlockSpec((1, t, hw), lambda i: (i, 0, 0))
    old = pl.BlockSpec((1, h, e, past), lambda i: (i, 0, 0, 0))
    return pl.pallas_call(
        functools.partial(_attn_sample_body, n_heads=h),
        grid=(n,),
        in_specs=[pl.BlockSpec(memory_space=pltpu.SMEM), new, new, new, old, old],
        out_specs=new,
        out_shape=jax.ShapeDtypeStruct((n, t, hw), F32),
        compiler_params=_params("arbitrary"), name="attn_sample",
    )(slopes, q, k_new, v_new, cache_kt, cache_vt)


def _pool_windows(width):
    gw = width // len(POOL_WINDOWS)
    lane = lax.broadcasted_iota(I32, (1, width), 1)
    win = jnp.zeros((1, width), I32)
    for g, w in enumerate(POOL_WINDOWS):
        win = jnp.where((lane >= g * gw) & (lane < (g + 1) * gw), w, win)
    return win


def _pool_prompt_body(u_ref, w_ref, sc_ref, o_ref, ext_scr, *, tm):
    j = pl.program_id(1)
    width = u_ref.shape[2]
    pad = 2 * (POOL_STATE + 1)
    start = pl.multiple_of(j * tm, tm)
    lead = POOL_STATE + 1
    prev = u_ref[0, pl.ds(pl.multiple_of(jnp.maximum(start - lead, 0), lead), lead), :]
    ext_scr[0:pad - lead, :] = jnp.zeros((pad - lead, width), F32)
    ext_scr[pad - lead:pad, :] = jnp.where(j > 0, prev, 0.0)
    ext_scr[pad:, :] = u_ref[0, pl.ds(start, tm), :]
    win = _pool_windows(width)
    tok = ext_scr[pad:, :]
    acc = tok
    for i in range(1, max(POOL_WINDOWS)):
        acc = acc + jnp.where(i < win, ext_scr[pl.ds(pad - i, tm), :], 0.0)
    pos = start + lax.broadcasted_iota(I32, (tm, width), 0)
    cnt = jnp.minimum(win, pos + 1).astype(F32)
    diff = (acc / cnt - tok).astype(BF16)
    o_ref[0] = jnp.dot(diff, w_ref[...], preferred_element_type=F32) * sc_ref[...]


def _pool_prompt(u, w_bd, scale):
    b, s, w = u.shape
    tm = TM_PROJ
    return pl.pallas_call(
        functools.partial(_pool_prompt_body, tm=tm),
        grid=(b, s // tm),
        in_specs=[pl.BlockSpec((1, s, w), lambda i, j: (i, 0, 0)), pl.BlockSpec((w, w), lambda i, j: (0, 0)),
                  pl.BlockSpec((1, w), lambda i, j: (0, 0))],
        out_specs=pl.BlockSpec((1, tm, w), lambda i, j: (i, j, 0)),
        out_shape=jax.ShapeDtypeStruct((b, s, w), F32),
        scratch_shapes=[pltpu.VMEM((tm + 2 * (POOL_STATE + 1), w), F32)],
        compiler_params=_params("arbitrary", "arbitrary"), name="pool_prompt",
    )(u, w_bd, scale)


def _pool_sample_body(st_ref, u_ref, w_ref, sc_ref, o_ref, ns_ref):
    t_new, _, width = u_ref.shape
    n_state = st_ref.shape[0]
    win = _pool_windows(width)

    def row(k):
        return st_ref[k] if k < n_state else u_ref[k - n_state]

    for t in range(t_new):
        tok = u_ref[t]
        acc = tok
        for i in range(1, max(POOL_WINDOWS)):
            acc = acc + jnp.where(i < win, row(n_state + t - i), 0.0)
        diff = (acc / win.astype(F32) - tok).astype(BF16)
        o_ref[t] = jnp.dot(diff, w_ref[...], preferred_element_type=F32) * sc_ref[...]
    for k in range(n_state):
        ns_ref[k] = row(k + t_new)


def _pool_sample(state_t, u, w_bd, scale):
    n_state = state_t.shape[0]
    assert n_state >= POOL_STATE
    return pl.pallas_call(
        _pool_sample_body,
        out_shape=[jax.ShapeDtypeStruct(u.shape, F32), jax.ShapeDtypeStruct(state_t.shape, F32)],
        compiler_params=pltpu.CompilerParams(vmem_limit_bytes=VMEM_LIMIT), name="pool_sample",
    )(state_t, u, w_bd, scale)


ROUTE_F1, ROUTE_F2, ROUTE_G1, ROUTE_G2, ROUTE_R1, ROUTE_R2 = range(6)


def _split_bf16(x):
    hi = x.astype(BF16)
    return hi, (x - hi.astype(F32)).astype(BF16)


def _mix_route_body(x_ref, pool_ref, attn_ref, wo_ref, g_ref, wr_ref, br_ref, cnt_in_ref, h_ref, route_ref,
                    cnt_out_ref, carry_scr, *, n_groups, n_experts):
    i = pl.program_id(0)
    tm = x_ref.shape[0]
    pool_w = pool_ref.shape[1]

    @pl.when(i == 0)
    def _():
        carry_scr[...] = cnt_in_ref[...]

    h = x_ref[...]
    h = h + jnp.dot(pool_ref[...].astype(BF16), wo_ref[0:pool_w, :], preferred_element_type=F32)
    h = h + jnp.dot(attn_ref[...].astype(BF16), wo_ref[pool_w:, :], preferred_element_type=F32)
    h_ref[...] = h

    hn_hi, hn_lo = _split_bf16(_rms(h, g_ref[...]))
    w_hi, w_lo = _split_bf16(wr_ref[...])
    hi_both = jnp.dot(hn_hi, jnp.concatenate([w_hi, w_lo], axis=1), preferred_element_type=F32)
    logits = (hi_both[:, :LANES] + hi_both[:, LANES:]
              + jnp.dot(hn_lo, w_hi, preferred_element_type=F32)) + br_ref[...]

    lane = lax.broadcasted_iota(I32, (tm, LANES), 1).astype(F32)

    def first_lane(mask):
        return jnp.min(jnp.where(mask, lane, float(LANES)), axis=1, keepdims=True)

    is_g = lane < n_groups
    lg = jnp.where(is_g, logits, NEG_INF)
    mg = jnp.max(lg, axis=1, keepdims=True)
    p_sel = 1.0 / jnp.sum(jnp.exp(lg - mg), axis=1, keepdims=True)
    g_top = first_lane(lg == mg)
    lo = n_groups + g_top * n_experts
    in_grp = (lane >= lo) & (lane < lo + n_experts)
    le = jnp.where(in_grp, logits, NEG_INF)
    ee = jnp.exp(le - jnp.max(le, axis=1, keepdims=True))
    pe = ee / jnp.sum(ee, axis=1, keepdims=True)
    v1 = jnp.max(jnp.where(in_grp, pe, -1.0), axis=1, keepdims=True)
    i1 = first_lane(in_grp & (pe == v1))
    rest = in_grp & (lane != i1)
    v2 = jnp.max(jnp.where(rest, pe, -1.0), axis=1, keepdims=True)
    i2 = first_lane(rest & (pe == v2))
    gate1 = p_sel * (v1 / (v1 + v2))
    gate2 = p_sel * (v2 / (v1 + v2))

    sel1, sel2 = lane == i1, lane == i2
    onehot = (sel1 | sel2).astype(BF16)
    tri = (lax.broadcasted_iota(I32, (tm, tm), 1) < lax.broadcasted_iota(I32, (tm, tm), 0)).astype(BF16)
    running = jnp.dot(tri, onehot, preferred_element_type=F32) + carry_scr[...]
    rank1 = jnp.sum(jnp.where(sel1, running, 0.0), axis=1, keepdims=True)
    rank2 = jnp.sum(jnp.where(sel2, running, 0.0), axis=1, keepdims=True)
    carry_scr[...] = carry_scr[...] + jnp.sum(onehot.astype(F32), axis=0, keepdims=True)
    cnt_out_ref[...] = carry_scr[...]

    rec = jnp.zeros((tm, LANES), F32)
    for idx, val in ((ROUTE_F1, i1 - n_groups), (ROUTE_F2, i2 - n_groups),
                     (ROUTE_G1, gate1), (ROUTE_G2, gate2), (ROUTE_R1, rank1), (ROUTE_R2, rank2)):
        rec = jnp.where(lane == idx, val, rec)
    route_ref[...] = rec


def _mix_route(x, pool, attn, w_out_bf16, g_ffn, w_route, b_route, cnt_in, *, n_groups, n_experts):
    n, d = x.shape
    tm = TM_MIX
    row = lambda i: (i, 0)
    fix = lambda i: (0, 0)
    return pl.pallas_call(
        functools.partial(_mix_route_body, n_groups=n_groups, n_experts=n_experts),
        grid=(n // tm,),
        in_specs=[pl.BlockSpec((tm, d), row), pl.BlockSpec((tm, pool.shape[1]), row),
                  pl.BlockSpec((tm, attn.shape[1]), row), pl.BlockSpec(w_out_bf16.shape, fix),
                  pl.BlockSpec((1, d), fix), pl.BlockSpec(w_route.shape, fix), pl.BlockSpec((1, LANES), fix),
                  pl.BlockSpec((1, LANES), fix)],
        out_specs=[pl.BlockSpec((tm, d), row), pl.BlockSpec((tm, LANES), row), pl.BlockSpec((1, LANES), fix)],
        out_shape=[jax.ShapeDtypeStruct((n, d), F32), jax.ShapeDtypeStruct((n, LANES), F32),
                   jax.ShapeDtypeStruct((1, LANES), F32)],
        scratch_shapes=[pltpu.VMEM((1, LANES), F32)],
        compiler_params=_params("arbitrary"), name="mix_route",
    )(x, pool, attn, w_out_bf16, g_ffn, w_route, b_route, cnt_in)


def _dispatch_body(zlo_ref, zhi_ref, dest_ref, hp_ref, hs_ref, g_ref, xs_ref, xn_scr, zero_scr, sems, *, tiles_p,
                   n_steps):
    tm = hp_ref.shape[0]
    i = pl.program_id(0)
    zero_sem = 2 * TOP_K_INNER

    @pl.when(i == 0)
    def _():
        zero_scr[...] = jnp.zeros(zero_scr.shape, F32)
        for start in (True, False):
            def chunk(c, carry, start=start):
                cp = pltpu.make_async_copy(zero_scr, xs_ref.at[pl.ds(pl.multiple_of(c * ZERO_ROWS, ZERO_ROWS), ZERO_ROWS)],
                                           sems.at[zero_sem])
                cp.start() if start else cp.wait()
                return carry

            def segment(e, carry, chunk=chunk):
                return lax.fori_loop(zlo_ref[e], zhi_ref[e], chunk, carry)
            lax.fori_loop(0, zlo_ref.shape[0], segment, 0)

    buf = i % 2

    def wait_rows(b):
        for slot in range(TOP_K_INNER):
            pltpu.make_async_copy(xn_scr.at[b], xs_ref.at[pl.ds(0, tm)], sems.at[b * TOP_K_INNER + slot]).wait()

    @pl.when(i >= 2)
    def _():
        wait_rows(buf)

    @pl.when(i < tiles_p)
    def _():
        xn_scr[buf] = _rms(hp_ref[...], g_ref[...])

    @pl.when(i >= tiles_p)
    def _():
        xn_scr[buf] = _rms(hs_ref[...], g_ref[...])

    for r in range(tm):
        for slot in range(TOP_K_INNER):
            d = dest_ref[0, 0, slot * tm + r]
            pltpu.make_async_copy(xn_scr.at[buf, pl.ds(r, 1)], xs_ref.at[pl.ds(d, 1)],
                                  sems.at[buf * TOP_K_INNER + slot]).start(priority=slot)

    @pl.when(i == n_steps - 1)
    def _():
        if n_steps > 1:
            wait_rows(1 - buf)
        wait_rows(buf)


def _dispatch(zero_lo, zero_hi, dest, h_p, h_s, g_ffn, *, rows):
    tm = TM_ROW
    d = h_p.shape[1]
    tiles_p, tiles_s = h_p.shape[0] // tm, h_s.shape[0] // tm
    return pl.pallas_call(
        functools.partial(_dispatch_body, tiles_p=tiles_p, n_steps=tiles_p + tiles_s),
        grid_spec=pltpu.PrefetchScalarGridSpec(
            num_scalar_prefetch=2, grid=(tiles_p + tiles_s,),
            in_specs=[pl.BlockSpec((1, 1, TOP_K_INNER * tm), lambda i, lo, hi: (i, 0, 0), memory_space=pltpu.SMEM),
                      pl.BlockSpec((tm, d), lambda i, lo, hi: (jnp.minimum(i, tiles_p - 1), 0)),
                      pl.BlockSpec((tm, d), lambda i, lo, hi: (jnp.maximum(i - tiles_p, 0), 0)),
                      pl.BlockSpec((1, d), lambda i, lo, hi: (0, 0))],
            out_specs=pl.BlockSpec(memory_space=pl.ANY),
            scratch_shapes=[pltpu.VMEM((2, tm, d), F32), pltpu.VMEM((ZERO_ROWS, d), F32),
                            pltpu.SemaphoreType.DMA((2 * TOP_K_INNER + 1,))]),
        out_shape=jax.ShapeDtypeStruct((rows, d), F32),
        compiler_params=_params("arbitrary"), name="dispatch",
    )(zero_lo, zero_hi, dest, h_p, h_s, g_ffn)


def _moe_gemm_body(expert_ref, block_ref, valid_ref, x_ref, wg_ref, wu_ref, wd_ref, y_ref, wg_scr, wu_scr, wd_scr):
    t = pl.program_id(0)

    @pl.when((t == 0) | (expert_ref[t] != expert_ref[jnp.maximum(t - 1, 0)]))
    def _():
        wg_scr[...] = wg_ref[0].astype(BF16)
        wu_scr[...] = wu_ref[0].astype(BF16)
        wd_scr[...] = wd_ref[0].astype(BF16)

    @pl.when(valid_ref[t] > 0)
    def _():
        x = x_ref[...].astype(BF16)
        gate = jnp.dot(x, wg_scr[...], preferred_element_type=F32)
        up = jnp.dot(x, wu_scr[...], preferred_element_type=F32)
        mid = (gate * jax.nn.sigmoid(gate) * up).astype(BF16)
        y_ref[...] = jnp.dot(mid, wd_scr[...], preferred_element_type=F32)

    @pl.when(valid_ref[t] == 0)
    def _():
        y_ref[...] = jnp.zeros(y_ref.shape, F32)


def _moe_gemm(tile_expert, tile_block, tile_valid, xs, w_gate, w_up, w_down):
    rows, d = xs.shape
    _, _, f = w_gate.shape
    tm = TM_GEMM
    xmap = lambda t, e, b, v: (b[t], 0)
    return pl.pallas_call(
        _moe_gemm_body,
        grid_spec=pltpu.PrefetchScalarGridSpec(
            num_scalar_prefetch=3, grid=(rows // tm,),
            in_specs=[pl.BlockSpec((tm, d), xmap),
                      pl.BlockSpec((1, d, f), lambda t, e, b, v: (e[t], 0, 0)),
                      pl.BlockSpec((1, d, f), lambda t, e, b, v: (e[t], 0, 0)),
                      pl.BlockSpec((1, f, d), lambda t, e, b, v: (e[t], 0, 0))],
            out_specs=pl.BlockSpec((tm, d), lambda t, e, b, v: (t, 0)),
            scratch_shapes=[pltpu.VMEM((d, f), BF16), pltpu.VMEM((d, f), BF16), pltpu.VMEM((f, d), BF16)]),
        out_shape=jax.ShapeDtypeStruct((rows, d), F32),
        compiler_params=_params("arbitrary"), name="moe_gemm",
    )(tile_expert, tile_block, tile_valid, xs, w_gate, w_up, w_down)


def _combine_body(dest_ref, next_ref, h_ref, route_ref, g_ref, ys_ref, y_ref, rows_scr, sems, *, n_steps):
    tm = h_ref.shape[0]
    i = pl.program_id(0)
    buf = i % 2

    def fetch(idx_ref, b):
        for r in range(tm):
            for slot in range(TOP_K_INNER):
                d = idx_ref[0, 0, slot * tm + r]
                pltpu.make_async_copy(ys_ref.at[pl.ds(d, 1)], rows_scr.at[b, slot, pl.ds(r, 1)],
                                      sems.at[b * TOP_K_INNER + slot]).start(priority=slot)

    @pl.when(i == 0)
    def _():
        fetch(dest_ref, 0)

    @pl.when(i + 1 < n_steps)
    def _():
        fetch(next_ref, 1 - buf)

    for slot in range(TOP_K_INNER):
        pltpu.make_async_copy(ys_ref.at[pl.ds(0, tm)], rows_scr.at[buf, slot], sems.at[buf * TOP_K_INNER + slot]).wait()
    route = route_ref[...]
    out = h_ref[...] + (route[:, ROUTE_G1:ROUTE_G1 + 1] * rows_scr[buf, 0]
                        + route[:, ROUTE_G2:ROUTE_G2 + 1] * rows_scr[buf, 1])
    y_ref[...] = _rms(out, g_ref[...])


def _combine(dest, h, route, g_final, ys):
    n, d = h.shape
    tm = TM_ROW
    n_steps = n // tm
    row = lambda i: (i, 0)
    idx_block = (1, 1, TOP_K_INNER * tm)
    return pl.pallas_call(
        functools.partial(_combine_body, n_steps=n_steps),
        grid=(n_steps,),
        in_specs=[pl.BlockSpec(idx_block, lambda i: (i, 0, 0), memory_space=pltpu.SMEM),
                  pl.BlockSpec(idx_block, lambda i: (jnp.minimum(i + 1, n_steps - 1), 0, 0), memory_space=pltpu.SMEM),
                  pl.BlockSpec((tm, d), row), pl.BlockSpec((tm, LANES), row), pl.BlockSpec((1, d), lambda i: (0, 0)),
                  pl.BlockSpec(memory_space=pl.ANY)],
        out_specs=pl.BlockSpec((tm, d), row),
        out_shape=jax.ShapeDtypeStruct((n, d), F32),
        scratch_shapes=[pltpu.VMEM((2, TOP_K_INNER, tm, d), F32), pltpu.SemaphoreType.DMA((2 * TOP_K_INNER,))],
        compiler_params=_params("arbitrary"), name="combine",
    )(dest, dest, h, route, g_final, ys)


def _sort_tables(counts, n_tiles):
    padded = ((counts + TM_GEMM - 1) // TM_GEMM) * TM_GEMM
    ends = jnp.cumsum(padded)
    offsets = ends - padded
    total = ends[-1]
    tile_start = jnp.arange(n_tiles, dtype=I32) * TM_GEMM
    tile_valid = (tile_start < total).astype(I32)
    last_block = jnp.maximum(total // TM_GEMM - 1, 0)
    tile_block = jnp.minimum(jnp.arange(n_tiles, dtype=I32), last_block)
    n_flat = counts.shape[0]
    tile_expert = jnp.minimum(jnp.sum((tile_block[:, None] * TM_GEMM >= ends[None, :]).astype(I32), axis=1), n_flat - 1)
    zero_lo = jnp.concatenate([(offsets + counts) // ZERO_ROWS, total[None] // ZERO_ROWS]).astype(I32)
    zero_hi = jnp.concatenate([ends // ZERO_ROWS, jnp.full((1,), n_tiles * TM_GEMM // ZERO_ROWS, I32)]).astype(I32)
    return offsets, tile_expert.astype(I32), tile_block.astype(I32), tile_valid, zero_lo, zero_hi


def _dest_blocks(route, offsets):
    n = route.shape[0]
    f = route[:, ROUTE_F1:ROUTE_F2 + 1].astype(I32)
    rank = route[:, ROUTE_R1:ROUTE_R2 + 1].astype(I32)
    dest = jnp.sum(jnp.where(f[..., None] == jnp.arange(offsets.shape[0], dtype=I32), offsets, 0), axis=-1) + rank
    return dest.reshape(n // TM_ROW, TM_ROW, TOP_K_INNER).transpose(0, 2, 1).reshape(n // TM_ROW, 1, TOP_K_INNER * TM_ROW)


def kernel(x_prompt, x_sample, cache_k, cache_v, state_pool, g_mix, w_in, w_pool, pool_scale, w_out, g_ffn,
           w_router_group, b_router_group, w_router_expert, b_router_expert, w_gate, w_up, w_down, g_final):
    depth = g_mix.shape[0]
    assert depth == 1, "single-layer step"
    b, s, d = x_prompt.shape
    nd, t_new, _ = x_sample.shape
    n_heads = cache_k.shape[3]
    attn_w = n_heads * HEAD_DIM
    pool_w = state_pool.shape[3]
    past = cache_k.shape[2]
    keep = min(ATTN_REACH, s)
    n_groups, n_experts = w_router_expert.shape[1], w_router_expert.shape[3]
    n_flat = n_groups * n_experts
    assert n_groups + n_flat <= LANES
    slopes = _alibi_slopes(n_heads)

    w_in_b = w_in[0].astype(BF16)
    w_out_b = w_out[0].astype(BF16)
    gw = pool_w // len(POOL_WINDOWS)
    w_bd = jnp.zeros((pool_w, pool_w), F32)
    for g in range(len(POOL_WINDOWS)):
        w_bd = w_bd.at[g * gw:(g + 1) * gw, g * gw:(g + 1) * gw].set(w_pool[0, g])
    w_bd = w_bd.astype(BF16)
    w_route = jnp.concatenate([w_router_group[0], jnp.transpose(w_router_expert[0], (1, 0, 2)).reshape(d, n_flat)], axis=1)
    w_route = jnp.pad(w_route, ((0, 0), (0, LANES - n_groups - n_flat)))
    b_route = jnp.pad(jnp.concatenate([b_router_group[0], b_router_expert[0].reshape(n_flat)]),
                      (0, LANES - n_groups - n_flat))[None]
    w_gate_f, w_up_f = w_gate[0].reshape(n_flat, d, -1), w_up[0].reshape(n_flat, d, -1)
    w_down_f = w_down[0].reshape(n_flat, -1, d)

    n_p = b * s
    u_p, q_p, k_p, v_p, kt_p, vt_p = _proj_in(x_prompt.reshape(n_p, d), g_mix, w_in_b, pool_w=pool_w, attn_w=attn_w,
                                               seq=s, keep=keep)
    attn_p = _attn_prompt(slopes, q_p.reshape(b, s, attn_w), k_p.reshape(b, s, attn_w), v_p.reshape(b, s, attn_w))
    u_p3 = u_p.reshape(b, s, pool_w)
    pool_p = _pool_prompt(u_p3, w_bd, pool_scale)

    n_s = nd * t_new
    u_s, q_s, k_s, v_s = _proj_in(x_sample.reshape(n_s, d), g_mix, w_in_b, pool_w=pool_w, attn_w=attn_w)
    cache_kt = jnp.transpose(cache_k[0], (0, 2, 3, 1))
    cache_vt = jnp.transpose(cache_v[0], (0, 2, 3, 1))
    as3 = lambda a: a.reshape(nd, t_new, attn_w)
    attn_s = _attn_sample(slopes, as3(q_s), as3(k_s), as3(v_s), cache_kt, cache_vt)
    state_t = jnp.transpose(state_pool[0], (1, 0, 2))
    u_st = jnp.transpose(u_s.reshape(nd, t_new, pool_w), (1, 0, 2))
    pool_st, new_state_t = _pool_sample(state_t, u_st, w_bd, pool_scale)
    pool_s = jnp.transpose(pool_st, (1, 0, 2)).reshape(n_s, pool_w)

    route_kw = dict(n_groups=n_groups, n_experts=n_experts)
    h_p, route_p, cnt_p = _mix_route(x_prompt.reshape(n_p, d), pool_p.reshape(n_p, pool_w), attn_p.reshape(n_p, attn_w),
                                     w_out_b, g_ffn, w_route, b_route, jnp.zeros((1, LANES), F32), **route_kw)
    h_s, route_s, cnt_all = _mix_route(x_sample.reshape(n_s, d), pool_s, attn_s.reshape(n_s, attn_w),
                                       w_out_b, g_ffn, w_route, b_route, cnt_p, **route_kw)

    counts = cnt_all[0, n_groups:n_groups + n_flat].astype(I32)
    n_tiles = (TOP_K_INNER * (n_p + n_s) + n_flat * (TM_GEMM - 1)) // TM_GEMM
    offsets, tile_expert, tile_block, tile_valid, zero_lo, zero_hi = _sort_tables(counts, n_tiles)
    dest_p, dest_s = _dest_blocks(route_p, offsets), _dest_blocks(route_s, offsets)

    xs = _dispatch(zero_lo, zero_hi, jnp.concatenate([dest_p, dest_s], axis=0), h_p, h_s, g_ffn,
                   rows=n_tiles * TM_GEMM)
    ys = _moe_gemm(tile_expert, tile_block, tile_valid, xs, w_gate_f, w_up_f, w_down_f)
    y_p = _combine(dest_p, h_p, route_p, g_final[None], ys)
    y_s = _combine(dest_s, h_s, route_s, g_final[None], ys)

    y_prompt = y_p.reshape(b, s, d)
    y_sample = y_s.reshape(nd, t_new, d)
    k_prompt = jnp.transpose(kt_p.reshape(b, n_heads, HEAD_DIM, keep), (0, 3, 1, 2))[None]
    v_prompt = jnp.transpose(vt_p.reshape(b, n_heads, HEAD_DIM, keep), (0, 3, 1, 2))[None]
    pool_prompt = u_p3[:, s - POOL_STATE:][None]
    k_sample = k_s.reshape(1, nd, t_new, n_heads, HEAD_DIM)
    v_sample = v_s.reshape(1, nd, t_new, n_heads, HEAD_DIM)
    pool_sample = jnp.transpose(new_state_t[-POOL_STATE:], (1, 0, 2))[None]
    return (y_prompt, y_sample, k_prompt, v_prompt, pool_prompt, k_sample, v_sample, pool_sample)
```

```python
import functools
import math

import numpy as np
import jax
import jax.numpy as jnp
from jax import lax
from jax.experimental import pallas as pl
from jax.experimental.pallas import tpu as pltpu

F32 = jnp.float32
BF16 = jnp.bfloat16
I32 = jnp.int32

HEAD_DIM = 64
POOL_WINDOWS = (2, 4, 8, 16)
POOL_STATE = max(POOL_WINDOWS) - 1
BRANCHES = ((128, 1), (512, 4), (2048, 16))
STEPS = BRANCHES[0][0] // BRANCHES[0][1]
ATTN_REACH = max(w for w, _ in BRANCHES)
MAX_DIL = max(d for _, d in BRANCHES)
TOP_K_INNER = 2
RMS_EPS = 1e-6
LANES = 128
NEG_INF = float("-inf")

VMEM_LIMIT = 56 * 1024 * 1024

TM_PROJ = 512
TM_MIX = 512
TM_ROW = 512
TM_GEMM = 512
ZERO_ROWS = 64
ATTN_GROUP = 8


def _alibi_slopes(n_heads):
    def geometric(n):
        start = 2.0 ** (-8.0 / n)
        return [start ** (i + 1) for i in range(n)]
    c = 2 ** int(math.floor(math.log2(n_heads)))
    s = geometric(c)
    if c < n_heads:
        s = s + geometric(2 * c)[0::2][: n_heads - c]
    return jnp.asarray(s, dtype=F32)


def _rms(x, g):
    return x * lax.rsqrt(jnp.mean(x * x, axis=-1, keepdims=True) + RMS_EPS) * g


def _params(*sem):
    return pltpu.CompilerParams(dimension_semantics=sem, vmem_limit_bytes=VMEM_LIMIT)


def _proj_in_body(x_ref, g_ref, w_ref, u_ref, q_ref, k_ref, v_ref, *t_refs, pool_w, attn_w, tiles_per_seq,
                  keep_tiles):
    xn = _rms(x_ref[...], g_ref[...]).astype(BF16)

    def proj(lo, n):
        return jnp.dot(xn, w_ref[:, lo:lo + n], preferred_element_type=F32)

    u_ref[...] = proj(0, pool_w)
    q_ref[...] = proj(pool_w, attn_w) * (HEAD_DIM ** -0.5)
    k = proj(pool_w + attn_w, attn_w)
    v = proj(pool_w + 2 * attn_w, attn_w)
    k_ref[...] = k
    v_ref[...] = v
    if t_refs:
        kt_ref, vt_ref = t_refs
        j = pl.program_id(0) % tiles_per_seq

        @pl.when(j >= tiles_per_seq - keep_tiles)
        def _():
            kt_ref[0] = k.T
            vt_ref[0] = v.T


def _proj_in(x, g, w_bf16, *, pool_w, attn_w, seq=None, keep=None):
    n, d = x.shape
    tm = TM_PROJ
    grid = (n // tm,)
    row = lambda i: (i, 0)
    out_shape = [jax.ShapeDtypeStruct((n, pool_w), F32)] + [jax.ShapeDtypeStruct((n, attn_w), F32)] * 3
    out_specs = [pl.BlockSpec((tm, pool_w), row)] + [pl.BlockSpec((tm, attn_w), row)] * 3
    tiles_per_seq = keep_tiles = 0
    if seq is not None:
        tiles_per_seq, keep_tiles = seq // tm, keep // tm
        first = tiles_per_seq - keep_tiles
        tmap = lambda i: (i // tiles_per_seq, 0, jnp.maximum(i % tiles_per_seq - first, 0))
        out_shape += [jax.ShapeDtypeStruct((n // seq, attn_w, keep), F32)] * 2
        out_specs += [pl.BlockSpec((1, attn_w, tm), tmap)] * 2
    body = functools.partial(_proj_in_body, pool_w=pool_w, attn_w=attn_w, tiles_per_seq=tiles_per_seq,
                             keep_tiles=keep_tiles)
    return pl.pallas_call(
        body, grid=grid,
        in_specs=[pl.BlockSpec((tm, d), row), pl.BlockSpec((1, d), lambda i: (0, 0)),
                  pl.BlockSpec(w_bf16.shape, lambda i: (0, 0))],
        out_specs=out_specs, out_shape=out_shape,
        compiler_params=_params("arbitrary"), name="proj_in",
    )(x, g, w_bf16)


def _attn_prompt_body(slopes_ref, q_hbm, k_hbm, v_hbm, o_ref, qkv_scr, bias_scr, s_scr, p_scr, o_scr, m_scr, l_scr, sems,
                      *, seq, n_pairs, n_steps):
    hp = pl.program_id(1)
    gstep = pl.program_id(0) * n_pairs + hp
    buf = gstep % 2
    blk = STEPS
    n_it = seq // blk
    cls = seq // MAX_DIL

    def class_copies(st, b):
        lanes = pl.ds(pl.multiple_of((st % n_pairs) * LANES, LANES), LANES)
        return [pltpu.make_async_copy(src.at[st // n_pairs, :, c, lanes], qkv_scr.at[b, a, pl.ds(c * cls, cls), :],
                                      sems.at[b, a])
                for a, src in enumerate((q_hbm, k_hbm, v_hbm)) for c in range(MAX_DIL)]

    @pl.when(gstep == 0)
    def _():
        for cp in class_copies(gstep, 0):
            cp.start()

    @pl.when(gstep + 1 < n_steps)
    def _():
        for cp in class_copies(gstep + 1, 1 - buf):
            cp.start()

    for cp in class_copies(gstep, buf):
        cp.wait()
    qp_scr, kp_scr, vp_scr = (qkv_scr.at[buf, a] for a in range(3))

    lane = lax.broadcasted_iota(I32, (blk, LANES), 1)
    head_a = lane < HEAD_DIM
    row = lax.broadcasted_iota(I32, (2 * blk, 2 * blk), 0)
    col = lax.broadcasted_iota(I32, (2 * blk, 2 * blk), 1)
    slope = jnp.where(row < blk, slopes_ref[2 * hp], slopes_ref[2 * hp + 1])

    for bi, (_, d) in enumerate(BRANCHES):
        nb = n_it // d
        n_chunk = MAX_DIL // d
        a_rows = blk // n_chunk
        sh = a_rows.bit_length() - 1
        assert a_rows == 1 << sh and a_rows % 8 == 0

        def seq_index(i, n_chunk=n_chunk, a_rows=a_rows, sh=sh):
            i = i & (blk - 1)
            return (i & (a_rows - 1)) * n_chunk + (i >> sh)

        step = seq_index(row) + blk - (seq_index(col) + (col & blk))
        bias = jnp.where((step >= 0) & (step <= STEPS), -slope * (d * step).astype(F32), NEG_INF)
        bias_scr[0] = bias
        bias_scr[1] = jnp.where(col < blk, NEG_INF, bias)

        def group(j, carry, bi=bi, d=d, nb=nb, n_chunk=n_chunk, a_rows=a_rows):
            def chunks(g):
                it = j * ATTN_GROUP + g
                r = it // nb
                n = it % nb
                cur = [pl.ds(pl.multiple_of((r + d * c) * cls + n * a_rows, 8), a_rows) for c in range(n_chunk)]
                prev = [pl.ds(pl.multiple_of((r + d * c) * cls + jnp.maximum(n - 1, 0) * a_rows, 8), a_rows)
                        for c in range(n_chunk)]
                return n, cur, prev

            def gather(ref, sls):
                return jnp.concatenate([ref[sl, :] for sl in sls], axis=0)

            for g in range(ATTN_GROUP):
                n, cur, prev = chunks(g)
                qb = gather(qp_scr, cur)
                q2 = jnp.concatenate([jnp.where(head_a, qb, 0.0), jnp.where(head_a, 0.0, qb)], axis=0).astype(BF16)
                kc = gather(kp_scr, prev + cur).astype(BF16)
                s = lax.dot_general(q2, kc, (((1,), (1,)), ((), ())), preferred_element_type=F32)
                s_scr[g] = s + bias_scr[jnp.where(n == 0, 1, 0)]
            for g in range(ATTN_GROUP):
                _, cur, _ = chunks(g)
                s = s_scr[g]
                m = jnp.max(s, axis=1, keepdims=True)
                p_scr[g] = jnp.exp(s - m).astype(BF16)
                m2 = jnp.where(head_a, m[:blk], m[blk:])
                for c, sl in enumerate(cur):
                    m_scr[bi, sl, :] = m2[c * a_rows:(c + 1) * a_rows]
            ones = jnp.ones((2 * blk, LANES), BF16)
            for g in range(ATTN_GROUP):
                _, cur, prev = chunks(g)
                vc = jnp.concatenate([gather(vp_scr, prev + cur).astype(BF16), ones], axis=1)
                ol = jnp.dot(p_scr[g], vc, preferred_element_type=F32)
                o2 = jnp.where(head_a, ol[:blk, :LANES], ol[blk:, :LANES])
                l2 = jnp.where(head_a, ol[:blk, LANES:], ol[blk:, LANES:])
                for c, sl in enumerate(cur):
                    o_scr[bi, sl, :] = o2[c * a_rows:(c + 1) * a_rows]
                    l_scr[bi, sl, :] = l2[c * a_rows:(c + 1) * a_rows]
            return carry

        lax.fori_loop(0, n_it // ATTN_GROUP, group, 0)

    def merge(c, carry):
        rs = pl.ds(pl.multiple_of(c * cls, cls), cls)
        ms = [m_scr[b, rs, :] for b in range(len(BRANCHES))]
        mx = functools.reduce(jnp.maximum, ms)
        num = jnp.zeros((cls, LANES), F32)
        den = jnp.zeros((cls, LANES), F32)
        for b in range(len(BRANCHES)):
            a = jnp.exp(ms[b] - mx)
            num = num + a * o_scr[b, rs, :]
            den = den + a * l_scr[b, rs, :]
        o_ref[0, pl.ds(c, cls, stride=MAX_DIL), :] = num / den
        return carry

    lax.fori_loop(0, MAX_DIL, merge, 0)


def _attn_prompt(slopes, q, k, v):
    b, s, hw = q.shape
    assert s % (STEPS * max(d for _, d in BRANCHES)) == 0, "sequence must be a multiple of the widest span"
    assert hw % LANES == 0 and LANES == 2 * HEAD_DIM
    n_pairs = hw // LANES
    nbr = len(BRANCHES)
    by_class = lambda a: a.reshape(b, s // MAX_DIL, MAX_DIL, hw)
    hbm = pl.BlockSpec(memory_space=pl.ANY)
    return pl.pallas_call(
        functools.partial(_attn_prompt_body, seq=s, n_pairs=n_pairs, n_steps=b * n_pairs),
        grid=(b, n_pairs),
        in_specs=[pl.BlockSpec(memory_space=pltpu.SMEM), hbm, hbm, hbm],
        out_specs=pl.BlockSpec((1, s, LANES), lambda i, j: (i, 0, j)),
        out_shape=jax.ShapeDtypeStruct((b, s, hw), F32),
        scratch_shapes=[pltpu.VMEM((2, 3, s, LANES), F32),
                        pltpu.VMEM((2, 2 * STEPS, 2 * STEPS), F32),
                        pltpu.VMEM((ATTN_GROUP, 2 * STEPS, 2 * STEPS), F32),
                        pltpu.VMEM((ATTN_GROUP, 2 * STEPS, 2 * STEPS), BF16)]
        + [pltpu.VMEM((nbr, s, LANES), F32)] * 3 + [pltpu.SemaphoreType.DMA((2, 3))],
        compiler_params=_params("arbitrary", "arbitrary"), name="attn_prompt",
    )(slopes, by_class(q), by_class(k), by_class(v))


def _multiplicity(dist):
    mult = jnp.zeros(dist.shape, F32)
    for w, d in BRANCHES:
        assert d & (d - 1) == 0
        mult = mult + ((dist >= 0) & ((dist & (d - 1)) == 0) & (dist <= w)).astype(F32)
    return mult


def _attn_sample_body(slopes_ref, q_ref, kn_ref, vn_ref, kt_ref, vt_ref, o_ref, *, n_heads):
    t_new = q_ref.shape[1]
    past = kt_ref.shape[3]
    dist = past + lax.broadcasted_iota(I32, (t_new, past), 0) - lax.broadcasted_iota(I32, (t_new, past), 1)
    dist_n = lax.broadcasted_iota(I32, (t_new, t_new), 0) - lax.broadcasted_iota(I32, (t_new, t_new), 1)
    mult, mult_n = _multiplicity(dist), _multiplicity(dist_n)
    dist_f, dist_nf = dist.astype(F32), dist_n.astype(F32)
    nt = (((1,), (1,)), ((), ()))
    for h in range(n_heads):
        slope = slopes_ref[h]
        cols = slice(h * HEAD_DIM, (h + 1) * HEAD_DIM)
        qh = q_ref[0, :, cols].astype(BF16)
        kn = kn_ref[0, :, cols].astype(BF16)
        vn = vn_ref[0, :, cols].astype(BF16)
        s = jnp.dot(qh, kt_ref[0, h].astype(BF16), preferred_element_type=F32)
        s = jnp.where(mult > 0, s - slope * dist_f, NEG_INF)
        sn = lax.dot_general(qh, kn, nt, preferred_element_type=F32)
        sn = jnp.where(mult_n > 0, sn - slope * dist_nf, NEG_INF)
        m = jnp.maximum(jnp.max(s, axis=1, keepdims=True), jnp.max(sn, axis=1, keepdims=True))
        p = mult * jnp.exp(s - m)
        pn = mult_n * jnp.exp(sn - m)
        l = jnp.sum(p, axis=1, keepdims=True) + jnp.sum(pn, axis=1, keepdims=True)
        o = lax.dot_general(p.astype(BF16), vt_ref[0, h].astype(BF16), nt, preferred_element_type=F32)
        o = o + jnp.dot(pn.astype(BF16), vn, preferred_element_type=F32)
        o_ref[0, :, cols] = o / l


def _attn_sample(slopes, q, k_new, v_new, cache_kt, cache_vt):
    n, t, hw = q.shape
    _, h, e, past = cache_kt.shape
    assert past >= ATTN_REACH, "every strided key of every branch must exist in the window buffer"
    new = pl.BlockSpec((1, t, hw), lambda i: (i, 0, 0))
    old = pl.BlockSpec((1, h, e, past), lambda i: (i, 0, 0, 0))
    return pl.pallas_call(
        functools.partial(_attn_sample_body, n_heads=h),
        grid=(n,),
        in_specs=[pl.BlockSpec(memory_space=pltpu.SMEM), new, new, new, old, old],
        out_specs=new,
        out_shape=jax.ShapeDtypeStruct((n, t, hw), F32),
        compiler_params=_params("arbitrary"), name="attn_sample",
    )(slopes, q, k_new, v_new, cache_kt, cache_vt)


def _pool_windows(width):
    gw = width // len(POOL_WINDOWS)
    lane = lax.broadcasted_iota(I32, (1, width), 1)
    win = jnp.zeros((1, width), I32)
    for g, w in enumerate(POOL_WINDOWS):
        win = jnp.where((lane >= g * gw) & (lane < (g + 1) * gw), w, win)
    return win


def _pool_prompt_body(u_ref, w_ref, sc_ref, o_ref, ext_scr, *, tm):
    j = pl.program_id(1)
    width = u_ref.shape[2]
    pad = 2 * (POOL_STATE + 1)
    start = pl.multiple_of(j * tm, tm)
    lead = POOL_STATE + 1
    prev = u_ref[0, pl.ds(pl.multiple_of(jnp.maximum(start - lead, 0), lead), lead), :]
    ext_scr[0:pad - lead, :] = jnp.zeros((pad - lead, width), F32)
    ext_scr[pad - lead:pad, :] = jnp.where(j > 0, prev, 0.0)
    ext_scr[pad:, :] = u_ref[0, pl.ds(start, tm), :]
    win = _pool_windows(width)
    tok = ext_scr[pad:, :]
    acc = tok
    for i in range(1, max(POOL_WINDOWS)):
        acc = acc + jnp.where(i < win, ext_scr[pl.ds(pad - i, tm), :], 0.0)
    pos = start + lax.broadcasted_iota(I32, (tm, width), 0)
    cnt = jnp.minimum(win, pos + 1).astype(F32)
    diff = (acc / cnt - tok).astype(BF16)
    o_ref[0] = jnp.dot(diff, w_ref[...], preferred_element_type=F32) * sc_ref[...]


def _pool_prompt(u, w_bd, scale):
    b, s, w = u.shape
    tm = TM_PROJ
    return pl.pallas_call(
        functools.partial(_pool_prompt_body, tm=tm),
        grid=(b, s // tm),
        in_specs=[pl.BlockSpec((1, s, w), lambda i, j: (i, 0, 0)), pl.BlockSpec((w, w), lambda i, j: (0, 0)),
                  pl.BlockSpec((1, w), lambda i, j: (0, 0))],
        out_specs=pl.BlockSpec((1, tm, w), lambda i, j: (i, j, 0)),
        out_shape=jax.ShapeDtypeStruct((b, s, w), F32),
        scratch_shapes=[pltpu.VMEM((tm + 2 * (POOL_STATE + 1), w), F32)],
        compiler_params=_params("arbitrary", "arbitrary"), name="pool_prompt",
    )(u, w_bd, scale)


def _pool_sample_body(st_ref, u_ref, w_ref, sc_ref, o_ref, ns_ref):
    t_new, _, width = u_ref.shape
    n_state = st_ref.shape[0]
    win = _pool_windows(width)

    def row(k):
        return st_ref[k] if k < n_state else u_ref[k - n_state]

    for t in range(t_new):
        tok = u_ref[t]
        acc = tok
        for i in range(1, max(POOL_WINDOWS)):
            acc = acc + jnp.where(i < win, row(n_state + t - i), 0.0)
        diff = (acc / win.astype(F32) - tok).astype(BF16)
        o_ref[t] = jnp.dot(diff, w_ref[...], preferred_element_type=F32) * sc_ref[...]
    for k in range(n_state):
        ns_ref[k] = row(k + t_new)


def _pool_sample(state_t, u, w_bd, scale):
    n_state = state_t.shape[0]
    assert n_state >= POOL_STATE
    return pl.pallas_call(
        _pool_sample_body,
        out_shape=[jax.ShapeDtypeStruct(u.shape, F32), jax.ShapeDtypeStruct(state_t.shape, F32)],
        compiler_params=pltpu.CompilerParams(vmem_limit_bytes=VMEM_LIMIT), name="pool_sample",
    )(state_t, u, w_bd, scale)


ROUTE_F1, ROUTE_F2, ROUTE_G1, ROUTE_G2, ROUTE_R1, ROUTE_R2 = range(6)


def _split_bf16(x):
    hi = x.astype(BF16)
    return hi, (x - hi.astype(F32)).astype(BF16)


def _mix_route_body(x_ref, pool_ref, attn_ref, wo_ref, g_ref, wr_ref, br_ref, cnt_in_ref, h_ref, route_ref,
                    cnt_out_ref, carry_scr, *, n_groups, n_experts):
    i = pl.program_id(0)
    tm = x_ref.shape[0]
    pool_w = pool_ref.shape[1]

    @pl.when(i == 0)
    def _():
        carry_scr[...] = cnt_in_ref[...]

    h = x_ref[...]
    h = h + jnp.dot(pool_ref[...].astype(BF16), wo_ref[0:pool_w, :], preferred_element_type=F32)
    h = h + jnp.dot(attn_ref[...].astype(BF16), wo_ref[pool_w:, :], preferred_element_type=F32)
    h_ref[...] = h

    hn_hi, hn_lo = _split_bf16(_rms(h, g_ref[...]))
    w_hi, w_lo = _split_bf16(wr_ref[...])
    hi_both = jnp.dot(hn_hi, jnp.concatenate([w_hi, w_lo], axis=1), preferred_element_type=F32)
    logits = (hi_both[:, :LANES] + hi_both[:, LANES:]
              + jnp.dot(hn_lo, w_hi, preferred_element_type=F32)) + br_ref[...]

    lane = lax.broadcasted_iota(I32, (tm, LANES), 1).astype(F32)

    def first_lane(mask):
        return jnp.min(jnp.where(mask, lane, float(LANES)), axis=1, keepdims=True)

    is_g = lane < n_groups
    lg = jnp.where(is_g, logits, NEG_INF)
    mg = jnp.max(lg, axis=1, keepdims=True)
    p_sel = 1.0 / jnp.sum(jnp.exp(lg - mg), axis=1, keepdims=True)
    g_top = first_lane(lg == mg)
    lo = n_groups + g_top * n_experts
    in_grp = (lane >= lo) & (lane < lo + n_experts)
    le = jnp.where(in_grp, logits, NEG_INF)
    ee = jnp.exp(le - jnp.max(le, axis=1, keepdims=True))
    pe = ee / jnp.sum(ee, axis=1, keepdims=True)
    v1 = jnp.max(jnp.where(in_grp, pe, -1.0), axis=1, keepdims=True)
    i1 = first_lane(in_grp & (pe == v1))
    rest = in_grp & (lane != i1)
    v2 = jnp.max(jnp.where(rest, pe, -1.0), axis=1, keepdims=True)
    i2 = first_lane(rest & (pe == v2))
    gate1 = p_sel * (v1 / (v1 + v2))
    gate2 = p_sel * (v2 / (v1 + v2))

    sel1, sel2 = lane == i1, lane == i2
    onehot = (sel1 | sel2).astype(BF16)
    tri = (lax.broadcasted_iota(I32, (tm, tm), 1) < lax.broadcasted_iota(I32, (tm, tm), 0)).astype(BF16)
    running = jnp.dot(tri, onehot, preferred_element_type=F32) + carry_scr[...]
    rank1 = jnp.sum(jnp.where(sel1, running, 0.0), axis=1, keepdims=True)
    rank2 = jnp.sum(jnp.where(sel2, running, 0.0), axis=1, keepdims=True)
    carry_scr[...] = carry_scr[...] + jnp.sum(onehot.astype(F32), axis=0, keepdims=True)
    cnt_out_ref[...] = carry_scr[...]

    rec = jnp.zeros((tm, LANES), F32)
    for idx, val in ((ROUTE_F1, i1 - n_groups), (ROUTE_F2, i2 - n_groups),
                     (ROUTE_G1, gate1), (ROUTE_G2, gate2), (ROUTE_R1, rank1), (ROUTE_R2, rank2)):
        rec = jnp.where(lane == idx, val, rec)
    route_ref[...] = rec


def _mix_route(x, pool, attn, w_out_bf16, g_ffn, w_route, b_route, cnt_in, *, n_groups, n_experts):
    n, d = x.shape
    tm = TM_MIX
    row = lambda i: (i, 0)
    fix = lambda i: (0, 0)
    return pl.pallas_call(
        functools.partial(_mix_route_body, n_groups=n_groups, n_experts=n_experts),
        grid=(n // tm,),
        in_specs=[pl.BlockSpec((tm, d), row), pl.BlockSpec((tm, pool.shape[1]), row),
                  pl.BlockSpec((tm, attn.shape[1]), row), pl.BlockSpec(w_out_bf16.shape, fix),
                  pl.BlockSpec((1, d), fix), pl.BlockSpec(w_route.shape, fix), pl.BlockSpec((1, LANES), fix),
                  pl.BlockSpec((1, LANES), fix)],
        out_specs=[pl.BlockSpec((tm, d), row), pl.BlockSpec((tm, LANES), row), pl.BlockSpec((1, LANES), fix)],
        out_shape=[jax.ShapeDtypeStruct((n, d), F32), jax.ShapeDtypeStruct((n, LANES), F32),
                   jax.ShapeDtypeStruct((1, LANES), F32)],
        scratch_shapes=[pltpu.VMEM((1, LANES), F32)],
        compiler_params=_params("arbitrary"), name="mix_route",
    )(x, pool, attn, w_out_bf16, g_ffn, w_route, b_route, cnt_in)


def _dispatch_body(zlo_ref, zhi_ref, dest_ref, hp_ref, hs_ref, g_ref, xs_ref, xn_scr, zero_scr, sems, *, tiles_p,
                   n_steps):
    tm = hp_ref.shape[0]
    i = pl.program_id(0)
    zero_sem = 2 * TOP_K_INNER

    @pl.when(i == 0)
    def _():
        zero_scr[...] = jnp.zeros(zero_scr.shape, F32)
        for start in (True, False):
            def chunk(c, carry, start=start):
                cp = pltpu.make_async_copy(zero_scr, xs_ref.at[pl.ds(pl.multiple_of(c * ZERO_ROWS, ZERO_ROWS), ZERO_ROWS)],
                                           sems.at[zero_sem])
                cp.start() if start else cp.wait()
                return carry

            def segment(e, carry, chunk=chunk):
                return lax.fori_loop(zlo_ref[e], zhi_ref[e], chunk, carry)
            lax.fori_loop(0, zlo_ref.shape[0], segment, 0)

    buf = i % 2

    def wait_rows(b):
        for slot in range(TOP_K_INNER):
            pltpu.make_async_copy(xn_scr.at[b], xs_ref.at[pl.ds(0, tm)], sems.at[b * TOP_K_INNER + slot]).wait()

    @pl.when(i >= 2)
    def _():
        wait_rows(buf)

    @pl.when(i < tiles_p)
    def _():
        xn_scr[buf] = _rms(hp_ref[...], g_ref[...])

    @pl.when(i >= tiles_p)
    def _():
        xn_scr[buf] = _rms(hs_ref[...], g_ref[...])

    for r in range(tm):
        for slot in range(TOP_K_INNER):
            d = dest_ref[0, 0, slot * tm + r]
            pltpu.make_async_copy(xn_scr.at[buf, pl.ds(r, 1)], xs_ref.at[pl.ds(d, 1)],
                                  sems.at[buf * TOP_K_INNER + slot]).start(priority=slot)

    @pl.when(i == n_steps - 1)
    def _():
        if n_steps > 1:
            wait_rows(1 - buf)
        wait_rows(buf)


def _dispatch(zero_lo, zero_hi, dest, h_p, h_s, g_ffn, *, rows):
    tm = TM_ROW
    d = h_p.shape[1]
    tiles_p, tiles_s = h_p.shape[0] // tm, h_s.shape[0] // tm
    return pl.pallas_call(
        functools.partial(_dispatch_body, tiles_p=tiles_p, n_steps=tiles_p + tiles_s),
        grid_spec=pltpu.PrefetchScalarGridSpec(
            num_scalar_prefetch=2, grid=(tiles_p + tiles_s,),
            in_specs=[pl.BlockSpec((1, 1, TOP_K_INNER * tm), lambda i, lo, hi: (i, 0, 0), memory_space=pltpu.SMEM),
                      pl.BlockSpec((tm, d), lambda i, lo, hi: (jnp.minimum(i, tiles_p - 1), 0)),
                      pl.BlockSpec((tm, d), lambda i, lo, hi: (jnp.maximum(i - tiles_p, 0), 0)),
                      pl.BlockSpec((1, d), lambda i, lo, hi: (0, 0))],
            out_specs=pl.BlockSpec(memory_space=pl.ANY),
            scratch_shapes=[pltpu.VMEM((2, tm, d), F32), pltpu.VMEM((ZERO_ROWS, d), F32),
                            pltpu.SemaphoreType.DMA((2 * TOP_K_INNER + 1,))]),
        out_shape=jax.ShapeDtypeStruct((rows, d), F32),
        compiler_params=_params("arbitrary"), name="dispatch",
    )(zero_lo, zero_hi, dest, h_p, h_s, g_ffn)


GEMM_X_BUFS = 3
GEMM_Y_BUFS = 2


def _moe_gemm_body(expert_ref, valid_ref, xs_hbm, wg_ref, wu_ref, wd_ref, ys_hbm, x_buf, y_buf, wg_scr, wu_scr, wd_scr,
                   x_sems, y_sems, *, n_tiles):
    t = pl.program_id(0)
    tm = x_buf.shape[1]
    ahead = GEMM_X_BUFS - 1

    def x_copy(s):
        return pltpu.make_async_copy(xs_hbm.at[pl.ds(pl.multiple_of(s * tm, tm), tm)], x_buf.at[s % GEMM_X_BUFS],
                                     x_sems.at[s % GEMM_X_BUFS])

    def y_copy(s):
        return pltpu.make_async_copy(y_buf.at[s % GEMM_Y_BUFS], ys_hbm.at[pl.ds(pl.multiple_of(s * tm, tm), tm)],
                                     y_sems.at[s % GEMM_Y_BUFS])

    def tile_is_valid(s):
        return (s < n_tiles) & (valid_ref[jnp.minimum(s, n_tiles - 1)] > 0)

    @pl.when(t == 0)
    def _():
        for s in range(ahead):
            @pl.when(tile_is_valid(s))
            def _():
                x_copy(s).start()

    @pl.when(tile_is_valid(t + ahead))
    def _():
        x_copy(t + ahead).start()

    @pl.when((t == 0) | (expert_ref[t] != expert_ref[jnp.maximum(t - 1, 0)]))
    def _():
        wg_scr[...] = wg_ref[0].astype(BF16)
        wu_scr[...] = wu_ref[0].astype(BF16)
        wd_scr[...] = wd_ref[0].astype(BF16)

    @pl.when(t >= GEMM_Y_BUFS)
    def _():
        y_copy(t - GEMM_Y_BUFS).wait()

    ybuf = y_buf.at[t % GEMM_Y_BUFS]

    @pl.when(valid_ref[t] > 0)
    def _():
        x_copy(t).wait()
        x = x_buf[t % GEMM_X_BUFS].astype(BF16)
        gate = jnp.dot(x, wg_scr[...], preferred_element_type=F32)
        up = jnp.dot(x, wu_scr[...], preferred_element_type=F32)
        mid = (gate * jax.nn.sigmoid(gate) * up).astype(BF16)
        ybuf[...] = jnp.dot(mid, wd_scr[...], preferred_element_type=F32)

    @pl.when(valid_ref[t] == 0)
    def _():
        ybuf[...] = jnp.zeros(ybuf.shape, F32)

    y_copy(t).start()

    @pl.when(t == n_tiles - 1)
    def _():
        for back in range(min(GEMM_Y_BUFS, n_tiles) - 1, -1, -1):
            y_copy(t - back).wait()


def _moe_gemm(tile_expert, tile_valid, xs, w_gate, w_up, w_down):
    rows, d = xs.shape
    _, _, f = w_gate.shape
    tm = TM_GEMM
    n_tiles = rows // tm
    hbm = pl.BlockSpec(memory_space=pl.ANY)
    return pl.pallas_call(
        functools.partial(_moe_gemm_body, n_tiles=n_tiles),
        grid_spec=pltpu.PrefetchScalarGridSpec(
            num_scalar_prefetch=2, grid=(n_tiles,),
            in_specs=[hbm,
                      pl.BlockSpec((1, d, f), lambda t, e, v: (e[t], 0, 0)),
                      pl.BlockSpec((1, d, f), lambda t, e, v: (e[t], 0, 0)),
                      pl.BlockSpec((1, f, d), lambda t, e, v: (e[t], 0, 0))],
            out_specs=hbm,
            scratch_shapes=[pltpu.VMEM((GEMM_X_BUFS, tm, d), F32), pltpu.VMEM((GEMM_Y_BUFS, tm, d), F32),
                            pltpu.VMEM((d, f), BF16), pltpu.VMEM((d, f), BF16), pltpu.VMEM((f, d), BF16),
                            pltpu.SemaphoreType.DMA((GEMM_X_BUFS,)), pltpu.SemaphoreType.DMA((GEMM_Y_BUFS,))]),
        out_shape=jax.ShapeDtypeStruct((rows, d), F32),
        compiler_params=_params("arbitrary"), name="moe_gemm",
    )(tile_expert, tile_valid, xs, w_gate, w_up, w_down)


def _combine_body(dest_ref, next_ref, h_ref, route_ref, g_ref, ys_ref, y_ref, rows_scr, sems, *, n_steps):
    tm = h_ref.shape[0]
    i = pl.program_id(0)
    buf = i % 2

    def fetch(idx_ref, b):
        for r in range(tm):
            for slot in range(TOP_K_INNER):
                d = idx_ref[0, 0, slot * tm + r]
                pltpu.make_async_copy(ys_ref.at[pl.ds(d, 1)], rows_scr.at[b, slot, pl.ds(r, 1)],
                                      sems.at[b * TOP_K_INNER + slot]).start(priority=slot)

    @pl.when(i == 0)
    def _():
        fetch(dest_ref, 0)

    @pl.when(i + 1 < n_steps)
    def _():
        fetch(next_ref, 1 - buf)

    for slot in range(TOP_K_INNER):
        pltpu.make_async_copy(ys_ref.at[pl.ds(0, tm)], rows_scr.at[buf, slot], sems.at[buf * TOP_K_INNER + slot]).wait()
    route = route_ref[...]
    out = h_ref[...] + (route[:, ROUTE_G1:ROUTE_G1 + 1] * rows_scr[buf, 0]
                        + route[:, ROUTE_G2:ROUTE_G2 + 1] * rows_scr[buf, 1])
    y_ref[...] = _rms(out, g_ref[...])


def _combine(dest, h, route, g_final, ys):
    n, d = h.shape
    tm = TM_ROW
    n_steps = n // tm
    row = lambda i: (i, 0)
    idx_block = (1, 1, TOP_K_INNER * tm)
    return pl.pallas_call(
        functools.partial(_combine_body, n_steps=n_steps),
        grid=(n_steps,),
        in_specs=[pl.BlockSpec(idx_block, lambda i: (i, 0, 0), memory_space=pltpu.SMEM),
                  pl.BlockSpec(idx_block, lambda i: (jnp.minimum(i + 1, n_steps - 1), 0, 0), memory_space=pltpu.SMEM),
                  pl.BlockSpec((tm, d), row), pl.BlockSpec((tm, LANES), row), pl.BlockSpec((1, d), lambda i: (0, 0)),
                  pl.BlockSpec(memory_space=pl.ANY)],
        out_specs=pl.BlockSpec((tm, d), row),
        out_shape=jax.ShapeDtypeStruct((n, d), F32),
        scratch_shapes=[pltpu.VMEM((2, TOP_K_INNER, tm, d), F32), pltpu.SemaphoreType.DMA((2 * TOP_K_INNER,))],
        compiler_params=_params("arbitrary"), name="combine",
    )(dest, dest, h, route, g_final, ys)


def _sort_tables(counts, n_tiles):
    padded = ((counts + TM_GEMM - 1) // TM_GEMM) * TM_GEMM
    ends = jnp.cumsum(padded)
    offsets = ends - padded
    total = ends[-1]
    tile_start = jnp.arange(n_tiles, dtype=I32) * TM_GEMM
    tile_valid = (tile_start < total).astype(I32)
    last_block = jnp.maximum(total // TM_GEMM - 1, 0)
    tile_block = jnp.minimum(jnp.arange(n_tiles, dtype=I32), last_block)
    n_flat = counts.shape[0]
    tile_expert = jnp.minimum(jnp.sum((tile_block[:, None] * TM_GEMM >= ends[None, :]).astype(I32), axis=1), n_flat - 1)
    zero_lo = jnp.concatenate([(offsets + counts) // ZERO_ROWS, total[None] // ZERO_ROWS]).astype(I32)
    zero_hi = jnp.concatenate([ends // ZERO_ROWS, jnp.full((1,), n_tiles * TM_GEMM // ZERO_ROWS, I32)]).astype(I32)
    return offsets, tile_expert.astype(I32), tile_valid, zero_lo, zero_hi


def _dest_blocks(route, offsets):
    n = route.shape[0]
    f = route[:, ROUTE_F1:ROUTE_F2 + 1].astype(I32)
    rank = route[:, ROUTE_R1:ROUTE_R2 + 1].astype(I32)
    dest = jnp.sum(jnp.where(f[..., None] == jnp.arange(offsets.shape[0], dtype=I32), offsets, 0), axis=-1) + rank
    return dest.reshape(n // TM_ROW, TM_ROW, TOP_K_INNER).transpose(0, 2, 1).reshape(n // TM_ROW, 1, TOP_K_INNER * TM_ROW)


def kernel(x_prompt, x_sample, cache_k, cache_v, state_pool, g_mix, w_in, w_pool, pool_scale, w_out, g_ffn,
           w_router_group, b_router_group, w_router_expert, b_router_expert, w_gate, w_up, w_down, g_final):
    depth = g_mix.shape[0]
    assert depth == 1, "single-layer step"
    b, s, d = x_prompt.shape
    nd, t_new, _ = x_sample.shape
    n_heads = cache_k.shape[3]
    attn_w = n_heads * HEAD_DIM
    pool_w = state_pool.shape[3]
    past = cache_k.shape[2]
    keep = min(ATTN_REACH, s)
    n_groups, n_experts = w_router_expert.shape[1], w_router_expert.shape[3]
    n_flat = n_groups * n_experts
    assert n_groups + n_flat <= LANES
    slopes = _alibi_slopes(n_heads)

    w_in_b = w_in[0].astype(BF16)
    w_out_b = w_out[0].astype(BF16)
    gw = pool_w // len(POOL_WINDOWS)
    w_bd = jnp.zeros((pool_w, pool_w), F32)
    for g in range(len(POOL_WINDOWS)):
        w_bd = w_bd.at[g * gw:(g + 1) * gw, g * gw:(g + 1) * gw].set(w_pool[0, g])
    w_bd = w_bd.astype(BF16)
    w_route = jnp.concatenate([w_router_group[0], jnp.transpose(w_router_expert[0], (1, 0, 2)).reshape(d, n_flat)], axis=1)
    w_route = jnp.pad(w_route, ((0, 0), (0, LANES - n_groups - n_flat)))
    b_route = jnp.pad(jnp.concatenate([b_router_group[0], b_router_expert[0].reshape(n_flat)]),
                      (0, LANES - n_groups - n_flat))[None]
    w_gate_f, w_up_f = w_gate[0].reshape(n_flat, d, -1), w_up[0].reshape(n_flat, d, -1)
    w_down_f = w_down[0].reshape(n_flat, -1, d)

    n_p = b * s
    u_p, q_p, k_p, v_p, kt_p, vt_p = _proj_in(x_prompt.reshape(n_p, d), g_mix, w_in_b, pool_w=pool_w, attn_w=attn_w,
                                               seq=s, keep=keep)
    attn_p = _attn_prompt(slopes, q_p.reshape(b, s, attn_w), k_p.reshape(b, s, attn_w), v_p.reshape(b, s, attn_w))
    u_p3 = u_p.reshape(b, s, pool_w)
    pool_p = _pool_prompt(u_p3, w_bd, pool_scale)

    n_s = nd * t_new
    u_s, q_s, k_s, v_s = _proj_in(x_sample.reshape(n_s, d), g_mix, w_in_b, pool_w=pool_w, attn_w=attn_w)
    cache_kt = jnp.transpose(cache_k[0], (0, 2, 3, 1))
    cache_vt = jnp.transpose(cache_v[0], (0, 2, 3, 1))
    as3 = lambda a: a.reshape(nd, t_new, attn_w)
    attn_s = _attn_sample(slopes, as3(q_s), as3(k_s), as3(v_s), cache_kt, cache_vt)
    state_t = jnp.transpose(state_pool[0], (1, 0, 2))
    u_st = jnp.transpose(u_s.reshape(nd, t_new, pool_w), (1, 0, 2))
    pool_st, new_state_t = _pool_sample(state_t, u_st, w_bd, pool_scale)
    pool_s = jnp.transpose(pool_st, (1, 0, 2)).reshape(n_s, pool_w)

    route_kw = dict(n_groups=n_groups, n_experts=n_experts)
    h_p, route_p, cnt_p = _mix_route(x_prompt.reshape(n_p, d), pool_p.reshape(n_p, pool_w), attn_p.reshape(n_p, attn_w),
                                     w_out_b, g_ffn, w_route, b_route, jnp.zeros((1, LANES), F32), **route_kw)
    h_s, route_s, cnt_all = _mix_route(x_sample.reshape(n_s, d), pool_s, attn_s.reshape(n_s, attn_w),
                                       w_out_b, g_ffn, w_route, b_route, cnt_p, **route_kw)

    counts = cnt_all[0, n_groups:n_groups + n_flat].astype(I32)
    n_tiles = (TOP_K_INNER * (n_p + n_s) + n_flat * (TM_GEMM - 1)) // TM_GEMM
    offsets, tile_expert, tile_valid, zero_lo, zero_hi = _sort_tables(counts, n_tiles)
    dest_p, dest_s = _dest_blocks(route_p, offsets), _dest_blocks(route_s, offsets)

    xs = _dispatch(zero_lo, zero_hi, jnp.concatenate([dest_p, dest_s], axis=0), h_p, h_s, g_ffn,
                   rows=n_tiles * TM_GEMM)
    ys = _moe_gemm(tile_expert, tile_valid, xs, w_gate_f, w_up_f, w_down_f)
    y_p = _combine(dest_p, h_p, route_p, g_final[None], ys)
    y_s = _combine(dest_s, h_s, route_s, g_final[None], ys)

    y_prompt = y_p.reshape(b, s, d)
    y_sample = y_s.reshape(nd, t_new, d)
    k_prompt = jnp.transpose(kt_p.reshape(b, n_heads, HEAD_DIM, keep), (0, 3, 1, 2))[None]
    v_prompt = jnp.transpose(vt_p.reshape(b, n_heads, HEAD_DIM, keep), (0, 3, 1, 2))[None]
    pool_prompt = u_p3[:, s - POOL_STATE:][None]
    k_sample = k_s.reshape(1, nd, t_new, n_heads, HEAD_DIM)
    v_sample = v_s.reshape(1, nd, t_new, n_heads, HEAD_DIM)
    pool_sample = jnp.transpose(new_state_t[-POOL_STATE:], (1, 0, 2))[None]
    return (y_prompt, y_sample, k_prompt, v_prompt, pool_prompt, k_sample, v_sample, pool_sample)
```

```python
import functools
import math

import numpy as np
import jax
import jax.numpy as jnp
from jax import lax
from jax.experimental import pallas as pl
from jax.experimental.pallas import tpu as pltpu

F32 = jnp.float32
BF16 = jnp.bfloat16
I32 = jnp.int32

HEAD_DIM = 64
POOL_WINDOWS = (2, 4, 8, 16)
POOL_STATE = max(POOL_WINDOWS) - 1
BRANCHES = ((128, 1), (512, 4), (2048, 16))
STEPS = BRANCHES[0][0] // BRANCHES[0][1]
ATTN_REACH = max(w for w, _ in BRANCHES)
MAX_DIL = max(d for _, d in BRANCHES)
TOP_K_INNER = 2
RMS_EPS = 1e-6
LANES = 128
NEG_INF = float("-inf")

VMEM_LIMIT = 56 * 1024 * 1024

TM_PROJ = 512
TM_MIX = 512
TM_ROW = 512
TM_GEMM = 512
ZERO_ROWS = 64
ATTN_GROUP = 8


def _alibi_slopes(n_heads):
    def geometric(n):
        start = 2.0 ** (-8.0 / n)
        return [start ** (i + 1) for i in range(n)]
    c = 2 ** int(math.floor(math.log2(n_heads)))
    s = geometric(c)
    if c < n_heads:
        s = s + geometric(2 * c)[0::2][: n_heads - c]
    return jnp.asarray(s, dtype=F32)


def _rms(x, g):
    return x * lax.rsqrt(jnp.mean(x * x, axis=-1, keepdims=True) + RMS_EPS) * g


def _params(*sem):
    return pltpu.CompilerParams(dimension_semantics=sem, vmem_limit_bytes=VMEM_LIMIT)


def _proj_in_body(x_ref, g_ref, w_ref, u_ref, q_ref, k_ref, v_ref, *t_refs, pool_w, attn_w, tiles_per_seq,
                  keep_tiles, t_new):
    xn = _rms(x_ref[...], g_ref[...]).astype(BF16)

    def proj(lo, n):
        return jnp.dot(xn, w_ref[:, lo:lo + n], preferred_element_type=F32)

    u_ref[...] = proj(0, pool_w)
    q_ref[...] = proj(pool_w, attn_w) * (HEAD_DIM ** -0.5)
    k = proj(pool_w + attn_w, attn_w)
    v = proj(pool_w + 2 * attn_w, attn_w)
    k_ref[...] = k
    v_ref[...] = v
    if t_new is None:
        kt_ref, vt_ref = t_refs
        j = pl.program_id(0) % tiles_per_seq

        @pl.when(j >= tiles_per_seq - keep_tiles)
        def _():
            kt_ref[0] = k.T
            vt_ref[0] = v.T
    else:
        ktn_ref, vtn_ref, cols_scr = t_refs
        n_seq = ktn_ref.shape[2]
        for val, dst in ((k, ktn_ref), (v, vtn_ref)):
            for c in range(attn_w // LANES):
                lanes = slice(c * LANES, (c + 1) * LANES)
                cols_scr[...] = val[:, lanes]
                for t in range(t_new):
                    dst[t, lanes, :] = cols_scr[pl.ds(t, n_seq, stride=t_new), :].T


def _proj_in(x, g, w_bf16, *, pool_w, attn_w, seq=None, keep=None, t_new=None):
    n, d = x.shape
    tm = TM_PROJ if t_new is None else n
    grid = (n // tm,)
    row = lambda i: (i, 0)
    out_shape = [jax.ShapeDtypeStruct((n, pool_w), F32)] + [jax.ShapeDtypeStruct((n, attn_w), F32)] * 3
    out_specs = [pl.BlockSpec((tm, pool_w), row)] + [pl.BlockSpec((tm, attn_w), row)] * 3
    tiles_per_seq = keep_tiles = 0
    scratch = []
    if t_new is None:
        tiles_per_seq, keep_tiles = seq // tm, keep // tm
        first = tiles_per_seq - keep_tiles
        tmap = lambda i: (i // tiles_per_seq, 0, jnp.maximum(i % tiles_per_seq - first, 0))
        out_shape += [jax.ShapeDtypeStruct((n // seq, attn_w, keep), F32)] * 2
        out_specs += [pl.BlockSpec((1, attn_w, tm), tmap)] * 2
    else:
        assert n // t_new == LANES, "one square transpose per (step, column block)"
        out_shape += [jax.ShapeDtypeStruct((t_new, attn_w, n // t_new), F32)] * 2
        out_specs += [pl.BlockSpec((t_new, attn_w, n // t_new), lambda i: (0, 0, 0))] * 2
        scratch = [pltpu.VMEM((n, LANES), F32)]
    body = functools.partial(_proj_in_body, pool_w=pool_w, attn_w=attn_w, tiles_per_seq=tiles_per_seq,
                             keep_tiles=keep_tiles, t_new=t_new)
    return pl.pallas_call(
        body, grid=grid,
        in_specs=[pl.BlockSpec((tm, d), row), pl.BlockSpec((1, d), lambda i: (0, 0)),
                  pl.BlockSpec(w_bf16.shape, lambda i: (0, 0))],
        out_specs=out_specs, out_shape=out_shape, scratch_shapes=scratch,
        compiler_params=_params("arbitrary"), name="proj_in",
    )(x, g, w_bf16)


def _attn_prompt_body(slopes_ref, q_hbm, k_hbm, v_hbm, o_ref, qkv_scr, bias_scr, s_scr, p_scr, o_scr, m_scr, l_scr, sems,
                      *, seq, n_pairs, n_steps):
    hp = pl.program_id(1)
    gstep = pl.program_id(0) * n_pairs + hp
    buf = gstep % 2
    blk = STEPS
    n_it = seq // blk
    cls = seq // MAX_DIL

    def class_copies(st, b):
        lanes = pl.ds(pl.multiple_of((st % n_pairs) * LANES, LANES), LANES)
        return [pltpu.make_async_copy(src.at[st // n_pairs, :, c, lanes], qkv_scr.at[b, a, pl.ds(c * cls, cls), :],
                                      sems.at[b, a])
                for a, src in enumerate((q_hbm, k_hbm, v_hbm)) for c in range(MAX_DIL)]

    @pl.when(gstep == 0)
    def _():
        for cp in class_copies(gstep, 0):
            cp.start()

    @pl.when(gstep + 1 < n_steps)
    def _():
        for cp in class_copies(gstep + 1, 1 - buf):
            cp.start()

    for cp in class_copies(gstep, buf):
        cp.wait()
    qp_scr, kp_scr, vp_scr = (qkv_scr.at[buf, a] for a in range(3))

    lane = lax.broadcasted_iota(I32, (blk, LANES), 1)
    head_a = lane < HEAD_DIM
    row = lax.broadcasted_iota(I32, (2 * blk, 2 * blk), 0)
    col = lax.broadcasted_iota(I32, (2 * blk, 2 * blk), 1)
    slope = jnp.where(row < blk, slopes_ref[2 * hp], slopes_ref[2 * hp + 1])

    for bi, (_, d) in enumerate(BRANCHES):
        nb = n_it // d
        n_chunk = MAX_DIL // d
        a_rows = blk // n_chunk
        sh = a_rows.bit_length() - 1
        assert a_rows == 1 << sh and a_rows % 8 == 0

        def seq_index(i, n_chunk=n_chunk, a_rows=a_rows, sh=sh):
            i = i & (blk - 1)
            return (i & (a_rows - 1)) * n_chunk + (i >> sh)

        step = seq_index(row) + blk - (seq_index(col) + (col & blk))
        bias = jnp.where((step >= 0) & (step <= STEPS), -slope * (d * step).astype(F32), NEG_INF)
        bias_scr[0] = bias
        bias_scr[1] = jnp.where(col < blk, NEG_INF, bias)

        def group(j, carry, bi=bi, d=d, nb=nb, n_chunk=n_chunk, a_rows=a_rows):
            def chunks(g):
                it = j * ATTN_GROUP + g
                r = it // nb
                n = it % nb
                cur = [pl.ds(pl.multiple_of((r + d * c) * cls + n * a_rows, 8), a_rows) for c in range(n_chunk)]
                prev = [pl.ds(pl.multiple_of((r + d * c) * cls + jnp.maximum(n - 1, 0) * a_rows, 8), a_rows)
                        for c in range(n_chunk)]
                return n, cur, prev

            def gather(ref, sls):
                return jnp.concatenate([ref[sl, :] for sl in sls], axis=0)

            for g in range(ATTN_GROUP):
                n, cur, prev = chunks(g)
                qb = gather(qp_scr, cur)
                q2 = jnp.concatenate([jnp.where(head_a, qb, 0.0), jnp.where(head_a, 0.0, qb)], axis=0).astype(BF16)
                kc = gather(kp_scr, prev + cur).astype(BF16)
                s = lax.dot_general(q2, kc, (((1,), (1,)), ((), ())), preferred_element_type=F32)
                s_scr[g] = s + bias_scr[jnp.where(n == 0, 1, 0)]
            for g in range(ATTN_GROUP):
                _, cur, _ = chunks(g)
                s = s_scr[g]
                m = jnp.max(s, axis=1, keepdims=True)
                p_scr[g] = jnp.exp(s - m).astype(BF16)
                m2 = jnp.where(head_a, m[:blk], m[blk:])
                for c, sl in enumerate(cur):
                    m_scr[bi, sl, :] = m2[c * a_rows:(c + 1) * a_rows]
            ones = jnp.ones((2 * blk, LANES), BF16)
            for g in range(ATTN_GROUP):
                _, cur, prev = chunks(g)
                vc = jnp.concatenate([gather(vp_scr, prev + cur).astype(BF16), ones], axis=1)
                ol = jnp.dot(p_scr[g], vc, preferred_element_type=F32)
                o2 = jnp.where(head_a, ol[:blk, :LANES], ol[blk:, :LANES])
                l2 = jnp.where(head_a, ol[:blk, LANES:], ol[blk:, LANES:])
                for c, sl in enumerate(cur):
                    o_scr[bi, sl, :] = o2[c * a_rows:(c + 1) * a_rows]
                    l_scr[bi, sl, :] = l2[c * a_rows:(c + 1) * a_rows]
            return carry

        lax.fori_loop(0, n_it // ATTN_GROUP, group, 0)

    def merge(c, carry):
        rs = pl.ds(pl.multiple_of(c * cls, cls), cls)
        ms = [m_scr[b, rs, :] for b in range(len(BRANCHES))]
        mx = functools.reduce(jnp.maximum, ms)
        num = jnp.zeros((cls, LANES), F32)
        den = jnp.zeros((cls, LANES), F32)
        for b in range(len(BRANCHES)):
            a = jnp.exp(ms[b] - mx)
            num = num + a * o_scr[b, rs, :]
            den = den + a * l_scr[b, rs, :]
        o_ref[0, pl.ds(c, cls, stride=MAX_DIL), :] = num / den
        return carry

    lax.fori_loop(0, MAX_DIL, merge, 0)


def _attn_prompt(slopes, q, k, v):
    b, s, hw = q.shape
    assert s % (STEPS * max(d for _, d in BRANCHES)) == 0, "sequence must be a multiple of the widest span"
    assert hw % LANES == 0 and LANES == 2 * HEAD_DIM
    n_pairs = hw // LANES
    nbr = len(BRANCHES)
    by_class = lambda a: a.reshape(b, s // MAX_DIL, MAX_DIL, hw)
    hbm = pl.BlockSpec(memory_space=pl.ANY)
    return pl.pallas_call(
        functools.partial(_attn_prompt_body, seq=s, n_pairs=n_pairs, n_steps=b * n_pairs),
        grid=(b, n_pairs),
        in_specs=[pl.BlockSpec(memory_space=pltpu.SMEM), hbm, hbm, hbm],
        out_specs=pl.BlockSpec((1, s, LANES), lambda i, j: (i, 0, j)),
        out_shape=jax.ShapeDtypeStruct((b, s, hw), F32),
        scratch_shapes=[pltpu.VMEM((2, 3, s, LANES), F32),
                        pltpu.VMEM((2, 2 * STEPS, 2 * STEPS), F32),
                        pltpu.VMEM((ATTN_GROUP, 2 * STEPS, 2 * STEPS), F32),
                        pltpu.VMEM((ATTN_GROUP, 2 * STEPS, 2 * STEPS), BF16)]
        + [pltpu.VMEM((nbr, s, LANES), F32)] * 3 + [pltpu.SemaphoreType.DMA((2, 3))],
        compiler_params=_params("arbitrary", "arbitrary"), name="attn_prompt",
    )(slopes, by_class(q), by_class(k), by_class(v))


def _multiplicity(dist):
    mult = jnp.zeros(dist.shape, F32)
    for w, d in BRANCHES:
        assert d & (d - 1) == 0
        mult = mult + ((dist >= 0) & ((dist & (d - 1)) == 0) & (dist <= w)).astype(F32)
    return mult


def _attn_sample_body(slopes_ref, q_ref, kn_ref, vn_ref, kt_ref, vt_ref, o_ref, *, n_heads):
    t_new = q_ref.shape[1]
    past = kt_ref.shape[3]
    dist = past + lax.broadcasted_iota(I32, (t_new, past), 0) - lax.broadcasted_iota(I32, (t_new, past), 1)
    dist_n = lax.broadcasted_iota(I32, (t_new, t_new), 0) - lax.broadcasted_iota(I32, (t_new, t_new), 1)
    mult, mult_n = _multiplicity(dist), _multiplicity(dist_n)
    dist_f, dist_nf = dist.astype(F32), dist_n.astype(F32)
    nt = (((1,), (1,)), ((), ()))
    for h in range(n_heads):
        slope = slopes_ref[h]
        cols = slice(h * HEAD_DIM, (h + 1) * HEAD_DIM)
        qh = q_ref[0, :, cols].astype(BF16)
        kn = kn_ref[0, :, cols].astype(BF16)
        vn = vn_ref[0, :, cols].astype(BF16)
        s = jnp.dot(qh, kt_ref[0, h].astype(BF16), preferred_element_type=F32)
        s = jnp.where(mult > 0, s - slope * dist_f, NEG_INF)
        sn = lax.dot_general(qh, kn, nt, preferred_element_type=F32)
        sn = jnp.where(mult_n > 0, sn - slope * dist_nf, NEG_INF)
        m = jnp.maximum(jnp.max(s, axis=1, keepdims=True), jnp.max(sn, axis=1, keepdims=True))
        p = mult * jnp.exp(s - m)
        pn = mult_n * jnp.exp(sn - m)
        l = jnp.sum(p, axis=1, keepdims=True) + jnp.sum(pn, axis=1, keepdims=True)
        o = lax.dot_general(p.astype(BF16), vt_ref[0, h].astype(BF16), nt, preferred_element_type=F32)
        o = o + jnp.dot(pn.astype(BF16), vn, preferred_element_type=F32)
        o_ref[0, :, cols] = o / l


def _attn_sample(slopes, q, k_new, v_new, cache_kt, cache_vt):
    n, t, hw = q.shape
    _, h, e, past = cache_kt.shape
    assert past >= ATTN_REACH, "every strided key of every branch must exist in the window buffer"
    new = pl.BlockSpec((1, t, hw), lambda i: (i, 0, 0))
    old = pl.BlockSpec((1, h, e, past), lambda i: (i, 0, 0, 0))
    return pl.pallas_call(
        functools.partial(_attn_sample_body, n_heads=h),
        grid=(n,),
        in_specs=[pl.BlockSpec(memory_space=pltpu.SMEM), new, new, new, old, old],
        out_specs=new,
        out_shape=jax.ShapeDtypeStruct((n, t, hw), F32),
        compiler_params=_params("arbitrary"), name="attn_sample",
    )(slopes, q, k_new, v_new, cache_kt, cache_vt)


def _pool_windows(width):
    gw = width // len(POOL_WINDOWS)
    lane = lax.broadcasted_iota(I32, (1, width), 1)
    win = jnp.zeros((1, width), I32)
    for g, w in enumerate(POOL_WINDOWS):
        win = jnp.where((lane >= g * gw) & (lane < (g + 1) * gw), w, win)
    return win


def _pool_prompt_body(u_ref, w_ref, sc_ref, o_ref, ext_scr, *, tm):
    j = pl.program_id(1)
    width = u_ref.shape[2]
    pad = 2 * (POOL_STATE + 1)
    start = pl.multiple_of(j * tm, tm)
    lead = POOL_STATE + 1
    prev = u_ref[0, pl.ds(pl.multiple_of(jnp.maximum(start - lead, 0), lead), lead), :]
    ext_scr[0:pad - lead, :] = jnp.zeros((pad - lead, width), F32)
    ext_scr[pad - lead:pad, :] = jnp.where(j > 0, prev, 0.0)
    ext_scr[pad:, :] = u_ref[0, pl.ds(start, tm), :]
    win = _pool_windows(width)
    tok = ext_scr[pad:, :]
    acc = tok
    for i in range(1, max(POOL_WINDOWS)):
        acc = acc + jnp.where(i < win, ext_scr[pl.ds(pad - i, tm), :], 0.0)
    pos = start + lax.broadcasted_iota(I32, (tm, width), 0)
    cnt = jnp.minimum(win, pos + 1).astype(F32)
    diff = (acc / cnt - tok).astype(BF16)
    o_ref[0] = jnp.dot(diff, w_ref[...], preferred_element_type=F32) * sc_ref[...]


def _pool_prompt(u, w_bd, scale):
    b, s, w = u.shape
    tm = TM_PROJ
    return pl.pallas_call(
        functools.partial(_pool_prompt_body, tm=tm),
        grid=(b, s // tm),
        in_specs=[pl.BlockSpec((1, s, w), lambda i, j: (i, 0, 0)), pl.BlockSpec((w, w), lambda i, j: (0, 0)),
                  pl.BlockSpec((1, w), lambda i, j: (0, 0))],
        out_specs=pl.BlockSpec((1, tm, w), lambda i, j: (i, j, 0)),
        out_shape=jax.ShapeDtypeStruct((b, s, w), F32),
        scratch_shapes=[pltpu.VMEM((tm + 2 * (POOL_STATE + 1), w), F32)],
        compiler_params=_params("arbitrary", "arbitrary"), name="pool_prompt",
    )(u, w_bd, scale)


def _pool_sample_body(st_ref, u_ref, w_ref, sc_ref, o_ref, ns_ref):
    t_new, _, width = u_ref.shape
    n_state = st_ref.shape[0]
    win = _pool_windows(width)

    def row(k):
        return st_ref[k] if k < n_state else u_ref[k - n_state]

    for t in range(t_new):
        tok = u_ref[t]
        acc = tok
        for i in range(1, max(POOL_WINDOWS)):
            acc = acc + jnp.where(i < win, row(n_state + t - i), 0.0)
        diff = (acc / win.astype(F32) - tok).astype(BF16)
        o_ref[t] = jnp.dot(diff, w_ref[...], preferred_element_type=F32) * sc_ref[...]
    for k in range(n_state):
        ns_ref[k] = row(k + t_new)


def _pool_sample(state_t, u, w_bd, scale):
    n_state = state_t.shape[0]
    assert n_state >= POOL_STATE
    return pl.pallas_call(
        _pool_sample_body,
        out_shape=[jax.ShapeDtypeStruct(u.shape, F32), jax.ShapeDtypeStruct(state_t.shape, F32)],
        compiler_params=pltpu.CompilerParams(vmem_limit_bytes=VMEM_LIMIT), name="pool_sample",
    )(state_t, u, w_bd, scale)


ROUTE_F1, ROUTE_F2, ROUTE_G1, ROUTE_G2, ROUTE_R1, ROUTE_R2 = range(6)
ROUTE_ROWS = 8


def _split_bf16(x):
    hi = x.astype(BF16)
    return hi, (x - hi.astype(F32)).astype(BF16)


def _mix_route_body(x_ref, pool_ref, attn_ref, wo_ref, g_ref, wr_ref, br_ref, cnt_in_ref, h_ref, route_ref,
                    route_t_ref, cnt_out_ref, carry_scr, *, n_groups, n_experts):
    i = pl.program_id(0)
    tm = x_ref.shape[0]
    pool_w = pool_ref.shape[1]

    @pl.when(i == 0)
    def _():
        carry_scr[...] = cnt_in_ref[...]

    h = x_ref[...]
    h = h + jnp.dot(pool_ref[...].astype(BF16), wo_ref[0:pool_w, :], preferred_element_type=F32)
    h = h + jnp.dot(attn_ref[...].astype(BF16), wo_ref[pool_w:, :], preferred_element_type=F32)
    h_ref[...] = h

    hn_hi, hn_lo = _split_bf16(_rms(h, g_ref[...]))
    w_hi, w_lo = _split_bf16(wr_ref[...])
    hi_both = jnp.dot(hn_hi, jnp.concatenate([w_hi, w_lo], axis=1), preferred_element_type=F32)
    logits = (hi_both[:, :LANES] + hi_both[:, LANES:]
              + jnp.dot(hn_lo, w_hi, preferred_element_type=F32)) + br_ref[...]

    lane = lax.broadcasted_iota(I32, (tm, LANES), 1).astype(F32)

    def first_lane(mask):
        return jnp.min(jnp.where(mask, lane, float(LANES)), axis=1, keepdims=True)

    is_g = lane < n_groups
    lg = jnp.where(is_g, logits, NEG_INF)
    mg = jnp.max(lg, axis=1, keepdims=True)
    p_sel = 1.0 / jnp.sum(jnp.exp(lg - mg), axis=1, keepdims=True)
    g_top = first_lane(lg == mg)
    lo = n_groups + g_top * n_experts
    in_grp = (lane >= lo) & (lane < lo + n_experts)
    le = jnp.where(in_grp, logits, NEG_INF)
    ee = jnp.exp(le - jnp.max(le, axis=1, keepdims=True))
    pe = ee / jnp.sum(ee, axis=1, keepdims=True)
    v1 = jnp.max(jnp.where(in_grp, pe, -1.0), axis=1, keepdims=True)
    i1 = first_lane(in_grp & (pe == v1))
    rest = in_grp & (lane != i1)
    v2 = jnp.max(jnp.where(rest, pe, -1.0), axis=1, keepdims=True)
    i2 = first_lane(rest & (pe == v2))
    gate1 = p_sel * (v1 / (v1 + v2))
    gate2 = p_sel * (v2 / (v1 + v2))

    sel1, sel2 = lane == i1, lane == i2
    onehot = (sel1 | sel2).astype(BF16)
    tri = (lax.broadcasted_iota(I32, (tm, tm), 1) < lax.broadcasted_iota(I32, (tm, tm), 0)).astype(BF16)
    running = jnp.dot(tri, onehot, preferred_element_type=F32) + carry_scr[...]
    rank1 = jnp.sum(jnp.where(sel1, running, 0.0), axis=1, keepdims=True)
    rank2 = jnp.sum(jnp.where(sel2, running, 0.0), axis=1, keepdims=True)
    carry_scr[...] = carry_scr[...] + jnp.sum(onehot.astype(F32), axis=0, keepdims=True)
    cnt_out_ref[...] = carry_scr[...]

    rec = jnp.zeros((tm, LANES), F32)
    for idx, val in ((ROUTE_F1, i1 - n_groups), (ROUTE_F2, i2 - n_groups),
                     (ROUTE_G1, gate1), (ROUTE_G2, gate2), (ROUTE_R1, rank1), (ROUTE_R2, rank2)):
        rec = jnp.where(lane == idx, val, rec)
    route_ref[...] = rec
    route_t_ref[...] = rec.T[:ROUTE_ROWS]


def _mix_route(x, pool, attn, w_out_bf16, g_ffn, w_route, b_route, cnt_in, *, n_groups, n_experts):
    n, d = x.shape
    tm = TM_MIX
    row = lambda i: (i, 0)
    fix = lambda i: (0, 0)
    return pl.pallas_call(
        functools.partial(_mix_route_body, n_groups=n_groups, n_experts=n_experts),
        grid=(n // tm,),
        in_specs=[pl.BlockSpec((tm, d), row), pl.BlockSpec((tm, pool.shape[1]), row),
                  pl.BlockSpec((tm, attn.shape[1]), row), pl.BlockSpec(w_out_bf16.shape, fix),
                  pl.BlockSpec((1, d), fix), pl.BlockSpec(w_route.shape, fix), pl.BlockSpec((1, LANES), fix),
                  pl.BlockSpec((1, LANES), fix)],
        out_specs=[pl.BlockSpec((tm, d), row), pl.BlockSpec((tm, LANES), row),
                   pl.BlockSpec((ROUTE_ROWS, tm), lambda i: (0, i)), pl.BlockSpec((1, LANES), fix)],
        out_shape=[jax.ShapeDtypeStruct((n, d), F32), jax.ShapeDtypeStruct((n, LANES), F32),
                   jax.ShapeDtypeStruct((ROUTE_ROWS, n), F32), jax.ShapeDtypeStruct((1, LANES), F32)],
        scratch_shapes=[pltpu.VMEM((1, LANES), F32)],
        compiler_params=_params("arbitrary"), name="mix_route",
    )(x, pool, attn, w_out_bf16, g_ffn, w_route, b_route, cnt_in)


def _dispatch_body(zlo_ref, zhi_ref, dest_ref, hp_ref, hs_ref, g_ref, xs_ref, xn_scr, zero_scr, sems, *, tiles_p,
                   n_steps):
    tm = hp_ref.shape[0]
    i = pl.program_id(0)
    zero_sem = 2 * TOP_K_INNER

    @pl.when(i == 0)
    def _():
        zero_scr[...] = jnp.zeros(zero_scr.shape, F32)
        for start in (True, False):
            def chunk(c, carry, start=start):
                cp = pltpu.make_async_copy(zero_scr, xs_ref.at[pl.ds(pl.multiple_of(c * ZERO_ROWS, ZERO_ROWS), ZERO_ROWS)],
                                           sems.at[zero_sem])
                cp.start() if start else cp.wait()
                return carry

            def segment(e, carry, chunk=chunk):
                return lax.fori_loop(zlo_ref[e], zhi_ref[e], chunk, carry)
            lax.fori_loop(0, zlo_ref.shape[0], segment, 0)

    buf = i % 2

    def wait_rows(b):
        for slot in range(TOP_K_INNER):
            pltpu.make_async_copy(xn_scr.at[b], xs_ref.at[pl.ds(0, tm)], sems.at[b * TOP_K_INNER + slot]).wait()

    @pl.when(i >= 2)
    def _():
        wait_rows(buf)

    @pl.when(i < tiles_p)
    def _():
        xn_scr[buf] = _rms(hp_ref[...], g_ref[...])

    @pl.when(i >= tiles_p)
    def _():
        xn_scr[buf] = _rms(hs_ref[...], g_ref[...])

    for r in range(tm):
        for slot in range(TOP_K_INNER):
            d = dest_ref[0, 0, slot * tm + r]
            pltpu.make_async_copy(xn_scr.at[buf, pl.ds(r, 1)], xs_ref.at[pl.ds(d, 1)],
                                  sems.at[buf * TOP_K_INNER + slot]).start(priority=slot)

    @pl.when(i == n_steps - 1)
    def _():
        if n_steps > 1:
            wait_rows(1 - buf)
        wait_rows(buf)


def _dispatch(zero_lo, zero_hi, dest, h_p, h_s, g_ffn, *, rows):
    tm = TM_ROW
    d = h_p.shape[1]
    tiles_p, tiles_s = h_p.shape[0] // tm, h_s.shape[0] // tm
    return pl.pallas_call(
        functools.partial(_dispatch_body, tiles_p=tiles_p, n_steps=tiles_p + tiles_s),
        grid_spec=pltpu.PrefetchScalarGridSpec(
            num_scalar_prefetch=2, grid=(tiles_p + tiles_s,),
            in_specs=[pl.BlockSpec((1, 1, TOP_K_INNER * tm), lambda i, lo, hi: (i, 0, 0), memory_space=pltpu.SMEM),
                      pl.BlockSpec((tm, d), lambda i, lo, hi: (jnp.minimum(i, tiles_p - 1), 0)),
                      pl.BlockSpec((tm, d), lambda i, lo, hi: (jnp.maximum(i - tiles_p, 0), 0)),
                      pl.BlockSpec((1, d), lambda i, lo, hi: (0, 0))],
            out_specs=pl.BlockSpec(memory_space=pl.ANY),
            scratch_shapes=[pltpu.VMEM((2, tm, d), F32), pltpu.VMEM((ZERO_ROWS, d), F32),
                            pltpu.SemaphoreType.DMA((2 * TOP_K_INNER + 1,))]),
        out_shape=jax.ShapeDtypeStruct((rows, d), F32),
        compiler_params=_params("arbitrary"), name="dispatch",
    )(zero_lo, zero_hi, dest, h_p, h_s, g_ffn)


GEMM_X_BUFS = 3
GEMM_Y_BUFS = 2


def _moe_gemm_body(expert_ref, valid_ref, xs_hbm, wg_ref, wu_ref, wd_ref, ys_hbm, x_buf, y_buf, wg_scr, wu_scr, wd_scr,
                   x_sems, y_sems, *, n_tiles):
    t = pl.program_id(0)
    tm = x_buf.shape[1]
    ahead = GEMM_X_BUFS - 1

    def x_copy(s):
        return pltpu.make_async_copy(xs_hbm.at[pl.ds(pl.multiple_of(s * tm, tm), tm)], x_buf.at[s % GEMM_X_BUFS],
                                     x_sems.at[s % GEMM_X_BUFS])

    def y_copy(s):
        return pltpu.make_async_copy(y_buf.at[s % GEMM_Y_BUFS], ys_hbm.at[pl.ds(pl.multiple_of(s * tm, tm), tm)],
                                     y_sems.at[s % GEMM_Y_BUFS])

    def tile_is_valid(s):
        return (s < n_tiles) & (valid_ref[jnp.minimum(s, n_tiles - 1)] > 0)

    @pl.when(t == 0)
    def _():
        for s in range(ahead):
            @pl.when(tile_is_valid(s))
            def _():
                x_copy(s).start()

    @pl.when(tile_is_valid(t + ahead))
    def _():
        x_copy(t + ahead).start()

    @pl.when((t == 0) | (expert_ref[t] != expert_ref[jnp.maximum(t - 1, 0)]))
    def _():
        wg_scr[...] = wg_ref[0].astype(BF16)
        wu_scr[...] = wu_ref[0].astype(BF16)
        wd_scr[...] = wd_ref[0].astype(BF16)

    @pl.when(t >= GEMM_Y_BUFS)
    def _():
        y_copy(t - GEMM_Y_BUFS).wait()

    ybuf = y_buf.at[t % GEMM_Y_BUFS]

    @pl.when(valid_ref[t] > 0)
    def _():
        x_copy(t).wait()
        x = x_buf[t % GEMM_X_BUFS].astype(BF16)
        gate = jnp.dot(x, wg_scr[...], preferred_element_type=F32)
        up = jnp.dot(x, wu_scr[...], preferred_element_type=F32)
        mid = (gate * jax.nn.sigmoid(gate) * up).astype(BF16)
        ybuf[...] = jnp.dot(mid, wd_scr[...], preferred_element_type=F32)

    @pl.when(valid_ref[t] == 0)
    def _():
        ybuf[...] = jnp.zeros(ybuf.shape, F32)

    y_copy(t).start()

    @pl.when(t == n_tiles - 1)
    def _():
        for back in range(min(GEMM_Y_BUFS, n_tiles) - 1, -1, -1):
            y_copy(t - back).wait()


def _moe_gemm(tile_expert, tile_valid, xs, w_gate, w_up, w_down):
    rows, d = xs.shape
    _, _, f = w_gate.shape
    tm = TM_GEMM
    n_tiles = rows // tm
    hbm = pl.BlockSpec(memory_space=pl.ANY)
    return pl.pallas_call(
        functools.partial(_moe_gemm_body, n_tiles=n_tiles),
        grid_spec=pltpu.PrefetchScalarGridSpec(
            num_scalar_prefetch=2, grid=(n_tiles,),
            in_specs=[hbm,
                      pl.BlockSpec((1, d, f), lambda t, e, v: (e[t], 0, 0)),
                      pl.BlockSpec((1, d, f), lambda t, e, v: (e[t], 0, 0)),
                      pl.BlockSpec((1, f, d), lambda t, e, v: (e[t], 0, 0))],
            out_specs=hbm,
            scratch_shapes=[pltpu.VMEM((GEMM_X_BUFS, tm, d), F32), pltpu.VMEM((GEMM_Y_BUFS, tm, d), F32),
                            pltpu.VMEM((d, f), BF16), pltpu.VMEM((d, f), BF16), pltpu.VMEM((f, d), BF16),
                            pltpu.SemaphoreType.DMA((GEMM_X_BUFS,)), pltpu.SemaphoreType.DMA((GEMM_Y_BUFS,))]),
        out_shape=jax.ShapeDtypeStruct((rows, d), F32),
        compiler_params=_params("arbitrary"), name="moe_gemm",
    )(tile_expert, tile_valid, xs, w_gate, w_up, w_down)


def _combine_body(dest_ref, next_ref, h_ref, route_ref, g_ref, ys_ref, y_ref, rows_scr, sems, *, n_steps):
    tm = h_ref.shape[0]
    i = pl.program_id(0)
    buf = i % 2

    def fetch(idx_ref, b):
        for r in range(tm):
            for slot in range(TOP_K_INNER):
                d = idx_ref[0, 0, slot * tm + r]
                pltpu.make_async_copy(ys_ref.at[pl.ds(d, 1)], rows_scr.at[b, slot, pl.ds(r, 1)],
                                      sems.at[b * TOP_K_INNER + slot]).start(priority=slot)

    @pl.when(i == 0)
    def _():
        fetch(dest_ref, 0)

    @pl.when(i + 1 < n_steps)
    def _():
        fetch(next_ref, 1 - buf)

    for slot in range(TOP_K_INNER):
        pltpu.make_async_copy(ys_ref.at[pl.ds(0, tm)], rows_scr.at[buf, slot], sems.at[buf * TOP_K_INNER + slot]).wait()
    route = route_ref[...]
    out = h_ref[...] + (route[:, ROUTE_G1:ROUTE_G1 + 1] * rows_scr[buf, 0]
                        + route[:, ROUTE_G2:ROUTE_G2 + 1] * rows_scr[buf, 1])
    y_ref[...] = _rms(out, g_ref[...])


def _combine(dest, h, route, g_final, ys):
    n, d = h.shape
    tm = TM_ROW
    n_steps = n // tm
    row = lambda i: (i, 0)
    idx_block = (1, 1, TOP_K_INNER * tm)
    return pl.pallas_call(
        functools.partial(_combine_body, n_steps=n_steps),
        grid=(n_steps,),
        in_specs=[pl.BlockSpec(idx_block, lambda i: (i, 0, 0), memory_space=pltpu.SMEM),
                  pl.BlockSpec(idx_block, lambda i: (jnp.minimum(i + 1, n_steps - 1), 0, 0), memory_space=pltpu.SMEM),
                  pl.BlockSpec((tm, d), row), pl.BlockSpec((tm, LANES), row), pl.BlockSpec((1, d), lambda i: (0, 0)),
                  pl.BlockSpec(memory_space=pl.ANY)],
        out_specs=pl.BlockSpec((tm, d), row),
        out_shape=jax.ShapeDtypeStruct((n, d), F32),
        scratch_shapes=[pltpu.VMEM((2, TOP_K_INNER, tm, d), F32), pltpu.SemaphoreType.DMA((2 * TOP_K_INNER,))],
        compiler_params=_params("arbitrary"), name="combine",
    )(dest, dest, h, route, g_final, ys)


def _sort_tables(counts, n_tiles):
    padded = ((counts + TM_GEMM - 1) // TM_GEMM) * TM_GEMM
    ends = jnp.cumsum(padded)
    offsets = ends - padded
    total = ends[-1]
    tile_start = jnp.arange(n_tiles, dtype=I32) * TM_GEMM
    tile_valid = (tile_start < total).astype(I32)
    last_block = jnp.maximum(total // TM_GEMM - 1, 0)
    tile_block = jnp.minimum(jnp.arange(n_tiles, dtype=I32), last_block)
    n_flat = counts.shape[0]
    tile_expert = jnp.minimum(jnp.sum((tile_block[:, None] * TM_GEMM >= ends[None, :]).astype(I32), axis=1), n_flat - 1)
    zero_lo = jnp.concatenate([(offsets + counts) // ZERO_ROWS, total[None] // ZERO_ROWS]).astype(I32)
    zero_hi = jnp.concatenate([ends // ZERO_ROWS, jnp.full((1,), n_tiles * TM_GEMM // ZERO_ROWS, I32)]).astype(I32)
    return offsets, tile_expert.astype(I32), tile_valid, zero_lo, zero_hi


def _dest_blocks(route_t, offsets):
    n = route_t.shape[1]
    experts = jnp.arange(offsets.shape[0], dtype=I32)[:, None]

    def dest(f_row, r_row):
        f = route_t[f_row].astype(I32)
        base = jnp.sum(jnp.where(f[None, :] == experts, offsets[:, None], 0), axis=0)
        return (base + route_t[r_row].astype(I32)).reshape(n // TM_ROW, TM_ROW)

    return jnp.concatenate([dest(ROUTE_F1, ROUTE_R1), dest(ROUTE_F2, ROUTE_R2)], axis=1)[:, None, :]


def kernel(x_prompt, x_sample, cache_k, cache_v, state_pool, g_mix, w_in, w_pool, pool_scale, w_out, g_ffn,
           w_router_group, b_router_group, w_router_expert, b_router_expert, w_gate, w_up, w_down, g_final):
    depth = g_mix.shape[0]
    assert depth == 1, "single-layer step"
    b, s, d = x_prompt.shape
    nd, t_new, _ = x_sample.shape
    n_heads = cache_k.shape[3]
    attn_w = n_heads * HEAD_DIM
    pool_w = state_pool.shape[3]
    past = cache_k.shape[2]
    keep = min(ATTN_REACH, s)
    n_groups, n_experts = w_router_expert.shape[1], w_router_expert.shape[3]
    n_flat = n_groups * n_experts
    assert n_groups + n_flat <= LANES
    slopes = _alibi_slopes(n_heads)

    w_in_b = w_in[0].astype(BF16)
    w_out_b = w_out[0].astype(BF16)
    gw = pool_w // len(POOL_WINDOWS)
    w_bd = jnp.zeros((pool_w, pool_w), F32)
    for g in range(len(POOL_WINDOWS)):
        w_bd = w_bd.at[g * gw:(g + 1) * gw, g * gw:(g + 1) * gw].set(w_pool[0, g])
    w_bd = w_bd.astype(BF16)
    w_route = jnp.concatenate([w_router_group[0], jnp.transpose(w_router_expert[0], (1, 0, 2)).reshape(d, n_flat)], axis=1)
    w_route = jnp.pad(w_route, ((0, 0), (0, LANES - n_groups - n_flat)))
    b_route = jnp.pad(jnp.concatenate([b_router_group[0], b_router_expert[0].reshape(n_flat)]),
                      (0, LANES - n_groups - n_flat))[None]
    w_gate_f, w_up_f = w_gate[0].reshape(n_flat, d, -1), w_up[0].reshape(n_flat, d, -1)
    w_down_f = w_down[0].reshape(n_flat, -1, d)

    n_p = b * s
    u_p, q_p, k_p, v_p, kt_p, vt_p = _proj_in(x_prompt.reshape(n_p, d), g_mix, w_in_b, pool_w=pool_w, attn_w=attn_w,
                                               seq=s, keep=keep)
    attn_p = _attn_prompt(slopes, q_p.reshape(b, s, attn_w), k_p.reshape(b, s, attn_w), v_p.reshape(b, s, attn_w))
    u_p3 = u_p.reshape(b, s, pool_w)
    pool_p = _pool_prompt(u_p3, w_bd, pool_scale)

    n_s = nd * t_new
    u_s, q_s, k_s, v_s, ktn_s, vtn_s = _proj_in(x_sample.reshape(n_s, d), g_mix, w_in_b, pool_w=pool_w, attn_w=attn_w,
                                                t_new=t_new)
    cache_kt = jnp.transpose(cache_k[0], (0, 2, 3, 1))
    cache_vt = jnp.transpose(cache_v[0], (0, 2, 3, 1))
    as3 = lambda a: a.reshape(nd, t_new, attn_w)
    attn_s = _attn_sample(slopes, as3(q_s), as3(k_s), as3(v_s), cache_kt, cache_vt)
    state_t = jnp.transpose(state_pool[0], (1, 0, 2))
    u_st = jnp.transpose(u_s.reshape(nd, t_new, pool_w), (1, 0, 2))
    pool_st, new_state_t = _pool_sample(state_t, u_st, w_bd, pool_scale)
    pool_s = jnp.transpose(pool_st, (1, 0, 2)).reshape(n_s, pool_w)

    route_kw = dict(n_groups=n_groups, n_experts=n_experts)
    h_p, route_p, route_pt, cnt_p = _mix_route(x_prompt.reshape(n_p, d), pool_p.reshape(n_p, pool_w),
                                               attn_p.reshape(n_p, attn_w), w_out_b, g_ffn, w_route, b_route,
                                               jnp.zeros((1, LANES), F32), **route_kw)
    h_s, route_s, route_st, cnt_all = _mix_route(x_sample.reshape(n_s, d), pool_s, attn_s.reshape(n_s, attn_w),
                                                 w_out_b, g_ffn, w_route, b_route, cnt_p, **route_kw)

    counts = cnt_all[0, n_groups:n_groups + n_flat].astype(I32)
    n_tiles = (TOP_K_INNER * (n_p + n_s) + n_flat * (TM_GEMM - 1)) // TM_GEMM
    offsets, tile_expert, tile_valid, zero_lo, zero_hi = _sort_tables(counts, n_tiles)
    dest_p, dest_s = _dest_blocks(route_pt, offsets), _dest_blocks(route_st, offsets)

    xs = _dispatch(zero_lo, zero_hi, jnp.concatenate([dest_p, dest_s], axis=0), h_p, h_s, g_ffn,
                   rows=n_tiles * TM_GEMM)
    ys = _moe_gemm(tile_expert, tile_valid, xs, w_gate_f, w_up_f, w_down_f)
    y_p = _combine(dest_p, h_p, route_p, g_final[None], ys)
    y_s = _combine(dest_s, h_s, route_s, g_final[None], ys)

    y_prompt = y_p.reshape(b, s, d)
    y_sample = y_s.reshape(nd, t_new, d)
    k_prompt = jnp.transpose(kt_p.reshape(b, n_heads, HEAD_DIM, keep), (0, 3, 1, 2))[None]
    v_prompt = jnp.transpose(vt_p.reshape(b, n_heads, HEAD_DIM, keep), (0, 3, 1, 2))[None]
    pool_prompt = u_p3[:, s - POOL_STATE:][None]
    k_sample = jnp.transpose(ktn_s.reshape(t_new, n_heads, HEAD_DIM, nd), (3, 0, 1, 2))[None]
    v_sample = jnp.transpose(vtn_s.reshape(t_new, n_heads, HEAD_DIM, nd), (3, 0, 1, 2))[None]
    pool_sample = jnp.transpose(new_state_t[-POOL_STATE:], (1, 0, 2))[None]
    return (y_prompt, y_sample, k_prompt, v_prompt, pool_prompt, k_sample, v_sample, pool_sample)
```

```python
import functools
import math

import numpy as np
import jax
import jax.numpy as jnp
from jax import lax
from jax.experimental import pallas as pl
from jax.experimental.pallas import tpu as pltpu

F32 = jnp.float32
BF16 = jnp.bfloat16
I32 = jnp.int32

HEAD_DIM = 64
POOL_WINDOWS = (2, 4, 8, 16)
POOL_STATE = max(POOL_WINDOWS) - 1
BRANCHES = ((128, 1), (512, 4), (2048, 16))
STEPS = BRANCHES[0][0] // BRANCHES[0][1]
ATTN_REACH = max(w for w, _ in BRANCHES)
MAX_DIL = max(d for _, d in BRANCHES)
TOP_K_INNER = 2
RMS_EPS = 1e-6
LANES = 128
NEG_INF = float("-inf")

VMEM_LIMIT = 56 * 1024 * 1024

TM_PROJ = 512
TM_MIX = 512
TM_ROW = 512
TM_GEMM = 512
ZERO_ROWS = 64
ATTN_GROUP = 8


def _alibi_slopes(n_heads):
    def geometric(n):
        start = 2.0 ** (-8.0 / n)
        return [start ** (i + 1) for i in range(n)]
    c = 2 ** int(math.floor(math.log2(n_heads)))
    s = geometric(c)
    if c < n_heads:
        s = s + geometric(2 * c)[0::2][: n_heads - c]
    return jnp.asarray(s, dtype=F32)


def _rms(x, g):
    return x * lax.rsqrt(jnp.mean(x * x, axis=-1, keepdims=True) + RMS_EPS) * g


def _params(*sem):
    return pltpu.CompilerParams(dimension_semantics=sem, vmem_limit_bytes=VMEM_LIMIT)


def _proj_in_body(x_ref, g_ref, w_ref, u_ref, q_ref, k_ref, v_ref, *t_refs, pool_w, attn_w, tiles_per_seq,
                  keep_tiles, t_new):
    xn = _rms(x_ref[...], g_ref[...]).astype(BF16)

    def proj(lo, n):
        return jnp.dot(xn, w_ref[:, lo:lo + n], preferred_element_type=F32)

    u_ref[...] = proj(0, pool_w)
    q_ref[...] = proj(pool_w, attn_w) * (HEAD_DIM ** -0.5)
    k = proj(pool_w + attn_w, attn_w)
    v = proj(pool_w + 2 * attn_w, attn_w)
    k_ref[...] = k
    v_ref[...] = v
    if t_new is None:
        kt_ref, vt_ref = t_refs
        j = pl.program_id(0) % tiles_per_seq

        @pl.when(j >= tiles_per_seq - keep_tiles)
        def _():
            kt_ref[0] = k.T
            vt_ref[0] = v.T
    else:
        ktn_ref, vtn_ref, cols_scr = t_refs
        n_seq = ktn_ref.shape[2]
        for val, dst in ((k, ktn_ref), (v, vtn_ref)):
            for c in range(attn_w // LANES):
                lanes = slice(c * LANES, (c + 1) * LANES)
                cols_scr[...] = val[:, lanes]
                for t in range(t_new):
                    dst[t, lanes, :] = cols_scr[pl.ds(t, n_seq, stride=t_new), :].T


def _proj_in(x, g, w_bf16, *, pool_w, attn_w, seq=None, keep=None, t_new=None):
    n, d = x.shape
    tm = TM_PROJ if t_new is None else n
    grid = (n // tm,)
    row = lambda i: (i, 0)
    out_shape = [jax.ShapeDtypeStruct((n, pool_w), F32)] + [jax.ShapeDtypeStruct((n, attn_w), F32)] * 3
    out_specs = [pl.BlockSpec((tm, pool_w), row)] + [pl.BlockSpec((tm, attn_w), row)] * 3
    tiles_per_seq = keep_tiles = 0
    scratch = []
    if t_new is None:
        tiles_per_seq, keep_tiles = seq // tm, keep // tm
        first = tiles_per_seq - keep_tiles
        tmap = lambda i: (i // tiles_per_seq, 0, jnp.maximum(i % tiles_per_seq - first, 0))
        out_shape += [jax.ShapeDtypeStruct((n // seq, attn_w, keep), F32)] * 2
        out_specs += [pl.BlockSpec((1, attn_w, tm), tmap)] * 2
    else:
        assert n // t_new == LANES, "one square transpose per (step, column block)"
        out_shape += [jax.ShapeDtypeStruct((t_new, attn_w, n // t_new), F32)] * 2
        out_specs += [pl.BlockSpec((t_new, attn_w, n // t_new), lambda i: (0, 0, 0))] * 2
        scratch = [pltpu.VMEM((n, LANES), F32)]
    body = functools.partial(_proj_in_body, pool_w=pool_w, attn_w=attn_w, tiles_per_seq=tiles_per_seq,
                             keep_tiles=keep_tiles, t_new=t_new)
    return pl.pallas_call(
        body, grid=grid,
        in_specs=[pl.BlockSpec((tm, d), row), pl.BlockSpec((1, d), lambda i: (0, 0)),
                  pl.BlockSpec(w_bf16.shape, lambda i: (0, 0))],
        out_specs=out_specs, out_shape=out_shape, scratch_shapes=scratch,
        compiler_params=_params("arbitrary"), name="proj_in",
    )(x, g, w_bf16)


def _attn_prompt_body(slopes_ref, q_hbm, k_hbm, v_hbm, o_ref, qkv_scr, bias_scr, s_scr, p_scr, o_scr, m_scr, l_scr, sems,
                      *, seq, n_pairs, n_steps):
    hp = pl.program_id(1)
    gstep = pl.program_id(0) * n_pairs + hp
    buf = gstep % 2
    blk = STEPS
    n_it = seq // blk
    cls = seq // MAX_DIL

    def class_copies(st, b):
        lanes = pl.ds(pl.multiple_of((st % n_pairs) * LANES, LANES), LANES)
        return [pltpu.make_async_copy(src.at[st // n_pairs, :, c, lanes], qkv_scr.at[b, a, pl.ds(c * cls, cls), :],
                                      sems.at[b, a])
                for a, src in enumerate((q_hbm, k_hbm, v_hbm)) for c in range(MAX_DIL)]

    @pl.when(gstep == 0)
    def _():
        for cp in class_copies(gstep, 0):
            cp.start()

    @pl.when(gstep + 1 < n_steps)
    def _():
        for cp in class_copies(gstep + 1, 1 - buf):
            cp.start()

    for cp in class_copies(gstep, buf):
        cp.wait()
    qp_scr, kp_scr, vp_scr = (qkv_scr.at[buf, a] for a in range(3))

    lane = lax.broadcasted_iota(I32, (blk, LANES), 1)
    head_a = lane < HEAD_DIM
    row = lax.broadcasted_iota(I32, (2 * blk, 2 * blk), 0)
    col = lax.broadcasted_iota(I32, (2 * blk, 2 * blk), 1)
    slope = jnp.where(row < blk, slopes_ref[2 * hp], slopes_ref[2 * hp + 1])

    for bi, (_, d) in enumerate(BRANCHES):
        nb = n_it // d
        n_chunk = MAX_DIL // d
        a_rows = blk // n_chunk
        sh = a_rows.bit_length() - 1
        assert a_rows == 1 << sh and a_rows % 8 == 0

        def seq_index(i, n_chunk=n_chunk, a_rows=a_rows, sh=sh):
            i = i & (blk - 1)
            return (i & (a_rows - 1)) * n_chunk + (i >> sh)

        step = seq_index(row) + blk - (seq_index(col) + (col & blk))
        bias = jnp.where((step >= 0) & (step <= STEPS), -slope * (d * step).astype(F32), NEG_INF)
        bias_scr[0] = bias
        bias_scr[1] = jnp.where(col < blk, NEG_INF, bias)

        def group(j, carry, bi=bi, d=d, nb=nb, n_chunk=n_chunk, a_rows=a_rows):
            def chunks(g):
                it = j * ATTN_GROUP + g
                r = it // nb
                n = it % nb
                cur = [pl.ds(pl.multiple_of((r + d * c) * cls + n * a_rows, 8), a_rows) for c in range(n_chunk)]
                prev = [pl.ds(pl.multiple_of((r + d * c) * cls + jnp.maximum(n - 1, 0) * a_rows, 8), a_rows)
                        for c in range(n_chunk)]
                return n, cur, prev

            def gather(ref, sls):
                return jnp.concatenate([ref[sl, :] for sl in sls], axis=0)

            for g in range(ATTN_GROUP):
                n, cur, prev = chunks(g)
                qb = gather(qp_scr, cur)
                q2 = jnp.concatenate([jnp.where(head_a, qb, 0.0), jnp.where(head_a, 0.0, qb)], axis=0).astype(BF16)
                kc = gather(kp_scr, prev + cur).astype(BF16)
                s = lax.dot_general(q2, kc, (((1,), (1,)), ((), ())), preferred_element_type=F32)
                s_scr[g] = s + bias_scr[jnp.where(n == 0, 1, 0)]
            for g in range(ATTN_GROUP):
                _, cur, _ = chunks(g)
                s = s_scr[g]
                m = jnp.max(s, axis=1, keepdims=True)
                p_scr[g] = jnp.exp(s - m).astype(BF16)
                m2 = jnp.where(head_a, m[:blk], m[blk:])
                for c, sl in enumerate(cur):
                    m_scr[bi, sl, :] = m2[c * a_rows:(c + 1) * a_rows]
            ones = jnp.ones((2 * blk, LANES), BF16)
            for g in range(ATTN_GROUP):
                _, cur, prev = chunks(g)
                vc = jnp.concatenate([gather(vp_scr, prev + cur).astype(BF16), ones], axis=1)
                ol = jnp.dot(p_scr[g], vc, preferred_element_type=F32)
                o2 = jnp.where(head_a, ol[:blk, :LANES], ol[blk:, :LANES])
                l2 = jnp.where(head_a, ol[:blk, LANES:], ol[blk:, LANES:])
                for c, sl in enumerate(cur):
                    o_scr[bi, sl, :] = o2[c * a_rows:(c + 1) * a_rows]
                    l_scr[bi, sl, :] = l2[c * a_rows:(c + 1) * a_rows]
            return carry

        lax.fori_loop(0, n_it // ATTN_GROUP, group, 0)

    def merge(c, carry):
        rs = pl.ds(pl.multiple_of(c * cls, cls), cls)
        ms = [m_scr[b, rs, :] for b in range(len(BRANCHES))]
        mx = functools.reduce(jnp.maximum, ms)
        num = jnp.zeros((cls, LANES), F32)
        den = jnp.zeros((cls, LANES), F32)
        for b in range(len(BRANCHES)):
            a = jnp.exp(ms[b] - mx)
            num = num + a * o_scr[b, rs, :]
            den = den + a * l_scr[b, rs, :]
        o_ref[0, pl.ds(c, cls, stride=MAX_DIL), :] = num / den
        return carry

    lax.fori_loop(0, MAX_DIL, merge, 0)


def _attn_prompt(slopes, q, k, v):
    b, s, hw = q.shape
    assert s % (STEPS * max(d for _, d in BRANCHES)) == 0, "sequence must be a multiple of the widest span"
    assert hw % LANES == 0 and LANES == 2 * HEAD_DIM
    n_pairs = hw // LANES
    nbr = len(BRANCHES)
    by_class = lambda a: a.reshape(b, s // MAX_DIL, MAX_DIL, hw)
    hbm = pl.BlockSpec(memory_space=pl.ANY)
    return pl.pallas_call(
        functools.partial(_attn_prompt_body, seq=s, n_pairs=n_pairs, n_steps=b * n_pairs),
        grid=(b, n_pairs),
        in_specs=[pl.BlockSpec(memory_space=pltpu.SMEM), hbm, hbm, hbm],
        out_specs=pl.BlockSpec((1, s, LANES), lambda i, j: (i, 0, j)),
        out_shape=jax.ShapeDtypeStruct((b, s, hw), F32),
        scratch_shapes=[pltpu.VMEM((2, 3, s, LANES), F32),
                        pltpu.VMEM((2, 2 * STEPS, 2 * STEPS), F32),
                        pltpu.VMEM((ATTN_GROUP, 2 * STEPS, 2 * STEPS), F32),
                        pltpu.VMEM((ATTN_GROUP, 2 * STEPS, 2 * STEPS), BF16)]
        + [pltpu.VMEM((nbr, s, LANES), F32)] * 3 + [pltpu.SemaphoreType.DMA((2, 3))],
        compiler_params=_params("arbitrary", "arbitrary"), name="attn_prompt",
    )(slopes, by_class(q), by_class(k), by_class(v))


def _multiplicity(dist):
    mult = jnp.zeros(dist.shape, F32)
    for w, d in BRANCHES:
        assert d & (d - 1) == 0
        mult = mult + ((dist >= 0) & ((dist & (d - 1)) == 0) & (dist <= w)).astype(F32)
    return mult


def _attn_sample_body(slopes_ref, q_ref, kn_ref, vn_ref, kt_ref, vt_ref, o_ref, *, n_heads):
    t_new = q_ref.shape[1]
    past = kt_ref.shape[3]
    dist = past + lax.broadcasted_iota(I32, (t_new, past), 0) - lax.broadcasted_iota(I32, (t_new, past), 1)
    dist_n = lax.broadcasted_iota(I32, (t_new, t_new), 0) - lax.broadcasted_iota(I32, (t_new, t_new), 1)
    mult, mult_n = _multiplicity(dist), _multiplicity(dist_n)
    dist_f, dist_nf = dist.astype(F32), dist_n.astype(F32)
    nt = (((1,), (1,)), ((), ()))
    for h in range(n_heads):
        slope = slopes_ref[h]
        cols = slice(h * HEAD_DIM, (h + 1) * HEAD_DIM)
        qh = q_ref[0, :, cols].astype(BF16)
        kn = kn_ref[0, :, cols].astype(BF16)
        vn = vn_ref[0, :, cols].astype(BF16)
        s = jnp.dot(qh, kt_ref[0, h].astype(BF16), preferred_element_type=F32)
        s = jnp.where(mult > 0, s - slope * dist_f, NEG_INF)
        sn = lax.dot_general(qh, kn, nt, preferred_element_type=F32)
        sn = jnp.where(mult_n > 0, sn - slope * dist_nf, NEG_INF)
        m = jnp.maximum(jnp.max(s, axis=1, keepdims=True), jnp.max(sn, axis=1, keepdims=True))
        p = mult * jnp.exp(s - m)
        pn = mult_n * jnp.exp(sn - m)
        l = jnp.sum(p, axis=1, keepdims=True) + jnp.sum(pn, axis=1, keepdims=True)
        o = lax.dot_general(p.astype(BF16), vt_ref[0, h].astype(BF16), nt, preferred_element_type=F32)
        o = o + jnp.dot(pn.astype(BF16), vn, preferred_element_type=F32)
        o_ref[0, :, cols] = o / l


def _attn_sample_pool_body(slopes_ref, q_ref, kn_ref, vn_ref, kt_ref, vt_ref, u_ref, wbd_ref, sc_ref, o_ref, pool_ref,
                           ext_scr, *, n_heads, tm_pool, tiles_per_seq):
    _attn_sample_body(slopes_ref, q_ref, kn_ref, vn_ref, kt_ref, vt_ref, o_ref, n_heads=n_heads)
    _pool_prompt_body(u_ref, wbd_ref, sc_ref, pool_ref, ext_scr, tm=tm_pool, j=pl.program_id(0) % tiles_per_seq)


def _attn_sample_pool(slopes, q, k_new, v_new, cache_kt, cache_vt, u_prompt, w_bd, scale):
    n, t, hw = q.shape
    _, h, e, past = cache_kt.shape
    b, s, w = u_prompt.shape
    assert past >= ATTN_REACH, "every strided key of every branch must exist in the window buffer"
    tm_pool = b * s // n
    assert tm_pool * n == b * s and s % tm_pool == 0 and tm_pool % (POOL_STATE + 1) == 0
    tiles_per_seq = s // tm_pool
    new = pl.BlockSpec((1, t, hw), lambda i: (i, 0, 0))
    old = pl.BlockSpec((1, h, e, past), lambda i: (i, 0, 0, 0))
    fix = lambda i: (0, 0)
    return pl.pallas_call(
        functools.partial(_attn_sample_pool_body, n_heads=h, tm_pool=tm_pool, tiles_per_seq=tiles_per_seq),
        grid=(n,),
        in_specs=[pl.BlockSpec(memory_space=pltpu.SMEM), new, new, new, old, old,
                  pl.BlockSpec((1, s, w), lambda i: (i // tiles_per_seq, 0, 0)), pl.BlockSpec((w, w), fix),
                  pl.BlockSpec((1, w), fix)],
        out_specs=[new, pl.BlockSpec((1, tm_pool, w), lambda i: (i // tiles_per_seq, i % tiles_per_seq, 0))],
        out_shape=[jax.ShapeDtypeStruct((n, t, hw), F32), jax.ShapeDtypeStruct((b, s, w), F32)],
        scratch_shapes=[pltpu.VMEM((tm_pool + 2 * (POOL_STATE + 1), w), F32)],
        compiler_params=_params("arbitrary"), name="attn_sample",
    )(slopes, q, k_new, v_new, cache_kt, cache_vt, u_prompt, w_bd, scale)


def _pool_windows(width):
    gw = width // len(POOL_WINDOWS)
    lane = lax.broadcasted_iota(I32, (1, width), 1)
    win = jnp.zeros((1, width), I32)
    for g, w in enumerate(POOL_WINDOWS):
        win = jnp.where((lane >= g * gw) & (lane < (g + 1) * gw), w, win)
    return win


def _pool_prompt_body(u_ref, w_ref, sc_ref, o_ref, ext_scr, *, tm, j):
    width = u_ref.shape[2]
    pad = 2 * (POOL_STATE + 1)
    start = pl.multiple_of(j * tm, tm)
    lead = POOL_STATE + 1
    prev = u_ref[0, pl.ds(pl.multiple_of(jnp.maximum(start - lead, 0), lead), lead), :]
    ext_scr[0:pad - lead, :] = jnp.zeros((pad - lead, width), F32)
    ext_scr[pad - lead:pad, :] = jnp.where(j > 0, prev, 0.0)
    ext_scr[pad:, :] = u_ref[0, pl.ds(start, tm), :]
    win = _pool_windows(width)
    tok = ext_scr[pad:, :]
    acc = tok
    for i in range(1, max(POOL_WINDOWS)):
        acc = acc + jnp.where(i < win, ext_scr[pl.ds(pad - i, tm), :], 0.0)
    pos = start + lax.broadcasted_iota(I32, (tm, width), 0)
    cnt = jnp.minimum(win, pos + 1).astype(F32)
    diff = (acc / cnt - tok).astype(BF16)
    o_ref[0] = jnp.dot(diff, w_ref[...], preferred_element_type=F32) * sc_ref[...]


def _pool_sample_body(st_ref, u_ref, w_ref, sc_ref, o_ref, ns_ref):
    t_new, _, width = u_ref.shape
    n_state = st_ref.shape[0]
    win = _pool_windows(width)

    def row(k):
        return st_ref[k] if k < n_state else u_ref[k - n_state]

    for t in range(t_new):
        tok = u_ref[t]
        acc = tok
        for i in range(1, max(POOL_WINDOWS)):
            acc = acc + jnp.where(i < win, row(n_state + t - i), 0.0)
        diff = (acc / win.astype(F32) - tok).astype(BF16)
        o_ref[t] = jnp.dot(diff, w_ref[...], preferred_element_type=F32) * sc_ref[...]
    for k in range(n_state):
        ns_ref[k] = row(k + t_new)


def _pool_sample(state_t, u, w_bd, scale):
    n_state = state_t.shape[0]
    assert n_state >= POOL_STATE
    return pl.pallas_call(
        _pool_sample_body,
        out_shape=[jax.ShapeDtypeStruct(u.shape, F32), jax.ShapeDtypeStruct(state_t.shape, F32)],
        compiler_params=pltpu.CompilerParams(vmem_limit_bytes=VMEM_LIMIT), name="pool_sample",
    )(state_t, u, w_bd, scale)


ROUTE_F1, ROUTE_F2, ROUTE_G1, ROUTE_G2, ROUTE_R1, ROUTE_R2 = range(6)
ROUTE_ROWS = 8


def _split_bf16(x):
    hi = x.astype(BF16)
    return hi, (x - hi.astype(F32)).astype(BF16)


def _mix_route_body(x_ref, pool_ref, attn_ref, wo_ref, g_ref, wr_ref, br_ref, cnt_in_ref, h_ref, route_ref,
                    route_t_ref, cnt_out_ref, carry_scr, *, n_groups, n_experts):
    i = pl.program_id(0)
    tm = x_ref.shape[0]
    pool_w = pool_ref.shape[1]

    @pl.when(i == 0)
    def _():
        carry_scr[...] = cnt_in_ref[...]

    h = x_ref[...]
    h = h + jnp.dot(pool_ref[...].astype(BF16), wo_ref[0:pool_w, :], preferred_element_type=F32)
    h = h + jnp.dot(attn_ref[...].astype(BF16), wo_ref[pool_w:, :], preferred_element_type=F32)
    h_ref[...] = h

    hn_hi, hn_lo = _split_bf16(_rms(h, g_ref[...]))
    w_hi, w_lo = _split_bf16(wr_ref[...])
    hi_both = jnp.dot(hn_hi, jnp.concatenate([w_hi, w_lo], axis=1), preferred_element_type=F32)
    logits = (hi_both[:, :LANES] + hi_both[:, LANES:]
              + jnp.dot(hn_lo, w_hi, preferred_element_type=F32)) + br_ref[...]

    lane = lax.broadcasted_iota(I32, (tm, LANES), 1).astype(F32)

    def first_lane(mask):
        return jnp.min(jnp.where(mask, lane, float(LANES)), axis=1, keepdims=True)

    is_g = lane < n_groups
    lg = jnp.where(is_g, logits, NEG_INF)
    mg = jnp.max(lg, axis=1, keepdims=True)
    p_sel = 1.0 / jnp.sum(jnp.exp(lg - mg), axis=1, keepdims=True)
    g_top = first_lane(lg == mg)
    lo = n_groups + g_top * n_experts
    in_grp = (lane >= lo) & (lane < lo + n_experts)
    le = jnp.where(in_grp, logits, NEG_INF)
    ee = jnp.exp(le - jnp.max(le, axis=1, keepdims=True))
    pe = ee / jnp.sum(ee, axis=1, keepdims=True)
    v1 = jnp.max(jnp.where(in_grp, pe, -1.0), axis=1, keepdims=True)
    i1 = first_lane(in_grp & (pe == v1))
    rest = in_grp & (lane != i1)
    v2 = jnp.max(jnp.where(rest, pe, -1.0), axis=1, keepdims=True)
    i2 = first_lane(rest & (pe == v2))
    gate1 = p_sel * (v1 / (v1 + v2))
    gate2 = p_sel * (v2 / (v1 + v2))

    sel1, sel2 = lane == i1, lane == i2
    onehot = (sel1 | sel2).astype(BF16)
    tri = (lax.broadcasted_iota(I32, (tm, tm), 1) < lax.broadcasted_iota(I32, (tm, tm), 0)).astype(BF16)
    running = jnp.dot(tri, onehot, preferred_element_type=F32) + carry_scr[...]
    rank1 = jnp.sum(jnp.where(sel1, running, 0.0), axis=1, keepdims=True)
    rank2 = jnp.sum(jnp.where(sel2, running, 0.0), axis=1, keepdims=True)
    carry_scr[...] = carry_scr[...] + jnp.sum(onehot.astype(F32), axis=0, keepdims=True)
    cnt_out_ref[...] = carry_scr[...]

    rec = jnp.zeros((tm, LANES), F32)
    for idx, val in ((ROUTE_F1, i1 - n_groups), (ROUTE_F2, i2 - n_groups),
                     (ROUTE_G1, gate1), (ROUTE_G2, gate2), (ROUTE_R1, rank1), (ROUTE_R2, rank2)):
        rec = jnp.where(lane == idx, val, rec)
    route_ref[...] = rec
    route_t_ref[...] = rec.T[:ROUTE_ROWS]


def _mix_route(x, pool, attn, w_out_bf16, g_ffn, w_route, b_route, cnt_in, *, n_groups, n_experts):
    n, d = x.shape
    tm = TM_MIX
    row = lambda i: (i, 0)
    fix = lambda i: (0, 0)
    return pl.pallas_call(
        functools.partial(_mix_route_body, n_groups=n_groups, n_experts=n_experts),
        grid=(n // tm,),
        in_specs=[pl.BlockSpec((tm, d), row), pl.BlockSpec((tm, pool.shape[1]), row),
                  pl.BlockSpec((tm, attn.shape[1]), row), pl.BlockSpec(w_out_bf16.shape, fix),
                  pl.BlockSpec((1, d), fix), pl.BlockSpec(w_route.shape, fix), pl.BlockSpec((1, LANES), fix),
                  pl.BlockSpec((1, LANES), fix)],
        out_specs=[pl.BlockSpec((tm, d), row), pl.BlockSpec((tm, LANES), row),
                   pl.BlockSpec((ROUTE_ROWS, tm), lambda i: (0, i)), pl.BlockSpec((1, LANES), fix)],
        out_shape=[jax.ShapeDtypeStruct((n, d), F32), jax.ShapeDtypeStruct((n, LANES), F32),
                   jax.ShapeDtypeStruct((ROUTE_ROWS, n), F32), jax.ShapeDtypeStruct((1, LANES), F32)],
        scratch_shapes=[pltpu.VMEM((1, LANES), F32)],
        compiler_params=_params("arbitrary"), name="mix_route",
    )(x, pool, attn, w_out_bf16, g_ffn, w_route, b_route, cnt_in)


def _dispatch_body(zlo_ref, zhi_ref, dest_ref, hp_ref, hs_ref, g_ref, xs_ref, xn_scr, zero_scr, sems, *, tiles_p,
                   n_steps):
    tm = hp_ref.shape[0]
    i = pl.program_id(0)
    zero_sem = 2 * TOP_K_INNER

    @pl.when(i == 0)
    def _():
        zero_scr[...] = jnp.zeros(zero_scr.shape, F32)
        for start in (True, False):
            def chunk(c, carry, start=start):
                cp = pltpu.make_async_copy(zero_scr, xs_ref.at[pl.ds(pl.multiple_of(c * ZERO_ROWS, ZERO_ROWS), ZERO_ROWS)],
                                           sems.at[zero_sem])
                cp.start() if start else cp.wait()
                return carry

            def segment(e, carry, chunk=chunk):
                return lax.fori_loop(zlo_ref[e], zhi_ref[e], chunk, carry)
            lax.fori_loop(0, zlo_ref.shape[0], segment, 0)

    buf = i % 2

    def wait_rows(b):
        for slot in range(TOP_K_INNER):
            pltpu.make_async_copy(xn_scr.at[b], xs_ref.at[pl.ds(0, tm)], sems.at[b * TOP_K_INNER + slot]).wait()

    @pl.when(i >= 2)
    def _():
        wait_rows(buf)

    @pl.when(i < tiles_p)
    def _():
        xn_scr[buf] = _rms(hp_ref[...], g_ref[...])

    @pl.when(i >= tiles_p)
    def _():
        xn_scr[buf] = _rms(hs_ref[...], g_ref[...])

    for r in range(tm):
        for slot in range(TOP_K_INNER):
            d = dest_ref[0, 0, slot * tm + r]
            pltpu.make_async_copy(xn_scr.at[buf, pl.ds(r, 1)], xs_ref.at[pl.ds(d, 1)],
                                  sems.at[buf * TOP_K_INNER + slot]).start(priority=slot)

    @pl.when(i == n_steps - 1)
    def _():
        if n_steps > 1:
            wait_rows(1 - buf)
        wait_rows(buf)


def _dispatch(zero_lo, zero_hi, dest, h_p, h_s, g_ffn, *, rows):
    tm = TM_ROW
    d = h_p.shape[1]
    tiles_p, tiles_s = h_p.shape[0] // tm, h_s.shape[0] // tm
    return pl.pallas_call(
        functools.partial(_dispatch_body, tiles_p=tiles_p, n_steps=tiles_p + tiles_s),
        grid_spec=pltpu.PrefetchScalarGridSpec(
            num_scalar_prefetch=2, grid=(tiles_p + tiles_s,),
            in_specs=[pl.BlockSpec((1, 1, TOP_K_INNER * tm), lambda i, lo, hi: (i, 0, 0), memory_space=pltpu.SMEM),
                      pl.BlockSpec((tm, d), lambda i, lo, hi: (jnp.minimum(i, tiles_p - 1), 0)),
                      pl.BlockSpec((tm, d), lambda i, lo, hi: (jnp.maximum(i - tiles_p, 0), 0)),
                      pl.BlockSpec((1, d), lambda i, lo, hi: (0, 0))],
            out_specs=pl.BlockSpec(memory_space=pl.ANY),
            scratch_shapes=[pltpu.VMEM((2, tm, d), F32), pltpu.VMEM((ZERO_ROWS, d), F32),
                            pltpu.SemaphoreType.DMA((2 * TOP_K_INNER + 1,))]),
        out_shape=jax.ShapeDtypeStruct((rows, d), F32),
        compiler_params=_params("arbitrary"), name="dispatch",
    )(zero_lo, zero_hi, dest, h_p, h_s, g_ffn)


GEMM_X_BUFS = 3
GEMM_Y_BUFS = 2


def _moe_gemm_body(expert_ref, valid_ref, xs_hbm, wg_ref, wu_ref, wd_ref, ys_hbm, x_buf, y_buf, wg_scr, wu_scr, wd_scr,
                   x_sems, y_sems, *, n_tiles):
    t = pl.program_id(0)
    tm = x_buf.shape[1]
    ahead = GEMM_X_BUFS - 1

    def x_copy(s):
        return pltpu.make_async_copy(xs_hbm.at[pl.ds(pl.multiple_of(s * tm, tm), tm)], x_buf.at[s % GEMM_X_BUFS],
                                     x_sems.at[s % GEMM_X_BUFS])

    def y_copy(s):
        return pltpu.make_async_copy(y_buf.at[s % GEMM_Y_BUFS], ys_hbm.at[pl.ds(pl.multiple_of(s * tm, tm), tm)],
                                     y_sems.at[s % GEMM_Y_BUFS])

    def tile_is_valid(s):
        return (s < n_tiles) & (valid_ref[jnp.minimum(s, n_tiles - 1)] > 0)

    @pl.when(t == 0)
    def _():
        for s in range(ahead):
            @pl.when(tile_is_valid(s))
            def _():
                x_copy(s).start()

    @pl.when(tile_is_valid(t + ahead))
    def _():
        x_copy(t + ahead).start()

    @pl.when((t == 0) | (expert_ref[t] != expert_ref[jnp.maximum(t - 1, 0)]))
    def _():
        wg_scr[...] = wg_ref[0].astype(BF16)
        wu_scr[...] = wu_ref[0].astype(BF16)
        wd_scr[...] = wd_ref[0].astype(BF16)

    @pl.when(t >= GEMM_Y_BUFS)
    def _():
        y_copy(t - GEMM_Y_BUFS).wait()

    ybuf = y_buf.at[t % GEMM_Y_BUFS]

    @pl.when(valid_ref[t] > 0)
    def _():
        x_copy(t).wait()
        x = x_buf[t % GEMM_X_BUFS].astype(BF16)
        gate = jnp.dot(x, wg_scr[...], preferred_element_type=F32)
        up = jnp.dot(x, wu_scr[...], preferred_element_type=F32)
        mid = (gate * jax.nn.sigmoid(gate) * up).astype(BF16)
        ybuf[...] = jnp.dot(mid, wd_scr[...], preferred_element_type=F32)

    @pl.when(valid_ref[t] == 0)
    def _():
        ybuf[...] = jnp.zeros(ybuf.shape, F32)

    y_copy(t).start()

    @pl.when(t == n_tiles - 1)
    def _():
        for back in range(min(GEMM_Y_BUFS, n_tiles) - 1, -1, -1):
            y_copy(t - back).wait()


def _moe_gemm(tile_expert, tile_valid, xs, w_gate, w_up, w_down):
    rows, d = xs.shape
    _, _, f = w_gate.shape
    tm = TM_GEMM
    n_tiles = rows // tm
    hbm = pl.BlockSpec(memory_space=pl.ANY)
    return pl.pallas_call(
        functools.partial(_moe_gemm_body, n_tiles=n_tiles),
        grid_spec=pltpu.PrefetchScalarGridSpec(
            num_scalar_prefetch=2, grid=(n_tiles,),
            in_specs=[hbm,
                      pl.BlockSpec((1, d, f), lambda t, e, v: (e[t], 0, 0)),
                      pl.BlockSpec((1, d, f), lambda t, e, v: (e[t], 0, 0)),
                      pl.BlockSpec((1, f, d), lambda t, e, v: (e[t], 0, 0))],
            out_specs=hbm,
            scratch_shapes=[pltpu.VMEM((GEMM_X_BUFS, tm, d), F32), pltpu.VMEM((GEMM_Y_BUFS, tm, d), F32),
                            pltpu.VMEM((d, f), BF16), pltpu.VMEM((d, f), BF16), pltpu.VMEM((f, d), BF16),
                            pltpu.SemaphoreType.DMA((GEMM_X_BUFS,)), pltpu.SemaphoreType.DMA((GEMM_Y_BUFS,))]),
        out_shape=jax.ShapeDtypeStruct((rows, d), F32),
        compiler_params=_params("arbitrary"), name="moe_gemm",
    )(tile_expert, tile_valid, xs, w_gate, w_up, w_down)


def _combine_body(dest_ref, next_ref, h_ref, route_ref, g_ref, ys_ref, y_ref, rows_scr, sems, *, n_steps):
    tm = h_ref.shape[0]
    i = pl.program_id(0)
    buf = i % 2

    def fetch(idx_ref, b):
        for r in range(tm):
            for slot in range(TOP_K_INNER):
                d = idx_ref[0, 0, slot * tm + r]
                pltpu.make_async_copy(ys_ref.at[pl.ds(d, 1)], rows_scr.at[b, slot, pl.ds(r, 1)],
                                      sems.at[b * TOP_K_INNER + slot]).start(priority=slot)

    @pl.when(i == 0)
    def _():
        fetch(dest_ref, 0)

    @pl.when(i + 1 < n_steps)
    def _():
        fetch(next_ref, 1 - buf)

    for slot in range(TOP_K_INNER):
        pltpu.make_async_copy(ys_ref.at[pl.ds(0, tm)], rows_scr.at[buf, slot], sems.at[buf * TOP_K_INNER + slot]).wait()
    route = route_ref[...]
    out = h_ref[...] + (route[:, ROUTE_G1:ROUTE_G1 + 1] * rows_scr[buf, 0]
                        + route[:, ROUTE_G2:ROUTE_G2 + 1] * rows_scr[buf, 1])
    y_ref[...] = _rms(out, g_ref[...])


def _combine(dest, h, route, g_final, ys):
    n, d = h.shape
    tm = TM_ROW
    n_steps = n // tm
    row = lambda i: (i, 0)
    idx_block = (1, 1, TOP_K_INNER * tm)
    return pl.pallas_call(
        functools.partial(_combine_body, n_steps=n_steps),
        grid=(n_steps,),
        in_specs=[pl.BlockSpec(idx_block, lambda i: (i, 0, 0), memory_space=pltpu.SMEM),
                  pl.BlockSpec(idx_block, lambda i: (jnp.minimum(i + 1, n_steps - 1), 0, 0), memory_space=pltpu.SMEM),
                  pl.BlockSpec((tm, d), row), pl.BlockSpec((tm, LANES), row), pl.BlockSpec((1, d), lambda i: (0, 0)),
                  pl.BlockSpec(memory_space=pl.ANY)],
        out_specs=pl.BlockSpec((tm, d), row),
        out_shape=jax.ShapeDtypeStruct((n, d), F32),
        scratch_shapes=[pltpu.VMEM((2, TOP_K_INNER, tm, d), F32), pltpu.SemaphoreType.DMA((2 * TOP_K_INNER,))],
        compiler_params=_params("arbitrary"), name="combine",
    )(dest, dest, h, route, g_final, ys)


def _sort_tables(counts, n_tiles):
    padded = ((counts + TM_GEMM - 1) // TM_GEMM) * TM_GEMM
    ends = jnp.cumsum(padded)
    offsets = ends - padded
    total = ends[-1]
    tile_start = jnp.arange(n_tiles, dtype=I32) * TM_GEMM
    tile_valid = (tile_start < total).astype(I32)
    last_block = jnp.maximum(total // TM_GEMM - 1, 0)
    tile_block = jnp.minimum(jnp.arange(n_tiles, dtype=I32), last_block)
    n_flat = counts.shape[0]
    tile_expert = jnp.minimum(jnp.sum((tile_block[:, None] * TM_GEMM >= ends[None, :]).astype(I32), axis=1), n_flat - 1)
    zero_lo = jnp.concatenate([(offsets + counts) // ZERO_ROWS, total[None] // ZERO_ROWS]).astype(I32)
    zero_hi = jnp.concatenate([ends // ZERO_ROWS, jnp.full((1,), n_tiles * TM_GEMM // ZERO_ROWS, I32)]).astype(I32)
    return offsets, tile_expert.astype(I32), tile_valid, zero_lo, zero_hi


def _dest_blocks(route_t, offsets):
    n = route_t.shape[1]
    experts = jnp.arange(offsets.shape[0], dtype=I32)[:, None]

    def dest(f_row, r_row):
        f = route_t[f_row].astype(I32)
        base = jnp.sum(jnp.where(f[None, :] == experts, offsets[:, None], 0), axis=0)
        return (base + route_t[r_row].astype(I32)).reshape(n // TM_ROW, TM_ROW)

    return jnp.concatenate([dest(ROUTE_F1, ROUTE_R1), dest(ROUTE_F2, ROUTE_R2)], axis=1)[:, None, :]


def kernel(x_prompt, x_sample, cache_k, cache_v, state_pool, g_mix, w_in, w_pool, pool_scale, w_out, g_ffn,
           w_router_group, b_router_group, w_router_expert, b_router_expert, w_gate, w_up, w_down, g_final):
    depth = g_mix.shape[0]
    assert depth == 1, "single-layer step"
    b, s, d = x_prompt.shape
    nd, t_new, _ = x_sample.shape
    n_heads = cache_k.shape[3]
    attn_w = n_heads * HEAD_DIM
    pool_w = state_pool.shape[3]
    past = cache_k.shape[2]
    keep = min(ATTN_REACH, s)
    n_groups, n_experts = w_router_expert.shape[1], w_router_expert.shape[3]
    n_flat = n_groups * n_experts
    assert n_groups + n_flat <= LANES
    slopes = _alibi_slopes(n_heads)

    w_in_b = w_in[0].astype(BF16)
    w_out_b = w_out[0].astype(BF16)
    gw = pool_w // len(POOL_WINDOWS)
    w_bd = jnp.zeros((pool_w, pool_w), F32)
    for g in range(len(POOL_WINDOWS)):
        w_bd = w_bd.at[g * gw:(g + 1) * gw, g * gw:(g + 1) * gw].set(w_pool[0, g])
    w_bd = w_bd.astype(BF16)
    w_route = jnp.concatenate([w_router_group[0], jnp.transpose(w_router_expert[0], (1, 0, 2)).reshape(d, n_flat)], axis=1)
    w_route = jnp.pad(w_route, ((0, 0), (0, LANES - n_groups - n_flat)))
    b_route = jnp.pad(jnp.concatenate([b_router_group[0], b_router_expert[0].reshape(n_flat)]),
                      (0, LANES - n_groups - n_flat))[None]
    w_gate_f, w_up_f = w_gate[0].reshape(n_flat, d, -1), w_up[0].reshape(n_flat, d, -1)
    w_down_f = w_down[0].reshape(n_flat, -1, d)

    n_p = b * s
    u_p, q_p, k_p, v_p, kt_p, vt_p = _proj_in(x_prompt.reshape(n_p, d), g_mix, w_in_b, pool_w=pool_w, attn_w=attn_w,
                                               seq=s, keep=keep)
    attn_p = _attn_prompt(slopes, q_p.reshape(b, s, attn_w), k_p.reshape(b, s, attn_w), v_p.reshape(b, s, attn_w))
    u_p3 = u_p.reshape(b, s, pool_w)

    n_s = nd * t_new
    u_s, q_s, k_s, v_s, ktn_s, vtn_s = _proj_in(x_sample.reshape(n_s, d), g_mix, w_in_b, pool_w=pool_w, attn_w=attn_w,
                                                t_new=t_new)
    cache_kt = jnp.transpose(cache_k[0], (0, 2, 3, 1))
    cache_vt = jnp.transpose(cache_v[0], (0, 2, 3, 1))
    as3 = lambda a: a.reshape(nd, t_new, attn_w)
    attn_s, pool_p = _attn_sample_pool(slopes, as3(q_s), as3(k_s), as3(v_s), cache_kt, cache_vt, u_p3, w_bd, pool_scale)
    state_t = jnp.transpose(state_pool[0], (1, 0, 2))
    u_st = jnp.transpose(u_s.reshape(nd, t_new, pool_w), (1, 0, 2))
    pool_st, new_state_t = _pool_sample(state_t, u_st, w_bd, pool_scale)
    pool_s = jnp.transpose(pool_st, (1, 0, 2)).reshape(n_s, pool_w)

    route_kw = dict(n_groups=n_groups, n_experts=n_experts)
    h_p, route_p, route_pt, cnt_p = _mix_route(x_prompt.reshape(n_p, d), pool_p.reshape(n_p, pool_w),
                                               attn_p.reshape(n_p, attn_w), w_out_b, g_ffn, w_route, b_route,
                                               jnp.zeros((1, LANES), F32), **route_kw)
    h_s, route_s, route_st, cnt_all = _mix_route(x_sample.reshape(n_s, d), pool_s, attn_s.reshape(n_s, attn_w),
                                                 w_out_b, g_ffn, w_route, b_route, cnt_p, **route_kw)

    counts = cnt_all[0, n_groups:n_groups + n_flat].astype(I32)
    n_tiles = (TOP_K_INNER * (n_p + n_s) + n_flat * (TM_GEMM - 1)) // TM_GEMM
    offsets, tile_expert, tile_valid, zero_lo, zero_hi = _sort_tables(counts, n_tiles)
    dest_p, dest_s = _dest_blocks(route_pt, offsets), _dest_blocks(route_st, offsets)

    xs = _dispatch(zero_lo, zero_hi, jnp.concatenate([dest_p, dest_s], axis=0), h_p, h_s, g_ffn,
                   rows=n_tiles * TM_GEMM)
    ys = _moe_gemm(tile_expert, tile_valid, xs, w_gate_f, w_up_f, w_down_f)
    y_p = _combine(dest_p, h_p, route_p, g_final[None], ys)
    y_s = _combine(dest_s, h_s, route_s, g_final[None], ys)

    y_prompt = y_p.reshape(b, s, d)
    y_sample = y_s.reshape(nd, t_new, d)
    k_prompt = jnp.transpose(kt_p.reshape(b, n_heads, HEAD_DIM, keep), (0, 3, 1, 2))[None]
    v_prompt = jnp.transpose(vt_p.reshape(b, n_heads, HEAD_DIM, keep), (0, 3, 1, 2))[None]
    pool_prompt = u_p3[:, s - POOL_STATE:][None]
    k_sample = jnp.transpose(ktn_s.reshape(t_new, n_heads, HEAD_DIM, nd), (3, 0, 1, 2))[None]
    v_sample = jnp.transpose(vtn_s.reshape(t_new, n_heads, HEAD_DIM, nd), (3, 0, 1, 2))[None]
    pool_sample = jnp.transpose(new_state_t[-POOL_STATE:], (1, 0, 2))[None]
    return (y_prompt, y_sample, k_prompt, v_prompt, pool_prompt, k_sample, v_sample, pool_sample)
```

```python
import functools
import math

import numpy as np
import jax
import jax.numpy as jnp
from jax import lax
from jax.experimental import pallas as pl
from jax.experimental.pallas import tpu as pltpu

F32 = jnp.float32
BF16 = jnp.bfloat16
I32 = jnp.int32

HEAD_DIM = 64
POOL_WINDOWS = (2, 4, 8, 16)
POOL_STATE = max(POOL_WINDOWS) - 1
BRANCHES = ((128, 1), (512, 4), (2048, 16))
STEPS = BRANCHES[0][0] // BRANCHES[0][1]
ATTN_REACH = max(w for w, _ in BRANCHES)
MAX_DIL = max(d for _, d in BRANCHES)
TOP_K_INNER = 2
RMS_EPS = 1e-6
LANES = 128
NEG_INF = float("-inf")

VMEM_LIMIT = 56 * 1024 * 1024

TM_PROJ = 512
TM_MIX = 512
TM_ROW = 512
TM_GEMM = 256
ZERO_ROWS = 64
ATTN_GROUP = 8


def _alibi_slopes(n_heads):
    def geometric(n):
        start = 2.0 ** (-8.0 / n)
        return [start ** (i + 1) for i in range(n)]
    c = 2 ** int(math.floor(math.log2(n_heads)))
    s = geometric(c)
    if c < n_heads:
        s = s + geometric(2 * c)[0::2][: n_heads - c]
    return jnp.asarray(s, dtype=F32)


def _rms(x, g):
    return x * lax.rsqrt(jnp.mean(x * x, axis=-1, keepdims=True) + RMS_EPS) * g


def _params(*sem):
    return pltpu.CompilerParams(dimension_semantics=sem, vmem_limit_bytes=VMEM_LIMIT)


def _proj_in_body(x_ref, g_ref, w_ref, u_ref, q_ref, k_ref, v_ref, *t_refs, pool_w, attn_w, tiles_per_seq,
                  keep_tiles, t_new):
    xn = _rms(x_ref[...], g_ref[...]).astype(BF16)

    def proj(lo, n):
        return jnp.dot(xn, w_ref[:, lo:lo + n], preferred_element_type=F32)

    u_ref[...] = proj(0, pool_w)
    q_ref[...] = proj(pool_w, attn_w) * (HEAD_DIM ** -0.5)
    k = proj(pool_w + attn_w, attn_w)
    v = proj(pool_w + 2 * attn_w, attn_w)
    k_ref[...] = k
    v_ref[...] = v
    if t_new is None:
        kt_ref, vt_ref = t_refs
        j = pl.program_id(0) % tiles_per_seq

        @pl.when(j >= tiles_per_seq - keep_tiles)
        def _():
            kt_ref[0] = k.T
            vt_ref[0] = v.T
    else:
        ktn_ref, vtn_ref, cols_scr = t_refs
        n_seq = ktn_ref.shape[2]
        for val, dst in ((k, ktn_ref), (v, vtn_ref)):
            for c in range(attn_w // LANES):
                lanes = slice(c * LANES, (c + 1) * LANES)
                cols_scr[...] = val[:, lanes]
                for t in range(t_new):
                    dst[t, lanes, :] = cols_scr[pl.ds(t, n_seq, stride=t_new), :].T


def _proj_in(x, g, w_bf16, *, pool_w, attn_w, seq=None, keep=None, t_new=None):
    n, d = x.shape
    tm = TM_PROJ if t_new is None else n
    grid = (n // tm,)
    row = lambda i: (i, 0)
    out_shape = [jax.ShapeDtypeStruct((n, pool_w), F32)] + [jax.ShapeDtypeStruct((n, attn_w), F32)] * 3
    out_specs = [pl.BlockSpec((tm, pool_w), row)] + [pl.BlockSpec((tm, attn_w), row)] * 3
    tiles_per_seq = keep_tiles = 0
    scratch = []
    if t_new is None:
        tiles_per_seq, keep_tiles = seq // tm, keep // tm
        first = tiles_per_seq - keep_tiles
        tmap = lambda i: (i // tiles_per_seq, 0, jnp.maximum(i % tiles_per_seq - first, 0))
        out_shape += [jax.ShapeDtypeStruct((n // seq, attn_w, keep), F32)] * 2
        out_specs += [pl.BlockSpec((1, attn_w, tm), tmap)] * 2
    else:
        assert n // t_new == LANES, "one square transpose per (step, column block)"
        out_shape += [jax.ShapeDtypeStruct((t_new, attn_w, n // t_new), F32)] * 2
        out_specs += [pl.BlockSpec((t_new, attn_w, n // t_new), lambda i: (0, 0, 0))] * 2
        scratch = [pltpu.VMEM((n, LANES), F32)]
    body = functools.partial(_proj_in_body, pool_w=pool_w, attn_w=attn_w, tiles_per_seq=tiles_per_seq,
                             keep_tiles=keep_tiles, t_new=t_new)
    return pl.pallas_call(
        body, grid=grid,
        in_specs=[pl.BlockSpec((tm, d), row), pl.BlockSpec((1, d), lambda i: (0, 0)),
                  pl.BlockSpec(w_bf16.shape, lambda i: (0, 0))],
        out_specs=out_specs, out_shape=out_shape, scratch_shapes=scratch,
        compiler_params=_params("arbitrary"), name="proj_in",
    )(x, g, w_bf16)


def _attn_prompt_body(slopes_ref, q_hbm, k_hbm, v_hbm, o_ref, qkv_scr, bias_scr, s_scr, p_scr, o_scr, m_scr, l_scr, sems,
                      *, seq, n_pairs, n_steps):
    hp = pl.program_id(1)
    gstep = pl.program_id(0) * n_pairs + hp
    buf = gstep % 2
    blk = STEPS
    n_it = seq // blk
    cls = seq // MAX_DIL

    def class_copies(st, b):
        lanes = pl.ds(pl.multiple_of((st % n_pairs) * LANES, LANES), LANES)
        return [pltpu.make_async_copy(src.at[st // n_pairs, :, c, lanes], qkv_scr.at[b, a, pl.ds(c * cls, cls), :],
                                      sems.at[b, a])
                for a, src in enumerate((q_hbm, k_hbm, v_hbm)) for c in range(MAX_DIL)]

    @pl.when(gstep == 0)
    def _():
        for cp in class_copies(gstep, 0):
            cp.start()

    @pl.when(gstep + 1 < n_steps)
    def _():
        for cp in class_copies(gstep + 1, 1 - buf):
            cp.start()

    for cp in class_copies(gstep, buf):
        cp.wait()
    qp_scr, kp_scr, vp_scr = (qkv_scr.at[buf, a] for a in range(3))

    lane = lax.broadcasted_iota(I32, (blk, LANES), 1)
    head_a = lane < HEAD_DIM
    row = lax.broadcasted_iota(I32, (2 * blk, 2 * blk), 0)
    col = lax.broadcasted_iota(I32, (2 * blk, 2 * blk), 1)
    slope = jnp.where(row < blk, slopes_ref[2 * hp], slopes_ref[2 * hp + 1])

    for bi, (_, d) in enumerate(BRANCHES):
        nb = n_it // d
        n_chunk = MAX_DIL // d
        a_rows = blk // n_chunk
        sh = a_rows.bit_length() - 1
        assert a_rows == 1 << sh and a_rows % 8 == 0

        def seq_index(i, n_chunk=n_chunk, a_rows=a_rows, sh=sh):
            i = i & (blk - 1)
            return (i & (a_rows - 1)) * n_chunk + (i >> sh)

        step = seq_index(row) + blk - (seq_index(col) + (col & blk))
        bias = jnp.where((step >= 0) & (step <= STEPS), -slope * (d * step).astype(F32), NEG_INF)
        bias_scr[0] = bias
        bias_scr[1] = jnp.where(col < blk, NEG_INF, bias)

        def group(j, carry, bi=bi, d=d, nb=nb, n_chunk=n_chunk, a_rows=a_rows):
            def chunks(g):
                it = j * ATTN_GROUP + g
                r = it // nb
                n = it % nb
                cur = [pl.ds(pl.multiple_of((r + d * c) * cls + n * a_rows, 8), a_rows) for c in range(n_chunk)]
                prev = [pl.ds(pl.multiple_of((r + d * c) * cls + jnp.maximum(n - 1, 0) * a_rows, 8), a_rows)
                        for c in range(n_chunk)]
                return n, cur, prev

            def gather(ref, sls):
                return jnp.concatenate([ref[sl, :] for sl in sls], axis=0)

            for g in range(ATTN_GROUP):
                n, cur, prev = chunks(g)
                qb = gather(qp_scr, cur)
                q2 = jnp.concatenate([jnp.where(head_a, qb, 0.0), jnp.where(head_a, 0.0, qb)], axis=0).astype(BF16)
                kc = gather(kp_scr, prev + cur).astype(BF16)
                s = lax.dot_general(q2, kc, (((1,), (1,)), ((), ())), preferred_element_type=F32)
                s_scr[g] = s + bias_scr[jnp.where(n == 0, 1, 0)]
            for g in range(ATTN_GROUP):
                _, cur, _ = chunks(g)
                s = s_scr[g]
                m = jnp.max(s, axis=1, keepdims=True)
                p_scr[g] = jnp.exp(s - m).astype(BF16)
                m2 = jnp.where(head_a, m[:blk], m[blk:])
                for c, sl in enumerate(cur):
                    m_scr[bi, sl, :] = m2[c * a_rows:(c + 1) * a_rows]
            ones = jnp.ones((2 * blk, LANES), BF16)
            for g in range(ATTN_GROUP):
                _, cur, prev = chunks(g)
                vc = jnp.concatenate([gather(vp_scr, prev + cur).astype(BF16), ones], axis=1)
                ol = jnp.dot(p_scr[g], vc, preferred_element_type=F32)
                o2 = jnp.where(head_a, ol[:blk, :LANES], ol[blk:, :LANES])
                l2 = jnp.where(head_a, ol[:blk, LANES:], ol[blk:, LANES:])
                for c, sl in enumerate(cur):
                    o_scr[bi, sl, :] = o2[c * a_rows:(c + 1) * a_rows]
                    l_scr[bi, sl, :] = l2[c * a_rows:(c + 1) * a_rows]
            return carry

        lax.fori_loop(0, n_it // ATTN_GROUP, group, 0)

    def merge(c, carry):
        rs = pl.ds(pl.multiple_of(c * cls, cls), cls)
        ms = [m_scr[b, rs, :] for b in range(len(BRANCHES))]
        mx = functools.reduce(jnp.maximum, ms)
        num = jnp.zeros((cls, LANES), F32)
        den = jnp.zeros((cls, LANES), F32)
        for b in range(len(BRANCHES)):
            a = jnp.exp(ms[b] - mx)
            num = num + a * o_scr[b, rs, :]
            den = den + a * l_scr[b, rs, :]
        o_ref[0, pl.ds(c, cls, stride=MAX_DIL), :] = num / den
        return carry

    lax.fori_loop(0, MAX_DIL, merge, 0)


def _attn_prompt(slopes, q, k, v):
    b, s, hw = q.shape
    assert s % (STEPS * max(d for _, d in BRANCHES)) == 0, "sequence must be a multiple of the widest span"
    assert hw % LANES == 0 and LANES == 2 * HEAD_DIM
    n_pairs = hw // LANES
    nbr = len(BRANCHES)
    by_class = lambda a: a.reshape(b, s // MAX_DIL, MAX_DIL, hw)
    hbm = pl.BlockSpec(memory_space=pl.ANY)
    return pl.pallas_call(
        functools.partial(_attn_prompt_body, seq=s, n_pairs=n_pairs, n_steps=b * n_pairs),
        grid=(b, n_pairs),
        in_specs=[pl.BlockSpec(memory_space=pltpu.SMEM), hbm, hbm, hbm],
        out_specs=pl.BlockSpec((1, s, LANES), lambda i, j: (i, 0, j)),
        out_shape=jax.ShapeDtypeStruct((b, s, hw), F32),
        scratch_shapes=[pltpu.VMEM((2, 3, s, LANES), F32),
                        pltpu.VMEM((2, 2 * STEPS, 2 * STEPS), F32),
                        pltpu.VMEM((ATTN_GROUP, 2 * STEPS, 2 * STEPS), F32),
                        pltpu.VMEM((ATTN_GROUP, 2 * STEPS, 2 * STEPS), BF16)]
        + [pltpu.VMEM((nbr, s, LANES), F32)] * 3 + [pltpu.SemaphoreType.DMA((2, 3))],
        compiler_params=_params("arbitrary", "arbitrary"), name="attn_prompt",
    )(slopes, by_class(q), by_class(k), by_class(v))


def _multiplicity(dist):
    mult = jnp.zeros(dist.shape, F32)
    for w, d in BRANCHES:
        assert d & (d - 1) == 0
        mult = mult + ((dist >= 0) & ((dist & (d - 1)) == 0) & (dist <= w)).astype(F32)
    return mult


def _attn_sample_body(slopes_ref, q_ref, kn_ref, vn_ref, kt_ref, vt_ref, o_ref, *, n_heads):
    t_new = q_ref.shape[1]
    past = kt_ref.shape[3]
    dist = past + lax.broadcasted_iota(I32, (t_new, past), 0) - lax.broadcasted_iota(I32, (t_new, past), 1)
    dist_n = lax.broadcasted_iota(I32, (t_new, t_new), 0) - lax.broadcasted_iota(I32, (t_new, t_new), 1)
    mult, mult_n = _multiplicity(dist), _multiplicity(dist_n)
    dist_f, dist_nf = dist.astype(F32), dist_n.astype(F32)
    nt = (((1,), (1,)), ((), ()))
    for h in range(n_heads):
        slope = slopes_ref[h]
        cols = slice(h * HEAD_DIM, (h + 1) * HEAD_DIM)
        qh = q_ref[0, :, cols].astype(BF16)
        kn = kn_ref[0, :, cols].astype(BF16)
        vn = vn_ref[0, :, cols].astype(BF16)
        s = jnp.dot(qh, kt_ref[0, h].astype(BF16), preferred_element_type=F32)
        s = jnp.where(mult > 0, s - slope * dist_f, NEG_INF)
        sn = lax.dot_general(qh, kn, nt, preferred_element_type=F32)
        sn = jnp.where(mult_n > 0, sn - slope * dist_nf, NEG_INF)
        m = jnp.maximum(jnp.max(s, axis=1, keepdims=True), jnp.max(sn, axis=1, keepdims=True))
        p = mult * jnp.exp(s - m)
        pn = mult_n * jnp.exp(sn - m)
        l = jnp.sum(p, axis=1, keepdims=True) + jnp.sum(pn, axis=1, keepdims=True)
        o = lax.dot_general(p.astype(BF16), vt_ref[0, h].astype(BF16), nt, preferred_element_type=F32)
        o = o + jnp.dot(pn.astype(BF16), vn, preferred_element_type=F32)
        o_ref[0, :, cols] = o / l


def _attn_sample_pool_body(slopes_ref, q_ref, kn_ref, vn_ref, kt_ref, vt_ref, u_ref, wbd_ref, sc_ref, o_ref, pool_ref,
                           ext_scr, *, n_heads, tm_pool, tiles_per_seq):
    _attn_sample_body(slopes_ref, q_ref, kn_ref, vn_ref, kt_ref, vt_ref, o_ref, n_heads=n_heads)
    _pool_prompt_body(u_ref, wbd_ref, sc_ref, pool_ref, ext_scr, tm=tm_pool, j=pl.program_id(0) % tiles_per_seq)


def _attn_sample_pool(slopes, q, k_new, v_new, cache_kt, cache_vt, u_prompt, w_bd, scale):
    n, t, hw = q.shape
    _, h, e, past = cache_kt.shape
    b, s, w = u_prompt.shape
    assert past >= ATTN_REACH, "every strided key of every branch must exist in the window buffer"
    tm_pool = b * s // n
    assert tm_pool * n == b * s and s % tm_pool == 0 and tm_pool % (POOL_STATE + 1) == 0
    tiles_per_seq = s // tm_pool
    new = pl.BlockSpec((1, t, hw), lambda i: (i, 0, 0))
    old = pl.BlockSpec((1, h, e, past), lambda i: (i, 0, 0, 0))
    fix = lambda i: (0, 0)
    return pl.pallas_call(
        functools.partial(_attn_sample_pool_body, n_heads=h, tm_pool=tm_pool, tiles_per_seq=tiles_per_seq),
        grid=(n,),
        in_specs=[pl.BlockSpec(memory_space=pltpu.SMEM), new, new, new, old, old,
                  pl.BlockSpec((1, s, w), lambda i: (i // tiles_per_seq, 0, 0)), pl.BlockSpec((w, w), fix),
                  pl.BlockSpec((1, w), fix)],
        out_specs=[new, pl.BlockSpec((1, tm_pool, w), lambda i: (i // tiles_per_seq, i % tiles_per_seq, 0))],
        out_shape=[jax.ShapeDtypeStruct((n, t, hw), F32), jax.ShapeDtypeStruct((b, s, w), F32)],
        scratch_shapes=[pltpu.VMEM((tm_pool + 2 * (POOL_STATE + 1), w), F32)],
        compiler_params=_params("arbitrary"), name="attn_sample",
    )(slopes, q, k_new, v_new, cache_kt, cache_vt, u_prompt, w_bd, scale)


def _pool_windows(width):
    gw = width // len(POOL_WINDOWS)
    lane = lax.broadcasted_iota(I32, (1, width), 1)
    win = jnp.zeros((1, width), I32)
    for g, w in enumerate(POOL_WINDOWS):
        win = jnp.where((lane >= g * gw) & (lane < (g + 1) * gw), w, win)
    return win


def _pool_prompt_body(u_ref, w_ref, sc_ref, o_ref, ext_scr, *, tm, j):
    width = u_ref.shape[2]
    pad = 2 * (POOL_STATE + 1)
    start = pl.multiple_of(j * tm, tm)
    lead = POOL_STATE + 1
    prev = u_ref[0, pl.ds(pl.multiple_of(jnp.maximum(start - lead, 0), lead), lead), :]
    ext_scr[0:pad - lead, :] = jnp.zeros((pad - lead, width), F32)
    ext_scr[pad - lead:pad, :] = jnp.where(j > 0, prev, 0.0)
    ext_scr[pad:, :] = u_ref[0, pl.ds(start, tm), :]
    win = _pool_windows(width)
    tok = ext_scr[pad:, :]
    acc = tok
    for i in range(1, max(POOL_WINDOWS)):
        acc = acc + jnp.where(i < win, ext_scr[pl.ds(pad - i, tm), :], 0.0)
    pos = start + lax.broadcasted_iota(I32, (tm, width), 0)
    cnt = jnp.minimum(win, pos + 1).astype(F32)
    diff = (acc / cnt - tok).astype(BF16)
    o_ref[0] = jnp.dot(diff, w_ref[...], preferred_element_type=F32) * sc_ref[...]


def _pool_sample_body(st_ref, u_ref, w_ref, sc_ref, o_ref, ns_ref):
    t_new, _, width = u_ref.shape
    n_state = st_ref.shape[0]
    win = _pool_windows(width)

    def row(k):
        return st_ref[k] if k < n_state else u_ref[k - n_state]

    for t in range(t_new):
        tok = u_ref[t]
        acc = tok
        for i in range(1, max(POOL_WINDOWS)):
            acc = acc + jnp.where(i < win, row(n_state + t - i), 0.0)
        diff = (acc / win.astype(F32) - tok).astype(BF16)
        o_ref[t] = jnp.dot(diff, w_ref[...], preferred_element_type=F32) * sc_ref[...]
    for k in range(n_state):
        ns_ref[k] = row(k + t_new)


def _pool_sample(state_t, u, w_bd, scale):
    n_state = state_t.shape[0]
    assert n_state >= POOL_STATE
    return pl.pallas_call(
        _pool_sample_body,
        out_shape=[jax.ShapeDtypeStruct(u.shape, F32), jax.ShapeDtypeStruct(state_t.shape, F32)],
        compiler_params=pltpu.CompilerParams(vmem_limit_bytes=VMEM_LIMIT), name="pool_sample",
    )(state_t, u, w_bd, scale)


ROUTE_F1, ROUTE_F2, ROUTE_G1, ROUTE_G2, ROUTE_R1, ROUTE_R2 = range(6)
ROUTE_ROWS = 8


def _split_bf16(x):
    hi = x.astype(BF16)
    return hi, (x - hi.astype(F32)).astype(BF16)


def _mix_route_body(x_ref, pool_ref, attn_ref, wo_ref, g_ref, wr_ref, br_ref, cnt_in_ref, h_ref, route_ref,
                    route_t_ref, cnt_out_ref, carry_scr, *, n_groups, n_experts):
    i = pl.program_id(0)
    tm = x_ref.shape[0]
    pool_w = pool_ref.shape[1]

    @pl.when(i == 0)
    def _():
        carry_scr[...] = cnt_in_ref[...]

    h = x_ref[...]
    h = h + jnp.dot(pool_ref[...].astype(BF16), wo_ref[0:pool_w, :], preferred_element_type=F32)
    h = h + jnp.dot(attn_ref[...].astype(BF16), wo_ref[pool_w:, :], preferred_element_type=F32)
    h_ref[...] = h

    hn_hi, hn_lo = _split_bf16(_rms(h, g_ref[...]))
    w_hi, w_lo = _split_bf16(wr_ref[...])
    hi_both = jnp.dot(hn_hi, jnp.concatenate([w_hi, w_lo], axis=1), preferred_element_type=F32)
    logits = (hi_both[:, :LANES] + hi_both[:, LANES:]
              + jnp.dot(hn_lo, w_hi, preferred_element_type=F32)) + br_ref[...]

    lane = lax.broadcasted_iota(I32, (tm, LANES), 1).astype(F32)

    def first_lane(mask):
        return jnp.min(jnp.where(mask, lane, float(LANES)), axis=1, keepdims=True)

    is_g = lane < n_groups
    lg = jnp.where(is_g, logits, NEG_INF)
    mg = jnp.max(lg, axis=1, keepdims=True)
    p_sel = 1.0 / jnp.sum(jnp.exp(lg - mg), axis=1, keepdims=True)
    g_top = first_lane(lg == mg)
    lo = n_groups + g_top * n_experts
    in_grp = (lane >= lo) & (lane < lo + n_experts)
    le = jnp.where(in_grp, logits, NEG_INF)
    ee = jnp.exp(le - jnp.max(le, axis=1, keepdims=True))
    pe = ee / jnp.sum(ee, axis=1, keepdims=True)
    v1 = jnp.max(jnp.where(in_grp, pe, -1.0), axis=1, keepdims=True)
    i1 = first_lane(in_grp & (pe == v1))
    rest = in_grp & (lane != i1)
    v2 = jnp.max(jnp.where(rest, pe, -1.0), axis=1, keepdims=True)
    i2 = first_lane(rest & (pe == v2))
    gate1 = p_sel * (v1 / (v1 + v2))
    gate2 = p_sel * (v2 / (v1 + v2))

    sel1, sel2 = lane == i1, lane == i2
    onehot = (sel1 | sel2).astype(BF16)
    tri = (lax.broadcasted_iota(I32, (tm, tm), 1) < lax.broadcasted_iota(I32, (tm, tm), 0)).astype(BF16)
    running = jnp.dot(tri, onehot, preferred_element_type=F32) + carry_scr[...]
    rank1 = jnp.sum(jnp.where(sel1, running, 0.0), axis=1, keepdims=True)
    rank2 = jnp.sum(jnp.where(sel2, running, 0.0), axis=1, keepdims=True)
    carry_scr[...] = carry_scr[...] + jnp.sum(onehot.astype(F32), axis=0, keepdims=True)
    cnt_out_ref[...] = carry_scr[...]

    rec = jnp.zeros((tm, LANES), F32)
    for idx, val in ((ROUTE_F1, i1 - n_groups), (ROUTE_F2, i2 - n_groups),
                     (ROUTE_G1, gate1), (ROUTE_G2, gate2), (ROUTE_R1, rank1), (ROUTE_R2, rank2)):
        rec = jnp.where(lane == idx, val, rec)
    route_ref[...] = rec
    route_t_ref[...] = rec.T[:ROUTE_ROWS]


def _mix_route(x, pool, attn, w_out_bf16, g_ffn, w_route, b_route, cnt_in, *, n_groups, n_experts):
    n, d = x.shape
    tm = TM_MIX
    row = lambda i: (i, 0)
    fix = lambda i: (0, 0)
    return pl.pallas_call(
        functools.partial(_mix_route_body, n_groups=n_groups, n_experts=n_experts),
        grid=(n // tm,),
        in_specs=[pl.BlockSpec((tm, d), row), pl.BlockSpec((tm, pool.shape[1]), row),
                  pl.BlockSpec((tm, attn.shape[1]), row), pl.BlockSpec(w_out_bf16.shape, fix),
                  pl.BlockSpec((1, d), fix), pl.BlockSpec(w_route.shape, fix), pl.BlockSpec((1, LANES), fix),
                  pl.BlockSpec((1, LANES), fix)],
        out_specs=[pl.BlockSpec((tm, d), row), pl.BlockSpec((tm, LANES), row),
                   pl.BlockSpec((ROUTE_ROWS, tm), lambda i: (0, i)), pl.BlockSpec((1, LANES), fix)],
        out_shape=[jax.ShapeDtypeStruct((n, d), F32), jax.ShapeDtypeStruct((n, LANES), F32),
                   jax.ShapeDtypeStruct((ROUTE_ROWS, n), F32), jax.ShapeDtypeStruct((1, LANES), F32)],
        scratch_shapes=[pltpu.VMEM((1, LANES), F32)],
        compiler_params=_params("arbitrary"), name="mix_route",
    )(x, pool, attn, w_out_bf16, g_ffn, w_route, b_route, cnt_in)


def _dispatch_body(zlo_ref, zhi_ref, dest_ref, hp_ref, hs_ref, g_ref, xs_ref, xn_scr, zero_scr, sems, *, tiles_p,
                   n_steps):
    tm = hp_ref.shape[0]
    i = pl.program_id(0)
    zero_sem = 2 * TOP_K_INNER

    @pl.when(i == 0)
    def _():
        zero_scr[...] = jnp.zeros(zero_scr.shape, F32)
        for start in (True, False):
            def chunk(c, carry, start=start):
                cp = pltpu.make_async_copy(zero_scr, xs_ref.at[pl.ds(pl.multiple_of(c * ZERO_ROWS, ZERO_ROWS), ZERO_ROWS)],
                                           sems.at[zero_sem])
                cp.start() if start else cp.wait()
                return carry

            def segment(e, carry, chunk=chunk):
                return lax.fori_loop(zlo_ref[e], zhi_ref[e], chunk, carry)
            lax.fori_loop(0, zlo_ref.shape[0], segment, 0)

    buf = i % 2

    def wait_rows(b):
        for slot in range(TOP_K_INNER):
            pltpu.make_async_copy(xn_scr.at[b], xs_ref.at[pl.ds(0, tm)], sems.at[b * TOP_K_INNER + slot]).wait()

    @pl.when(i >= 2)
    def _():
        wait_rows(buf)

    @pl.when(i < tiles_p)
    def _():
        xn_scr[buf] = _rms(hp_ref[...], g_ref[...])

    @pl.when(i >= tiles_p)
    def _():
        xn_scr[buf] = _rms(hs_ref[...], g_ref[...])

    for r in range(tm):
        for slot in range(TOP_K_INNER):
            d = dest_ref[0, 0, slot * tm + r]
            pltpu.make_async_copy(xn_scr.at[buf, pl.ds(r, 1)], xs_ref.at[pl.ds(d, 1)],
                                  sems.at[buf * TOP_K_INNER + slot]).start(priority=slot)

    @pl.when(i == n_steps - 1)
    def _():
        if n_steps > 1:
            wait_rows(1 - buf)
        wait_rows(buf)


def _dispatch(zero_lo, zero_hi, dest, h_p, h_s, g_ffn, *, rows):
    tm = TM_ROW
    d = h_p.shape[1]
    tiles_p, tiles_s = h_p.shape[0] // tm, h_s.shape[0] // tm
    return pl.pallas_call(
        functools.partial(_dispatch_body, tiles_p=tiles_p, n_steps=tiles_p + tiles_s),
        grid_spec=pltpu.PrefetchScalarGridSpec(
            num_scalar_prefetch=2, grid=(tiles_p + tiles_s,),
            in_specs=[pl.BlockSpec((1, 1, TOP_K_INNER * tm), lambda i, lo, hi: (i, 0, 0), memory_space=pltpu.SMEM),
                      pl.BlockSpec((tm, d), lambda i, lo, hi: (jnp.minimum(i, tiles_p - 1), 0)),
                      pl.BlockSpec((tm, d), lambda i, lo, hi: (jnp.maximum(i - tiles_p, 0), 0)),
                      pl.BlockSpec((1, d), lambda i, lo, hi: (0, 0))],
            out_specs=pl.BlockSpec(memory_space=pl.ANY),
            scratch_shapes=[pltpu.VMEM((2, tm, d), F32), pltpu.VMEM((ZERO_ROWS, d), F32),
                            pltpu.SemaphoreType.DMA((2 * TOP_K_INNER + 1,))]),
        out_shape=jax.ShapeDtypeStruct((rows, d), F32),
        compiler_params=_params("arbitrary"), name="dispatch",
    )(zero_lo, zero_hi, dest, h_p, h_s, g_ffn)


GEMM_X_BUFS = 3
GEMM_Y_BUFS = 2


def _moe_gemm_body(expert_ref, valid_ref, xs_hbm, wg_ref, wu_ref, wd_ref, ys_hbm, x_buf, y_buf, wg_scr, wu_scr, wd_scr,
                   x_sems, y_sems, *, n_tiles):
    t = pl.program_id(0)
    tm = x_buf.shape[1]
    ahead = GEMM_X_BUFS - 1

    def x_copy(s):
        return pltpu.make_async_copy(xs_hbm.at[pl.ds(pl.multiple_of(s * tm, tm), tm)], x_buf.at[s % GEMM_X_BUFS],
                                     x_sems.at[s % GEMM_X_BUFS])

    def y_copy(s):
        return pltpu.make_async_copy(y_buf.at[s % GEMM_Y_BUFS], ys_hbm.at[pl.ds(pl.multiple_of(s * tm, tm), tm)],
                                     y_sems.at[s % GEMM_Y_BUFS])

    def tile_is_valid(s):
        return (s < n_tiles) & (valid_ref[jnp.minimum(s, n_tiles - 1)] > 0)

    @pl.when(t == 0)
    def _():
        for s in range(ahead):
            @pl.when(tile_is_valid(s))
            def _():
                x_copy(s).start()

    @pl.when(tile_is_valid(t + ahead))
    def _():
        x_copy(t + ahead).start()

    @pl.when((t == 0) | (expert_ref[t] != expert_ref[jnp.maximum(t - 1, 0)]))
    def _():
        wg_scr[...] = wg_ref[0].astype(BF16)
        wu_scr[...] = wu_ref[0].astype(BF16)
        wd_scr[...] = wd_ref[0].astype(BF16)

    @pl.when(t >= GEMM_Y_BUFS)
    def _():
        y_copy(t - GEMM_Y_BUFS).wait()

    ybuf = y_buf.at[t % GEMM_Y_BUFS]

    @pl.when(valid_ref[t] > 0)
    def _():
        x_copy(t).wait()
        x = x_buf[t % GEMM_X_BUFS].astype(BF16)
        gate = jnp.dot(x, wg_scr[...], preferred_element_type=F32)
        up = jnp.dot(x, wu_scr[...], preferred_element_type=F32)
        mid = (gate * jax.nn.sigmoid(gate) * up).astype(BF16)
        ybuf[...] = jnp.dot(mid, wd_scr[...], preferred_element_type=F32)

    @pl.when(valid_ref[t] == 0)
    def _():
        ybuf[...] = jnp.zeros(ybuf.shape, F32)

    y_copy(t).start()

    @pl.when(t == n_tiles - 1)
    def _():
        for back in range(min(GEMM_Y_BUFS, n_tiles) - 1, -1, -1):
            y_copy(t - back).wait()


def _moe_gemm(tile_expert, tile_valid, xs, w_gate, w_up, w_down):
    rows, d = xs.shape
    _, _, f = w_gate.shape
    tm = TM_GEMM
    n_tiles = rows // tm
    hbm = pl.BlockSpec(memory_space=pl.ANY)
    return pl.pallas_call(
        functools.partial(_moe_gemm_body, n_tiles=n_tiles),
        grid_spec=pltpu.PrefetchScalarGridSpec(
            num_scalar_prefetch=2, grid=(n_tiles,),
            in_specs=[hbm,
                      pl.BlockSpec((1, d, f), lambda t, e, v: (e[t], 0, 0)),
                      pl.BlockSpec((1, d, f), lambda t, e, v: (e[t], 0, 0)),
                      pl.BlockSpec((1, f, d), lambda t, e, v: (e[t], 0, 0))],
            out_specs=hbm,
            scratch_shapes=[pltpu.VMEM((GEMM_X_BUFS, tm, d), F32), pltpu.VMEM((GEMM_Y_BUFS, tm, d), F32),
                            pltpu.VMEM((d, f), BF16), pltpu.VMEM((d, f), BF16), pltpu.VMEM((f, d), BF16),
                            pltpu.SemaphoreType.DMA((GEMM_X_BUFS,)), pltpu.SemaphoreType.DMA((GEMM_Y_BUFS,))]),
        out_shape=jax.ShapeDtypeStruct((rows, d), F32),
        compiler_params=_params("arbitrary"), name="moe_gemm",
    )(tile_expert, tile_valid, xs, w_gate, w_up, w_down)


def _combine_body(dest_ref, next_ref, h_ref, route_ref, g_ref, ys_ref, y_ref, rows_scr, sems, *, n_steps):
    tm = h_ref.shape[0]
    i = pl.program_id(0)
    buf = i % 2

    def fetch(idx_ref, b):
        for r in range(tm):
            for slot in range(TOP_K_INNER):
                d = idx_ref[0, 0, slot * tm + r]
                pltpu.make_async_copy(ys_ref.at[pl.ds(d, 1)], rows_scr.at[b, slot, pl.ds(r, 1)],
                                      sems.at[b * TOP_K_INNER + slot]).start(priority=slot)

    @pl.when(i == 0)
    def _():
        fetch(dest_ref, 0)

    @pl.when(i + 1 < n_steps)
    def _():
        fetch(next_ref, 1 - buf)

    for slot in range(TOP_K_INNER):
        pltpu.make_async_copy(ys_ref.at[pl.ds(0, tm)], rows_scr.at[buf, slot], sems.at[buf * TOP_K_INNER + slot]).wait()
    route = route_ref[...]
    out = h_ref[...] + (route[:, ROUTE_G1:ROUTE_G1 + 1] * rows_scr[buf, 0]
                        + route[:, ROUTE_G2:ROUTE_G2 + 1] * rows_scr[buf, 1])
    y_ref[...] = _rms(out, g_ref[...])


def _combine(dest, h, route, g_final, ys):
    n, d = h.shape
    tm = TM_ROW
    n_steps = n // tm
    row = lambda i: (i, 0)
    idx_block = (1, 1, TOP_K_INNER * tm)
    return pl.pallas_call(
        functools.partial(_combine_body, n_steps=n_steps),
        grid=(n_steps,),
        in_specs=[pl.BlockSpec(idx_block, lambda i: (i, 0, 0), memory_space=pltpu.SMEM),
                  pl.BlockSpec(idx_block, lambda i: (jnp.minimum(i + 1, n_steps - 1), 0, 0), memory_space=pltpu.SMEM),
                  pl.BlockSpec((tm, d), row), pl.BlockSpec((tm, LANES), row), pl.BlockSpec((1, d), lambda i: (0, 0)),
                  pl.BlockSpec(memory_space=pl.ANY)],
        out_specs=pl.BlockSpec((tm, d), row),
        out_shape=jax.ShapeDtypeStruct((n, d), F32),
        scratch_shapes=[pltpu.VMEM((2, TOP_K_INNER, tm, d), F32), pltpu.SemaphoreType.DMA((2 * TOP_K_INNER,))],
        compiler_params=_params("arbitrary"), name="combine",
    )(dest, dest, h, route, g_final, ys)


def _sort_tables(counts, n_tiles):
    padded = ((counts + TM_GEMM - 1) // TM_GEMM) * TM_GEMM
    ends = jnp.cumsum(padded)
    offsets = ends - padded
    total = ends[-1]
    tile_start = jnp.arange(n_tiles, dtype=I32) * TM_GEMM
    tile_valid = (tile_start < total).astype(I32)
    last_block = jnp.maximum(total // TM_GEMM - 1, 0)
    tile_block = jnp.minimum(jnp.arange(n_tiles, dtype=I32), last_block)
    n_flat = counts.shape[0]
    tile_expert = jnp.minimum(jnp.sum((tile_block[:, None] * TM_GEMM >= ends[None, :]).astype(I32), axis=1), n_flat - 1)
    zero_lo = jnp.concatenate([(offsets + counts) // ZERO_ROWS, total[None] // ZERO_ROWS]).astype(I32)
    zero_hi = jnp.concatenate([ends // ZERO_ROWS, jnp.full((1,), n_tiles * TM_GEMM // ZERO_ROWS, I32)]).astype(I32)
    return offsets, tile_expert.astype(I32), tile_valid, zero_lo, zero_hi


def _dest_blocks(route_t, offsets):
    n = route_t.shape[1]
    experts = jnp.arange(offsets.shape[0], dtype=I32)[:, None]

    def dest(f_row, r_row):
        f = route_t[f_row].astype(I32)
        base = jnp.sum(jnp.where(f[None, :] == experts, offsets[:, None], 0), axis=0)
        return (base + route_t[r_row].astype(I32)).reshape(n // TM_ROW, TM_ROW)

    return jnp.concatenate([dest(ROUTE_F1, ROUTE_R1), dest(ROUTE_F2, ROUTE_R2)], axis=1)[:, None, :]


def kernel(x_prompt, x_sample, cache_k, cache_v, state_pool, g_mix, w_in, w_pool, pool_scale, w_out, g_ffn,
           w_router_group, b_router_group, w_router_expert, b_router_expert, w_gate, w_up, w_down, g_final):
    depth = g_mix.shape[0]
    assert depth == 1, "single-layer step"
    b, s, d = x_prompt.shape
    nd, t_new, _ = x_sample.shape
    n_heads = cache_k.shape[3]
    attn_w = n_heads * HEAD_DIM
    pool_w = state_pool.shape[3]
    past = cache_k.shape[2]
    keep = min(ATTN_REACH, s)
    n_groups, n_experts = w_router_expert.shape[1], w_router_expert.shape[3]
    n_flat = n_groups * n_experts
    assert n_groups + n_flat <= LANES
    slopes = _alibi_slopes(n_heads)

    w_in_b = w_in[0].astype(BF16)
    w_out_b = w_out[0].astype(BF16)
    gw = pool_w // len(POOL_WINDOWS)
    w_bd = jnp.zeros((pool_w, pool_w), F32)
    for g in range(len(POOL_WINDOWS)):
        w_bd = w_bd.at[g * gw:(g + 1) * gw, g * gw:(g + 1) * gw].set(w_pool[0, g])
    w_bd = w_bd.astype(BF16)
    w_route = jnp.concatenate([w_router_group[0], jnp.transpose(w_router_expert[0], (1, 0, 2)).reshape(d, n_flat)], axis=1)
    w_route = jnp.pad(w_route, ((0, 0), (0, LANES - n_groups - n_flat)))
    b_route = jnp.pad(jnp.concatenate([b_router_group[0], b_router_expert[0].reshape(n_flat)]),
                      (0, LANES - n_groups - n_flat))[None]
    w_gate_f, w_up_f = w_gate[0].reshape(n_flat, d, -1), w_up[0].reshape(n_flat, d, -1)
    w_down_f = w_down[0].reshape(n_flat, -1, d)

    n_p = b * s
    u_p, q_p, k_p, v_p, kt_p, vt_p = _proj_in(x_prompt.reshape(n_p, d), g_mix, w_in_b, pool_w=pool_w, attn_w=attn_w,
                                               seq=s, keep=keep)
    attn_p = _attn_prompt(slopes, q_p.reshape(b, s, attn_w), k_p.reshape(b, s, attn_w), v_p.reshape(b, s, attn_w))
    u_p3 = u_p.reshape(b, s, pool_w)

    n_s = nd * t_new
    u_s, q_s, k_s, v_s, ktn_s, vtn_s = _proj_in(x_sample.reshape(n_s, d), g_mix, w_in_b, pool_w=pool_w, attn_w=attn_w,
                                                t_new=t_new)
    cache_kt = jnp.transpose(cache_k[0], (0, 2, 3, 1))
    cache_vt = jnp.transpose(cache_v[0], (0, 2, 3, 1))
    as3 = lambda a: a.reshape(nd, t_new, attn_w)
    attn_s, pool_p = _attn_sample_pool(slopes, as3(q_s), as3(k_s), as3(v_s), cache_kt, cache_vt, u_p3, w_bd, pool_scale)
    state_t = jnp.transpose(state_pool[0], (1, 0, 2))
    u_st = jnp.transpose(u_s.reshape(nd, t_new, pool_w), (1, 0, 2))
    pool_st, new_state_t = _pool_sample(state_t, u_st, w_bd, pool_scale)
    pool_s = jnp.transpose(pool_st, (1, 0, 2)).reshape(n_s, pool_w)

    route_kw = dict(n_groups=n_groups, n_experts=n_experts)
    h_p, route_p, route_pt, cnt_p = _mix_route(x_prompt.reshape(n_p, d), pool_p.reshape(n_p, pool_w),
                                               attn_p.reshape(n_p, attn_w), w_out_b, g_ffn, w_route, b_route,
                                               jnp.zeros((1, LANES), F32), **route_kw)
    h_s, route_s, route_st, cnt_all = _mix_route(x_sample.reshape(n_s, d), pool_s, attn_s.reshape(n_s, attn_w),
                                                 w_out_b, g_ffn, w_route, b_route, cnt_p, **route_kw)

    counts = cnt_all[0, n_groups:n_groups + n_flat].astype(I32)
    n_tiles = (TOP_K_INNER * (n_p + n_s) + n_flat * (TM_GEMM - 1)) // TM_GEMM
    offsets, tile_expert, tile_valid, zero_lo, zero_hi = _sort_tables(counts, n_tiles)
    dest_p, dest_s = _dest_blocks(route_pt, offsets), _dest_blocks(route_st, offsets)

    xs = _dispatch(zero_lo, zero_hi, jnp.concatenate([dest_p, dest_s], axis=0), h_p, h_s, g_ffn,
                   rows=n_tiles * TM_GEMM)
    ys = _moe_gemm(tile_expert, tile_valid, xs, w_gate_f, w_up_f, w_down_f)
    y_p = _combine(dest_p, h_p, route_p, g_final[None], ys)
    y_s = _combine(dest_s, h_s, route_s, g_final[None], ys)

    y_prompt = y_p.reshape(b, s, d)
    y_sample = y_s.reshape(nd, t_new, d)
    k_prompt = jnp.transpose(kt_p.reshape(b, n_heads, HEAD_DIM, keep), (0, 3, 1, 2))[None]
    v_prompt = jnp.transpose(vt_p.reshape(b, n_heads, HEAD_DIM, keep), (0, 3, 1, 2))[None]
    pool_prompt = u_p3[:, s - POOL_STATE:][None]
    k_sample = jnp.transpose(ktn_s.reshape(t_new, n_heads, HEAD_DIM, nd), (3, 0, 1, 2))[None]
    v_sample = jnp.transpose(vtn_s.reshape(t_new, n_heads, HEAD_DIM, nd), (3, 0, 1, 2))[None]
    pool_sample = jnp.transpose(new_state_t[-POOL_STATE:], (1, 0, 2))[None]
    return (y_prompt, y_sample, k_prompt, v_prompt, pool_prompt, k_sample, v_sample, pool_sample)
```

```python
import functools
import math

import numpy as np
import jax
import jax.numpy as jnp
from jax import lax
from jax.experimental import pallas as pl
from jax.experimental.pallas import tpu as pltpu

F32 = jnp.float32
BF16 = jnp.bfloat16
I32 = jnp.int32

HEAD_DIM = 64
POOL_WINDOWS = (2, 4, 8, 16)
POOL_STATE = max(POOL_WINDOWS) - 1
BRANCHES = ((128, 1), (512, 4), (2048, 16))
STEPS = BRANCHES[0][0] // BRANCHES[0][1]
ATTN_REACH = max(w for w, _ in BRANCHES)
MAX_DIL = max(d for _, d in BRANCHES)
TOP_K_INNER = 2
RMS_EPS = 1e-6
LANES = 128
NEG_INF = float("-inf")

VMEM_LIMIT = 56 * 1024 * 1024

TM_PROJ = 512
TM_MIX = 512
TM_ROW = 512
TM_GEMM = 512
ZERO_ROWS = 64
ATTN_GROUP = 8


def _alibi_slopes(n_heads):
    def geometric(n):
        start = 2.0 ** (-8.0 / n)
        return [start ** (i + 1) for i in range(n)]
    c = 2 ** int(math.floor(math.log2(n_heads)))
    s = geometric(c)
    if c < n_heads:
        s = s + geometric(2 * c)[0::2][: n_heads - c]
    return jnp.asarray(s, dtype=F32)


def _rms(x, g):
    return x * lax.rsqrt(jnp.mean(x * x, axis=-1, keepdims=True) + RMS_EPS) * g


def _params(*sem):
    return pltpu.CompilerParams(dimension_semantics=sem, vmem_limit_bytes=VMEM_LIMIT)


def _proj_in_body(x_ref, g_ref, w_ref, u_ref, q_ref, k_ref, v_ref, *t_refs, pool_w, attn_w, tiles_per_seq,
                  keep_tiles, t_new):
    xn = _rms(x_ref[...], g_ref[...]).astype(BF16)

    def proj(lo, n):
        return jnp.dot(xn, w_ref[:, lo:lo + n], preferred_element_type=F32)

    u_ref[...] = proj(0, pool_w)
    q_ref[...] = proj(pool_w, attn_w) * (HEAD_DIM ** -0.5)
    k = proj(pool_w + attn_w, attn_w)
    v = proj(pool_w + 2 * attn_w, attn_w)
    k_ref[...] = k
    v_ref[...] = v
    if t_new is None:
        kt_ref, vt_ref = t_refs
        j = pl.program_id(0) % tiles_per_seq

        @pl.when(j >= tiles_per_seq - keep_tiles)
        def _():
            kt_ref[0] = k.T
            vt_ref[0] = v.T
    else:
        ktn_ref, vtn_ref, cols_scr = t_refs
        n_seq = ktn_ref.shape[2]
        for val, dst in ((k, ktn_ref), (v, vtn_ref)):
            for c in range(attn_w // LANES):
                lanes = slice(c * LANES, (c + 1) * LANES)
                cols_scr[...] = val[:, lanes]
                for t in range(t_new):
                    dst[t, lanes, :] = cols_scr[pl.ds(t, n_seq, stride=t_new), :].T


def _proj_in(x, g, w_bf16, *, pool_w, attn_w, seq=None, keep=None, t_new=None):
    n, d = x.shape
    tm = TM_PROJ if t_new is None else n
    grid = (n // tm,)
    row = lambda i: (i, 0)
    out_shape = [jax.ShapeDtypeStruct((n, pool_w), F32)] + [jax.ShapeDtypeStruct((n, attn_w), F32)] * 3
    out_specs = [pl.BlockSpec((tm, pool_w), row)] + [pl.BlockSpec((tm, attn_w), row)] * 3
    tiles_per_seq = keep_tiles = 0
    scratch = []
    if t_new is None:
        tiles_per_seq, keep_tiles = seq // tm, keep // tm
        first = tiles_per_seq - keep_tiles
        tmap = lambda i: (i // tiles_per_seq, 0, jnp.maximum(i % tiles_per_seq - first, 0))
        out_shape += [jax.ShapeDtypeStruct((n // seq, attn_w, keep), F32)] * 2
        out_specs += [pl.BlockSpec((1, attn_w, tm), tmap)] * 2
    else:
        assert n // t_new == LANES, "one square transpose per (step, column block)"
        out_shape += [jax.ShapeDtypeStruct((t_new, attn_w, n // t_new), F32)] * 2
        out_specs += [pl.BlockSpec((t_new, attn_w, n // t_new), lambda i: (0, 0, 0))] * 2
        scratch = [pltpu.VMEM((n, LANES), F32)]
    body = functools.partial(_proj_in_body, pool_w=pool_w, attn_w=attn_w, tiles_per_seq=tiles_per_seq,
                             keep_tiles=keep_tiles, t_new=t_new)
    return pl.pallas_call(
        body, grid=grid,
        in_specs=[pl.BlockSpec((tm, d), row), pl.BlockSpec((1, d), lambda i: (0, 0)),
                  pl.BlockSpec(w_bf16.shape, lambda i: (0, 0))],
        out_specs=out_specs, out_shape=out_shape, scratch_shapes=scratch,
        compiler_params=_params("arbitrary"), name="proj_in",
    )(x, g, w_bf16)


def _attn_prompt_body(slopes_ref, q_hbm, k_hbm, v_hbm, o_ref, qkv_scr, bias_scr, s_scr, p_scr, o_scr, m_scr, l_scr, sems,
                      *, seq, n_pairs, n_steps):
    hp = pl.program_id(1)
    gstep = pl.program_id(0) * n_pairs + hp
    buf = gstep % 2
    blk = STEPS
    n_it = seq // blk
    cls = seq // MAX_DIL

    def class_copies(st, b):
        lanes = pl.ds(pl.multiple_of((st % n_pairs) * LANES, LANES), LANES)
        return [pltpu.make_async_copy(src.at[st // n_pairs, :, c, lanes], qkv_scr.at[b, a, pl.ds(c * cls, cls), :],
                                      sems.at[b, a])
                for a, src in enumerate((q_hbm, k_hbm, v_hbm)) for c in range(MAX_DIL)]

    @pl.when(gstep == 0)
    def _():
        for cp in class_copies(gstep, 0):
            cp.start()

    @pl.when(gstep + 1 < n_steps)
    def _():
        for cp in class_copies(gstep + 1, 1 - buf):
            cp.start()

    for cp in class_copies(gstep, buf):
        cp.wait()
    qp_scr, kp_scr, vp_scr = (qkv_scr.at[buf, a] for a in range(3))

    lane = lax.broadcasted_iota(I32, (blk, LANES), 1)
    head_a = lane < HEAD_DIM
    row = lax.broadcasted_iota(I32, (2 * blk, 2 * blk), 0)
    col = lax.broadcasted_iota(I32, (2 * blk, 2 * blk), 1)
    slope = jnp.where(row < blk, slopes_ref[2 * hp], slopes_ref[2 * hp + 1])

    for bi, (_, d) in enumerate(BRANCHES):
        nb = n_it // d
        n_chunk = MAX_DIL // d
        a_rows = blk // n_chunk
        sh = a_rows.bit_length() - 1
        assert a_rows == 1 << sh and a_rows % 8 == 0

        def seq_index(i, n_chunk=n_chunk, a_rows=a_rows, sh=sh):
            i = i & (blk - 1)
            return (i & (a_rows - 1)) * n_chunk + (i >> sh)

        step = seq_index(row) + blk - (seq_index(col) + (col & blk))
        bias = jnp.where((step >= 0) & (step <= STEPS), -slope * (d * step).astype(F32), NEG_INF)
        bias_scr[0] = bias
        bias_scr[1] = jnp.where(col < blk, NEG_INF, bias)

        def group(j, carry, bi=bi, d=d, nb=nb, n_chunk=n_chunk, a_rows=a_rows):
            def chunks(g):
                it = j * ATTN_GROUP + g
                r = it // nb
                n = it % nb
                cur = [pl.ds(pl.multiple_of((r + d * c) * cls + n * a_rows, 8), a_rows) for c in range(n_chunk)]
                prev = [pl.ds(pl.multiple_of((r + d * c) * cls + jnp.maximum(n - 1, 0) * a_rows, 8), a_rows)
                        for c in range(n_chunk)]
                return n, cur, prev

            def gather(ref, sls):
                return jnp.concatenate([ref[sl, :] for sl in sls], axis=0)

            for g in range(ATTN_GROUP):
                n, cur, prev = chunks(g)
                qb = gather(qp_scr, cur)
                q2 = jnp.concatenate([jnp.where(head_a, qb, 0.0), jnp.where(head_a, 0.0, qb)], axis=0).astype(BF16)
                kc = gather(kp_scr, prev + cur).astype(BF16)
                s = lax.dot_general(q2, kc, (((1,), (1,)), ((), ())), preferred_element_type=F32)
                s_scr[g] = s + bias_scr[jnp.where(n == 0, 1, 0)]
            for g in range(ATTN_GROUP):
                _, cur, _ = chunks(g)
                s = s_scr[g]
                m = jnp.max(s, axis=1, keepdims=True)
                p_scr[g] = jnp.exp(s - m).astype(BF16)
                m2 = jnp.where(head_a, m[:blk], m[blk:])
                for c, sl in enumerate(cur):
                    m_scr[bi, sl, :] = m2[c * a_rows:(c + 1) * a_rows]
            ones = jnp.ones((2 * blk, LANES), BF16)
            for g in range(ATTN_GROUP):
                _, cur, prev = chunks(g)
                vc = jnp.concatenate([gather(vp_scr, prev + cur).astype(BF16), ones], axis=1)
                ol = jnp.dot(p_scr[g], vc, preferred_element_type=F32)
                o2 = jnp.where(head_a, ol[:blk, :LANES], ol[blk:, :LANES])
                l2 = jnp.where(head_a, ol[:blk, LANES:], ol[blk:, LANES:])
                for c, sl in enumerate(cur):
                    o_scr[bi, sl, :] = o2[c * a_rows:(c + 1) * a_rows]
                    l_scr[bi, sl, :] = l2[c * a_rows:(c + 1) * a_rows]
            return carry

        lax.fori_loop(0, n_it // ATTN_GROUP, group, 0)

    def merge(c, carry):
        rs = pl.ds(pl.multiple_of(c * cls, cls), cls)
        ms = [m_scr[b, rs, :] for b in range(len(BRANCHES))]
        mx = functools.reduce(jnp.maximum, ms)
        num = jnp.zeros((cls, LANES), F32)
        den = jnp.zeros((cls, LANES), F32)
        for b in range(len(BRANCHES)):
            a = jnp.exp(ms[b] - mx)
            num = num + a * o_scr[b, rs, :]
            den = den + a * l_scr[b, rs, :]
        o_ref[0, pl.ds(c, cls, stride=MAX_DIL), :] = num / den
        return carry

    lax.fori_loop(0, MAX_DIL, merge, 0)


def _attn_prompt(slopes, q, k, v):
    b, s, hw = q.shape
    assert s % (STEPS * max(d for _, d in BRANCHES)) == 0, "sequence must be a multiple of the widest span"
    assert hw % LANES == 0 and LANES == 2 * HEAD_DIM
    n_pairs = hw // LANES
    nbr = len(BRANCHES)
    by_class = lambda a: a.reshape(b, s // MAX_DIL, MAX_DIL, hw)
    hbm = pl.BlockSpec(memory_space=pl.ANY)
    return pl.pallas_call(
        functools.partial(_attn_prompt_body, seq=s, n_pairs=n_pairs, n_steps=b * n_pairs),
        grid=(b, n_pairs),
        in_specs=[pl.BlockSpec(memory_space=pltpu.SMEM), hbm, hbm, hbm],
        out_specs=pl.BlockSpec((1, s, LANES), lambda i, j: (i, 0, j)),
        out_shape=jax.ShapeDtypeStruct((b, s, hw), F32),
        scratch_shapes=[pltpu.VMEM((2, 3, s, LANES), F32),
                        pltpu.VMEM((2, 2 * STEPS, 2 * STEPS), F32),
                        pltpu.VMEM((ATTN_GROUP, 2 * STEPS, 2 * STEPS), F32),
                        pltpu.VMEM((ATTN_GROUP, 2 * STEPS, 2 * STEPS), BF16)]
        + [pltpu.VMEM((nbr, s, LANES), F32)] * 3 + [pltpu.SemaphoreType.DMA((2, 3))],
        compiler_params=_params("arbitrary", "arbitrary"), name="attn_prompt",
    )(slopes, by_class(q), by_class(k), by_class(v))


def _multiplicity(dist):
    mult = jnp.zeros(dist.shape, F32)
    for w, d in BRANCHES:
        assert d & (d - 1) == 0
        mult = mult + ((dist >= 0) & ((dist & (d - 1)) == 0) & (dist <= w)).astype(F32)
    return mult


def _attn_sample_body(slopes_ref, q_ref, kn_ref, vn_ref, kt_ref, vt_ref, o_ref, *, n_heads):
    t_new = q_ref.shape[1]
    past = kt_ref.shape[3]
    dist = past + lax.broadcasted_iota(I32, (t_new, past), 0) - lax.broadcasted_iota(I32, (t_new, past), 1)
    dist_n = lax.broadcasted_iota(I32, (t_new, t_new), 0) - lax.broadcasted_iota(I32, (t_new, t_new), 1)
    mult, mult_n = _multiplicity(dist), _multiplicity(dist_n)
    dist_f, dist_nf = dist.astype(F32), dist_n.astype(F32)
    nt = (((1,), (1,)), ((), ()))
    for h in range(n_heads):
        slope = slopes_ref[h]
        cols = slice(h * HEAD_DIM, (h + 1) * HEAD_DIM)
        qh = q_ref[0, :, cols].astype(BF16)
        kn = kn_ref[0, :, cols].astype(BF16)
        vn = vn_ref[0, :, cols].astype(BF16)
        s = jnp.dot(qh, kt_ref[0, h].astype(BF16), preferred_element_type=F32)
        s = jnp.where(mult > 0, s - slope * dist_f, NEG_INF)
        sn = lax.dot_general(qh, kn, nt, preferred_element_type=F32)
        sn = jnp.where(mult_n > 0, sn - slope * dist_nf, NEG_INF)
        m = jnp.maximum(jnp.max(s, axis=1, keepdims=True), jnp.max(sn, axis=1, keepdims=True))
        p = mult * jnp.exp(s - m)
        pn = mult_n * jnp.exp(sn - m)
        l = jnp.sum(p, axis=1, keepdims=True) + jnp.sum(pn, axis=1, keepdims=True)
        o = lax.dot_general(p.astype(BF16), vt_ref[0, h].astype(BF16), nt, preferred_element_type=F32)
        o = o + jnp.dot(pn.astype(BF16), vn, preferred_element_type=F32)
        o_ref[0, :, cols] = o / l


def _proj_pool_body(x_ref, g_ref, w_ref, wbd_ref, sc_ref, pool_ref, utail_ref, q_ref, k_ref, v_ref, kt_ref, vt_ref,
                    ext_scr, *, pool_w, attn_w, tiles_per_seq, keep_tiles):
    tm = x_ref.shape[0]
    j = pl.program_id(0) % tiles_per_seq
    lead = POOL_STATE + 1
    pad = 2 * lead
    xn = _rms(x_ref[...], g_ref[...]).astype(BF16)

    def proj(lo, n):
        return jnp.dot(xn, w_ref[:, lo:lo + n], preferred_element_type=F32)

    @pl.when(j == 0)
    def _():
        ext_scr[0:pad, :] = jnp.zeros((pad, pool_w), F32)

    ext_scr[pad:, :] = proj(0, pool_w)
    win = _pool_windows(pool_w)
    tok = ext_scr[pad:, :]
    acc = tok
    for i in range(1, max(POOL_WINDOWS)):
        acc = acc + jnp.where(i < win, ext_scr[pl.ds(pad - i, tm), :], 0.0)
    pos = j * tm + lax.broadcasted_iota(I32, (tm, pool_w), 0)
    cnt = jnp.minimum(win, pos + 1).astype(F32)
    diff = (acc / cnt - tok).astype(BF16)
    pool_ref[...] = jnp.dot(diff, wbd_ref[...], preferred_element_type=F32) * sc_ref[...]
    tail = ext_scr[tm + pad - lead:tm + pad, :]
    ext_scr[pad - lead:pad, :] = tail

    @pl.when(j == tiles_per_seq - 1)
    def _():
        utail_ref[0] = tail

    q_ref[...] = proj(pool_w, attn_w) * (HEAD_DIM ** -0.5)
    k = proj(pool_w + attn_w, attn_w)
    v = proj(pool_w + 2 * attn_w, attn_w)
    k_ref[...] = k
    v_ref[...] = v

    @pl.when(j >= tiles_per_seq - keep_tiles)
    def _():
        kt_ref[0] = k.T
        vt_ref[0] = v.T


def _attn_sample_proj_body(slopes_ref, q_ref, kn_ref, vn_ref, kt_ref, vt_ref, x_ref, g_ref, w_ref, wbd_ref, sc_ref,
                           o_ref, pool_ref, utail_ref, qp_ref, kp_ref, vp_ref, ktp_ref, vtp_ref, ext_scr, *, n_heads,
                           pool_w, attn_w, tiles_per_seq, keep_tiles):
    _attn_sample_body(slopes_ref, q_ref, kn_ref, vn_ref, kt_ref, vt_ref, o_ref, n_heads=n_heads)
    _proj_pool_body(x_ref, g_ref, w_ref, wbd_ref, sc_ref, pool_ref, utail_ref, qp_ref, kp_ref, vp_ref, ktp_ref, vtp_ref,
                    ext_scr, pool_w=pool_w, attn_w=attn_w, tiles_per_seq=tiles_per_seq, keep_tiles=keep_tiles)


def _attn_sample_proj(slopes, q, k_new, v_new, cache_kt, cache_vt, x_p, g, w_bf16, w_bd, scale, *, seq, keep):
    n, t, hw = q.shape
    _, h, e, past = cache_kt.shape
    n_p, d = x_p.shape
    w = w_bd.shape[0]
    lead = POOL_STATE + 1
    assert past >= ATTN_REACH, "every strided key of every branch must exist in the window buffer"
    tm = n_p // n
    assert tm * n == n_p and seq % tm == 0 and keep % tm == 0 and tm % LANES == 0
    tiles_per_seq, keep_tiles = seq // tm, keep // tm
    first = tiles_per_seq - keep_tiles
    new = pl.BlockSpec((1, t, hw), lambda i: (i, 0, 0))
    old = pl.BlockSpec((1, h, e, past), lambda i: (i, 0, 0, 0))
    row = lambda i: (i, 0)
    fix = lambda i: (0, 0)
    tmap = lambda i: (i // tiles_per_seq, 0, jnp.maximum(i % tiles_per_seq - first, 0))
    return pl.pallas_call(
        functools.partial(_attn_sample_proj_body, n_heads=h, pool_w=w, attn_w=hw, tiles_per_seq=tiles_per_seq,
                          keep_tiles=keep_tiles),
        grid=(n,),
        in_specs=[pl.BlockSpec(memory_space=pltpu.SMEM), new, new, new, old, old,
                  pl.BlockSpec((tm, d), row), pl.BlockSpec((1, d), fix), pl.BlockSpec(w_bf16.shape, fix),
                  pl.BlockSpec((w, w), fix), pl.BlockSpec((1, w), fix)],
        out_specs=[new, pl.BlockSpec((tm, w), row), pl.BlockSpec((1, lead, w), lambda i: (i // tiles_per_seq, 0, 0))]
        + [pl.BlockSpec((tm, hw), row)] * 3 + [pl.BlockSpec((1, hw, tm), tmap)] * 2,
        out_shape=[jax.ShapeDtypeStruct((n, t, hw), F32), jax.ShapeDtypeStruct((n_p, w), F32),
                   jax.ShapeDtypeStruct((n_p // seq, lead, w), F32)]
        + [jax.ShapeDtypeStruct((n_p, hw), F32)] * 3 + [jax.ShapeDtypeStruct((n_p // seq, hw, keep), F32)] * 2,
        scratch_shapes=[pltpu.VMEM((tm + 2 * lead, w), F32)],
        compiler_params=_params("arbitrary"), name="attn_sample",
    )(slopes, q, k_new, v_new, cache_kt, cache_vt, x_p, g, w_bf16, w_bd, scale)


def _attn_sample_pool_body(slopes_ref, q_ref, kn_ref, vn_ref, kt_ref, vt_ref, u_ref, wbd_ref, sc_ref, o_ref, pool_ref,
                           ext_scr, *, n_heads, tm_pool, tiles_per_seq):
    _attn_sample_body(slopes_ref, q_ref, kn_ref, vn_ref, kt_ref, vt_ref, o_ref, n_heads=n_heads)
    _pool_prompt_body(u_ref, wbd_ref, sc_ref, pool_ref, ext_scr, tm=tm_pool, j=pl.program_id(0) % tiles_per_seq)


def _attn_sample_pool(slopes, q, k_new, v_new, cache_kt, cache_vt, u_prompt, w_bd, scale):
    n, t, hw = q.shape
    _, h, e, past = cache_kt.shape
    b, s, w = u_prompt.shape
    assert past >= ATTN_REACH, "every strided key of every branch must exist in the window buffer"
    tm_pool = b * s // n
    assert tm_pool * n == b * s and s % tm_pool == 0 and tm_pool % (POOL_STATE + 1) == 0
    tiles_per_seq = s // tm_pool
    new = pl.BlockSpec((1, t, hw), lambda i: (i, 0, 0))
    old = pl.BlockSpec((1, h, e, past), lambda i: (i, 0, 0, 0))
    fix = lambda i: (0, 0)
    return pl.pallas_call(
        functools.partial(_attn_sample_pool_body, n_heads=h, tm_pool=tm_pool, tiles_per_seq=tiles_per_seq),
        grid=(n,),
        in_specs=[pl.BlockSpec(memory_space=pltpu.SMEM), new, new, new, old, old,
                  pl.BlockSpec((1, s, w), lambda i: (i // tiles_per_seq, 0, 0)), pl.BlockSpec((w, w), fix),
                  pl.BlockSpec((1, w), fix)],
        out_specs=[new, pl.BlockSpec((1, tm_pool, w), lambda i: (i // tiles_per_seq, i % tiles_per_seq, 0))],
        out_shape=[jax.ShapeDtypeStruct((n, t, hw), F32), jax.ShapeDtypeStruct((b, s, w), F32)],
        scratch_shapes=[pltpu.VMEM((tm_pool + 2 * (POOL_STATE + 1), w), F32)],
        compiler_params=_params("arbitrary"), name="attn_sample",
    )(slopes, q, k_new, v_new, cache_kt, cache_vt, u_prompt, w_bd, scale)


def _pool_windows(width):
    gw = width // len(POOL_WINDOWS)
    lane = lax.broadcasted_iota(I32, (1, width), 1)
    win = jnp.zeros((1, width), I32)
    for g, w in enumerate(POOL_WINDOWS):
        win = jnp.where((lane >= g * gw) & (lane < (g + 1) * gw), w, win)
    return win


def _pool_prompt_body(u_ref, w_ref, sc_ref, o_ref, ext_scr, *, tm, j):
    width = u_ref.shape[2]
    pad = 2 * (POOL_STATE + 1)
    start = pl.multiple_of(j * tm, tm)
    lead = POOL_STATE + 1
    prev = u_ref[0, pl.ds(pl.multiple_of(jnp.maximum(start - lead, 0), lead), lead), :]
    ext_scr[0:pad - lead, :] = jnp.zeros((pad - lead, width), F32)
    ext_scr[pad - lead:pad, :] = jnp.where(j > 0, prev, 0.0)
    ext_scr[pad:, :] = u_ref[0, pl.ds(start, tm), :]
    win = _pool_windows(width)
    tok = ext_scr[pad:, :]
    acc = tok
    for i in range(1, max(POOL_WINDOWS)):
        acc = acc + jnp.where(i < win, ext_scr[pl.ds(pad - i, tm), :], 0.0)
    pos = start + lax.broadcasted_iota(I32, (tm, width), 0)
    cnt = jnp.minimum(win, pos + 1).astype(F32)
    diff = (acc / cnt - tok).astype(BF16)
    o_ref[0] = jnp.dot(diff, w_ref[...], preferred_element_type=F32) * sc_ref[...]


def _pool_sample_body(st_ref, u_ref, w_ref, sc_ref, o_ref, ns_ref):
    t_new, _, width = u_ref.shape
    n_state = st_ref.shape[0]
    win = _pool_windows(width)

    def row(k):
        return st_ref[k] if k < n_state else u_ref[k - n_state]

    for t in range(t_new):
        tok = u_ref[t]
        acc = tok
        for i in range(1, max(POOL_WINDOWS)):
            acc = acc + jnp.where(i < win, row(n_state + t - i), 0.0)
        diff = (acc / win.astype(F32) - tok).astype(BF16)
        o_ref[t] = jnp.dot(diff, w_ref[...], preferred_element_type=F32) * sc_ref[...]
    for k in range(n_state):
        ns_ref[k] = row(k + t_new)


def _pool_sample(state_t, u, w_bd, scale):
    n_state = state_t.shape[0]
    assert n_state >= POOL_STATE
    return pl.pallas_call(
        _pool_sample_body,
        out_shape=[jax.ShapeDtypeStruct(u.shape, F32), jax.ShapeDtypeStruct(state_t.shape, F32)],
        compiler_params=pltpu.CompilerParams(vmem_limit_bytes=VMEM_LIMIT), name="pool_sample",
    )(state_t, u, w_bd, scale)


ROUTE_F1, ROUTE_F2, ROUTE_G1, ROUTE_G2, ROUTE_R1, ROUTE_R2 = range(6)
ROUTE_ROWS = 8


def _split_bf16(x):
    hi = x.astype(BF16)
    return hi, (x - hi.astype(F32)).astype(BF16)


def _mix_route_body(x_ref, pool_ref, attn_ref, wo_ref, g_ref, wr_ref, br_ref, cnt_in_ref, h_ref, route_ref,
                    route_t_ref, cnt_out_ref, carry_scr, *, n_groups, n_experts):
    i = pl.program_id(0)
    tm = x_ref.shape[0]
    pool_w = pool_ref.shape[1]

    @pl.when(i == 0)
    def _():
        carry_scr[...] = cnt_in_ref[...]

    h = x_ref[...]
    h = h + jnp.dot(pool_ref[...].astype(BF16), wo_ref[0:pool_w, :], preferred_element_type=F32)
    h = h + jnp.dot(attn_ref[...].astype(BF16), wo_ref[pool_w:, :], preferred_element_type=F32)
    h_ref[...] = h

    hn_hi, hn_lo = _split_bf16(_rms(h, g_ref[...]))
    w_hi, w_lo = _split_bf16(wr_ref[...])
    hi_both = jnp.dot(hn_hi, jnp.concatenate([w_hi, w_lo], axis=1), preferred_element_type=F32)
    logits = (hi_both[:, :LANES] + hi_both[:, LANES:]
              + jnp.dot(hn_lo, w_hi, preferred_element_type=F32)) + br_ref[...]

    lane = lax.broadcasted_iota(I32, (tm, LANES), 1).astype(F32)

    def first_lane(mask):
        return jnp.min(jnp.where(mask, lane, float(LANES)), axis=1, keepdims=True)

    is_g = lane < n_groups
    lg = jnp.where(is_g, logits, NEG_INF)
    mg = jnp.max(lg, axis=1, keepdims=True)
    p_sel = 1.0 / jnp.sum(jnp.exp(lg - mg), axis=1, keepdims=True)
    g_top = first_lane(lg == mg)
    lo = n_groups + g_top * n_experts
    in_grp = (lane >= lo) & (lane < lo + n_experts)
    le = jnp.where(in_grp, logits, NEG_INF)
    ee = jnp.exp(le - jnp.max(le, axis=1, keepdims=True))
    pe = ee / jnp.sum(ee, axis=1, keepdims=True)
    v1 = jnp.max(jnp.where(in_grp, pe, -1.0), axis=1, keepdims=True)
    i1 = first_lane(in_grp & (pe == v1))
    rest = in_grp & (lane != i1)
    v2 = jnp.max(jnp.where(rest, pe, -1.0), axis=1, keepdims=True)
    i2 = first_lane(rest & (pe == v2))
    gate1 = p_sel * (v1 / (v1 + v2))
    gate2 = p_sel * (v2 / (v1 + v2))

    sel1, sel2 = lane == i1, lane == i2
    onehot = (sel1 | sel2).astype(BF16)
    tri = (lax.broadcasted_iota(I32, (tm, tm), 1) < lax.broadcasted_iota(I32, (tm, tm), 0)).astype(BF16)
    running = jnp.dot(tri, onehot, preferred_element_type=F32) + carry_scr[...]
    rank1 = jnp.sum(jnp.where(sel1, running, 0.0), axis=1, keepdims=True)
    rank2 = jnp.sum(jnp.where(sel2, running, 0.0), axis=1, keepdims=True)
    carry_scr[...] = carry_scr[...] + jnp.sum(onehot.astype(F32), axis=0, keepdims=True)
    cnt_out_ref[...] = carry_scr[...]

    rec = jnp.zeros((tm, LANES), F32)
    for idx, val in ((ROUTE_F1, i1 - n_groups), (ROUTE_F2, i2 - n_groups),
                     (ROUTE_G1, gate1), (ROUTE_G2, gate2), (ROUTE_R1, rank1), (ROUTE_R2, rank2)):
        rec = jnp.where(lane == idx, val, rec)
    route_ref[...] = rec
    route_t_ref[...] = rec.T[:ROUTE_ROWS]


def _mix_route(x, pool, attn, w_out_bf16, g_ffn, w_route, b_route, cnt_in, *, n_groups, n_experts):
    n, d = x.shape
    tm = TM_MIX
    row = lambda i: (i, 0)
    fix = lambda i: (0, 0)
    return pl.pallas_call(
        functools.partial(_mix_route_body, n_groups=n_groups, n_experts=n_experts),
        grid=(n // tm,),
        in_specs=[pl.BlockSpec((tm, d), row), pl.BlockSpec((tm, pool.shape[1]), row),
                  pl.BlockSpec((tm, attn.shape[1]), row), pl.BlockSpec(w_out_bf16.shape, fix),
                  pl.BlockSpec((1, d), fix), pl.BlockSpec(w_route.shape, fix), pl.BlockSpec((1, LANES), fix),
                  pl.BlockSpec((1, LANES), fix)],
        out_specs=[pl.BlockSpec((tm, d), row), pl.BlockSpec((tm, LANES), row),
                   pl.BlockSpec((ROUTE_ROWS, tm), lambda i: (0, i)), pl.BlockSpec((1, LANES), fix)],
        out_shape=[jax.ShapeDtypeStruct((n, d), F32), jax.ShapeDtypeStruct((n, LANES), F32),
                   jax.ShapeDtypeStruct((ROUTE_ROWS, n), F32), jax.ShapeDtypeStruct((1, LANES), F32)],
        scratch_shapes=[pltpu.VMEM((1, LANES), F32)],
        compiler_params=_params("arbitrary"), name="mix_route",
    )(x, pool, attn, w_out_bf16, g_ffn, w_route, b_route, cnt_in)


def _dispatch_body(zlo_ref, zhi_ref, dest_ref, hp_ref, hs_ref, g_ref, xs_ref, xn_scr, zero_scr, sems, *, tiles_p,
                   n_steps):
    tm = hp_ref.shape[0]
    i = pl.program_id(0)
    zero_sem = 2 * TOP_K_INNER

    @pl.when(i == 0)
    def _():
        zero_scr[...] = jnp.zeros(zero_scr.shape, F32)
        for start in (True, False):
            def chunk(c, carry, start=start):
                cp = pltpu.make_async_copy(zero_scr, xs_ref.at[pl.ds(pl.multiple_of(c * ZERO_ROWS, ZERO_ROWS), ZERO_ROWS)],
                                           sems.at[zero_sem])
                cp.start() if start else cp.wait()
                return carry

            def segment(e, carry, chunk=chunk):
                return lax.fori_loop(zlo_ref[e], zhi_ref[e], chunk, carry)
            lax.fori_loop(0, zlo_ref.shape[0], segment, 0)

    buf = i % 2

    def wait_rows(b):
        for slot in range(TOP_K_INNER):
            pltpu.make_async_copy(xn_scr.at[b], xs_ref.at[pl.ds(0, tm)], sems.at[b * TOP_K_INNER + slot]).wait()

    @pl.when(i >= 2)
    def _():
        wait_rows(buf)

    @pl.when(i < tiles_p)
    def _():
        xn_scr[buf] = _rms(hp_ref[...], g_ref[...])

    @pl.when(i >= tiles_p)
    def _():
        xn_scr[buf] = _rms(hs_ref[...], g_ref[...])

    for r in range(tm):
        for slot in range(TOP_K_INNER):
            d = dest_ref[0, 0, slot * tm + r]
            pltpu.make_async_copy(xn_scr.at[buf, pl.ds(r, 1)], xs_ref.at[pl.ds(d, 1)],
                                  sems.at[buf * TOP_K_INNER + slot]).start(priority=slot)

    @pl.when(i == n_steps - 1)
    def _():
        if n_steps > 1:
            wait_rows(1 - buf)
        wait_rows(buf)


def _dispatch(zero_lo, zero_hi, dest, h_p, h_s, g_ffn, *, rows):
    tm = TM_ROW
    d = h_p.shape[1]
    tiles_p, tiles_s = h_p.shape[0] // tm, h_s.shape[0] // tm
    return pl.pallas_call(
        functools.partial(_dispatch_body, tiles_p=tiles_p, n_steps=tiles_p + tiles_s),
        grid_spec=pltpu.PrefetchScalarGridSpec(
            num_scalar_prefetch=2, grid=(tiles_p + tiles_s,),
            in_specs=[pl.BlockSpec((1, 1, TOP_K_INNER * tm), lambda i, lo, hi: (i, 0, 0), memory_space=pltpu.SMEM),
                      pl.BlockSpec((tm, d), lambda i, lo, hi: (jnp.minimum(i, tiles_p - 1), 0)),
                      pl.BlockSpec((tm, d), lambda i, lo, hi: (jnp.maximum(i - tiles_p, 0), 0)),
                      pl.BlockSpec((1, d), lambda i, lo, hi: (0, 0))],
            out_specs=pl.BlockSpec(memory_space=pl.ANY),
            scratch_shapes=[pltpu.VMEM((2, tm, d), F32), pltpu.VMEM((ZERO_ROWS, d), F32),
                            pltpu.SemaphoreType.DMA((2 * TOP_K_INNER + 1,))]),
        out_shape=jax.ShapeDtypeStruct((rows, d), F32),
        compiler_params=_params("arbitrary"), name="dispatch",
    )(zero_lo, zero_hi, dest, h_p, h_s, g_ffn)


GEMM_X_BUFS = 3
GEMM_Y_BUFS = 2


def _moe_gemm_body(expert_ref, valid_ref, xs_hbm, wg_ref, wu_ref, wd_ref, ys_hbm, x_buf, y_buf, wg_scr, wu_scr, wd_scr,
                   x_sems, y_sems, *, n_tiles):
    t = pl.program_id(0)
    tm = x_buf.shape[1]
    ahead = GEMM_X_BUFS - 1

    def x_copy(s):
        return pltpu.make_async_copy(xs_hbm.at[pl.ds(pl.multiple_of(s * tm, tm), tm)], x_buf.at[s % GEMM_X_BUFS],
                                     x_sems.at[s % GEMM_X_BUFS])

    def y_copy(s):
        return pltpu.make_async_copy(y_buf.at[s % GEMM_Y_BUFS], ys_hbm.at[pl.ds(pl.multiple_of(s * tm, tm), tm)],
                                     y_sems.at[s % GEMM_Y_BUFS])

    def tile_is_valid(s):
        return (s < n_tiles) & (valid_ref[jnp.minimum(s, n_tiles - 1)] > 0)

    @pl.when(t == 0)
    def _():
        for s in range(ahead):
            @pl.when(tile_is_valid(s))
            def _():
                x_copy(s).start()

    @pl.when(tile_is_valid(t + ahead))
    def _():
        x_copy(t + ahead).start()

    @pl.when((t == 0) | (expert_ref[t] != expert_ref[jnp.maximum(t - 1, 0)]))
    def _():
        wg_scr[...] = wg_ref[0].astype(BF16)
        wu_scr[...] = wu_ref[0].astype(BF16)
        wd_scr[...] = wd_ref[0].astype(BF16)

    @pl.when(t >= GEMM_Y_BUFS)
    def _():
        y_copy(t - GEMM_Y_BUFS).wait()

    ybuf = y_buf.at[t % GEMM_Y_BUFS]

    @pl.when(valid_ref[t] > 0)
    def _():
        x_copy(t).wait()
        x = x_buf[t % GEMM_X_BUFS].astype(BF16)
        gate = jnp.dot(x, wg_scr[...], preferred_element_type=F32)
        up = jnp.dot(x, wu_scr[...], preferred_element_type=F32)
        mid = (gate * jax.nn.sigmoid(gate) * up).astype(BF16)
        ybuf[...] = jnp.dot(mid, wd_scr[...], preferred_element_type=F32)

    @pl.when(valid_ref[t] == 0)
    def _():
        ybuf[...] = jnp.zeros(ybuf.shape, F32)

    y_copy(t).start()

    @pl.when(t == n_tiles - 1)
    def _():
        for back in range(min(GEMM_Y_BUFS, n_tiles) - 1, -1, -1):
            y_copy(t - back).wait()


def _moe_gemm(tile_expert, tile_valid, xs, w_gate, w_up, w_down):
    rows, d = xs.shape
    _, _, f = w_gate.shape
    tm = TM_GEMM
    n_tiles = rows // tm
    hbm = pl.BlockSpec(memory_space=pl.ANY)
    return pl.pallas_call(
        functools.partial(_moe_gemm_body, n_tiles=n_tiles),
        grid_spec=pltpu.PrefetchScalarGridSpec(
            num_scalar_prefetch=2, grid=(n_tiles,),
            in_specs=[hbm,
                      pl.BlockSpec((1, d, f), lambda t, e, v: (e[t], 0, 0)),
                      pl.BlockSpec((1, d, f), lambda t, e, v: (e[t], 0, 0)),
                      pl.BlockSpec((1, f, d), lambda t, e, v: (e[t], 0, 0))],
            out_specs=hbm,
            scratch_shapes=[pltpu.VMEM((GEMM_X_BUFS, tm, d), F32), pltpu.VMEM((GEMM_Y_BUFS, tm, d), F32),
                            pltpu.VMEM((d, f), BF16), pltpu.VMEM((d, f), BF16), pltpu.VMEM((f, d), BF16),
                            pltpu.SemaphoreType.DMA((GEMM_X_BUFS,)), pltpu.SemaphoreType.DMA((GEMM_Y_BUFS,))]),
        out_shape=jax.ShapeDtypeStruct((rows, d), F32),
        compiler_params=_params("arbitrary"), name="moe_gemm",
    )(tile_expert, tile_valid, xs, w_gate, w_up, w_down)


def _combine_body(dest_ref, next_ref, h_ref, route_ref, g_ref, ys_ref, y_ref, rows_scr, sems, *, n_steps):
    tm = h_ref.shape[0]
    i = pl.program_id(0)
    buf = i % 2

    def fetch(idx_ref, b):
        for r in range(tm):
            for slot in range(TOP_K_INNER):
                d = idx_ref[0, 0, slot * tm + r]
                pltpu.make_async_copy(ys_ref.at[pl.ds(d, 1)], rows_scr.at[b, slot, pl.ds(r, 1)],
                                      sems.at[b * TOP_K_INNER + slot]).start(priority=slot)

    @pl.when(i == 0)
    def _():
        fetch(dest_ref, 0)

    @pl.when(i + 1 < n_steps)
    def _():
        fetch(next_ref, 1 - buf)

    for slot in range(TOP_K_INNER):
        pltpu.make_async_copy(ys_ref.at[pl.ds(0, tm)], rows_scr.at[buf, slot], sems.at[buf * TOP_K_INNER + slot]).wait()
    route = route_ref[...]
    out = h_ref[...] + (route[:, ROUTE_G1:ROUTE_G1 + 1] * rows_scr[buf, 0]
                        + route[:, ROUTE_G2:ROUTE_G2 + 1] * rows_scr[buf, 1])
    y_ref[...] = _rms(out, g_ref[...])


def _combine(dest, h, route, g_final, ys):
    n, d = h.shape
    tm = TM_ROW
    n_steps = n // tm
    row = lambda i: (i, 0)
    idx_block = (1, 1, TOP_K_INNER * tm)
    return pl.pallas_call(
        functools.partial(_combine_body, n_steps=n_steps),
        grid=(n_steps,),
        in_specs=[pl.BlockSpec(idx_block, lambda i: (i, 0, 0), memory_space=pltpu.SMEM),
                  pl.BlockSpec(idx_block, lambda i: (jnp.minimum(i + 1, n_steps - 1), 0, 0), memory_space=pltpu.SMEM),
                  pl.BlockSpec((tm, d), row), pl.BlockSpec((tm, LANES), row), pl.BlockSpec((1, d), lambda i: (0, 0)),
                  pl.BlockSpec(memory_space=pl.ANY)],
        out_specs=pl.BlockSpec((tm, d), row),
        out_shape=jax.ShapeDtypeStruct((n, d), F32),
        scratch_shapes=[pltpu.VMEM((2, TOP_K_INNER, tm, d), F32), pltpu.SemaphoreType.DMA((2 * TOP_K_INNER,))],
        compiler_params=_params("arbitrary"), name="combine",
    )(dest, dest, h, route, g_final, ys)


def _sort_tables(counts, n_tiles):
    padded = ((counts + TM_GEMM - 1) // TM_GEMM) * TM_GEMM
    ends = jnp.cumsum(padded)
    offsets = ends - padded
    total = ends[-1]
    tile_start = jnp.arange(n_tiles, dtype=I32) * TM_GEMM
    tile_valid = (tile_start < total).astype(I32)
    last_block = jnp.maximum(total // TM_GEMM - 1, 0)
    tile_block = jnp.minimum(jnp.arange(n_tiles, dtype=I32), last_block)
    n_flat = counts.shape[0]
    tile_expert = jnp.minimum(jnp.sum((tile_block[:, None] * TM_GEMM >= ends[None, :]).astype(I32), axis=1), n_flat - 1)
    zero_lo = jnp.concatenate([(offsets + counts) // ZERO_ROWS, total[None] // ZERO_ROWS]).astype(I32)
    zero_hi = jnp.concatenate([ends // ZERO_ROWS, jnp.full((1,), n_tiles * TM_GEMM // ZERO_ROWS, I32)]).astype(I32)
    return offsets, tile_expert.astype(I32), tile_valid, zero_lo, zero_hi


def _dest_blocks(route_t, offsets):
    n = route_t.shape[1]
    experts = jnp.arange(offsets.shape[0], dtype=I32)[:, None]

    def dest(f_row, r_row):
        f = route_t[f_row].astype(I32)
        base = jnp.sum(jnp.where(f[None, :] == experts, offsets[:, None], 0), axis=0)
        return (base + route_t[r_row].astype(I32)).reshape(n // TM_ROW, TM_ROW)

    return jnp.concatenate([dest(ROUTE_F1, ROUTE_R1), dest(ROUTE_F2, ROUTE_R2)], axis=1)[:, None, :]


def kernel(x_prompt, x_sample, cache_k, cache_v, state_pool, g_mix, w_in, w_pool, pool_scale, w_out, g_ffn,
           w_router_group, b_router_group, w_router_expert, b_router_expert, w_gate, w_up, w_down, g_final):
    depth = g_mix.shape[0]
    assert depth == 1, "single-layer step"
    b, s, d = x_prompt.shape
    nd, t_new, _ = x_sample.shape
    n_heads = cache_k.shape[3]
    attn_w = n_heads * HEAD_DIM
    pool_w = state_pool.shape[3]
    past = cache_k.shape[2]
    keep = min(ATTN_REACH, s)
    n_groups, n_experts = w_router_expert.shape[1], w_router_expert.shape[3]
    n_flat = n_groups * n_experts
    assert n_groups + n_flat <= LANES
    slopes = _alibi_slopes(n_heads)

    w_in_b = w_in[0].astype(BF16)
    w_out_b = w_out[0].astype(BF16)
    gw = pool_w // len(POOL_WINDOWS)
    w_bd = jnp.zeros((pool_w, pool_w), F32)
    for g in range(len(POOL_WINDOWS)):
        w_bd = w_bd.at[g * gw:(g + 1) * gw, g * gw:(g + 1) * gw].set(w_pool[0, g])
    w_bd = w_bd.astype(BF16)
    w_route = jnp.concatenate([w_router_group[0], jnp.transpose(w_router_expert[0], (1, 0, 2)).reshape(d, n_flat)], axis=1)
    w_route = jnp.pad(w_route, ((0, 0), (0, LANES - n_groups - n_flat)))
    b_route = jnp.pad(jnp.concatenate([b_router_group[0], b_router_expert[0].reshape(n_flat)]),
                      (0, LANES - n_groups - n_flat))[None]
    w_gate_f, w_up_f = w_gate[0].reshape(n_flat, d, -1), w_up[0].reshape(n_flat, d, -1)
    w_down_f = w_down[0].reshape(n_flat, -1, d)

    n_p, n_s = b * s, nd * t_new
    u_s, q_s, k_s, v_s, ktn_s, vtn_s = _proj_in(x_sample.reshape(n_s, d), g_mix, w_in_b, pool_w=pool_w, attn_w=attn_w,
                                                t_new=t_new)
    cache_kt = jnp.transpose(cache_k[0], (0, 2, 3, 1))
    cache_vt = jnp.transpose(cache_v[0], (0, 2, 3, 1))
    as3 = lambda a: a.reshape(nd, t_new, attn_w)
    attn_s, pool_p, u_tail, q_p, k_p, v_p, kt_p, vt_p = _attn_sample_proj(
        slopes, as3(q_s), as3(k_s), as3(v_s), cache_kt, cache_vt, x_prompt.reshape(n_p, d), g_mix, w_in_b, w_bd,
        pool_scale, seq=s, keep=keep)
    attn_p = _attn_prompt(slopes, q_p.reshape(b, s, attn_w), k_p.reshape(b, s, attn_w), v_p.reshape(b, s, attn_w))
    state_t = jnp.transpose(state_pool[0], (1, 0, 2))
    u_st = jnp.transpose(u_s.reshape(nd, t_new, pool_w), (1, 0, 2))
    pool_st, new_state_t = _pool_sample(state_t, u_st, w_bd, pool_scale)
    pool_s = jnp.transpose(pool_st, (1, 0, 2)).reshape(n_s, pool_w)

    route_kw = dict(n_groups=n_groups, n_experts=n_experts)
    h_p, route_p, route_pt, cnt_p = _mix_route(x_prompt.reshape(n_p, d), pool_p.reshape(n_p, pool_w),
                                               attn_p.reshape(n_p, attn_w), w_out_b, g_ffn, w_route, b_route,
                                               jnp.zeros((1, LANES), F32), **route_kw)
    h_s, route_s, route_st, cnt_all = _mix_route(x_sample.reshape(n_s, d), pool_s, attn_s.reshape(n_s, attn_w),
                                                 w_out_b, g_ffn, w_route, b_route, cnt_p, **route_kw)

    counts = cnt_all[0, n_groups:n_groups + n_flat].astype(I32)
    n_tiles = (TOP_K_INNER * (n_p + n_s) + n_flat * (TM_GEMM - 1)) // TM_GEMM
    offsets, tile_expert, tile_valid, zero_lo, zero_hi = _sort_tables(counts, n_tiles)
    dest_p, dest_s = _dest_blocks(route_pt, offsets), _dest_blocks(route_st, offsets)

    xs = _dispatch(zero_lo, zero_hi, jnp.concatenate([dest_p, dest_s], axis=0), h_p, h_s, g_ffn,
                   rows=n_tiles * TM_GEMM)
    ys = _moe_gemm(tile_expert, tile_valid, xs, w_gate_f, w_up_f, w_down_f)
    y_p = _combine(dest_p, h_p, route_p, g_final[None], ys)
    y_s = _combine(dest_s, h_s, route_s, g_final[None], ys)

    y_prompt = y_p.reshape(b, s, d)
    y_sample = y_s.reshape(nd, t_new, d)
    k_prompt = jnp.transpose(kt_p.reshape(b, n_heads, HEAD_DIM, keep), (0, 3, 1, 2))[None]
    v_prompt = jnp.transpose(vt_p.reshape(b, n_heads, HEAD_DIM, keep), (0, 3, 1, 2))[None]
    pool_prompt = u_tail[:, -POOL_STATE:][None]
    k_sample = jnp.transpose(ktn_s.reshape(t_new, n_heads, HEAD_DIM, nd), (3, 0, 1, 2))[None]
    v_sample = jnp.transpose(vtn_s.reshape(t_new, n_heads, HEAD_DIM, nd), (3, 0, 1, 2))[None]
    pool_sample = jnp.transpose(new_state_t[-POOL_STATE:], (1, 0, 2))[None]
    return (y_prompt, y_sample, k_prompt, v_prompt, pool_prompt, k_sample, v_sample, pool_sample)
```

```python
import functools
import math

import numpy as np
import jax
import jax.numpy as jnp
from jax import lax
from jax.experimental import pallas as pl
from jax.experimental.pallas import tpu as pltpu

F32 = jnp.float32
BF16 = jnp.bfloat16
I32 = jnp.int32

HEAD_DIM = 64
POOL_WINDOWS = (2, 4, 8, 16)
POOL_STATE = max(POOL_WINDOWS) - 1
BRANCHES = ((128, 1), (512, 4), (2048, 16))
STEPS = BRANCHES[0][0] // BRANCHES[0][1]
ATTN_REACH = max(w for w, _ in BRANCHES)
MAX_DIL = max(d for _, d in BRANCHES)
TOP_K_INNER = 2
RMS_EPS = 1e-6
LANES = 128
NEG_INF = float("-inf")

VMEM_LIMIT = 56 * 1024 * 1024

TM_PROJ = 1024
TM_MIX = 512
TM_ROW = 512
TM_GEMM = 512
ZERO_ROWS = 64
ATTN_GROUP = 8


def _alibi_slopes(n_heads):
    def geometric(n):
        start = 2.0 ** (-8.0 / n)
        return [start ** (i + 1) for i in range(n)]
    c = 2 ** int(math.floor(math.log2(n_heads)))
    s = geometric(c)
    if c < n_heads:
        s = s + geometric(2 * c)[0::2][: n_heads - c]
    return jnp.asarray(s, dtype=F32)


def _rms(x, g):
    return x * lax.rsqrt(jnp.mean(x * x, axis=-1, keepdims=True) + RMS_EPS) * g


def _params(*sem):
    return pltpu.CompilerParams(dimension_semantics=sem, vmem_limit_bytes=VMEM_LIMIT)


def _proj_in_body(x_ref, g_ref, w_ref, u_ref, q_ref, k_ref, v_ref, *t_refs, pool_w, attn_w, tiles_per_seq,
                  keep_tiles, t_new):
    xn = _rms(x_ref[...], g_ref[...]).astype(BF16)

    def proj(lo, n):
        return jnp.dot(xn, w_ref[:, lo:lo + n], preferred_element_type=F32)

    u_ref[...] = proj(0, pool_w)
    q_ref[...] = proj(pool_w, attn_w) * (HEAD_DIM ** -0.5)
    k = proj(pool_w + attn_w, attn_w)
    v = proj(pool_w + 2 * attn_w, attn_w)
    k_ref[...] = k
    v_ref[...] = v
    if t_new is None:
        kt_ref, vt_ref = t_refs
        j = pl.program_id(0) % tiles_per_seq

        @pl.when(j >= tiles_per_seq - keep_tiles)
        def _():
            kt_ref[0] = k.T
            vt_ref[0] = v.T
    else:
        ktn_ref, vtn_ref, cols_scr = t_refs
        n_seq = ktn_ref.shape[2]
        for val, dst in ((k, ktn_ref), (v, vtn_ref)):
            for c in range(attn_w // LANES):
                lanes = slice(c * LANES, (c + 1) * LANES)
                cols_scr[...] = val[:, lanes]
                for t in range(t_new):
                    dst[t, lanes, :] = cols_scr[pl.ds(t, n_seq, stride=t_new), :].T


def _proj_in(x, g, w_bf16, *, pool_w, attn_w, seq=None, keep=None, t_new=None):
    n, d = x.shape
    tm = TM_PROJ if t_new is None else n
    grid = (n // tm,)
    row = lambda i: (i, 0)
    out_shape = [jax.ShapeDtypeStruct((n, pool_w), F32)] + [jax.ShapeDtypeStruct((n, attn_w), F32)] * 3
    out_specs = [pl.BlockSpec((tm, pool_w), row)] + [pl.BlockSpec((tm, attn_w), row)] * 3
    tiles_per_seq = keep_tiles = 0
    scratch = []
    if t_new is None:
        tiles_per_seq, keep_tiles = seq // tm, keep // tm
        first = tiles_per_seq - keep_tiles
        tmap = lambda i: (i // tiles_per_seq, 0, jnp.maximum(i % tiles_per_seq - first, 0))
        out_shape += [jax.ShapeDtypeStruct((n // seq, attn_w, keep), F32)] * 2
        out_specs += [pl.BlockSpec((1, attn_w, tm), tmap)] * 2
    else:
        assert n // t_new == LANES, "one square transpose per (step, column block)"
        out_shape += [jax.ShapeDtypeStruct((t_new, attn_w, n // t_new), F32)] * 2
        out_specs += [pl.BlockSpec((t_new, attn_w, n // t_new), lambda i: (0, 0, 0))] * 2
        scratch = [pltpu.VMEM((n, LANES), F32)]
    body = functools.partial(_proj_in_body, pool_w=pool_w, attn_w=attn_w, tiles_per_seq=tiles_per_seq,
                             keep_tiles=keep_tiles, t_new=t_new)
    return pl.pallas_call(
        body, grid=grid,
        in_specs=[pl.BlockSpec((tm, d), row), pl.BlockSpec((1, d), lambda i: (0, 0)),
                  pl.BlockSpec(w_bf16.shape, lambda i: (0, 0))],
        out_specs=out_specs, out_shape=out_shape, scratch_shapes=scratch,
        compiler_params=_params("arbitrary"), name="proj_in",
    )(x, g, w_bf16)


def _attn_prompt_body(slopes_ref, q_hbm, k_hbm, v_hbm, o_ref, qkv_scr, bias_scr, s_scr, p_scr, o_scr, m_scr, l_scr, sems,
                      *, seq, n_pairs, n_steps):
    hp = pl.program_id(1)
    gstep = pl.program_id(0) * n_pairs + hp
    buf = gstep % 2
    blk = STEPS
    n_it = seq // blk
    cls = seq // MAX_DIL

    def class_copies(st, b):
        lanes = pl.ds(pl.multiple_of((st % n_pairs) * LANES, LANES), LANES)
        return [pltpu.make_async_copy(src.at[st // n_pairs, :, c, lanes], qkv_scr.at[b, a, pl.ds(c * cls, cls), :],
                                      sems.at[b, a])
                for a, src in enumerate((q_hbm, k_hbm, v_hbm)) for c in range(MAX_DIL)]

    @pl.when(gstep == 0)
    def _():
        for cp in class_copies(gstep, 0):
            cp.start()

    @pl.when(gstep + 1 < n_steps)
    def _():
        for cp in class_copies(gstep + 1, 1 - buf):
            cp.start()

    for cp in class_copies(gstep, buf):
        cp.wait()
    qp_scr, kp_scr, vp_scr = (qkv_scr.at[buf, a] for a in range(3))

    lane = lax.broadcasted_iota(I32, (blk, LANES), 1)
    head_a = lane < HEAD_DIM
    row = lax.broadcasted_iota(I32, (2 * blk, 2 * blk), 0)
    col = lax.broadcasted_iota(I32, (2 * blk, 2 * blk), 1)
    slope = jnp.where(row < blk, slopes_ref[2 * hp], slopes_ref[2 * hp + 1])

    for bi, (_, d) in enumerate(BRANCHES):
        nb = n_it // d
        n_chunk = MAX_DIL // d
        a_rows = blk // n_chunk
        sh = a_rows.bit_length() - 1
        assert a_rows == 1 << sh and a_rows % 8 == 0

        def seq_index(i, n_chunk=n_chunk, a_rows=a_rows, sh=sh):
            i = i & (blk - 1)
            return (i & (a_rows - 1)) * n_chunk + (i >> sh)

        step = seq_index(row) + blk - (seq_index(col) + (col & blk))
        bias = jnp.where((step >= 0) & (step <= STEPS), -slope * (d * step).astype(F32), NEG_INF)
        bias_scr[0] = bias
        bias_scr[1] = jnp.where(col < blk, NEG_INF, bias)

        def group(j, carry, bi=bi, d=d, nb=nb, n_chunk=n_chunk, a_rows=a_rows):
            def chunks(g):
                it = j * ATTN_GROUP + g
                r = it // nb
                n = it % nb
                cur = [pl.ds(pl.multiple_of((r + d * c) * cls + n * a_rows, 8), a_rows) for c in range(n_chunk)]
                prev = [pl.ds(pl.multiple_of((r + d * c) * cls + jnp.maximum(n - 1, 0) * a_rows, 8), a_rows)
                        for c in range(n_chunk)]
                return n, cur, prev

            def gather(ref, sls):
                return jnp.concatenate([ref[sl, :] for sl in sls], axis=0)

            for g in range(ATTN_GROUP):
                n, cur, prev = chunks(g)
                qb = gather(qp_scr, cur)
                q2 = jnp.concatenate([jnp.where(head_a, qb, 0.0), jnp.where(head_a, 0.0, qb)], axis=0).astype(BF16)
                kc = gather(kp_scr, prev + cur).astype(BF16)
                s = lax.dot_general(q2, kc, (((1,), (1,)), ((), ())), preferred_element_type=F32)
                s_scr[g] = s + bias_scr[jnp.where(n == 0, 1, 0)]
            for g in range(ATTN_GROUP):
                _, cur, _ = chunks(g)
                s = s_scr[g]
                m = jnp.max(s, axis=1, keepdims=True)
                p_scr[g] = jnp.exp(s - m).astype(BF16)
                m2 = jnp.where(head_a, m[:blk], m[blk:])
                for c, sl in enumerate(cur):
                    m_scr[bi, sl, :] = m2[c * a_rows:(c + 1) * a_rows]
            ones = jnp.ones((2 * blk, LANES), BF16)
            for g in range(ATTN_GROUP):
                _, cur, prev = chunks(g)
                vc = jnp.concatenate([gather(vp_scr, prev + cur).astype(BF16), ones], axis=1)
                ol = jnp.dot(p_scr[g], vc, preferred_element_type=F32)
                o2 = jnp.where(head_a, ol[:blk, :LANES], ol[blk:, :LANES])
                l2 = jnp.where(head_a, ol[:blk, LANES:], ol[blk:, LANES:])
                for c, sl in enumerate(cur):
                    o_scr[bi, sl, :] = o2[c * a_rows:(c + 1) * a_rows]
                    l_scr[bi, sl, :] = l2[c * a_rows:(c + 1) * a_rows]
            return carry

        lax.fori_loop(0, n_it // ATTN_GROUP, group, 0)

    def merge(c, carry):
        rs = pl.ds(pl.multiple_of(c * cls, cls), cls)
        ms = [m_scr[b, rs, :] for b in range(len(BRANCHES))]
        mx = functools.reduce(jnp.maximum, ms)
        num = jnp.zeros((cls, LANES), F32)
        den = jnp.zeros((cls, LANES), F32)
        for b in range(len(BRANCHES)):
            a = jnp.exp(ms[b] - mx)
            num = num + a * o_scr[b, rs, :]
            den = den + a * l_scr[b, rs, :]
        o_ref[0, pl.ds(c, cls, stride=MAX_DIL), :] = num / den
        return carry

    lax.fori_loop(0, MAX_DIL, merge, 0)


def _attn_prompt(slopes, q, k, v):
    b, s, hw = q.shape
    assert s % (STEPS * max(d for _, d in BRANCHES)) == 0, "sequence must be a multiple of the widest span"
    assert hw % LANES == 0 and LANES == 2 * HEAD_DIM
    n_pairs = hw // LANES
    nbr = len(BRANCHES)
    by_class = lambda a: a.reshape(b, s // MAX_DIL, MAX_DIL, hw)
    hbm = pl.BlockSpec(memory_space=pl.ANY)
    return pl.pallas_call(
        functools.partial(_attn_prompt_body, seq=s, n_pairs=n_pairs, n_steps=b * n_pairs),
        grid=(b, n_pairs),
        in_specs=[pl.BlockSpec(memory_space=pltpu.SMEM), hbm, hbm, hbm],
        out_specs=pl.BlockSpec((1, s, LANES), lambda i, j: (i, 0, j)),
        out_shape=jax.ShapeDtypeStruct((b, s, hw), F32),
        scratch_shapes=[pltpu.VMEM((2, 3, s, LANES), F32),
                        pltpu.VMEM((2, 2 * STEPS, 2 * STEPS), F32),
                        pltpu.VMEM((ATTN_GROUP, 2 * STEPS, 2 * STEPS), F32),
                        pltpu.VMEM((ATTN_GROUP, 2 * STEPS, 2 * STEPS), BF16)]
        + [pltpu.VMEM((nbr, s, LANES), F32)] * 3 + [pltpu.SemaphoreType.DMA((2, 3))],
        compiler_params=_params("arbitrary", "arbitrary"), name="attn_prompt",
    )(slopes, by_class(q), by_class(k), by_class(v))


def _multiplicity(dist):
    mult = jnp.zeros(dist.shape, F32)
    for w, d in BRANCHES:
        assert d & (d - 1) == 0
        mult = mult + ((dist >= 0) & ((dist & (d - 1)) == 0) & (dist <= w)).astype(F32)
    return mult


def _attn_sample_body(slopes_ref, q_ref, kn_ref, vn_ref, kt_ref, vt_ref, o_ref, *, n_heads):
    t_new = q_ref.shape[1]
    past = kt_ref.shape[3]
    dist = past + lax.broadcasted_iota(I32, (t_new, past), 0) - lax.broadcasted_iota(I32, (t_new, past), 1)
    dist_n = lax.broadcasted_iota(I32, (t_new, t_new), 0) - lax.broadcasted_iota(I32, (t_new, t_new), 1)
    mult, mult_n = _multiplicity(dist), _multiplicity(dist_n)
    dist_f, dist_nf = dist.astype(F32), dist_n.astype(F32)
    nt = (((1,), (1,)), ((), ()))
    for h in range(n_heads):
        slope = slopes_ref[h]
        cols = slice(h * HEAD_DIM, (h + 1) * HEAD_DIM)
        qh = q_ref[0, :, cols].astype(BF16)
        kn = kn_ref[0, :, cols].astype(BF16)
        vn = vn_ref[0, :, cols].astype(BF16)
        s = jnp.dot(qh, kt_ref[0, h].astype(BF16), preferred_element_type=F32)
        s = jnp.where(mult > 0, s - slope * dist_f, NEG_INF)
        sn = lax.dot_general(qh, kn, nt, preferred_element_type=F32)
        sn = jnp.where(mult_n > 0, sn - slope * dist_nf, NEG_INF)
        m = jnp.maximum(jnp.max(s, axis=1, keepdims=True), jnp.max(sn, axis=1, keepdims=True))
        p = mult * jnp.exp(s - m)
        pn = mult_n * jnp.exp(sn - m)
        l = jnp.sum(p, axis=1, keepdims=True) + jnp.sum(pn, axis=1, keepdims=True)
        o = lax.dot_general(p.astype(BF16), vt_ref[0, h].astype(BF16), nt, preferred_element_type=F32)
        o = o + jnp.dot(pn.astype(BF16), vn, preferred_element_type=F32)
        o_ref[0, :, cols] = o / l


def _attn_sample_pool_body(slopes_ref, q_ref, kn_ref, vn_ref, kt_ref, vt_ref, u_ref, wbd_ref, sc_ref, o_ref, pool_ref,
                           ext_scr, *, n_heads, tm_pool, tiles_per_seq):
    _attn_sample_body(slopes_ref, q_ref, kn_ref, vn_ref, kt_ref, vt_ref, o_ref, n_heads=n_heads)
    _pool_prompt_body(u_ref, wbd_ref, sc_ref, pool_ref, ext_scr, tm=tm_pool, j=pl.program_id(0) % tiles_per_seq)


def _attn_sample_pool(slopes, q, k_new, v_new, cache_kt, cache_vt, u_prompt, w_bd, scale):
    n, t, hw = q.shape
    _, h, e, past = cache_kt.shape
    b, s, w = u_prompt.shape
    assert past >= ATTN_REACH, "every strided key of every branch must exist in the window buffer"
    tm_pool = b * s // n
    assert tm_pool * n == b * s and s % tm_pool == 0 and tm_pool % (POOL_STATE + 1) == 0
    tiles_per_seq = s // tm_pool
    new = pl.BlockSpec((1, t, hw), lambda i: (i, 0, 0))
    old = pl.BlockSpec((1, h, e, past), lambda i: (i, 0, 0, 0))
    fix = lambda i: (0, 0)
    return pl.pallas_call(
        functools.partial(_attn_sample_pool_body, n_heads=h, tm_pool=tm_pool, tiles_per_seq=tiles_per_seq),
        grid=(n,),
        in_specs=[pl.BlockSpec(memory_space=pltpu.SMEM), new, new, new, old, old,
                  pl.BlockSpec((1, s, w), lambda i: (i // tiles_per_seq, 0, 0)), pl.BlockSpec((w, w), fix),
                  pl.BlockSpec((1, w), fix)],
        out_specs=[new, pl.BlockSpec((1, tm_pool, w), lambda i: (i // tiles_per_seq, i % tiles_per_seq, 0))],
        out_shape=[jax.ShapeDtypeStruct((n, t, hw), F32), jax.ShapeDtypeStruct((b, s, w), F32)],
        scratch_shapes=[pltpu.VMEM((tm_pool + 2 * (POOL_STATE + 1), w), F32)],
        compiler_params=_params("arbitrary"), name="attn_sample",
    )(slopes, q, k_new, v_new, cache_kt, cache_vt, u_prompt, w_bd, scale)


def _pool_windows(width):
    gw = width // len(POOL_WINDOWS)
    lane = lax.broadcasted_iota(I32, (1, width), 1)
    win = jnp.zeros((1, width), I32)
    for g, w in enumerate(POOL_WINDOWS):
        win = jnp.where((lane >= g * gw) & (lane < (g + 1) * gw), w, win)
    return win


def _pool_prompt_body(u_ref, w_ref, sc_ref, o_ref, ext_scr, *, tm, j):
    width = u_ref.shape[2]
    pad = 2 * (POOL_STATE + 1)
    start = pl.multiple_of(j * tm, tm)
    lead = POOL_STATE + 1
    prev = u_ref[0, pl.ds(pl.multiple_of(jnp.maximum(start - lead, 0), lead), lead), :]
    ext_scr[0:pad - lead, :] = jnp.zeros((pad - lead, width), F32)
    ext_scr[pad - lead:pad, :] = jnp.where(j > 0, prev, 0.0)
    ext_scr[pad:, :] = u_ref[0, pl.ds(start, tm), :]
    win = _pool_windows(width)
    tok = ext_scr[pad:, :]
    acc = tok
    for i in range(1, max(POOL_WINDOWS)):
        acc = acc + jnp.where(i < win, ext_scr[pl.ds(pad - i, tm), :], 0.0)
    pos = start + lax.broadcasted_iota(I32, (tm, width), 0)
    cnt = jnp.minimum(win, pos + 1).astype(F32)
    diff = (acc / cnt - tok).astype(BF16)
    o_ref[0] = jnp.dot(diff, w_ref[...], preferred_element_type=F32) * sc_ref[...]


def _pool_sample_body(st_ref, u_ref, w_ref, sc_ref, o_ref, ns_ref):
    t_new, _, width = u_ref.shape
    n_state = st_ref.shape[0]
    win = _pool_windows(width)

    def row(k):
        return st_ref[k] if k < n_state else u_ref[k - n_state]

    for t in range(t_new):
        tok = u_ref[t]
        acc = tok
        for i in range(1, max(POOL_WINDOWS)):
            acc = acc + jnp.where(i < win, row(n_state + t - i), 0.0)
        diff = (acc / win.astype(F32) - tok).astype(BF16)
        o_ref[t] = jnp.dot(diff, w_ref[...], preferred_element_type=F32) * sc_ref[...]
    for k in range(n_state):
        ns_ref[k] = row(k + t_new)


def _pool_sample(state_t, u, w_bd, scale):
    n_state = state_t.shape[0]
    assert n_state >= POOL_STATE
    return pl.pallas_call(
        _pool_sample_body,
        out_shape=[jax.ShapeDtypeStruct(u.shape, F32), jax.ShapeDtypeStruct(state_t.shape, F32)],
        compiler_params=pltpu.CompilerParams(vmem_limit_bytes=VMEM_LIMIT), name="pool_sample",
    )(state_t, u, w_bd, scale)


ROUTE_F1, ROUTE_F2, ROUTE_G1, ROUTE_G2, ROUTE_R1, ROUTE_R2 = range(6)
ROUTE_ROWS = 8


def _split_bf16(x):
    hi = x.astype(BF16)
    return hi, (x - hi.astype(F32)).astype(BF16)


def _mix_route_body(x_ref, pool_ref, attn_ref, wo_ref, g_ref, wr_ref, br_ref, cnt_in_ref, h_ref, route_ref,
                    route_t_ref, cnt_out_ref, carry_scr, *, n_groups, n_experts):
    i = pl.program_id(0)
    tm = x_ref.shape[0]
    pool_w = pool_ref.shape[1]

    @pl.when(i == 0)
    def _():
        carry_scr[...] = cnt_in_ref[...]

    h = x_ref[...]
    h = h + jnp.dot(pool_ref[...].astype(BF16), wo_ref[0:pool_w, :], preferred_element_type=F32)
    h = h + jnp.dot(attn_ref[...].astype(BF16), wo_ref[pool_w:, :], preferred_element_type=F32)
    h_ref[...] = h

    hn_hi, hn_lo = _split_bf16(_rms(h, g_ref[...]))
    w_hi, w_lo = _split_bf16(wr_ref[...])
    hi_both = jnp.dot(hn_hi, jnp.concatenate([w_hi, w_lo], axis=1), preferred_element_type=F32)
    logits = (hi_both[:, :LANES] + hi_both[:, LANES:]
              + jnp.dot(hn_lo, w_hi, preferred_element_type=F32)) + br_ref[...]

    lane = lax.broadcasted_iota(I32, (tm, LANES), 1).astype(F32)

    def first_lane(mask):
        return jnp.min(jnp.where(mask, lane, float(LANES)), axis=1, keepdims=True)

    is_g = lane < n_groups
    lg = jnp.where(is_g, logits, NEG_INF)
    mg = jnp.max(lg, axis=1, keepdims=True)
    p_sel = 1.0 / jnp.sum(jnp.exp(lg - mg), axis=1, keepdims=True)
    g_top = first_lane(lg == mg)
    lo = n_groups + g_top * n_experts
    in_grp = (lane >= lo) & (lane < lo + n_experts)
    le = jnp.where(in_grp, logits, NEG_INF)
    ee = jnp.exp(le - jnp.max(le, axis=1, keepdims=True))
    pe = ee / jnp.sum(ee, axis=1, keepdims=True)
    v1 = jnp.max(jnp.where(in_grp, pe, -1.0), axis=1, keepdims=True)
    i1 = first_lane(in_grp & (pe == v1))
    rest = in_grp & (lane != i1)
    v2 = jnp.max(jnp.where(rest, pe, -1.0), axis=1, keepdims=True)
    i2 = first_lane(rest & (pe == v2))
    gate1 = p_sel * (v1 / (v1 + v2))
    gate2 = p_sel * (v2 / (v1 + v2))

    sel1, sel2 = lane == i1, lane == i2
    onehot = (sel1 | sel2).astype(BF16)
    tri = (lax.broadcasted_iota(I32, (tm, tm), 1) < lax.broadcasted_iota(I32, (tm, tm), 0)).astype(BF16)
    running = jnp.dot(tri, onehot, preferred_element_type=F32) + carry_scr[...]
    rank1 = jnp.sum(jnp.where(sel1, running, 0.0), axis=1, keepdims=True)
    rank2 = jnp.sum(jnp.where(sel2, running, 0.0), axis=1, keepdims=True)
    carry_scr[...] = carry_scr[...] + jnp.sum(onehot.astype(F32), axis=0, keepdims=True)
    cnt_out_ref[...] = carry_scr[...]

    rec = jnp.zeros((tm, LANES), F32)
    for idx, val in ((ROUTE_F1, i1 - n_groups), (ROUTE_F2, i2 - n_groups),
                     (ROUTE_G1, gate1), (ROUTE_G2, gate2), (ROUTE_R1, rank1), (ROUTE_R2, rank2)):
        rec = jnp.where(lane == idx, val, rec)
    route_ref[...] = rec
    route_t_ref[...] = rec.T[:ROUTE_ROWS]


def _mix_route(x, pool, attn, w_out_bf16, g_ffn, w_route, b_route, cnt_in, *, n_groups, n_experts):
    n, d = x.shape
    tm = TM_MIX
    row = lambda i: (i, 0)
    fix = lambda i: (0, 0)
    return pl.pallas_call(
        functools.partial(_mix_route_body, n_groups=n_groups, n_experts=n_experts),
        grid=(n // tm,),
        in_specs=[pl.BlockSpec((tm, d), row), pl.BlockSpec((tm, pool.shape[1]), row),
                  pl.BlockSpec((tm, attn.shape[1]), row), pl.BlockSpec(w_out_bf16.shape, fix),
                  pl.BlockSpec((1, d), fix), pl.BlockSpec(w_route.shape, fix), pl.BlockSpec((1, LANES), fix),
                  pl.BlockSpec((1, LANES), fix)],
        out_specs=[pl.BlockSpec((tm, d), row), pl.BlockSpec((tm, LANES), row),
                   pl.BlockSpec((ROUTE_ROWS, tm), lambda i: (0, i)), pl.BlockSpec((1, LANES), fix)],
        out_shape=[jax.ShapeDtypeStruct((n, d), F32), jax.ShapeDtypeStruct((n, LANES), F32),
                   jax.ShapeDtypeStruct((ROUTE_ROWS, n), F32), jax.ShapeDtypeStruct((1, LANES), F32)],
        scratch_shapes=[pltpu.VMEM((1, LANES), F32)],
        compiler_params=_params("arbitrary"), name="mix_route",
    )(x, pool, attn, w_out_bf16, g_ffn, w_route, b_route, cnt_in)


def _dispatch_body(zlo_ref, zhi_ref, dest_ref, hp_ref, hs_ref, g_ref, xs_ref, xn_scr, zero_scr, sems, *, tiles_p,
                   n_steps):
    tm = hp_ref.shape[0]
    i = pl.program_id(0)
    zero_sem = 2 * TOP_K_INNER

    @pl.when(i == 0)
    def _():
        zero_scr[...] = jnp.zeros(zero_scr.shape, F32)
        for start in (True, False):
            def chunk(c, carry, start=start):
                cp = pltpu.make_async_copy(zero_scr, xs_ref.at[pl.ds(pl.multiple_of(c * ZERO_ROWS, ZERO_ROWS), ZERO_ROWS)],
                                           sems.at[zero_sem])
                cp.start() if start else cp.wait()
                return carry

            def segment(e, carry, chunk=chunk):
                return lax.fori_loop(zlo_ref[e], zhi_ref[e], chunk, carry)
            lax.fori_loop(0, zlo_ref.shape[0], segment, 0)

    buf = i % 2

    def wait_rows(b):
        for slot in range(TOP_K_INNER):
            pltpu.make_async_copy(xn_scr.at[b], xs_ref.at[pl.ds(0, tm)], sems.at[b * TOP_K_INNER + slot]).wait()

    @pl.when(i >= 2)
    def _():
        wait_rows(buf)

    @pl.when(i < tiles_p)
    def _():
        xn_scr[buf] = _rms(hp_ref[...], g_ref[...])

    @pl.when(i >= tiles_p)
    def _():
        xn_scr[buf] = _rms(hs_ref[...], g_ref[...])

    for r in range(tm):
        for slot in range(TOP_K_INNER):
            d = dest_ref[0, 0, slot * tm + r]
            pltpu.make_async_copy(xn_scr.at[buf, pl.ds(r, 1)], xs_ref.at[pl.ds(d, 1)],
                                  sems.at[buf * TOP_K_INNER + slot]).start(priority=slot)

    @pl.when(i == n_steps - 1)
    def _():
        if n_steps > 1:
            wait_rows(1 - buf)
        wait_rows(buf)


def _dispatch(zero_lo, zero_hi, dest, h_p, h_s, g_ffn, *, rows):
    tm = TM_ROW
    d = h_p.shape[1]
    tiles_p, tiles_s = h_p.shape[0] // tm, h_s.shape[0] // tm
    return pl.pallas_call(
        functools.partial(_dispatch_body, tiles_p=tiles_p, n_steps=tiles_p + tiles_s),
        grid_spec=pltpu.PrefetchScalarGridSpec(
            num_scalar_prefetch=2, grid=(tiles_p + tiles_s,),
            in_specs=[pl.BlockSpec((1, 1, TOP_K_INNER * tm), lambda i, lo, hi: (i, 0, 0), memory_space=pltpu.SMEM),
                      pl.BlockSpec((tm, d), lambda i, lo, hi: (jnp.minimum(i, tiles_p - 1), 0)),
                      pl.BlockSpec((tm, d), lambda i, lo, hi: (jnp.maximum(i - tiles_p, 0), 0)),
                      pl.BlockSpec((1, d), lambda i, lo, hi: (0, 0))],
            out_specs=pl.BlockSpec(memory_space=pl.ANY),
            scratch_shapes=[pltpu.VMEM((2, tm, d), F32), pltpu.VMEM((ZERO_ROWS, d), F32),
                            pltpu.SemaphoreType.DMA((2 * TOP_K_INNER + 1,))]),
        out_shape=jax.ShapeDtypeStruct((rows, d), F32),
        compiler_params=_params("arbitrary"), name="dispatch",
    )(zero_lo, zero_hi, dest, h_p, h_s, g_ffn)


GEMM_X_BUFS = 3
GEMM_Y_BUFS = 2


def _moe_gemm_body(expert_ref, valid_ref, xs_hbm, wg_ref, wu_ref, wd_ref, ys_hbm, x_buf, y_buf, wg_scr, wu_scr, wd_scr,
                   x_sems, y_sems, *, n_tiles):
    t = pl.program_id(0)
    tm = x_buf.shape[1]
    ahead = GEMM_X_BUFS - 1

    def x_copy(s):
        return pltpu.make_async_copy(xs_hbm.at[pl.ds(pl.multiple_of(s * tm, tm), tm)], x_buf.at[s % GEMM_X_BUFS],
                                     x_sems.at[s % GEMM_X_BUFS])

    def y_copy(s):
        return pltpu.make_async_copy(y_buf.at[s % GEMM_Y_BUFS], ys_hbm.at[pl.ds(pl.multiple_of(s * tm, tm), tm)],
                                     y_sems.at[s % GEMM_Y_BUFS])

    def tile_is_valid(s):
        return (s < n_tiles) & (valid_ref[jnp.minimum(s, n_tiles - 1)] > 0)

    @pl.when(t == 0)
    def _():
        for s in range(ahead):
            @pl.when(tile_is_valid(s))
            def _():
                x_copy(s).start()

    @pl.when(tile_is_valid(t + ahead))
    def _():
        x_copy(t + ahead).start()

    @pl.when((t == 0) | (expert_ref[t] != expert_ref[jnp.maximum(t - 1, 0)]))
    def _():
        wg_scr[...] = wg_ref[0].astype(BF16)
        wu_scr[...] = wu_ref[0].astype(BF16)
        wd_scr[...] = wd_ref[0].astype(BF16)

    @pl.when(t >= GEMM_Y_BUFS)
    def _():
        y_copy(t - GEMM_Y_BUFS).wait()

    ybuf = y_buf.at[t % GEMM_Y_BUFS]

    @pl.when(valid_ref[t] > 0)
    def _():
        x_copy(t).wait()
        x = x_buf[t % GEMM_X_BUFS].astype(BF16)
        gate = jnp.dot(x, wg_scr[...], preferred_element_type=F32)
        up = jnp.dot(x, wu_scr[...], preferred_element_type=F32)
        mid = (gate * jax.nn.sigmoid(gate) * up).astype(BF16)
        ybuf[...] = jnp.dot(mid, wd_scr[...], preferred_element_type=F32)

    @pl.when(valid_ref[t] == 0)
    def _():
        ybuf[...] = jnp.zeros(ybuf.shape, F32)

    y_copy(t).start()

    @pl.when(t == n_tiles - 1)
    def _():
        for back in range(min(GEMM_Y_BUFS, n_tiles) - 1, -1, -1):
            y_copy(t - back).wait()


def _moe_gemm(tile_expert, tile_valid, xs, w_gate, w_up, w_down):
    rows, d = xs.shape
    _, _, f = w_gate.shape
    tm = TM_GEMM
    n_tiles = rows // tm
    hbm = pl.BlockSpec(memory_space=pl.ANY)
    return pl.pallas_call(
        functools.partial(_moe_gemm_body, n_tiles=n_tiles),
        grid_spec=pltpu.PrefetchScalarGridSpec(
            num_scalar_prefetch=2, grid=(n_tiles,),
            in_specs=[hbm,
                      pl.BlockSpec((1, d, f), lambda t, e, v: (e[t], 0, 0)),
                      pl.BlockSpec((1, d, f), lambda t, e, v: (e[t], 0, 0)),
                      pl.BlockSpec((1, f, d), lambda t, e, v: (e[t], 0, 0))],
            out_specs=hbm,
            scratch_shapes=[pltpu.VMEM((GEMM_X_BUFS, tm, d), F32), pltpu.VMEM((GEMM_Y_BUFS, tm, d), F32),
                            pltpu.VMEM((d, f), BF16), pltpu.VMEM((d, f), BF16), pltpu.VMEM((f, d), BF16),
                            pltpu.SemaphoreType.DMA((GEMM_X_BUFS,)), pltpu.SemaphoreType.DMA((GEMM_Y_BUFS,))]),
        out_shape=jax.ShapeDtypeStruct((rows, d), F32),
        compiler_params=_params("arbitrary"), name="moe_gemm",
    )(tile_expert, tile_valid, xs, w_gate, w_up, w_down)


def _combine_body(dest_ref, next_ref, h_ref, route_ref, g_ref, ys_ref, y_ref, rows_scr, sems, *, n_steps):
    tm = h_ref.shape[0]
    i = pl.program_id(0)
    buf = i % 2

    def fetch(idx_ref, b):
        for r in range(tm):
            for slot in range(TOP_K_INNER):
                d = idx_ref[0, 0, slot * tm + r]
                pltpu.make_async_copy(ys_ref.at[pl.ds(d, 1)], rows_scr.at[b, slot, pl.ds(r, 1)],
                                      sems.at[b * TOP_K_INNER + slot]).start(priority=slot)

    @pl.when(i == 0)
    def _():
        fetch(dest_ref, 0)

    @pl.when(i + 1 < n_steps)
    def _():
        fetch(next_ref, 1 - buf)

    for slot in range(TOP_K_INNER):
        pltpu.make_async_copy(ys_ref.at[pl.ds(0, tm)], rows_scr.at[buf, slot], sems.at[buf * TOP_K_INNER + slot]).wait()
    route = route_ref[...]
    out = h_ref[...] + (route[:, ROUTE_G1:ROUTE_G1 + 1] * rows_scr[buf, 0]
                        + route[:, ROUTE_G2:ROUTE_G2 + 1] * rows_scr[buf, 1])
    y_ref[...] = _rms(out, g_ref[...])


def _combine(dest, h, route, g_final, ys):
    n, d = h.shape
    tm = TM_ROW
    n_steps = n // tm
    row = lambda i: (i, 0)
    idx_block = (1, 1, TOP_K_INNER * tm)
    return pl.pallas_call(
        functools.partial(_combine_body, n_steps=n_steps),
        grid=(n_steps,),
        in_specs=[pl.BlockSpec(idx_block, lambda i: (i, 0, 0), memory_space=pltpu.SMEM),
                  pl.BlockSpec(idx_block, lambda i: (jnp.minimum(i + 1, n_steps - 1), 0, 0), memory_space=pltpu.SMEM),
                  pl.BlockSpec((tm, d), row), pl.BlockSpec((tm, LANES), row), pl.BlockSpec((1, d), lambda i: (0, 0)),
                  pl.BlockSpec(memory_space=pl.ANY)],
        out_specs=pl.BlockSpec((tm, d), row),
        out_shape=jax.ShapeDtypeStruct((n, d), F32),
        scratch_shapes=[pltpu.VMEM((2, TOP_K_INNER, tm, d), F32), pltpu.SemaphoreType.DMA((2 * TOP_K_INNER,))],
        compiler_params=_params("arbitrary"), name="combine",
    )(dest, dest, h, route, g_final, ys)


def _sort_tables(counts, n_tiles):
    padded = ((counts + TM_GEMM - 1) // TM_GEMM) * TM_GEMM
    ends = jnp.cumsum(padded)
    offsets = ends - padded
    total = ends[-1]
    tile_start = jnp.arange(n_tiles, dtype=I32) * TM_GEMM
    tile_valid = (tile_start < total).astype(I32)
    last_block = jnp.maximum(total // TM_GEMM - 1, 0)
    tile_block = jnp.minimum(jnp.arange(n_tiles, dtype=I32), last_block)
    n_flat = counts.shape[0]
    tile_expert = jnp.minimum(jnp.sum((tile_block[:, None] * TM_GEMM >= ends[None, :]).astype(I32), axis=1), n_flat - 1)
    zero_lo = jnp.concatenate([(offsets + counts) // ZERO_ROWS, total[None] // ZERO_ROWS]).astype(I32)
    zero_hi = jnp.concatenate([ends // ZERO_ROWS, jnp.full((1,), n_tiles * TM_GEMM // ZERO_ROWS, I32)]).astype(I32)
    return offsets, tile_expert.astype(I32), tile_valid, zero_lo, zero_hi


def _dest_blocks(route_t, offsets):
    n = route_t.shape[1]
    experts = jnp.arange(offsets.shape[0], dtype=I32)[:, None]

    def dest(f_row, r_row):
        f = route_t[f_row].astype(I32)
        base = jnp.sum(jnp.where(f[None, :] == experts, offsets[:, None], 0), axis=0)
        return (base + route_t[r_row].astype(I32)).reshape(n // TM_ROW, TM_ROW)

    return jnp.concatenate([dest(ROUTE_F1, ROUTE_R1), dest(ROUTE_F2, ROUTE_R2)], axis=1)[:, None, :]


def kernel(x_prompt, x_sample, cache_k, cache_v, state_pool, g_mix, w_in, w_pool, pool_scale, w_out, g_ffn,
           w_router_group, b_router_group, w_router_expert, b_router_expert, w_gate, w_up, w_down, g_final):
    depth = g_mix.shape[0]
    assert depth == 1, "single-layer step"
    b, s, d = x_prompt.shape
    nd, t_new, _ = x_sample.shape
    n_heads = cache_k.shape[3]
    attn_w = n_heads * HEAD_DIM
    pool_w = state_pool.shape[3]
    past = cache_k.shape[2]
    keep = min(ATTN_REACH, s)
    n_groups, n_experts = w_router_expert.shape[1], w_router_expert.shape[3]
    n_flat = n_groups * n_experts
    assert n_groups + n_flat <= LANES
    slopes = _alibi_slopes(n_heads)

    w_in_b = w_in[0].astype(BF16)
    w_out_b = w_out[0].astype(BF16)
    gw = pool_w // len(POOL_WINDOWS)
    w_bd = jnp.zeros((pool_w, pool_w), F32)
    for g in range(len(POOL_WINDOWS)):
        w_bd = w_bd.at[g * gw:(g + 1) * gw, g * gw:(g + 1) * gw].set(w_pool[0, g])
    w_bd = w_bd.astype(BF16)
    w_route = jnp.concatenate([w_router_group[0], jnp.transpose(w_router_expert[0], (1, 0, 2)).reshape(d, n_flat)], axis=1)
    w_route = jnp.pad(w_route, ((0, 0), (0, LANES - n_groups - n_flat)))
    b_route = jnp.pad(jnp.concatenate([b_router_group[0], b_router_expert[0].reshape(n_flat)]),
                      (0, LANES - n_groups - n_flat))[None]
    w_gate_f, w_up_f = w_gate[0].reshape(n_flat, d, -1), w_up[0].reshape(n_flat, d, -1)
    w_down_f = w_down[0].reshape(n_flat, -1, d)

    n_p = b * s
    u_p, q_p, k_p, v_p, kt_p, vt_p = _proj_in(x_prompt.reshape(n_p, d), g_mix, w_in_b, pool_w=pool_w, attn_w=attn_w,
                                               seq=s, keep=keep)
    attn_p = _attn_prompt(slopes, q_p.reshape(b, s, attn_w), k_p.reshape(b, s, attn_w), v_p.reshape(b, s, attn_w))
    u_p3 = u_p.reshape(b, s, pool_w)

    n_s = nd * t_new
    u_s, q_s, k_s, v_s, ktn_s, vtn_s = _proj_in(x_sample.reshape(n_s, d), g_mix, w_in_b, pool_w=pool_w, attn_w=attn_w,
                                                t_new=t_new)
    cache_kt = jnp.transpose(cache_k[0], (0, 2, 3, 1))
    cache_vt = jnp.transpose(cache_v[0], (0, 2, 3, 1))
    as3 = lambda a: a.reshape(nd, t_new, attn_w)
    attn_s, pool_p = _attn_sample_pool(slopes, as3(q_s), as3(k_s), as3(v_s), cache_kt, cache_vt, u_p3, w_bd, pool_scale)
    state_t = jnp.transpose(state_pool[0], (1, 0, 2))
    u_st = jnp.transpose(u_s.reshape(nd, t_new, pool_w), (1, 0, 2))
    pool_st, new_state_t = _pool_sample(state_t, u_st, w_bd, pool_scale)
    pool_s = jnp.transpose(pool_st, (1, 0, 2)).reshape(n_s, pool_w)

    route_kw = dict(n_groups=n_groups, n_experts=n_experts)
    h_p, route_p, route_pt, cnt_p = _mix_route(x_prompt.reshape(n_p, d), pool_p.reshape(n_p, pool_w),
                                               attn_p.reshape(n_p, attn_w), w_out_b, g_ffn, w_route, b_route,
                                               jnp.zeros((1, LANES), F32), **route_kw)
    h_s, route_s, route_st, cnt_all = _mix_route(x_sample.reshape(n_s, d), pool_s, attn_s.reshape(n_s, attn_w),
                                                 w_out_b, g_ffn, w_route, b_route, cnt_p, **route_kw)

    counts = cnt_all[0, n_groups:n_groups + n_flat].astype(I32)
    n_tiles = (TOP_K_INNER * (n_p + n_s) + n_flat * (TM_GEMM - 1)) // TM_GEMM
    offsets, tile_expert, tile_valid, zero_lo, zero_hi = _sort_tables(counts, n_tiles)
    dest_p, dest_s = _dest_blocks(route_pt, offsets), _dest_blocks(route_st, offsets)

    xs = _dispatch(zero_lo, zero_hi, jnp.concatenate([dest_p, dest_s], axis=0), h_p, h_s, g_ffn,
                   rows=n_tiles * TM_GEMM)
    ys = _moe_gemm(tile_expert, tile_valid, xs, w_gate_f, w_up_f, w_down_f)
    y_p = _combine(dest_p, h_p, route_p, g_final[None], ys)
    y_s = _combine(dest_s, h_s, route_s, g_final[None], ys)

    y_prompt = y_p.reshape(b, s, d)
    y_sample = y_s.reshape(nd, t_new, d)
    k_prompt = jnp.transpose(kt_p.reshape(b, n_heads, HEAD_DIM, keep), (0, 3, 1, 2))[None]
    v_prompt = jnp.transpose(vt_p.reshape(b, n_heads, HEAD_DIM, keep), (0, 3, 1, 2))[None]
    pool_prompt = u_p3[:, s - POOL_STATE:][None]
    k_sample = jnp.transpose(ktn_s.reshape(t_new, n_heads, HEAD_DIM, nd), (3, 0, 1, 2))[None]
    v_sample = jnp.transpose(vtn_s.reshape(t_new, n_heads, HEAD_DIM, nd), (3, 0, 1, 2))[None]
    pool_sample = jnp.transpose(new_state_t[-POOL_STATE:], (1, 0, 2))[None]
    return (y_prompt, y_sample, k_prompt, v_prompt, pool_prompt, k_sample, v_sample, pool_sample)
```

```python
import functools
import math

import numpy as np
import jax
import jax.numpy as jnp
from jax import lax
from jax.experimental import pallas as pl
from jax.experimental.pallas import tpu as pltpu

F32 = jnp.float32
BF16 = jnp.bfloat16
I32 = jnp.int32

HEAD_DIM = 64
POOL_WINDOWS = (2, 4, 8, 16)
POOL_STATE = max(POOL_WINDOWS) - 1
BRANCHES = ((128, 1), (512, 4), (2048, 16))
STEPS = BRANCHES[0][0] // BRANCHES[0][1]
ATTN_REACH = max(w for w, _ in BRANCHES)
MAX_DIL = max(d for _, d in BRANCHES)
TOP_K_INNER = 2
RMS_EPS = 1e-6
LANES = 128
NEG_INF = float("-inf")

VMEM_LIMIT = 56 * 1024 * 1024

TM_PROJ = 1024
TM_MIX = 512
TM_ROW = 256
TM_GEMM = 512
ZERO_ROWS = 64
ATTN_GROUP = 8


def _alibi_slopes(n_heads):
    def geometric(n):
        start = 2.0 ** (-8.0 / n)
        return [start ** (i + 1) for i in range(n)]
    c = 2 ** int(math.floor(math.log2(n_heads)))
    s = geometric(c)
    if c < n_heads:
        s = s + geometric(2 * c)[0::2][: n_heads - c]
    return jnp.asarray(s, dtype=F32)


def _rms(x, g):
    return x * lax.rsqrt(jnp.mean(x * x, axis=-1, keepdims=True) + RMS_EPS) * g


def _params(*sem):
    return pltpu.CompilerParams(dimension_semantics=sem, vmem_limit_bytes=VMEM_LIMIT)


def _proj_in_body(x_ref, g_ref, w_ref, u_ref, q_ref, k_ref, v_ref, *t_refs, pool_w, attn_w, tiles_per_seq,
                  keep_tiles, t_new):
    xn = _rms(x_ref[...], g_ref[...]).astype(BF16)

    def proj(lo, n):
        return jnp.dot(xn, w_ref[:, lo:lo + n], preferred_element_type=F32)

    u_ref[...] = proj(0, pool_w)
    q_ref[...] = proj(pool_w, attn_w) * (HEAD_DIM ** -0.5)
    k = proj(pool_w + attn_w, attn_w)
    v = proj(pool_w + 2 * attn_w, attn_w)
    k_ref[...] = k
    v_ref[...] = v
    if t_new is None:
        kt_ref, vt_ref = t_refs
        j = pl.program_id(0) % tiles_per_seq

        @pl.when(j >= tiles_per_seq - keep_tiles)
        def _():
            kt_ref[0] = k.T
            vt_ref[0] = v.T
    else:
        ktn_ref, vtn_ref, cols_scr = t_refs
        n_seq = ktn_ref.shape[2]
        for val, dst in ((k, ktn_ref), (v, vtn_ref)):
            for c in range(attn_w // LANES):
                lanes = slice(c * LANES, (c + 1) * LANES)
                cols_scr[...] = val[:, lanes]
                for t in range(t_new):
                    dst[t, lanes, :] = cols_scr[pl.ds(t, n_seq, stride=t_new), :].T


def _proj_in(x, g, w_bf16, *, pool_w, attn_w, seq=None, keep=None, t_new=None):
    n, d = x.shape
    tm = TM_PROJ if t_new is None else n
    grid = (n // tm,)
    row = lambda i: (i, 0)
    out_shape = [jax.ShapeDtypeStruct((n, pool_w), F32)] + [jax.ShapeDtypeStruct((n, attn_w), F32)] * 3
    out_specs = [pl.BlockSpec((tm, pool_w), row)] + [pl.BlockSpec((tm, attn_w), row)] * 3
    tiles_per_seq = keep_tiles = 0
    scratch = []
    if t_new is None:
        tiles_per_seq, keep_tiles = seq // tm, keep // tm
        first = tiles_per_seq - keep_tiles
        tmap = lambda i: (i // tiles_per_seq, 0, jnp.maximum(i % tiles_per_seq - first, 0))
        out_shape += [jax.ShapeDtypeStruct((n // seq, attn_w, keep), F32)] * 2
        out_specs += [pl.BlockSpec((1, attn_w, tm), tmap)] * 2
    else:
        assert n // t_new == LANES, "one square transpose per (step, column block)"
        out_shape += [jax.ShapeDtypeStruct((t_new, attn_w, n // t_new), F32)] * 2
        out_specs += [pl.BlockSpec((t_new, attn_w, n // t_new), lambda i: (0, 0, 0))] * 2
        scratch = [pltpu.VMEM((n, LANES), F32)]
    body = functools.partial(_proj_in_body, pool_w=pool_w, attn_w=attn_w, tiles_per_seq=tiles_per_seq,
                             keep_tiles=keep_tiles, t_new=t_new)
    return pl.pallas_call(
        body, grid=grid,
        in_specs=[pl.BlockSpec((tm, d), row), pl.BlockSpec((1, d), lambda i: (0, 0)),
                  pl.BlockSpec(w_bf16.shape, lambda i: (0, 0))],
        out_specs=out_specs, out_shape=out_shape, scratch_shapes=scratch,
        compiler_params=_params("arbitrary"), name="proj_in",
    )(x, g, w_bf16)


def _attn_prompt_body(slopes_ref, q_hbm, k_hbm, v_hbm, o_ref, qkv_scr, bias_scr, s_scr, p_scr, o_scr, m_scr, l_scr, sems,
                      *, seq, n_pairs, n_steps):
    hp = pl.program_id(1)
    gstep = pl.program_id(0) * n_pairs + hp
    buf = gstep % 2
    blk = STEPS
    n_it = seq // blk
    cls = seq // MAX_DIL

    def class_copies(st, b):
        lanes = pl.ds(pl.multiple_of((st % n_pairs) * LANES, LANES), LANES)
        return [pltpu.make_async_copy(src.at[st // n_pairs, :, c, lanes], qkv_scr.at[b, a, pl.ds(c * cls, cls), :],
                                      sems.at[b, a])
                for a, src in enumerate((q_hbm, k_hbm, v_hbm)) for c in range(MAX_DIL)]

    @pl.when(gstep == 0)
    def _():
        for cp in class_copies(gstep, 0):
            cp.start()

    @pl.when(gstep + 1 < n_steps)
    def _():
        for cp in class_copies(gstep + 1, 1 - buf):
            cp.start()

    for cp in class_copies(gstep, buf):
        cp.wait()
    qp_scr, kp_scr, vp_scr = (qkv_scr.at[buf, a] for a in range(3))

    lane = lax.broadcasted_iota(I32, (blk, LANES), 1)
    head_a = lane < HEAD_DIM
    row = lax.broadcasted_iota(I32, (2 * blk, 2 * blk), 0)
    col = lax.broadcasted_iota(I32, (2 * blk, 2 * blk), 1)
    slope = jnp.where(row < blk, slopes_ref[2 * hp], slopes_ref[2 * hp + 1])

    for bi, (_, d) in enumerate(BRANCHES):
        nb = n_it // d
        n_chunk = MAX_DIL // d
        a_rows = blk // n_chunk
        sh = a_rows.bit_length() - 1
        assert a_rows == 1 << sh and a_rows % 8 == 0

        def seq_index(i, n_chunk=n_chunk, a_rows=a_rows, sh=sh):
            i = i & (blk - 1)
            return (i & (a_rows - 1)) * n_chunk + (i >> sh)

        step = seq_index(row) + blk - (seq_index(col) + (col & blk))
        bias = jnp.where((step >= 0) & (step <= STEPS), -slope * (d * step).astype(F32), NEG_INF)
        bias_scr[0] = bias
        bias_scr[1] = jnp.where(col < blk, NEG_INF, bias)

        def group(j, carry, bi=bi, d=d, nb=nb, n_chunk=n_chunk, a_rows=a_rows):
            def chunks(g):
                it = j * ATTN_GROUP + g
                r = it // nb
                n = it % nb
                cur = [pl.ds(pl.multiple_of((r + d * c) * cls + n * a_rows, 8), a_rows) for c in range(n_chunk)]
                prev = [pl.ds(pl.multiple_of((r + d * c) * cls + jnp.maximum(n - 1, 0) * a_rows, 8), a_rows)
                        for c in range(n_chunk)]
                return n, cur, prev

            def gather(ref, sls):
                return jnp.concatenate([ref[sl, :] for sl in sls], axis=0)

            for g in range(ATTN_GROUP):
                n, cur, prev = chunks(g)
                qb = gather(qp_scr, cur)
                q2 = jnp.concatenate([jnp.where(head_a, qb, 0.0), jnp.where(head_a, 0.0, qb)], axis=0).astype(BF16)
                kc = gather(kp_scr, prev + cur).astype(BF16)
                s = lax.dot_general(q2, kc, (((1,), (1,)), ((), ())), preferred_element_type=F32)
                s_scr[g] = s + bias_scr[jnp.where(n == 0, 1, 0)]
            for g in range(ATTN_GROUP):
                _, cur, _ = chunks(g)
                s = s_scr[g]
                m = jnp.max(s, axis=1, keepdims=True)
                p_scr[g] = jnp.exp(s - m).astype(BF16)
                m2 = jnp.where(head_a, m[:blk], m[blk:])
                for c, sl in enumerate(cur):
                    m_scr[bi, sl, :] = m2[c * a_rows:(c + 1) * a_rows]
            ones = jnp.ones((2 * blk, LANES), BF16)
            for g in range(ATTN_GROUP):
                _, cur, prev = chunks(g)
                vc = jnp.concatenate([gather(vp_scr, prev + cur).astype(BF16), ones], axis=1)
                ol = jnp.dot(p_scr[g], vc, preferred_element_type=F32)
                o2 = jnp.where(head_a, ol[:blk, :LANES], ol[blk:, :LANES])
                l2 = jnp.where(head_a, ol[:blk, LANES:], ol[blk:, LANES:])
                for c, sl in enumerate(cur):
                    o_scr[bi, sl, :] = o2[c * a_rows:(c + 1) * a_rows]
                    l_scr[bi, sl, :] = l2[c * a_rows:(c + 1) * a_rows]
            return carry

        lax.fori_loop(0, n_it // ATTN_GROUP, group, 0)

    def merge(c, carry):
        rs = pl.ds(pl.multiple_of(c * cls, cls), cls)
        ms = [m_scr[b, rs, :] for b in range(len(BRANCHES))]
        mx = functools.reduce(jnp.maximum, ms)
        num = jnp.zeros((cls, LANES), F32)
        den = jnp.zeros((cls, LANES), F32)
        for b in range(len(BRANCHES)):
            a = jnp.exp(ms[b] - mx)
            num = num + a * o_scr[b, rs, :]
            den = den + a * l_scr[b, rs, :]
        o_ref[0, pl.ds(c, cls, stride=MAX_DIL), :] = num / den
        return carry

    lax.fori_loop(0, MAX_DIL, merge, 0)


def _attn_prompt(slopes, q, k, v):
    b, s, hw = q.shape
    assert s % (STEPS * max(d for _, d in BRANCHES)) == 0, "sequence must be a multiple of the widest span"
    assert hw % LANES == 0 and LANES == 2 * HEAD_DIM
    n_pairs = hw // LANES
    nbr = len(BRANCHES)
    by_class = lambda a: a.reshape(b, s // MAX_DIL, MAX_DIL, hw)
    hbm = pl.BlockSpec(memory_space=pl.ANY)
    return pl.pallas_call(
        functools.partial(_attn_prompt_body, seq=s, n_pairs=n_pairs, n_steps=b * n_pairs),
        grid=(b, n_pairs),
        in_specs=[pl.BlockSpec(memory_space=pltpu.SMEM), hbm, hbm, hbm],
        out_specs=pl.BlockSpec((1, s, LANES), lambda i, j: (i, 0, j)),
        out_shape=jax.ShapeDtypeStruct((b, s, hw), F32),
        scratch_shapes=[pltpu.VMEM((2, 3, s, LANES), F32),
                        pltpu.VMEM((2, 2 * STEPS, 2 * STEPS), F32),
                        pltpu.VMEM((ATTN_GROUP, 2 * STEPS, 2 * STEPS), F32),
                        pltpu.VMEM((ATTN_GROUP, 2 * STEPS, 2 * STEPS), BF16)]
        + [pltpu.VMEM((nbr, s, LANES), F32)] * 3 + [pltpu.SemaphoreType.DMA((2, 3))],
        compiler_params=_params("arbitrary", "arbitrary"), name="attn_prompt",
    )(slopes, by_class(q), by_class(k), by_class(v))


def _multiplicity(dist):
    mult = jnp.zeros(dist.shape, F32)
    for w, d in BRANCHES:
        assert d & (d - 1) == 0
        mult = mult + ((dist >= 0) & ((dist & (d - 1)) == 0) & (dist <= w)).astype(F32)
    return mult


def _attn_sample_body(slopes_ref, q_ref, kn_ref, vn_ref, kt_ref, vt_ref, o_ref, *, n_heads):
    t_new = q_ref.shape[1]
    past = kt_ref.shape[3]
    dist = past + lax.broadcasted_iota(I32, (t_new, past), 0) - lax.broadcasted_iota(I32, (t_new, past), 1)
    dist_n = lax.broadcasted_iota(I32, (t_new, t_new), 0) - lax.broadcasted_iota(I32, (t_new, t_new), 1)
    mult, mult_n = _multiplicity(dist), _multiplicity(dist_n)
    dist_f, dist_nf = dist.astype(F32), dist_n.astype(F32)
    nt = (((1,), (1,)), ((), ()))
    for h in range(n_heads):
        slope = slopes_ref[h]
        cols = slice(h * HEAD_DIM, (h + 1) * HEAD_DIM)
        qh = q_ref[0, :, cols].astype(BF16)
        kn = kn_ref[0, :, cols].astype(BF16)
        vn = vn_ref[0, :, cols].astype(BF16)
        s = jnp.dot(qh, kt_ref[0, h].astype(BF16), preferred_element_type=F32)
        s = jnp.where(mult > 0, s - slope * dist_f, NEG_INF)
        sn = lax.dot_general(qh, kn, nt, preferred_element_type=F32)
        sn = jnp.where(mult_n > 0, sn - slope * dist_nf, NEG_INF)
        m = jnp.maximum(jnp.max(s, axis=1, keepdims=True), jnp.max(sn, axis=1, keepdims=True))
        p = mult * jnp.exp(s - m)
        pn = mult_n * jnp.exp(sn - m)
        l = jnp.sum(p, axis=1, keepdims=True) + jnp.sum(pn, axis=1, keepdims=True)
        o = lax.dot_general(p.astype(BF16), vt_ref[0, h].astype(BF16), nt, preferred_element_type=F32)
        o = o + jnp.dot(pn.astype(BF16), vn, preferred_element_type=F32)
        o_ref[0, :, cols] = o / l


def _attn_sample_pool_body(slopes_ref, q_ref, kn_ref, vn_ref, kt_ref, vt_ref, u_ref, wbd_ref, sc_ref, o_ref, pool_ref,
                           ext_scr, *, n_heads, tm_pool, tiles_per_seq):
    _attn_sample_body(slopes_ref, q_ref, kn_ref, vn_ref, kt_ref, vt_ref, o_ref, n_heads=n_heads)
    _pool_prompt_body(u_ref, wbd_ref, sc_ref, pool_ref, ext_scr, tm=tm_pool, j=pl.program_id(0) % tiles_per_seq)


def _attn_sample_pool(slopes, q, k_new, v_new, cache_kt, cache_vt, u_prompt, w_bd, scale):
    n, t, hw = q.shape
    _, h, e, past = cache_kt.shape
    b, s, w = u_prompt.shape
    assert past >= ATTN_REACH, "every strided key of every branch must exist in the window buffer"
    tm_pool = b * s // n
    assert tm_pool * n == b * s and s % tm_pool == 0 and tm_pool % (POOL_STATE + 1) == 0
    tiles_per_seq = s // tm_pool
    new = pl.BlockSpec((1, t, hw), lambda i: (i, 0, 0))
    old = pl.BlockSpec((1, h, e, past), lambda i: (i, 0, 0, 0))
    fix = lambda i: (0, 0)
    return pl.pallas_call(
        functools.partial(_attn_sample_pool_body, n_heads=h, tm_pool=tm_pool, tiles_per_seq=tiles_per_seq),
        grid=(n,),
        in_specs=[pl.BlockSpec(memory_space=pltpu.SMEM), new, new, new, old, old,
                  pl.BlockSpec((1, s, w), lambda i: (i // tiles_per_seq, 0, 0)), pl.BlockSpec((w, w), fix),
                  pl.BlockSpec((1, w), fix)],
        out_specs=[new, pl.BlockSpec((1, tm_pool, w), lambda i: (i // tiles_per_seq, i % tiles_per_seq, 0))],
        out_shape=[jax.ShapeDtypeStruct((n, t, hw), F32), jax.ShapeDtypeStruct((b, s, w), F32)],
        scratch_shapes=[pltpu.VMEM((tm_pool + 2 * (POOL_STATE + 1), w), F32)],
        compiler_params=_params("arbitrary"), name="attn_sample",
    )(slopes, q, k_new, v_new, cache_kt, cache_vt, u_prompt, w_bd, scale)


def _pool_windows(width):
    gw = width // len(POOL_WINDOWS)
    lane = lax.broadcasted_iota(I32, (1, width), 1)
    win = jnp.zeros((1, width), I32)
    for g, w in enumerate(POOL_WINDOWS):
        win = jnp.where((lane >= g * gw) & (lane < (g + 1) * gw), w, win)
    return win


def _pool_prompt_body(u_ref, w_ref, sc_ref, o_ref, ext_scr, *, tm, j):
    width = u_ref.shape[2]
    pad = 2 * (POOL_STATE + 1)
    start = pl.multiple_of(j * tm, tm)
    lead = POOL_STATE + 1
    prev = u_ref[0, pl.ds(pl.multiple_of(jnp.maximum(start - lead, 0), lead), lead), :]
    ext_scr[0:pad - lead, :] = jnp.zeros((pad - lead, width), F32)
    ext_scr[pad - lead:pad, :] = jnp.where(j > 0, prev, 0.0)
    ext_scr[pad:, :] = u_ref[0, pl.ds(start, tm), :]
    win = _pool_windows(width)
    tok = ext_scr[pad:, :]
    acc = tok
    for i in range(1, max(POOL_WINDOWS)):
        acc = acc + jnp.where(i < win, ext_scr[pl.ds(pad - i, tm), :], 0.0)
    pos = start + lax.broadcasted_iota(I32, (tm, width), 0)
    cnt = jnp.minimum(win, pos + 1).astype(F32)
    diff = (acc / cnt - tok).astype(BF16)
    o_ref[0] = jnp.dot(diff, w_ref[...], preferred_element_type=F32) * sc_ref[...]


def _pool_sample_body(st_ref, u_ref, w_ref, sc_ref, o_ref, ns_ref):
    t_new, _, width = u_ref.shape
    n_state = st_ref.shape[0]
    win = _pool_windows(width)

    def row(k):
        return st_ref[k] if k < n_state else u_ref[k - n_state]

    for t in range(t_new):
        tok = u_ref[t]
        acc = tok
        for i in range(1, max(POOL_WINDOWS)):
            acc = acc + jnp.where(i < win, row(n_state + t - i), 0.0)
        diff = (acc / win.astype(F32) - tok).astype(BF16)
        o_ref[t] = jnp.dot(diff, w_ref[...], preferred_element_type=F32) * sc_ref[...]
    for k in range(n_state):
        ns_ref[k] = row(k + t_new)


def _pool_sample(state_t, u, w_bd, scale):
    n_state = state_t.shape[0]
    assert n_state >= POOL_STATE
    return pl.pallas_call(
        _pool_sample_body,
        out_shape=[jax.ShapeDtypeStruct(u.shape, F32), jax.ShapeDtypeStruct(state_t.shape, F32)],
        compiler_params=pltpu.CompilerParams(vmem_limit_bytes=VMEM_LIMIT), name="pool_sample",
    )(state_t, u, w_bd, scale)


ROUTE_F1, ROUTE_F2, ROUTE_G1, ROUTE_G2, ROUTE_R1, ROUTE_R2 = range(6)
ROUTE_ROWS = 8


def _split_bf16(x):
    hi = x.astype(BF16)
    return hi, (x - hi.astype(F32)).astype(BF16)


def _mix_route_body(x_ref, pool_ref, attn_ref, wo_ref, g_ref, wr_ref, br_ref, cnt_in_ref, h_ref, route_ref,
                    route_t_ref, cnt_out_ref, carry_scr, *, n_groups, n_experts):
    i = pl.program_id(0)
    tm = x_ref.shape[0]
    pool_w = pool_ref.shape[1]

    @pl.when(i == 0)
    def _():
        carry_scr[...] = cnt_in_ref[...]

    h = x_ref[...]
    h = h + jnp.dot(pool_ref[...].astype(BF16), wo_ref[0:pool_w, :], preferred_element_type=F32)
    h = h + jnp.dot(attn_ref[...].astype(BF16), wo_ref[pool_w:, :], preferred_element_type=F32)
    h_ref[...] = h

    hn_hi, hn_lo = _split_bf16(_rms(h, g_ref[...]))
    w_hi, w_lo = _split_bf16(wr_ref[...])
    hi_both = jnp.dot(hn_hi, jnp.concatenate([w_hi, w_lo], axis=1), preferred_element_type=F32)
    logits = (hi_both[:, :LANES] + hi_both[:, LANES:]
              + jnp.dot(hn_lo, w_hi, preferred_element_type=F32)) + br_ref[...]

    lane = lax.broadcasted_iota(I32, (tm, LANES), 1).astype(F32)

    def first_lane(mask):
        return jnp.min(jnp.where(mask, lane, float(LANES)), axis=1, keepdims=True)

    is_g = lane < n_groups
    lg = jnp.where(is_g, logits, NEG_INF)
    mg = jnp.max(lg, axis=1, keepdims=True)
    p_sel = 1.0 / jnp.sum(jnp.exp(lg - mg), axis=1, keepdims=True)
    g_top = first_lane(lg == mg)
    lo = n_groups + g_top * n_experts
    in_grp = (lane >= lo) & (lane < lo + n_experts)
    le = jnp.where(in_grp, logits, NEG_INF)
    ee = jnp.exp(le - jnp.max(le, axis=1, keepdims=True))
    pe = ee / jnp.sum(ee, axis=1, keepdims=True)
    v1 = jnp.max(jnp.where(in_grp, pe, -1.0), axis=1, keepdims=True)
    i1 = first_lane(in_grp & (pe == v1))
    rest = in_grp & (lane != i1)
    v2 = jnp.max(jnp.where(rest, pe, -1.0), axis=1, keepdims=True)
    i2 = first_lane(rest & (pe == v2))
    gate1 = p_sel * (v1 / (v1 + v2))
    gate2 = p_sel * (v2 / (v1 + v2))

    sel1, sel2 = lane == i1, lane == i2
    onehot = (sel1 | sel2).astype(BF16)
    tri = (lax.broadcasted_iota(I32, (tm, tm), 1) < lax.broadcasted_iota(I32, (tm, tm), 0)).astype(BF16)
    running = jnp.dot(tri, onehot, preferred_element_type=F32) + carry_scr[...]
    rank1 = jnp.sum(jnp.where(sel1, running, 0.0), axis=1, keepdims=True)
    rank2 = jnp.sum(jnp.where(sel2, running, 0.0), axis=1, keepdims=True)
    carry_scr[...] = carry_scr[...] + jnp.sum(onehot.astype(F32), axis=0, keepdims=True)
    cnt_out_ref[...] = carry_scr[...]

    rec = jnp.zeros((tm, LANES), F32)
    for idx, val in ((ROUTE_F1, i1 - n_groups), (ROUTE_F2, i2 - n_groups),
                     (ROUTE_G1, gate1), (ROUTE_G2, gate2), (ROUTE_R1, rank1), (ROUTE_R2, rank2)):
        rec = jnp.where(lane == idx, val, rec)
    route_ref[...] = rec
    route_t_ref[...] = rec.T[:ROUTE_ROWS]


def _mix_route(x, pool, attn, w_out_bf16, g_ffn, w_route, b_route, cnt_in, *, n_groups, n_experts):
    n, d = x.shape
    tm = TM_MIX
    row = lambda i: (i, 0)
    fix = lambda i: (0, 0)
    return pl.pallas_call(
        functools.partial(_mix_route_body, n_groups=n_groups, n_experts=n_experts),
        grid=(n // tm,),
        in_specs=[pl.BlockSpec((tm, d), row), pl.BlockSpec((tm, pool.shape[1]), row),
                  pl.BlockSpec((tm, attn.shape[1]), row), pl.BlockSpec(w_out_bf16.shape, fix),
                  pl.BlockSpec((1, d), fix), pl.BlockSpec(w_route.shape, fix), pl.BlockSpec((1, LANES), fix),
                  pl.BlockSpec((1, LANES), fix)],
        out_specs=[pl.BlockSpec((tm, d), row), pl.BlockSpec((tm, LANES), row),
                   pl.BlockSpec((ROUTE_ROWS, tm), lambda i: (0, i)), pl.BlockSpec((1, LANES), fix)],
        out_shape=[jax.ShapeDtypeStruct((n, d), F32), jax.ShapeDtypeStruct((n, LANES), F32),
                   jax.ShapeDtypeStruct((ROUTE_ROWS, n), F32), jax.ShapeDtypeStruct((1, LANES), F32)],
        scratch_shapes=[pltpu.VMEM((1, LANES), F32)],
        compiler_params=_params("arbitrary"), name="mix_route",
    )(x, pool, attn, w_out_bf16, g_ffn, w_route, b_route, cnt_in)


def _dispatch_body(zlo_ref, zhi_ref, dest_ref, hp_ref, hs_ref, g_ref, xs_ref, xn_scr, zero_scr, sems, *, tiles_p,
                   n_steps):
    tm = hp_ref.shape[0]
    i = pl.program_id(0)
    zero_sem = 2 * TOP_K_INNER

    @pl.when(i == 0)
    def _():
        zero_scr[...] = jnp.zeros(zero_scr.shape, F32)
        for start in (True, False):
            def chunk(c, carry, start=start):
                cp = pltpu.make_async_copy(zero_scr, xs_ref.at[pl.ds(pl.multiple_of(c * ZERO_ROWS, ZERO_ROWS), ZERO_ROWS)],
                                           sems.at[zero_sem])
                cp.start() if start else cp.wait()
                return carry

            def segment(e, carry, chunk=chunk):
                return lax.fori_loop(zlo_ref[e], zhi_ref[e], chunk, carry)
            lax.fori_loop(0, zlo_ref.shape[0], segment, 0)

    buf = i % 2

    def wait_rows(b):
        for slot in range(TOP_K_INNER):
            pltpu.make_async_copy(xn_scr.at[b], xs_ref.at[pl.ds(0, tm)], sems.at[b * TOP_K_INNER + slot]).wait()

    @pl.when(i >= 2)
    def _():
        wait_rows(buf)

    @pl.when(i < tiles_p)
    def _():
        xn_scr[buf] = _rms(hp_ref[...], g_ref[...])

    @pl.when(i >= tiles_p)
    def _():
        xn_scr[buf] = _rms(hs_ref[...], g_ref[...])

    for r in range(tm):
        for slot in range(TOP_K_INNER):
            d = dest_ref[0, 0, slot * tm + r]
            pltpu.make_async_copy(xn_scr.at[buf, pl.ds(r, 1)], xs_ref.at[pl.ds(d, 1)],
                                  sems.at[buf * TOP_K_INNER + slot]).start(priority=slot)

    @pl.when(i == n_steps - 1)
    def _():
        if n_steps > 1:
            wait_rows(1 - buf)
        wait_rows(buf)


def _dispatch(zero_lo, zero_hi, dest, h_p, h_s, g_ffn, *, rows):
    tm = TM_ROW
    d = h_p.shape[1]
    tiles_p, tiles_s = h_p.shape[0] // tm, h_s.shape[0] // tm
    return pl.pallas_call(
        functools.partial(_dispatch_body, tiles_p=tiles_p, n_steps=tiles_p + tiles_s),
        grid_spec=pltpu.PrefetchScalarGridSpec(
            num_scalar_prefetch=2, grid=(tiles_p + tiles_s,),
            in_specs=[pl.BlockSpec((1, 1, TOP_K_INNER * tm), lambda i, lo, hi: (i, 0, 0), memory_space=pltpu.SMEM),
                      pl.BlockSpec((tm, d), lambda i, lo, hi: (jnp.minimum(i, tiles_p - 1), 0)),
                      pl.BlockSpec((tm, d), lambda i, lo, hi: (jnp.maximum(i - tiles_p, 0), 0)),
                      pl.BlockSpec((1, d), lambda i, lo, hi: (0, 0))],
            out_specs=pl.BlockSpec(memory_space=pl.ANY),
            scratch_shapes=[pltpu.VMEM((2, tm, d), F32), pltpu.VMEM((ZERO_ROWS, d), F32),
                            pltpu.SemaphoreType.DMA((2 * TOP_K_INNER + 1,))]),
        out_shape=jax.ShapeDtypeStruct((rows, d), F32),
        compiler_params=_params("arbitrary"), name="dispatch",
    )(zero_lo, zero_hi, dest, h_p, h_s, g_ffn)


GEMM_X_BUFS = 3
GEMM_Y_BUFS = 2


def _moe_gemm_body(expert_ref, valid_ref, xs_hbm, wg_ref, wu_ref, wd_ref, ys_hbm, x_buf, y_buf, wg_scr, wu_scr, wd_scr,
                   x_sems, y_sems, *, n_tiles):
    t = pl.program_id(0)
    tm = x_buf.shape[1]
    ahead = GEMM_X_BUFS - 1

    def x_copy(s):
        return pltpu.make_async_copy(xs_hbm.at[pl.ds(pl.multiple_of(s * tm, tm), tm)], x_buf.at[s % GEMM_X_BUFS],
                                     x_sems.at[s % GEMM_X_BUFS])

    def y_copy(s):
        return pltpu.make_async_copy(y_buf.at[s % GEMM_Y_BUFS], ys_hbm.at[pl.ds(pl.multiple_of(s * tm, tm), tm)],
                                     y_sems.at[s % GEMM_Y_BUFS])

    def tile_is_valid(s):
        return (s < n_tiles) & (valid_ref[jnp.minimum(s, n_tiles - 1)] > 0)

    @pl.when(t == 0)
    def _():
        for s in range(ahead):
            @pl.when(tile_is_valid(s))
            def _():
                x_copy(s).start()

    @pl.when(tile_is_valid(t + ahead))
    def _():
        x_copy(t + ahead).start()

    @pl.when((t == 0) | (expert_ref[t] != expert_ref[jnp.maximum(t - 1, 0)]))
    def _():
        wg_scr[...] = wg_ref[0].astype(BF16)
        wu_scr[...] = wu_ref[0].astype(BF16)
        wd_scr[...] = wd_ref[0].astype(BF16)

    @pl.when(t >= GEMM_Y_BUFS)
    def _():
        y_copy(t - GEMM_Y_BUFS).wait()

    ybuf = y_buf.at[t % GEMM_Y_BUFS]

    @pl.when(valid_ref[t] > 0)
    def _():
        x_copy(t).wait()
        x = x_buf[t % GEMM_X_BUFS].astype(BF16)
        gate = jnp.dot(x, wg_scr[...], preferred_element_type=F32)
        up = jnp.dot(x, wu_scr[...], preferred_element_type=F32)
        mid = (gate * jax.nn.sigmoid(gate) * up).astype(BF16)
        ybuf[...] = jnp.dot(mid, wd_scr[...], preferred_element_type=F32)

    @pl.when(valid_ref[t] == 0)
    def _():
        ybuf[...] = jnp.zeros(ybuf.shape, F32)

    y_copy(t).start()

    @pl.when(t == n_tiles - 1)
    def _():
        for back in range(min(GEMM_Y_BUFS, n_tiles) - 1, -1, -1):
            y_copy(t - back).wait()


def _moe_gemm(tile_expert, tile_valid, xs, w_gate, w_up, w_down):
    rows, d = xs.shape
    _, _, f = w_gate.shape
    tm = TM_GEMM
    n_tiles = rows // tm
    hbm = pl.BlockSpec(memory_space=pl.ANY)
    return pl.pallas_call(
        functools.partial(_moe_gemm_body, n_tiles=n_tiles),
        grid_spec=pltpu.PrefetchScalarGridSpec(
            num_scalar_prefetch=2, grid=(n_tiles,),
            in_specs=[hbm,
                      pl.BlockSpec((1, d, f), lambda t, e, v: (e[t], 0, 0)),
                      pl.BlockSpec((1, d, f), lambda t, e, v: (e[t], 0, 0)),
                      pl.BlockSpec((1, f, d), lambda t, e, v: (e[t], 0, 0))],
            out_specs=hbm,
            scratch_shapes=[pltpu.VMEM((GEMM_X_BUFS, tm, d), F32), pltpu.VMEM((GEMM_Y_BUFS, tm, d), F32),
                            pltpu.VMEM((d, f), BF16), pltpu.VMEM((d, f), BF16), pltpu.VMEM((f, d), BF16),
                            pltpu.SemaphoreType.DMA((GEMM_X_BUFS,)), pltpu.SemaphoreType.DMA((GEMM_Y_BUFS,))]),
        out_shape=jax.ShapeDtypeStruct((rows, d), F32),
        compiler_params=_params("arbitrary"), name="moe_gemm",
    )(tile_expert, tile_valid, xs, w_gate, w_up, w_down)


def _combine_body(dest_ref, next_ref, h_ref, route_ref, g_ref, ys_ref, y_ref, rows_scr, sems, *, n_steps):
    tm = h_ref.shape[0]
    i = pl.program_id(0)
    buf = i % 2

    def fetch(idx_ref, b):
        for r in range(tm):
            for slot in range(TOP_K_INNER):
                d = idx_ref[0, 0, slot * tm + r]
                pltpu.make_async_copy(ys_ref.at[pl.ds(d, 1)], rows_scr.at[b, slot, pl.ds(r, 1)],
                                      sems.at[b * TOP_K_INNER + slot]).start(priority=slot)

    @pl.when(i == 0)
    def _():
        fetch(dest_ref, 0)

    @pl.when(i + 1 < n_steps)
    def _():
        fetch(next_ref, 1 - buf)

    for slot in range(TOP_K_INNER):
        pltpu.make_async_copy(ys_ref.at[pl.ds(0, tm)], rows_scr.at[buf, slot], sems.at[buf * TOP_K_INNER + slot]).wait()
    route = route_ref[...]
    out = h_ref[...] + (route[:, ROUTE_G1:ROUTE_G1 + 1] * rows_scr[buf, 0]
                        + route[:, ROUTE_G2:ROUTE_G2 + 1] * rows_scr[buf, 1])
    y_ref[...] = _rms(out, g_ref[...])


def _combine(dest, h, route, g_final, ys):
    n, d = h.shape
    tm = TM_ROW
    n_steps = n // tm
    row = lambda i: (i, 0)
    idx_block = (1, 1, TOP_K_INNER * tm)
    return pl.pallas_call(
        functools.partial(_combine_body, n_steps=n_steps),
        grid=(n_steps,),
        in_specs=[pl.BlockSpec(idx_block, lambda i: (i, 0, 0), memory_space=pltpu.SMEM),
                  pl.BlockSpec(idx_block, lambda i: (jnp.minimum(i + 1, n_steps - 1), 0, 0), memory_space=pltpu.SMEM),
                  pl.BlockSpec((tm, d), row), pl.BlockSpec((tm, LANES), row), pl.BlockSpec((1, d), lambda i: (0, 0)),
                  pl.BlockSpec(memory_space=pl.ANY)],
        out_specs=pl.BlockSpec((tm, d), row),
        out_shape=jax.ShapeDtypeStruct((n, d), F32),
        scratch_shapes=[pltpu.VMEM((2, TOP_K_INNER, tm, d), F32), pltpu.SemaphoreType.DMA((2 * TOP_K_INNER,))],
        compiler_params=_params("arbitrary"), name="combine",
    )(dest, dest, h, route, g_final, ys)


def _sort_tables(counts, n_tiles):
    padded = ((counts + TM_GEMM - 1) // TM_GEMM) * TM_GEMM
    ends = jnp.cumsum(padded)
    offsets = ends - padded
    total = ends[-1]
    tile_start = jnp.arange(n_tiles, dtype=I32) * TM_GEMM
    tile_valid = (tile_start < total).astype(I32)
    last_block = jnp.maximum(total // TM_GEMM - 1, 0)
    tile_block = jnp.minimum(jnp.arange(n_tiles, dtype=I32), last_block)
    n_flat = counts.shape[0]
    tile_expert = jnp.minimum(jnp.sum((tile_block[:, None] * TM_GEMM >= ends[None, :]).astype(I32), axis=1), n_flat - 1)
    zero_lo = jnp.concatenate([(offsets + counts) // ZERO_ROWS, total[None] // ZERO_ROWS]).astype(I32)
    zero_hi = jnp.concatenate([ends // ZERO_ROWS, jnp.full((1,), n_tiles * TM_GEMM // ZERO_ROWS, I32)]).astype(I32)
    return offsets, tile_expert.astype(I32), tile_valid, zero_lo, zero_hi


def _dest_blocks(route_t, offsets):
    n = route_t.shape[1]
    experts = jnp.arange(offsets.shape[0], dtype=I32)[:, None]

    def dest(f_row, r_row):
        f = route_t[f_row].astype(I32)
        base = jnp.sum(jnp.where(f[None, :] == experts, offsets[:, None], 0), axis=0)
        return (base + route_t[r_row].astype(I32)).reshape(n // TM_ROW, TM_ROW)

    return jnp.concatenate([dest(ROUTE_F1, ROUTE_R1), dest(ROUTE_F2, ROUTE_R2)], axis=1)[:, None, :]


def kernel(x_prompt, x_sample, cache_k, cache_v, state_pool, g_mix, w_in, w_pool, pool_scale, w_out, g_ffn,
           w_router_group, b_router_group, w_router_expert, b_router_expert, w_gate, w_up, w_down, g_final):
    depth = g_mix.shape[0]
    assert depth == 1, "single-layer step"
    b, s, d = x_prompt.shape
    nd, t_new, _ = x_sample.shape
    n_heads = cache_k.shape[3]
    attn_w = n_heads * HEAD_DIM
    pool_w = state_pool.shape[3]
    past = cache_k.shape[2]
    keep = min(ATTN_REACH, s)
    n_groups, n_experts = w_router_expert.shape[1], w_router_expert.shape[3]
    n_flat = n_groups * n_experts
    assert n_groups + n_flat <= LANES
    slopes = _alibi_slopes(n_heads)

    w_in_b = w_in[0].astype(BF16)
    w_out_b = w_out[0].astype(BF16)
    gw = pool_w // len(POOL_WINDOWS)
    w_bd = jnp.zeros((pool_w, pool_w), F32)
    for g in range(len(POOL_WINDOWS)):
        w_bd = w_bd.at[g * gw:(g + 1) * gw, g * gw:(g + 1) * gw].set(w_pool[0, g])
    w_bd = w_bd.astype(BF16)
    w_route = jnp.concatenate([w_router_group[0], jnp.transpose(w_router_expert[0], (1, 0, 2)).reshape(d, n_flat)], axis=1)
    w_route = jnp.pad(w_route, ((0, 0), (0, LANES - n_groups - n_flat)))
    b_route = jnp.pad(jnp.concatenate([b_router_group[0], b_router_expert[0].reshape(n_flat)]),
                      (0, LANES - n_groups - n_flat))[None]
    w_gate_f, w_up_f = w_gate[0].reshape(n_flat, d, -1), w_up[0].reshape(n_flat, d, -1)
    w_down_f = w_down[0].reshape(n_flat, -1, d)

    n_p = b * s
    u_p, q_p, k_p, v_p, kt_p, vt_p = _proj_in(x_prompt.reshape(n_p, d), g_mix, w_in_b, pool_w=pool_w, attn_w=attn_w,
                                               seq=s, keep=keep)
    attn_p = _attn_prompt(slopes, q_p.reshape(b, s, attn_w), k_p.reshape(b, s, attn_w), v_p.reshape(b, s, attn_w))
    u_p3 = u_p.reshape(b, s, pool_w)

    n_s = nd * t_new
    u_s, q_s, k_s, v_s, ktn_s, vtn_s = _proj_in(x_sample.reshape(n_s, d), g_mix, w_in_b, pool_w=pool_w, attn_w=attn_w,
                                                t_new=t_new)
    cache_kt = jnp.transpose(cache_k[0], (0, 2, 3, 1))
    cache_vt = jnp.transpose(cache_v[0], (0, 2, 3, 1))
    as3 = lambda a: a.reshape(nd, t_new, attn_w)
    attn_s, pool_p = _attn_sample_pool(slopes, as3(q_s), as3(k_s), as3(v_s), cache_kt, cache_vt, u_p3, w_bd, pool_scale)
    state_t = jnp.transpose(state_pool[0], (1, 0, 2))
    u_st = jnp.transpose(u_s.reshape(nd, t_new, pool_w), (1, 0, 2))
    pool_st, new_state_t = _pool_sample(state_t, u_st, w_bd, pool_scale)
    pool_s = jnp.transpose(pool_st, (1, 0, 2)).reshape(n_s, pool_w)

    route_kw = dict(n_groups=n_groups, n_experts=n_experts)
    h_p, route_p, route_pt, cnt_p = _mix_route(x_prompt.reshape(n_p, d), pool_p.reshape(n_p, pool_w),
                                               attn_p.reshape(n_p, attn_w), w_out_b, g_ffn, w_route, b_route,
                                               jnp.zeros((1, LANES), F32), **route_kw)
    h_s, route_s, route_st, cnt_all = _mix_route(x_sample.reshape(n_s, d), pool_s, attn_s.reshape(n_s, attn_w),
                                                 w_out_b, g_ffn, w_route, b_route, cnt_p, **route_kw)

    counts = cnt_all[0, n_groups:n_groups + n_flat].astype(I32)
    n_tiles = (TOP_K_INNER * (n_p + n_s) + n_flat * (TM_GEMM - 1)) // TM_GEMM
    offsets, tile_expert, tile_valid, zero_lo, zero_hi = _sort_tables(counts, n_tiles)
    dest_p, dest_s = _dest_blocks(route_pt, offsets), _dest_blocks(route_st, offsets)

    xs = _dispatch(zero_lo, zero_hi, jnp.concatenate([dest_p, dest_s], axis=0), h_p, h_s, g_ffn,
                   rows=n_tiles * TM_GEMM)
    ys = _moe_gemm(tile_expert, tile_valid, xs, w_gate_f, w_up_f, w_down_f)
    y_p = _combine(dest_p, h_p, route_p, g_final[None], ys)
    y_s = _combine(dest_s, h_s, route_s, g_final[None], ys)

    y_prompt = y_p.reshape(b, s, d)
    y_sample = y_s.reshape(nd, t_new, d)
    k_prompt = jnp.transpose(kt_p.reshape(b, n_heads, HEAD_DIM, keep), (0, 3, 1, 2))[None]
    v_prompt = jnp.transpose(vt_p.reshape(b, n_heads, HEAD_DIM, keep), (0, 3, 1, 2))[None]
    pool_prompt = u_p3[:, s - POOL_STATE:][None]
    k_sample = jnp.transpose(ktn_s.reshape(t_new, n_heads, HEAD_DIM, nd), (3, 0, 1, 2))[None]
    v_sample = jnp.transpose(vtn_s.reshape(t_new, n_heads, HEAD_DIM, nd), (3, 0, 1, 2))[None]
    pool_sample = jnp.transpose(new_state_t[-POOL_STATE:], (1, 0, 2))[None]
    return (y_prompt, y_sample, k_prompt, v_prompt, pool_prompt, k_sample, v_sample, pool_sample)
```

```python
import functools
import math

import numpy as np
import jax
import jax.numpy as jnp
from jax import lax
from jax.experimental import pallas as pl
from jax.experimental.pallas import tpu as pltpu

F32 = jnp.float32
BF16 = jnp.bfloat16
I32 = jnp.int32

HEAD_DIM = 64
POOL_WINDOWS = (2, 4, 8, 16)
POOL_STATE = max(POOL_WINDOWS) - 1
BRANCHES = ((128, 1), (512, 4), (2048, 16))
STEPS = BRANCHES[0][0] // BRANCHES[0][1]
ATTN_REACH = max(w for w, _ in BRANCHES)
MAX_DIL = max(d for _, d in BRANCHES)
TOP_K_INNER = 2
RMS_EPS = 1e-6
LANES = 128
NEG_INF = float("-inf")

VMEM_LIMIT = 56 * 1024 * 1024

TM_PROJ = 1024
TM_MIX = 512
TM_ROW = 512
TM_GEMM = 512
ZERO_ROWS = 64
ATTN_GROUP = 16


def _alibi_slopes(n_heads):
    def geometric(n):
        start = 2.0 ** (-8.0 / n)
        return [start ** (i + 1) for i in range(n)]
    c = 2 ** int(math.floor(math.log2(n_heads)))
    s = geometric(c)
    if c < n_heads:
        s = s + geometric(2 * c)[0::2][: n_heads - c]
    return jnp.asarray(s, dtype=F32)


def _rms(x, g):
    return x * lax.rsqrt(jnp.mean(x * x, axis=-1, keepdims=True) + RMS_EPS) * g


def _params(*sem):
    return pltpu.CompilerParams(dimension_semantics=sem, vmem_limit_bytes=VMEM_LIMIT)


def _proj_in_body(x_ref, g_ref, w_ref, u_ref, q_ref, k_ref, v_ref, *t_refs, pool_w, attn_w, tiles_per_seq,
                  keep_tiles, t_new):
    xn = _rms(x_ref[...], g_ref[...]).astype(BF16)

    def proj(lo, n):
        return jnp.dot(xn, w_ref[:, lo:lo + n], preferred_element_type=F32)

    u_ref[...] = proj(0, pool_w)
    q_ref[...] = proj(pool_w, attn_w) * (HEAD_DIM ** -0.5)
    k = proj(pool_w + attn_w, attn_w)
    v = proj(pool_w + 2 * attn_w, attn_w)
    k_ref[...] = k
    v_ref[...] = v
    if t_new is None:
        kt_ref, vt_ref = t_refs
        j = pl.program_id(0) % tiles_per_seq

        @pl.when(j >= tiles_per_seq - keep_tiles)
        def _():
            kt_ref[0] = k.T
            vt_ref[0] = v.T
    else:
        ktn_ref, vtn_ref, cols_scr = t_refs
        n_seq = ktn_ref.shape[2]
        for val, dst in ((k, ktn_ref), (v, vtn_ref)):
            for c in range(attn_w // LANES):
                lanes = slice(c * LANES, (c + 1) * LANES)
                cols_scr[...] = val[:, lanes]
                for t in range(t_new):
                    dst[t, lanes, :] = cols_scr[pl.ds(t, n_seq, stride=t_new), :].T


def _proj_in(x, g, w_bf16, *, pool_w, attn_w, seq=None, keep=None, t_new=None):
    n, d = x.shape
    tm = TM_PROJ if t_new is None else n
    grid = (n // tm,)
    row = lambda i: (i, 0)
    out_shape = [jax.ShapeDtypeStruct((n, pool_w), F32)] + [jax.ShapeDtypeStruct((n, attn_w), F32)] * 3
    out_specs = [pl.BlockSpec((tm, pool_w), row)] + [pl.BlockSpec((tm, attn_w), row)] * 3
    tiles_per_seq = keep_tiles = 0
    scratch = []
    if t_new is None:
        tiles_per_seq, keep_tiles = seq // tm, keep // tm
        first = tiles_per_seq - keep_tiles
        tmap = lambda i: (i // tiles_per_seq, 0, jnp.maximum(i % tiles_per_seq - first, 0))
        out_shape += [jax.ShapeDtypeStruct((n // seq, attn_w, keep), F32)] * 2
        out_specs += [pl.BlockSpec((1, attn_w, tm), tmap)] * 2
    else:
        assert n // t_new == LANES, "one square transpose per (step, column block)"
        out_shape += [jax.ShapeDtypeStruct((t_new, attn_w, n // t_new), F32)] * 2
        out_specs += [pl.BlockSpec((t_new, attn_w, n // t_new), lambda i: (0, 0, 0))] * 2
        scratch = [pltpu.VMEM((n, LANES), F32)]
    body = functools.partial(_proj_in_body, pool_w=pool_w, attn_w=attn_w, tiles_per_seq=tiles_per_seq,
                             keep_tiles=keep_tiles, t_new=t_new)
    return pl.pallas_call(
        body, grid=grid,
        in_specs=[pl.BlockSpec((tm, d), row), pl.BlockSpec((1, d), lambda i: (0, 0)),
                  pl.BlockSpec(w_bf16.shape, lambda i: (0, 0))],
        out_specs=out_specs, out_shape=out_shape, scratch_shapes=scratch,
        compiler_params=_params("arbitrary"), name="proj_in",
    )(x, g, w_bf16)


def _attn_prompt_body(slopes_ref, q_hbm, k_hbm, v_hbm, o_ref, qkv_scr, bias_scr, s_scr, p_scr, o_scr, m_scr, l_scr, sems,
                      *, seq, n_pairs, n_steps):
    hp = pl.program_id(1)
    gstep = pl.program_id(0) * n_pairs + hp
    buf = gstep % 2
    blk = STEPS
    n_it = seq // blk
    cls = seq // MAX_DIL

    def class_copies(st, b):
        lanes = pl.ds(pl.multiple_of((st % n_pairs) * LANES, LANES), LANES)
        return [pltpu.make_async_copy(src.at[st // n_pairs, :, c, lanes], qkv_scr.at[b, a, pl.ds(c * cls, cls), :],
                                      sems.at[b, a])
                for a, src in enumerate((q_hbm, k_hbm, v_hbm)) for c in range(MAX_DIL)]

    @pl.when(gstep == 0)
    def _():
        for cp in class_copies(gstep, 0):
            cp.start()

    @pl.when(gstep + 1 < n_steps)
    def _():
        for cp in class_copies(gstep + 1, 1 - buf):
            cp.start()

    for cp in class_copies(gstep, buf):
        cp.wait()
    qp_scr, kp_scr, vp_scr = (qkv_scr.at[buf, a] for a in range(3))

    lane = lax.broadcasted_iota(I32, (blk, LANES), 1)
    head_a = lane < HEAD_DIM
    row = lax.broadcasted_iota(I32, (2 * blk, 2 * blk), 0)
    col = lax.broadcasted_iota(I32, (2 * blk, 2 * blk), 1)
    slope = jnp.where(row < blk, slopes_ref[2 * hp], slopes_ref[2 * hp + 1])

    for bi, (_, d) in enumerate(BRANCHES):
        nb = n_it // d
        n_chunk = MAX_DIL // d
        a_rows = blk // n_chunk
        sh = a_rows.bit_length() - 1
        assert a_rows == 1 << sh and a_rows % 8 == 0

        def seq_index(i, n_chunk=n_chunk, a_rows=a_rows, sh=sh):
            i = i & (blk - 1)
            return (i & (a_rows - 1)) * n_chunk + (i >> sh)

        step = seq_index(row) + blk - (seq_index(col) + (col & blk))
        bias = jnp.where((step >= 0) & (step <= STEPS), -slope * (d * step).astype(F32), NEG_INF)
        bias_scr[0] = bias
        bias_scr[1] = jnp.where(col < blk, NEG_INF, bias)

        def group(j, carry, bi=bi, d=d, nb=nb, n_chunk=n_chunk, a_rows=a_rows):
            def chunks(g):
                it = j * ATTN_GROUP + g
                r = it // nb
                n = it % nb
                cur = [pl.ds(pl.multiple_of((r + d * c) * cls + n * a_rows, 8), a_rows) for c in range(n_chunk)]
                prev = [pl.ds(pl.multiple_of((r + d * c) * cls + jnp.maximum(n - 1, 0) * a_rows, 8), a_rows)
                        for c in range(n_chunk)]
                return n, cur, prev

            def gather(ref, sls):
                return jnp.concatenate([ref[sl, :] for sl in sls], axis=0)

            for g in range(ATTN_GROUP):
                n, cur, prev = chunks(g)
                qb = gather(qp_scr, cur)
                q2 = jnp.concatenate([jnp.where(head_a, qb, 0.0), jnp.where(head_a, 0.0, qb)], axis=0).astype(BF16)
                kc = gather(kp_scr, prev + cur).astype(BF16)
                s = lax.dot_general(q2, kc, (((1,), (1,)), ((), ())), preferred_element_type=F32)
                s_scr[g] = s + bias_scr[jnp.where(n == 0, 1, 0)]
            for g in range(ATTN_GROUP):
                _, cur, _ = chunks(g)
                s = s_scr[g]
                m = jnp.max(s, axis=1, keepdims=True)
                p_scr[g] = jnp.exp(s - m).astype(BF16)
                m2 = jnp.where(head_a, m[:blk], m[blk:])
                for c, sl in enumerate(cur):
                    m_scr[bi, sl, :] = m2[c * a_rows:(c + 1) * a_rows]
            ones = jnp.ones((2 * blk, LANES), BF16)
            for g in range(ATTN_GROUP):
                _, cur, prev = chunks(g)
                vc = jnp.concatenate([gather(vp_scr, prev + cur).astype(BF16), ones], axis=1)
                ol = jnp.dot(p_scr[g], vc, preferred_element_type=F32)
                o2 = jnp.where(head_a, ol[:blk, :LANES], ol[blk:, :LANES])
                l2 = jnp.where(head_a, ol[:blk, LANES:], ol[blk:, LANES:])
                for c, sl in enumerate(cur):
                    o_scr[bi, sl, :] = o2[c * a_rows:(c + 1) * a_rows]
                    l_scr[bi, sl, :] = l2[c * a_rows:(c + 1) * a_rows]
            return carry

        lax.fori_loop(0, n_it // ATTN_GROUP, group, 0)

    def merge(c, carry):
        rs = pl.ds(pl.multiple_of(c * cls, cls), cls)
        ms = [m_scr[b, rs, :] for b in range(len(BRANCHES))]
        mx = functools.reduce(jnp.maximum, ms)
        num = jnp.zeros((cls, LANES), F32)
        den = jnp.zeros((cls, LANES), F32)
        for b in range(len(BRANCHES)):
            a = jnp.exp(ms[b] - mx)
            num = num + a * o_scr[b, rs, :]
            den = den + a * l_scr[b, rs, :]
        o_ref[0, pl.ds(c, cls, stride=MAX_DIL), :] = num / den
        return carry

    lax.fori_loop(0, MAX_DIL, merge, 0)


def _attn_prompt(slopes, q, k, v):
    b, s, hw = q.shape
    assert s % (STEPS * max(d for _, d in BRANCHES)) == 0, "sequence must be a multiple of the widest span"
    assert hw % LANES == 0 and LANES == 2 * HEAD_DIM
    n_pairs = hw // LANES
    nbr = len(BRANCHES)
    by_class = lambda a: a.reshape(b, s // MAX_DIL, MAX_DIL, hw)
    hbm = pl.BlockSpec(memory_space=pl.ANY)
    return pl.pallas_call(
        functools.partial(_attn_prompt_body, seq=s, n_pairs=n_pairs, n_steps=b * n_pairs),
        grid=(b, n_pairs),
        in_specs=[pl.BlockSpec(memory_space=pltpu.SMEM), hbm, hbm, hbm],
        out_specs=pl.BlockSpec((1, s, LANES), lambda i, j: (i, 0, j)),
        out_shape=jax.ShapeDtypeStruct((b, s, hw), F32),
        scratch_shapes=[pltpu.VMEM((2, 3, s, LANES), F32),
                        pltpu.VMEM((2, 2 * STEPS, 2 * STEPS), F32),
                        pltpu.VMEM((ATTN_GROUP, 2 * STEPS, 2 * STEPS), F32),
                        pltpu.VMEM((ATTN_GROUP, 2 * STEPS, 2 * STEPS), BF16)]
        + [pltpu.VMEM((nbr, s, LANES), F32)] * 3 + [pltpu.SemaphoreType.DMA((2, 3))],
        compiler_params=_params("arbitrary", "arbitrary"), name="attn_prompt",
    )(slopes, by_class(q), by_class(k), by_class(v))


def _multiplicity(dist):
    mult = jnp.zeros(dist.shape, F32)
    for w, d in BRANCHES:
        assert d & (d - 1) == 0
        mult = mult + ((dist >= 0) & ((dist & (d - 1)) == 0) & (dist <= w)).astype(F32)
    return mult


def _attn_sample_body(slopes_ref, q_ref, kn_ref, vn_ref, kt_ref, vt_ref, o_ref, *, n_heads):
    t_new = q_ref.shape[1]
    past = kt_ref.shape[3]
    dist = past + lax.broadcasted_iota(I32, (t_new, past), 0) - lax.broadcasted_iota(I32, (t_new, past), 1)
    dist_n = lax.broadcasted_iota(I32, (t_new, t_new), 0) - lax.broadcasted_iota(I32, (t_new, t_new), 1)
    mult, mult_n = _multiplicity(dist), _multiplicity(dist_n)
    dist_f, dist_nf = dist.astype(F32), dist_n.astype(F32)
    nt = (((1,), (1,)), ((), ()))
    for h in range(n_heads):
        slope = slopes_ref[h]
        cols = slice(h * HEAD_DIM, (h + 1) * HEAD_DIM)
        qh = q_ref[0, :, cols].astype(BF16)
        kn = kn_ref[0, :, cols].astype(BF16)
        vn = vn_ref[0, :, cols].astype(BF16)
        s = jnp.dot(qh, kt_ref[0, h].astype(BF16), preferred_element_type=F32)
        s = jnp.where(mult > 0, s - slope * dist_f, NEG_INF)
        sn = lax.dot_general(qh, kn, nt, preferred_element_type=F32)
        sn = jnp.where(mult_n > 0, sn - slope * dist_nf, NEG_INF)
        m = jnp.maximum(jnp.max(s, axis=1, keepdims=True), jnp.max(sn, axis=1, keepdims=True))
        p = mult * jnp.exp(s - m)
        pn = mult_n * jnp.exp(sn - m)
        l = jnp.sum(p, axis=1, keepdims=True) + jnp.sum(pn, axis=1, keepdims=True)
        o = lax.dot_general(p.astype(BF16), vt_ref[0, h].astype(BF16), nt, preferred_element_type=F32)
        o = o + jnp.dot(pn.astype(BF16), vn, preferred_element_type=F32)
        o_ref[0, :, cols] = o / l


def _attn_sample_pool_body(slopes_ref, q_ref, kn_ref, vn_ref, kt_ref, vt_ref, u_ref, wbd_ref, sc_ref, o_ref, pool_ref,
                           ext_scr, *, n_heads, tm_pool, tiles_per_seq):
    _attn_sample_body(slopes_ref, q_ref, kn_ref, vn_ref, kt_ref, vt_ref, o_ref, n_heads=n_heads)
    _pool_prompt_body(u_ref, wbd_ref, sc_ref, pool_ref, ext_scr, tm=tm_pool, j=pl.program_id(0) % tiles_per_seq)


def _attn_sample_pool(slopes, q, k_new, v_new, cache_kt, cache_vt, u_prompt, w_bd, scale):
    n, t, hw = q.shape
    _, h, e, past = cache_kt.shape
    b, s, w = u_prompt.shape
    assert past >= ATTN_REACH, "every strided key of every branch must exist in the window buffer"
    tm_pool = b * s // n
    assert tm_pool * n == b * s and s % tm_pool == 0 and tm_pool % (POOL_STATE + 1) == 0
    tiles_per_seq = s // tm_pool
    new = pl.BlockSpec((1, t, hw), lambda i: (i, 0, 0))
    old = pl.BlockSpec((1, h, e, past), lambda i: (i, 0, 0, 0))
    fix = lambda i: (0, 0)
    return pl.pallas_call(
        functools.partial(_attn_sample_pool_body, n_heads=h, tm_pool=tm_pool, tiles_per_seq=tiles_per_seq),
        grid=(n,),
        in_specs=[pl.BlockSpec(memory_space=pltpu.SMEM), new, new, new, old, old,
                  pl.BlockSpec((1, s, w), lambda i: (i // tiles_per_seq, 0, 0)), pl.BlockSpec((w, w), fix),
                  pl.BlockSpec((1, w), fix)],
        out_specs=[new, pl.BlockSpec((1, tm_pool, w), lambda i: (i // tiles_per_seq, i % tiles_per_seq, 0))],
        out_shape=[jax.ShapeDtypeStruct((n, t, hw), F32), jax.ShapeDtypeStruct((b, s, w), F32)],
        scratch_shapes=[pltpu.VMEM((tm_pool + 2 * (POOL_STATE + 1), w), F32)],
        compiler_params=_params("arbitrary"), name="attn_sample",
    )(slopes, q, k_new, v_new, cache_kt, cache_vt, u_prompt, w_bd, scale)


def _pool_windows(width):
    gw = width // len(POOL_WINDOWS)
    lane = lax.broadcasted_iota(I32, (1, width), 1)
    win = jnp.zeros((1, width), I32)
    for g, w in enumerate(POOL_WINDOWS):
        win = jnp.where((lane >= g * gw) & (lane < (g + 1) * gw), w, win)
    return win


def _pool_prompt_body(u_ref, w_ref, sc_ref, o_ref, ext_scr, *, tm, j):
    width = u_ref.shape[2]
    pad = 2 * (POOL_STATE + 1)
    start = pl.multiple_of(j * tm, tm)
    lead = POOL_STATE + 1
    prev = u_ref[0, pl.ds(pl.multiple_of(jnp.maximum(start - lead, 0), lead), lead), :]
    ext_scr[0:pad - lead, :] = jnp.zeros((pad - lead, width), F32)
    ext_scr[pad - lead:pad, :] = jnp.where(j > 0, prev, 0.0)
    ext_scr[pad:, :] = u_ref[0, pl.ds(start, tm), :]
    win = _pool_windows(width)
    tok = ext_scr[pad:, :]
    acc = tok
    for i in range(1, max(POOL_WINDOWS)):
        acc = acc + jnp.where(i < win, ext_scr[pl.ds(pad - i, tm), :], 0.0)
    pos = start + lax.broadcasted_iota(I32, (tm, width), 0)
    cnt = jnp.minimum(win, pos + 1).astype(F32)
    diff = (acc / cnt - tok).astype(BF16)
    o_ref[0] = jnp.dot(diff, w_ref[...], preferred_element_type=F32) * sc_ref[...]


def _pool_sample_body(st_ref, u_ref, w_ref, sc_ref, o_ref, ns_ref):
    t_new, _, width = u_ref.shape
    n_state = st_ref.shape[0]
    win = _pool_windows(width)

    def row(k):
        return st_ref[k] if k < n_state else u_ref[k - n_state]

    for t in range(t_new):
        tok = u_ref[t]
        acc = tok
        for i in range(1, max(POOL_WINDOWS)):
            acc = acc + jnp.where(i < win, row(n_state + t - i), 0.0)
        diff = (acc / win.astype(F32) - tok).astype(BF16)
        o_ref[t] = jnp.dot(diff, w_ref[...], preferred_element_type=F32) * sc_ref[...]
    for k in range(n_state):
        ns_ref[k] = row(k + t_new)


def _pool_sample(state_t, u, w_bd, scale):
    n_state = state_t.shape[0]
    assert n_state >= POOL_STATE
    return pl.pallas_call(
        _pool_sample_body,
        out_shape=[jax.ShapeDtypeStruct(u.shape, F32), jax.ShapeDtypeStruct(state_t.shape, F32)],
        compiler_params=pltpu.CompilerParams(vmem_limit_bytes=VMEM_LIMIT), name="pool_sample",
    )(state_t, u, w_bd, scale)


ROUTE_F1, ROUTE_F2, ROUTE_G1, ROUTE_G2, ROUTE_R1, ROUTE_R2 = range(6)
ROUTE_ROWS = 8


def _split_bf16(x):
    hi = x.astype(BF16)
    return hi, (x - hi.astype(F32)).astype(BF16)


def _mix_route_body(x_ref, pool_ref, attn_ref, wo_ref, g_ref, wr_ref, br_ref, cnt_in_ref, h_ref, route_ref,
                    route_t_ref, cnt_out_ref, carry_scr, *, n_groups, n_experts):
    i = pl.program_id(0)
    tm = x_ref.shape[0]
    pool_w = pool_ref.shape[1]

    @pl.when(i == 0)
    def _():
        carry_scr[...] = cnt_in_ref[...]

    h = x_ref[...]
    h = h + jnp.dot(pool_ref[...].astype(BF16), wo_ref[0:pool_w, :], preferred_element_type=F32)
    h = h + jnp.dot(attn_ref[...].astype(BF16), wo_ref[pool_w:, :], preferred_element_type=F32)
    h_ref[...] = h

    hn_hi, hn_lo = _split_bf16(_rms(h, g_ref[...]))
    w_hi, w_lo = _split_bf16(wr_ref[...])
    hi_both = jnp.dot(hn_hi, jnp.concatenate([w_hi, w_lo], axis=1), preferred_element_type=F32)
    logits = (hi_both[:, :LANES] + hi_both[:, LANES:]
              + jnp.dot(hn_lo, w_hi, preferred_element_type=F32)) + br_ref[...]

    lane = lax.broadcasted_iota(I32, (tm, LANES), 1).astype(F32)

    def first_lane(mask):
        return jnp.min(jnp.where(mask, lane, float(LANES)), axis=1, keepdims=True)

    is_g = lane < n_groups
    lg = jnp.where(is_g, logits, NEG_INF)
    mg = jnp.max(lg, axis=1, keepdims=True)
    p_sel = 1.0 / jnp.sum(jnp.exp(lg - mg), axis=1, keepdims=True)
    g_top = first_lane(lg == mg)
    lo = n_groups + g_top * n_experts
    in_grp = (lane >= lo) & (lane < lo + n_experts)
    le = jnp.where(in_grp, logits, NEG_INF)
    ee = jnp.exp(le - jnp.max(le, axis=1, keepdims=True))
    pe = ee / jnp.sum(ee, axis=1, keepdims=True)
    v1 = jnp.max(jnp.where(in_grp, pe, -1.0), axis=1, keepdims=True)
    i1 = first_lane(in_grp & (pe == v1))
    rest = in_grp & (lane != i1)
    v2 = jnp.max(jnp.where(rest, pe, -1.0), axis=1, keepdims=True)
    i2 = first_lane(rest & (pe == v2))
    gate1 = p_sel * (v1 / (v1 + v2))
    gate2 = p_sel * (v2 / (v1 + v2))

    sel1, sel2 = lane == i1, lane == i2
    onehot = (sel1 | sel2).astype(BF16)
    tri = (lax.broadcasted_iota(I32, (tm, tm), 1) < lax.broadcasted_iota(I32, (tm, tm), 0)).astype(BF16)
    running = jnp.dot(tri, onehot, preferred_element_type=F32) + carry_scr[...]
    rank1 = jnp.sum(jnp.where(sel1, running, 0.0), axis=1, keepdims=True)
    rank2 = jnp.sum(jnp.where(sel2, running, 0.0), axis=1, keepdims=True)
    carry_scr[...] = carry_scr[...] + jnp.sum(onehot.astype(F32), axis=0, keepdims=True)
    cnt_out_ref[...] = carry_scr[...]

    rec = jnp.zeros((tm, LANES), F32)
    for idx, val in ((ROUTE_F1, i1 - n_groups), (ROUTE_F2, i2 - n_groups),
                     (ROUTE_G1, gate1), (ROUTE_G2, gate2), (ROUTE_R1, rank1), (ROUTE_R2, rank2)):
        rec = jnp.where(lane == idx, val, rec)
    route_ref[...] = rec
    route_t_ref[...] = rec.T[:ROUTE_ROWS]


def _mix_route(x, pool, attn, w_out_bf16, g_ffn, w_route, b_route, cnt_in, *, n_groups, n_experts):
    n, d = x.shape
    tm = TM_MIX
    row = lambda i: (i, 0)
    fix = lambda i: (0, 0)
    return pl.pallas_call(
        functools.partial(_mix_route_body, n_groups=n_groups, n_experts=n_experts),
        grid=(n // tm,),
        in_specs=[pl.BlockSpec((tm, d), row), pl.BlockSpec((tm, pool.shape[1]), row),
                  pl.BlockSpec((tm, attn.shape[1]), row), pl.BlockSpec(w_out_bf16.shape, fix),
                  pl.BlockSpec((1, d), fix), pl.BlockSpec(w_route.shape, fix), pl.BlockSpec((1, LANES), fix),
                  pl.BlockSpec((1, LANES), fix)],
        out_specs=[pl.BlockSpec((tm, d), row), pl.BlockSpec((tm, LANES), row),
                   pl.BlockSpec((ROUTE_ROWS, tm), lambda i: (0, i)), pl.BlockSpec((1, LANES), fix)],
        out_shape=[jax.ShapeDtypeStruct((n, d), F32), jax.ShapeDtypeStruct((n, LANES), F32),
                   jax.ShapeDtypeStruct((ROUTE_ROWS, n), F32), jax.ShapeDtypeStruct((1, LANES), F32)],
        scratch_shapes=[pltpu.VMEM((1, LANES), F32)],
        compiler_params=_params("arbitrary"), name="mix_route",
    )(x, pool, attn, w_out_bf16, g_ffn, w_route, b_route, cnt_in)


def _dispatch_body(zlo_ref, zhi_ref, dest_ref, hp_ref, hs_ref, g_ref, xs_ref, xn_scr, zero_scr, sems, *, tiles_p,
                   n_steps):
    tm = hp_ref.shape[0]
    i = pl.program_id(0)
    zero_sem = 2 * TOP_K_INNER

    @pl.when(i == 0)
    def _():
        zero_scr[...] = jnp.zeros(zero_scr.shape, F32)
        for start in (True, False):
            def chunk(c, carry, start=start):
                cp = pltpu.make_async_copy(zero_scr, xs_ref.at[pl.ds(pl.multiple_of(c * ZERO_ROWS, ZERO_ROWS), ZERO_ROWS)],
                                           sems.at[zero_sem])
                cp.start() if start else cp.wait()
                return carry

            def segment(e, carry, chunk=chunk):
                return lax.fori_loop(zlo_ref[e], zhi_ref[e], chunk, carry)
            lax.fori_loop(0, zlo_ref.shape[0], segment, 0)

    buf = i % 2

    def wait_rows(b):
        for slot in range(TOP_K_INNER):
            pltpu.make_async_copy(xn_scr.at[b], xs_ref.at[pl.ds(0, tm)], sems.at[b * TOP_K_INNER + slot]).wait()

    @pl.when(i >= 2)
    def _():
        wait_rows(buf)

    @pl.when(i < tiles_p)
    def _():
        xn_scr[buf] = _rms(hp_ref[...], g_ref[...])

    @pl.when(i >= tiles_p)
    def _():
        xn_scr[buf] = _rms(hs_ref[...], g_ref[...])

    for r in range(tm):
        for slot in range(TOP_K_INNER):
            d = dest_ref[0, 0, slot * tm + r]
            pltpu.make_async_copy(xn_scr.at[buf, pl.ds(r, 1)], xs_ref.at[pl.ds(d, 1)],
                                  sems.at[buf * TOP_K_INNER + slot]).start(priority=slot)

    @pl.when(i == n_steps - 1)
    def _():
        if n_steps > 1:
            wait_rows(1 - buf)
        wait_rows(buf)


def _dispatch(zero_lo, zero_hi, dest, h_p, h_s, g_ffn, *, rows):
    tm = TM_ROW
    d = h_p.shape[1]
    tiles_p, tiles_s = h_p.shape[0] // tm, h_s.shape[0] // tm
    return pl.pallas_call(
        functools.partial(_dispatch_body, tiles_p=tiles_p, n_steps=tiles_p + tiles_s),
        grid_spec=pltpu.PrefetchScalarGridSpec(
            num_scalar_prefetch=2, grid=(tiles_p + tiles_s,),
            in_specs=[pl.BlockSpec((1, 1, TOP_K_INNER * tm), lambda i, lo, hi: (i, 0, 0), memory_space=pltpu.SMEM),
                      pl.BlockSpec((tm, d), lambda i, lo, hi: (jnp.minimum(i, tiles_p - 1), 0)),
                      pl.BlockSpec((tm, d), lambda i, lo, hi: (jnp.maximum(i - tiles_p, 0), 0)),
                      pl.BlockSpec((1, d), lambda i, lo, hi: (0, 0))],
            out_specs=pl.BlockSpec(memory_space=pl.ANY),
            scratch_shapes=[pltpu.VMEM((2, tm, d), F32), pltpu.VMEM((ZERO_ROWS, d), F32),
                            pltpu.SemaphoreType.DMA((2 * TOP_K_INNER + 1,))]),
        out_shape=jax.ShapeDtypeStruct((rows, d), F32),
        compiler_params=_params("arbitrary"), name="dispatch",
    )(zero_lo, zero_hi, dest, h_p, h_s, g_ffn)


GEMM_X_BUFS = 3
GEMM_Y_BUFS = 2


def _moe_gemm_body(expert_ref, valid_ref, xs_hbm, wg_ref, wu_ref, wd_ref, ys_hbm, x_buf, y_buf, wg_scr, wu_scr, wd_scr,
                   x_sems, y_sems, *, n_tiles):
    t = pl.program_id(0)
    tm = x_buf.shape[1]
    ahead = GEMM_X_BUFS - 1

    def x_copy(s):
        return pltpu.make_async_copy(xs_hbm.at[pl.ds(pl.multiple_of(s * tm, tm), tm)], x_buf.at[s % GEMM_X_BUFS],
                                     x_sems.at[s % GEMM_X_BUFS])

    def y_copy(s):
        return pltpu.make_async_copy(y_buf.at[s % GEMM_Y_BUFS], ys_hbm.at[pl.ds(pl.multiple_of(s * tm, tm), tm)],
                                     y_sems.at[s % GEMM_Y_BUFS])

    def tile_is_valid(s):
        return (s < n_tiles) & (valid_ref[jnp.minimum(s, n_tiles - 1)] > 0)

    @pl.when(t == 0)
    def _():
        for s in range(ahead):
            @pl.when(tile_is_valid(s))
            def _():
                x_copy(s).start()

    @pl.when(tile_is_valid(t + ahead))
    def _():
        x_copy(t + ahead).start()

    @pl.when((t == 0) | (expert_ref[t] != expert_ref[jnp.maximum(t - 1, 0)]))
    def _():
        wg_scr[...] = wg_ref[0].astype(BF16)
        wu_scr[...] = wu_ref[0].astype(BF16)
        wd_scr[...] = wd_ref[0].astype(BF16)

    @pl.when(t >= GEMM_Y_BUFS)
    def _():
        y_copy(t - GEMM_Y_BUFS).wait()

    ybuf = y_buf.at[t % GEMM_Y_BUFS]

    @pl.when(valid_ref[t] > 0)
    def _():
        x_copy(t).wait()
        x = x_buf[t % GEMM_X_BUFS].astype(BF16)
        gate = jnp.dot(x, wg_scr[...], preferred_element_type=F32)
        up = jnp.dot(x, wu_scr[...], preferred_element_type=F32)
        mid = (gate * jax.nn.sigmoid(gate) * up).astype(BF16)
        ybuf[...] = jnp.dot(mid, wd_scr[...], preferred_element_type=F32)

    @pl.when(valid_ref[t] == 0)
    def _():
        ybuf[...] = jnp.zeros(ybuf.shape, F32)

    y_copy(t).start()

    @pl.when(t == n_tiles - 1)
    def _():
        for back in range(min(GEMM_Y_BUFS, n_tiles) - 1, -1, -1):
            y_copy(t - back).wait()


def _moe_gemm(tile_expert, tile_valid, xs, w_gate, w_up, w_down):
    rows, d = xs.shape
    _, _, f = w_gate.shape
    tm = TM_GEMM
    n_tiles = rows // tm
    hbm = pl.BlockSpec(memory_space=pl.ANY)
    return pl.pallas_call(
        functools.partial(_moe_gemm_body, n_tiles=n_tiles),
        grid_spec=pltpu.PrefetchScalarGridSpec(
            num_scalar_prefetch=2, grid=(n_tiles,),
            in_specs=[hbm,
                      pl.BlockSpec((1, d, f), lambda t, e, v: (e[t], 0, 0)),
                      pl.BlockSpec((1, d, f), lambda t, e, v: (e[t], 0, 0)),
                      pl.BlockSpec((1, f, d), lambda t, e, v: (e[t], 0, 0))],
            out_specs=hbm,
            scratch_shapes=[pltpu.VMEM((GEMM_X_BUFS, tm, d), F32), pltpu.VMEM((GEMM_Y_BUFS, tm, d), F32),
                            pltpu.VMEM((d, f), BF16), pltpu.VMEM((d, f), BF16), pltpu.VMEM((f, d), BF16),
                            pltpu.SemaphoreType.DMA((GEMM_X_BUFS,)), pltpu.SemaphoreType.DMA((GEMM_Y_BUFS,))]),
        out_shape=jax.ShapeDtypeStruct((rows, d), F32),
        compiler_params=_params("arbitrary"), name="moe_gemm",
    )(tile_expert, tile_valid, xs, w_gate, w_up, w_down)


def _combine_body(dest_ref, next_ref, h_ref, route_ref, g_ref, ys_ref, y_ref, rows_scr, sems, *, n_steps):
    tm = h_ref.shape[0]
    i = pl.program_id(0)
    buf = i % 2

    def fetch(idx_ref, b):
        for r in range(tm):
            for slot in range(TOP_K_INNER):
                d = idx_ref[0, 0, slot * tm + r]
                pltpu.make_async_copy(ys_ref.at[pl.ds(d, 1)], rows_scr.at[b, slot, pl.ds(r, 1)],
                                      sems.at[b * TOP_K_INNER + slot]).start(priority=slot)

    @pl.when(i == 0)
    def _():
        fetch(dest_ref, 0)

    @pl.when(i + 1 < n_steps)
    def _():
        fetch(next_ref, 1 - buf)

    for slot in range(TOP_K_INNER):
        pltpu.make_async_copy(ys_ref.at[pl.ds(0, tm)], rows_scr.at[buf, slot], sems.at[buf * TOP_K_INNER + slot]).wait()
    route = route_ref[...]
    out = h_ref[...] + (route[:, ROUTE_G1:ROUTE_G1 + 1] * rows_scr[buf, 0]
                        + route[:, ROUTE_G2:ROUTE_G2 + 1] * rows_scr[buf, 1])
    y_ref[...] = _rms(out, g_ref[...])


def _combine(dest, h, route, g_final, ys):
    n, d = h.shape
    tm = TM_ROW
    n_steps = n // tm
    row = lambda i: (i, 0)
    idx_block = (1, 1, TOP_K_INNER * tm)
    return pl.pallas_call(
        functools.partial(_combine_body, n_steps=n_steps),
        grid=(n_steps,),
        in_specs=[pl.BlockSpec(idx_block, lambda i: (i, 0, 0), memory_space=pltpu.SMEM),
                  pl.BlockSpec(idx_block, lambda i: (jnp.minimum(i + 1, n_steps - 1), 0, 0), memory_space=pltpu.SMEM),
                  pl.BlockSpec((tm, d), row), pl.BlockSpec((tm, LANES), row), pl.BlockSpec((1, d), lambda i: (0, 0)),
                  pl.BlockSpec(memory_space=pl.ANY)],
        out_specs=pl.BlockSpec((tm, d), row),
        out_shape=jax.ShapeDtypeStruct((n, d), F32),
        scratch_shapes=[pltpu.VMEM((2, TOP_K_INNER, tm, d), F32), pltpu.SemaphoreType.DMA((2 * TOP_K_INNER,))],
        compiler_params=_params("arbitrary"), name="combine",
    )(dest, dest, h, route, g_final, ys)


def _sort_tables(counts, n_tiles):
    padded = ((counts + TM_GEMM - 1) // TM_GEMM) * TM_GEMM
    ends = jnp.cumsum(padded)
    offsets = ends - padded
    total = ends[-1]
    tile_start = jnp.arange(n_tiles, dtype=I32) * TM_GEMM
    tile_valid = (tile_start < total).astype(I32)
    last_block = jnp.maximum(total // TM_GEMM - 1, 0)
    tile_block = jnp.minimum(jnp.arange(n_tiles, dtype=I32), last_block)
    n_flat = counts.shape[0]
    tile_expert = jnp.minimum(jnp.sum((tile_block[:, None] * TM_GEMM >= ends[None, :]).astype(I32), axis=1), n_flat - 1)
    zero_lo = jnp.concatenate([(offsets + counts) // ZERO_ROWS, total[None] // ZERO_ROWS]).astype(I32)
    zero_hi = jnp.concatenate([ends // ZERO_ROWS, jnp.full((1,), n_tiles * TM_GEMM // ZERO_ROWS, I32)]).astype(I32)
    return offsets, tile_expert.astype(I32), tile_valid, zero_lo, zero_hi


def _dest_blocks(route_t, offsets):
    n = route_t.shape[1]
    experts = jnp.arange(offsets.shape[0], dtype=I32)[:, None]

    def dest(f_row, r_row):
        f = route_t[f_row].astype(I32)
        base = jnp.sum(jnp.where(f[None, :] == experts, offsets[:, None], 0), axis=0)
        return (base + route_t[r_row].astype(I32)).reshape(n // TM_ROW, TM_ROW)

    return jnp.concatenate([dest(ROUTE_F1, ROUTE_R1), dest(ROUTE_F2, ROUTE_R2)], axis=1)[:, None, :]


def kernel(x_prompt, x_sample, cache_k, cache_v, state_pool, g_mix, w_in, w_pool, pool_scale, w_out, g_ffn,
           w_router_group, b_router_group, w_router_expert, b_router_expert, w_gate, w_up, w_down, g_final):
    depth = g_mix.shape[0]
    assert depth == 1, "single-layer step"
    b, s, d = x_prompt.shape
    nd, t_new, _ = x_sample.shape
    n_heads = cache_k.shape[3]
    attn_w = n_heads * HEAD_DIM
    pool_w = state_pool.shape[3]
    past = cache_k.shape[2]
    keep = min(ATTN_REACH, s)
    n_groups, n_experts = w_router_expert.shape[1], w_router_expert.shape[3]
    n_flat = n_groups * n_experts
    assert n_groups + n_flat <= LANES
    slopes = _alibi_slopes(n_heads)

    w_in_b = w_in[0].astype(BF16)
    w_out_b = w_out[0].astype(BF16)
    gw = pool_w // len(POOL_WINDOWS)
    w_bd = jnp.zeros((pool_w, pool_w), F32)
    for g in range(len(POOL_WINDOWS)):
        w_bd = w_bd.at[g * gw:(g + 1) * gw, g * gw:(g + 1) * gw].set(w_pool[0, g])
    w_bd = w_bd.astype(BF16)
    w_route = jnp.concatenate([w_router_group[0], jnp.transpose(w_router_expert[0], (1, 0, 2)).reshape(d, n_flat)], axis=1)
    w_route = jnp.pad(w_route, ((0, 0), (0, LANES - n_groups - n_flat)))
    b_route = jnp.pad(jnp.concatenate([b_router_group[0], b_router_expert[0].reshape(n_flat)]),
                      (0, LANES - n_groups - n_flat))[None]
    w_gate_f, w_up_f = w_gate[0].reshape(n_flat, d, -1), w_up[0].reshape(n_flat, d, -1)
    w_down_f = w_down[0].reshape(n_flat, -1, d)

    n_p = b * s
    u_p, q_p, k_p, v_p, kt_p, vt_p = _proj_in(x_prompt.reshape(n_p, d), g_mix, w_in_b, pool_w=pool_w, attn_w=attn_w,
                                               seq=s, keep=keep)
    attn_p = _attn_prompt(slopes, q_p.reshape(b, s, attn_w), k_p.reshape(b, s, attn_w), v_p.reshape(b, s, attn_w))
    u_p3 = u_p.reshape(b, s, pool_w)

    n_s = nd * t_new
    u_s, q_s, k_s, v_s, ktn_s, vtn_s = _proj_in(x_sample.reshape(n_s, d), g_mix, w_in_b, pool_w=pool_w, attn_w=attn_w,
                                                t_new=t_new)
    cache_kt = jnp.transpose(cache_k[0], (0, 2, 3, 1))
    cache_vt = jnp.transpose(cache_v[0], (0, 2, 3, 1))
    as3 = lambda a: a.reshape(nd, t_new, attn_w)
    attn_s, pool_p = _attn_sample_pool(slopes, as3(q_s), as3(k_s), as3(v_s), cache_kt, cache_vt, u_p3, w_bd, pool_scale)
    state_t = jnp.transpose(state_pool[0], (1, 0, 2))
    u_st = jnp.transpose(u_s.reshape(nd, t_new, pool_w), (1, 0, 2))
    pool_st, new_state_t = _pool_sample(state_t, u_st, w_bd, pool_scale)
    pool_s = jnp.transpose(pool_st, (1, 0, 2)).reshape(n_s, pool_w)

    route_kw = dict(n_groups=n_groups, n_experts=n_experts)
    h_p, route_p, route_pt, cnt_p = _mix_route(x_prompt.reshape(n_p, d), pool_p.reshape(n_p, pool_w),
                                               attn_p.reshape(n_p, attn_w), w_out_b, g_ffn, w_route, b_route,
                                               jnp.zeros((1, LANES), F32), **route_kw)
    h_s, route_s, route_st, cnt_all = _mix_route(x_sample.reshape(n_s, d), pool_s, attn_s.reshape(n_s, attn_w),
                                                 w_out_b, g_ffn, w_route, b_route, cnt_p, **route_kw)

    counts = cnt_all[0, n_groups:n_groups + n_flat].astype(I32)
    n_tiles = (TOP_K_INNER * (n_p + n_s) + n_flat * (TM_GEMM - 1)) // TM_GEMM
    offsets, tile_expert, tile_valid, zero_lo, zero_hi = _sort_tables(counts, n_tiles)
    dest_p, dest_s = _dest_blocks(route_pt, offsets), _dest_blocks(route_st, offsets)

    xs = _dispatch(zero_lo, zero_hi, jnp.concatenate([dest_p, dest_s], axis=0), h_p, h_s, g_ffn,
                   rows=n_tiles * TM_GEMM)
    ys = _moe_gemm(tile_expert, tile_valid, xs, w_gate_f, w_up_f, w_down_f)
    y_p = _combine(dest_p, h_p, route_p, g_final[None], ys)
    y_s = _combine(dest_s, h_s, route_s, g_final[None], ys)

    y_prompt = y_p.reshape(b, s, d)
    y_sample = y_s.reshape(nd, t_new, d)
    k_prompt = jnp.transpose(kt_p.reshape(b, n_heads, HEAD_DIM, keep), (0, 3, 1, 2))[None]
    v_prompt = jnp.transpose(vt_p.reshape(b, n_heads, HEAD_DIM, keep), (0, 3, 1, 2))[None]
    pool_prompt = u_p3[:, s - POOL_STATE:][None]
    k_sample = jnp.transpose(ktn_s.reshape(t_new, n_heads, HEAD_DIM, nd), (3, 0, 1, 2))[None]
    v_sample = jnp.transpose(vtn_s.reshape(t_new, n_heads, HEAD_DIM, nd), (3, 0, 1, 2))[None]
    pool_sample = jnp.transpose(new_state_t[-POOL_STATE:], (1, 0, 2))[None]
    return (y_prompt, y_sample, k_prompt, v_prompt, pool_prompt, k_sample, v_sample, pool_sample)
```
